```python
import math
import jax, jax.numpy as jnp
from jax import lax
import numpy as np

D_MODEL = 2048
BATCH = 2
SEQ = 4096
DEPTH = 1

GRID_W = 64
CTX_LEN = 256
RET_HEADS = 4
RET_HEAD_DIM = 256
RET_WIDTH = RET_HEADS * RET_HEAD_DIM
NA_HEADS = 8
NA_HEAD_DIM = 128
NA_WIDTH = NA_HEADS * NA_HEAD_DIM
MIX_WIDTH = RET_WIDTH + NA_WIDTH
IN_PROJ_WIDTH = 4 * RET_WIDTH + 3 * NA_WIDTH
RET_CHUNK = 128
NA_ROWS = 8
NA_COLS = 16
ROPE_BASE = 10000.0
N_GROUPS = 4
EXPERTS_PER_GROUP = 8
N_EXPERTS = N_GROUPS * EXPERTS_PER_GROUP
TOP_K_EXPERTS = 2
EXPERT_FF = 512
N_MOD = 6
NORM_EPS = 1e-6

kernel_name = "hymba_style_retention_natten_hmoe_dit"


def rmsnorm(x, w):
    xf = x.astype(jnp.float32)
    y = xf * lax.rsqrt(jnp.mean(xf * xf, axis=-1, keepdims=True) + NORM_EPS)
    return (y * w.astype(jnp.float32)).astype(x.dtype)


def modulate(h, shift, scale):
    return h * (1.0 + scale) + shift


def heads(t, n_heads):
    return t.reshape(t.shape[:-1] + (n_heads, t.shape[-1] // n_heads))


def bhnd(t):
    return jnp.swapaxes(t, 1, 2).astype(jnp.float32)


def split_proj(p):
    cuts = [RET_WIDTH, 2 * RET_WIDTH, 3 * RET_WIDTH, 4 * RET_WIDTH,
            4 * RET_WIDTH + NA_WIDTH, 4 * RET_WIDTH + 2 * NA_WIDTH]
    return jnp.split(p, cuts, axis=-1)


def rope_1d(x, pos):
    d = x.shape[-1]
    inv = ROPE_BASE ** (-jnp.arange(0, d, 2, dtype=jnp.float32) / d)
    ang = pos.astype(jnp.float32)[:, None] * inv[None, :]
    cos = jnp.cos(ang)[None, :, None, :]
    sin = jnp.sin(ang)[None, :, None, :]
    x1, x2 = x[..., : d // 2], x[..., d // 2:]
    return jnp.concatenate([x1 * cos - x2 * sin, x1 * sin + x2 * cos], axis=-1).astype(x.dtype)


def rope_2d(x, rows, cols):
    half = x.shape[-1] // 2
    return jnp.concatenate([rope_1d(x[..., :half], rows), rope_1d(x[..., half:], cols)], axis=-1)


def retention_chunked(q, k, v, log_gamma, state0):
    b, h, n, _ = q.shape
    dv = v.shape[-1]
    nc = n // RET_CHUNK
    pos = jnp.arange(RET_CHUNK, dtype=jnp.float32)
    diff = pos[:, None] - pos[None, :]
    lg = log_gamma[:, None, None]
    intra = jnp.where(diff >= 0, jnp.exp(lg * jnp.maximum(diff, 0.0)), 0.0)
    q_dec = jnp.exp(log_gamma[:, None] * (pos + 1.0))[..., None]
    k_dec = jnp.exp(log_gamma[:, None] * (RET_CHUNK - 1.0 - pos))[..., None]
    c_dec = jnp.exp(log_gamma * RET_CHUNK)[:, None, None]

    def chunks(t):
        return jnp.moveaxis(t.reshape(b, h, nc, RET_CHUNK, t.shape[-1]), 2, 0)

    def step(state, qkv):
        qi, ki, vi = qkv
        scores = jnp.einsum('bhid,bhjd->bhij', qi, ki) * intra
        out = (jnp.einsum('bhij,bhjv->bhiv', scores, vi)
               + jnp.einsum('bhid,bhdv->bhiv', qi * q_dec, state))
        state = state * c_dec + jnp.einsum('bhjd,bhjv->bhdv', ki * k_dec, vi)
        return state, out

    state, out = lax.scan(step, state0, (chunks(q), chunks(k), chunks(v)))
    return jnp.moveaxis(out, 0, 2).reshape(b, h, n, dv), state


def retention_bidir(q, k, v, lg_f, lg_b, s0_f, s0_b):
    o_f, s_f = retention_chunked(q, k, v, lg_f, s0_f)
    o_b, s_b = retention_chunked(jnp.flip(q, 2), jnp.flip(k, 2), jnp.flip(v, 2), lg_b, s0_b)
    return o_f + jnp.flip(o_b, 2), s_f, s_b


def retention_final_state(k, v, log_gamma, reverse):
    n = k.shape[2]
    pos = jnp.arange(n, dtype=jnp.float32)
    expo = pos if reverse else (n - 1.0 - pos)
    w = jnp.exp(log_gamma[:, None] * expo)[..., None]
    return jnp.einsum('bhjd,bhjv->bhdv', k * w, v)


def retention_output(o, gn_w, g):
    b, h, n, dv = o.shape
    mu = jnp.mean(o, axis=-1, keepdims=True)
    var = jnp.mean(jnp.square(o - mu), axis=-1, keepdims=True)
    o = (o - mu) * lax.rsqrt(var + NORM_EPS) * gn_w.astype(jnp.float32).reshape(h, 1, dv)
    o = jnp.swapaxes(o, 1, 2).reshape(b, n, h * dv)
    return (o * jax.nn.silu(g.astype(jnp.float32))).astype(g.dtype)


def neighbourhood_attention(q, k, v, kc, vc, rpb):
    b, n, h, d = q.shape
    rows = n // GRID_W
    win_r = min(NA_ROWS, rows)
    n_keys = win_r * NA_COLS
    scale = d ** -0.5
    qg = (q * scale).reshape(b, rows, GRID_W, h, d)
    kg = k.reshape(b, rows, GRID_W, h, d)
    vg = v.reshape(b, rows, GRID_W, h, d)
    kc_s = kc
    col = jnp.arange(GRID_W)
    col_start = jnp.clip(col - NA_COLS // 2, 0, GRID_W - NA_COLS)
    col_idx = col_start[:, None] + jnp.arange(NA_COLS)[None, :]
    col_off = col_idx - col[:, None] + (NA_COLS - 1)

    def gather_window(t, rs):
        band = lax.dynamic_slice_in_dim(t, rs, win_r, axis=1)
        win = band[:, :, col_idx]
        return jnp.moveaxis(win, 1, 2).reshape(b, GRID_W, n_keys, h, d)

    def row_fn(r):
        rs = jnp.clip(r - NA_ROWS // 2, 0, rows - win_r)
        qr = lax.dynamic_index_in_dim(qg, r, axis=1, keepdims=False)
        kw = gather_window(kg, rs)
        vw = gather_window(vg, rs)
        row_off = rs + jnp.arange(win_r) - r + (NA_ROWS - 1)
        bias = rpb[:, row_off[:, None, None], col_off[None, :, :]]
        bias = jnp.moveaxis(bias, 1, 2).reshape(h, GRID_W, n_keys)
        s_win = jnp.einsum('bqhd,bqkhd->bhqk', qr, kw) + bias[None]
        s_ctx = jnp.einsum('bqhd,bkhd->bhqk', qr, kc_s)
        p = jax.nn.softmax(jnp.concatenate([s_win, s_ctx], axis=-1).astype(jnp.float32), axis=-1).astype(v.dtype)
        return (jnp.einsum('bhqk,bqkhd->bqhd', p[..., :n_keys], vw)
                + jnp.einsum('bhqk,bkhd->bqhd', p[..., n_keys:], vc))

    out = lax.map(row_fn, jnp.arange(rows))
    return jnp.moveaxis(out, 0, 1).reshape(b, n, h * d)


def context_attention(q, k, v):
    b, l, h, d = q.shape
    s = jnp.einsum('bqhd,bkhd->bhqk', q * d ** -0.5, k)
    p = jax.nn.softmax(s.astype(jnp.float32), axis=-1).astype(v.dtype)
    return jnp.einsum('bhqk,bkhd->bqhd', p, v).reshape(b, l, h * d)


def hierarchical_moe(h, w_rg, b_rg, w_re, b_re, w_gate, w_up, w_down):
    g_logits = (h @ w_rg + b_rg).astype(jnp.float32)
    g_prob = jax.nn.softmax(g_logits, axis=-1)
    g_sel = jnp.argmax(g_logits, axis=-1)
    g_w = jnp.take_along_axis(g_prob, g_sel[:, None], axis=-1)
    e_logits = (jnp.einsum('td,gde->tge', h, w_re) + b_re).astype(jnp.float32)
    e_logits = jnp.take_along_axis(e_logits, g_sel[:, None, None], axis=1)[:, 0]
    top_v, top_i = lax.top_k(e_logits, TOP_K_EXPERTS)
    e_w = jax.nn.softmax(top_v, axis=-1) * g_w
    expert_id = g_sel[:, None] * EXPERTS_PER_GROUP + top_i
    combine = jnp.sum(jax.nn.one_hot(expert_id, N_EXPERTS, dtype=jnp.float32) * e_w[..., None], axis=1)
    combine = combine.astype(h.dtype)
    out = jnp.zeros_like(h)
    for e in range(N_EXPERTS):
        a = jax.nn.silu(h @ w_gate[e]) * (h @ w_up[e])
        out = out + combine[:, e:e + 1] * (a @ w_down[e])
    return out


def trunk_layer(x, xc, c, c_ctx, w_mod, b_mod, norm_mix_w, w_in, ret_decay_f, ret_decay_b,
                ret_gn_w, na_rpb, w_out, norm_ffn_w, w_rg, b_rg, w_re, b_re,
                w_gate, w_up, w_down, last):
    b, n, d_model = x.shape
    mod = jax.nn.silu(c) @ w_mod + b_mod
    sh_a, sc_a, g_a, sh_f, sc_f, g_f = [m[:, None, :] for m in jnp.split(mod, N_MOD, axis=-1)]
    mod_c = jax.nn.silu(c_ctx) @ w_mod + b_mod
    csh_a, csc_a, cg_a, csh_f, csc_f, cg_f = jnp.split(mod_c, N_MOD, axis=-1)

    hx = modulate(rmsnorm(x, norm_mix_w), sh_a, sc_a)
    hc = modulate(rmsnorm(xc, norm_mix_w), csh_a, csc_a)
    rq, rk, rv, rg, nq, nk, nv = split_proj(hx @ w_in)
    crq, crk, crv, crg, cnq, cnk, cnv = split_proj(hc @ w_in)

    t = jnp.arange(n)
    rows_pos, cols_pos = t // GRID_W, t % GRID_W
    lg_f = jax.nn.log_sigmoid(ret_decay_f.astype(jnp.float32))
    lg_b = jax.nn.log_sigmoid(ret_decay_b.astype(jnp.float32))
    k_scale = RET_HEAD_DIM ** -0.5

    ck = bhnd(heads(crk, RET_HEADS)) * k_scale
    cv = bhnd(heads(crv, RET_HEADS))
    if last:
        s_f = retention_final_state(ck, cv, lg_f, False)
        s_b = retention_final_state(ck, cv, lg_b, True)
    else:
        zeros = jnp.zeros((b, RET_HEADS, RET_HEAD_DIM, RET_HEAD_DIM), jnp.float32)
        oc, s_f, s_b = retention_bidir(bhnd(heads(crq, RET_HEADS)), ck, cv, lg_f, lg_b, zeros, zeros)
    q = bhnd(rope_2d(heads(rq, RET_HEADS), rows_pos, cols_pos))
    k = bhnd(rope_2d(heads(rk, RET_HEADS), rows_pos, cols_pos)) * k_scale
    v = bhnd(heads(rv, RET_HEADS))
    o, _, _ = retention_bidir(q, k, v, lg_f, lg_b, s_f, s_b)
    ret_x = retention_output(o, ret_gn_w, rg)

    kc_na = heads(cnk, NA_HEADS)
    vc_na = heads(cnv, NA_HEADS)
    na_x = neighbourhood_attention(heads(nq, NA_HEADS), heads(nk, NA_HEADS), heads(nv, NA_HEADS),
                                   kc_na, vc_na, na_rpb)
    x = x + g_a * (jnp.concatenate([ret_x, na_x], axis=-1) @ w_out)

    if not last:
        ret_c = retention_output(oc, ret_gn_w, crg)
        na_c = context_attention(heads(cnq, NA_HEADS), kc_na, vc_na)
        xc = xc + cg_a * (jnp.concatenate([ret_c, na_c], axis=-1) @ w_out)
        hcf = modulate(rmsnorm(xc, norm_ffn_w), csh_f, csc_f)
        xc = xc + cg_f * hierarchical_moe(hcf.reshape(-1, d_model), w_rg, b_rg, w_re, b_re,
                                          w_gate, w_up, w_down).reshape(xc.shape)

    hf = modulate(rmsnorm(x, norm_ffn_w), sh_f, sc_f)
    x = x + g_f * hierarchical_moe(hf.reshape(-1, d_model), w_rg, b_rg, w_re, b_re,
                                   w_gate, w_up, w_down).reshape(x.shape)
    return x, xc


def setup_inputs(seed: int = 0) -> dict:
    key = jax.random.key(seed)
    ks = jax.random.split(key, 24)
    D = D_MODEL
    nrm = jax.random.normal
    ladder = jnp.log(2.0 ** (5.0 + jnp.arange(RET_HEADS, dtype=jnp.float32)) - 1.0)
    return {
        "x": nrm(ks[0], (BATCH, SEQ, D), jnp.float32),
        "c": nrm(ks[1], (BATCH, D), jnp.float32),
        "ctx": nrm(ks[2], (BATCH, CTX_LEN, D), jnp.float32),
        "c_ctx": nrm(ks[3], (D,), jnp.float32),
        "w_mod": nrm(ks[4], (DEPTH, D, N_MOD * D), jnp.float32) * (0.5 * D ** -0.5),
        "b_mod": nrm(ks[5], (DEPTH, N_MOD * D), jnp.float32) * 0.01,
        "norm_mix_w": 1.0 + 0.05 * nrm(ks[6], (DEPTH, D), jnp.float32),
        "w_in": nrm(ks[7], (DEPTH, D, IN_PROJ_WIDTH), jnp.float32) * D ** -0.5,
        "ret_decay_f": ladder + 0.05 * nrm(ks[8], (DEPTH, RET_HEADS), jnp.float32),
        "ret_decay_b": ladder + 0.05 * nrm(ks[9], (DEPTH, RET_HEADS), jnp.float32),
        "ret_gn_w": 1.0 + 0.05 * nrm(ks[10], (DEPTH, RET_WIDTH), jnp.float32),
        "na_rpb": 0.1 * nrm(ks[11], (DEPTH, NA_HEADS, 2 * NA_ROWS - 1, 2 * NA_COLS - 1), jnp.float32),
        "w_out": nrm(ks[12], (DEPTH, MIX_WIDTH, D), jnp.float32) * MIX_WIDTH ** -0.5,
        "norm_ffn_w": 1.0 + 0.05 * nrm(ks[13], (DEPTH, D), jnp.float32),
        "w_router_group": nrm(ks[14], (DEPTH, D, N_GROUPS), jnp.float32) * D ** -0.5,
        "b_router_group": 0.01 * nrm(ks[15], (DEPTH, N_GROUPS), jnp.float32),
        "w_router_expert": nrm(ks[16], (DEPTH, N_GROUPS, D, EXPERTS_PER_GROUP), jnp.float32) * D ** -0.5,
        "b_router_expert": 0.01 * nrm(ks[17], (DEPTH, N_GROUPS, EXPERTS_PER_GROUP), jnp.float32),
        "w_gate": nrm(ks[18], (DEPTH, N_EXPERTS, D, EXPERT_FF), jnp.float32) * D ** -0.5,
        "w_up": nrm(ks[19], (DEPTH, N_EXPERTS, D, EXPERT_FF), jnp.float32) * D ** -0.5,
        "w_down": nrm(ks[20], (DEPTH, N_EXPERTS, EXPERT_FF, D), jnp.float32) * EXPERT_FF ** -0.5,
        "final_norm_w": 1.0 + 0.05 * nrm(ks[21], (D,), jnp.float32),
    }


def reference(x, c, ctx, c_ctx, w_mod, b_mod, norm_mix_w, w_in, ret_decay_f, ret_decay_b,
              ret_gn_w, na_rpb, w_out, norm_ffn_w, w_router_group, b_router_group,
              w_router_expert, b_router_expert, w_gate, w_up, w_down, final_norm_w):
    xc = ctx
    for l in range(DEPTH):
        x, xc = trunk_layer(x, xc, c, c_ctx, w_mod[l], b_mod[l], norm_mix_w[l], w_in[l],
                            ret_decay_f[l], ret_decay_b[l], ret_gn_w[l], na_rpb[l], w_out[l],
                            norm_ffn_w[l], w_router_group[l], b_router_group[l],
                            w_router_expert[l], b_router_expert[l], w_gate[l], w_up[l], w_down[l],
                            last=(l == DEPTH - 1))
    return rmsnorm(x, final_norm_w)
```

```python
import functools

import jax
import jax.numpy as jnp
from jax import lax
from jax.experimental import pallas as pl
from jax.experimental.pallas import tpu as pltpu

F32 = jnp.float32
BF16 = jnp.bfloat16

GRID_W = 64
RET_HEADS = 4
RET_HEAD_DIM = 256
RET_WIDTH = RET_HEADS * RET_HEAD_DIM
NA_HEADS = 8
NA_HEAD_DIM = 128
NA_WIDTH = NA_HEADS * NA_HEAD_DIM
RET_CHUNK = 128
NA_ROWS = 8
NA_COLS = 16
ROPE_BASE = 10000.0
N_MOD = 6
NORM_EPS = 1e-6
ROPE_HALF = RET_HEAD_DIM // 2

NA_QROWS = 4
NA_KROWS = NA_QROWS + NA_ROWS
NA_MASK = -1e30

LANES = 128
VMEM_LIMIT_BYTES = 56 * 1024 * 1024

R_ID0, R_ID1, R_W0, R_W1, R_RANK0, R_RANK1 = 0, 1, 2, 3, 4, 5

EXPERT_TILE = 256


def _params(*sem):
    return pltpu.CompilerParams(dimension_semantics=sem, vmem_limit_bytes=VMEM_LIMIT_BYTES)


def _dot(a, b):
    return jnp.dot(a, b, preferred_element_type=F32)


def _dot_nt(a, b):
    return lax.dot_general(a, b, (((1,), (1,)), ((), ())), preferred_element_type=F32)


def _dot_tn(a, b):
    return lax.dot_general(a, b, (((0,), (0,)), ((), ())), preferred_element_type=F32)


def _rms(x, w):
    return x * lax.rsqrt(jnp.mean(x * x, axis=-1, keepdims=True) + NORM_EPS) * w


def _silu(x):
    return x * jax.nn.sigmoid(x)


def _mod_kernel(c_ref, w_ref, b_ref, o_ref):
    a = _silu(c_ref[...]).astype(BF16)
    o_ref[...] = _dot(a, w_ref[...].astype(BF16)) + b_ref[...]


def _modulation(cc, w_mod, b_mod):
    rows, d = cc.shape
    width = w_mod.shape[1]
    tn = next(t for t in (1024, 512, 256, LANES) if width % t == 0)
    return pl.pallas_call(
        _mod_kernel,
        grid=(width // tn,),
        in_specs=[
            pl.BlockSpec((rows, d), lambda j: (0, 0)),
            pl.BlockSpec((d, tn), lambda j: (0, j)),
            pl.BlockSpec((1, tn), lambda j: (0, j)),
        ],
        out_specs=pl.BlockSpec((rows, tn), lambda j: (0, j)),
        out_shape=jax.ShapeDtypeStruct((rows, width), F32),
        compiler_params=_params("arbitrary"),
        name="modulation",
    )(cc, w_mod, b_mod.reshape(1, width))


def _inproj_kernel(x_ref, sh_ref, sc_ref, nw_ref, w_ref, o_ref, h_ref):
    @pl.when(pl.program_id(1) == 0)
    def _():
        y = _rms(x_ref[...], nw_ref[...])
        h_ref[...] = (y * (1.0 + sc_ref[0]) + sh_ref[0]).astype(BF16)

    o_ref[...] = _dot(h_ref[...], w_ref[...]).astype(BF16)


def _in_projection(x2d, mod3, mod_row_of_tile, norm_w, w_in, tm):
    m, d = x2d.shape
    width = w_in.shape[1]
    tn = 1024
    return pl.pallas_call(
        _inproj_kernel,
        grid=(m // tm, width // tn),
        in_specs=[
            pl.BlockSpec((tm, d), lambda i, j: (i, 0)),
            pl.BlockSpec((1, 1, d), lambda i, j: (mod_row_of_tile(i) * N_MOD + 0, 0, 0)),
            pl.BlockSpec((1, 1, d), lambda i, j: (mod_row_of_tile(i) * N_MOD + 1, 0, 0)),
            pl.BlockSpec((1, d), lambda i, j: (0, 0)),
            pl.BlockSpec((d, tn), lambda i, j: (0, j)),
        ],
        out_specs=pl.BlockSpec((tm, tn), lambda i, j: (i, j)),
        out_shape=jax.ShapeDtypeStruct((m, width), BF16),
        scratch_shapes=[pltpu.VMEM((tm, d), BF16)],
        compiler_params=_params("parallel", "arbitrary"),
        name="in_projection",
    )(x2d, mod3, mod3, norm_w.reshape(1, d), w_in)


def _ret_kernel(lg_ref, q_ref, k_ref, v_ref, g_ref, ck_ref, cv_ref, cosr_ref, sinr_ref,
                cosc_ref, sinc_ref, gnw_ref, o_ref, qr_s, kr_s, inter_s, sf_s, sb_s):
    head = pl.program_id(1)
    lgf = lg_ref[0, head]
    lgb = lg_ref[1, head]
    n = q_ref.shape[0]
    c = RET_CHUNK
    nc = n // c
    ctx_len = ck_ref.shape[0]
    k_scale = RET_HEAD_DIM ** -0.5

    posl = lax.broadcasted_iota(jnp.int32, (ctx_len, 1), 0).astype(F32)
    ck = ck_ref[...].astype(F32) * k_scale
    cv = cv_ref[...]
    sf_s[...] = _dot_tn((ck * jnp.exp(lgf * (ctx_len - 1.0 - posl))).astype(BF16), cv)
    sb_s[...] = _dot_tn((ck * jnp.exp(lgb * posl)).astype(BF16), cv)

    cosc = cosc_ref[...]
    sinc = sinc_ref[...]

    def rope_chunk(ci, carry):
        r0 = pl.multiple_of(ci * c, c)
        cosr = cosr_ref[pl.ds(r0, c), :]
        sinr = sinr_ref[pl.ds(r0, c), :]

        def rope(x):
            xa = x[:, :ROPE_HALF]
            xb = x[:, ROPE_HALF:]
            ya = xa * cosr + pltpu.roll(xa, ROPE_HALF // 2, 1) * sinr
            yb = xb * cosc + pltpu.roll(xb, ROPE_HALF // 2, 1) * sinc
            return jnp.concatenate([ya, yb], axis=1)

        qr_s[pl.ds(r0, c), :] = rope(q_ref[pl.ds(r0, c), :].astype(F32)).astype(BF16)
        kr_s[pl.ds(r0, c), :] = (rope(k_ref[pl.ds(r0, c), :].astype(F32)) * k_scale).astype(BF16)
        return carry

    lax.fori_loop(0, nc, rope_chunk, 0)

    pos = lax.broadcasted_iota(jnp.int32, (c, 1), 0).astype(F32)
    qdec_f = jnp.exp(lgf * (pos + 1.0))
    kdec_f = jnp.exp(lgf * (c - 1.0 - pos))
    cdec_f = jnp.exp(lgf * c)
    qdec_b = jnp.exp(lgb * (c - pos))
    kdec_b = jnp.exp(lgb * pos)
    cdec_b = jnp.exp(lgb * c)
    diff = (lax.broadcasted_iota(jnp.int32, (c, c), 0)
            - lax.broadcasted_iota(jnp.int32, (c, c), 1)).astype(F32)
    intra = (jnp.where(diff >= 0, jnp.exp(lgf * jnp.maximum(diff, 0.0)), 0.0)
             + jnp.where(diff <= 0, jnp.exp(lgb * jnp.maximum(-diff, 0.0)), 0.0))

    def fwd_chunk(ci, carry):
        r0 = pl.multiple_of(ci * c, c)
        q = qr_s[pl.ds(r0, c), :].astype(F32)
        k = kr_s[pl.ds(r0, c), :].astype(F32)
        v = v_ref[pl.ds(r0, c), :]
        state = sf_s[...]
        inter_s[pl.ds(r0, c), :] = _dot((q * qdec_f).astype(BF16), state.astype(BF16))
        sf_s[...] = state * cdec_f + _dot_tn((k * kdec_f).astype(BF16), v)
        return carry

    lax.fori_loop(0, nc, fwd_chunk, 0)

    gnw = gnw_ref[...]

    def bwd_chunk(i, carry):
        r0 = pl.multiple_of((nc - 1 - i) * c, c)
        qb = qr_s[pl.ds(r0, c), :]
        kb = kr_s[pl.ds(r0, c), :]
        v = v_ref[pl.ds(r0, c), :]
        q = qb.astype(F32)
        k = kb.astype(F32)
        state = sb_s[...]
        scores = _dot_nt(qb, kb) * intra
        o = (_dot(scores.astype(BF16), v) + inter_s[pl.ds(r0, c), :]
             + _dot((q * qdec_b).astype(BF16), state.astype(BF16)))
        sb_s[...] = state * cdec_b + _dot_tn((k * kdec_b).astype(BF16), v)
        mu = jnp.mean(o, axis=-1, keepdims=True)
        d = o - mu
        var = jnp.mean(d * d, axis=-1, keepdims=True)
        on = d * lax.rsqrt(var + NORM_EPS) * gnw
        gate = _silu(g_ref[pl.ds(r0, c), :].astype(F32))
        o_ref[pl.ds(r0, c), :] = (on * gate).astype(BF16)
        return carry

    lax.fori_loop(0, nc, bwd_chunk, 0)


def _rope_tables(n):
    inv = ROPE_BASE ** (-jnp.arange(0, ROPE_HALF, 2, dtype=F32) / ROPE_HALF)

    def tables(pos):
        ang = pos.astype(F32)[:, None] * inv[None, :]
        cos = jnp.cos(ang)
        sin = jnp.sin(ang)
        return jnp.concatenate([cos, cos], axis=1), jnp.concatenate([-sin, sin], axis=1)

    t = jnp.arange(n)
    cosr, sinr = tables(t // GRID_W)
    cosc, sinc = tables(jnp.arange(RET_CHUNK) % GRID_W)
    return cosr, sinr, cosc, sinc


def _retention(proj, cproj, lg, gn_w, batch, n, ctx_len):
    hd = RET_HEAD_DIM
    cosr, sinr, cosc, sinc = _rope_tables(n)
    col = lambda which: (lambda b, h, lg_ref: (b, which * RET_HEADS + h))
    const = lambda b, h, lg_ref: (0, 0)
    grid_spec = pltpu.PrefetchScalarGridSpec(
        num_scalar_prefetch=1,
        grid=(batch, RET_HEADS),
        in_specs=[
            pl.BlockSpec((n, hd), col(0)),
            pl.BlockSpec((n, hd), col(1)),
            pl.BlockSpec((n, hd), col(2)),
            pl.BlockSpec((n, hd), col(3)),
            pl.BlockSpec((ctx_len, hd), col(1)),
            pl.BlockSpec((ctx_len, hd), col(2)),
            pl.BlockSpec((n, ROPE_HALF), const),
            pl.BlockSpec((n, ROPE_HALF), const),
            pl.BlockSpec((RET_CHUNK, ROPE_HALF), const),
            pl.BlockSpec((RET_CHUNK, ROPE_HALF), const),
            pl.BlockSpec((1, hd), lambda b, h, lg_ref: (0, h)),
        ],
        out_specs=pl.BlockSpec((n, hd), lambda b, h, lg_ref: (b, h)),
        scratch_shapes=[
            pltpu.VMEM((n, hd), BF16),
            pltpu.VMEM((n, hd), BF16),
            pltpu.VMEM((n, hd), F32),
            pltpu.VMEM((hd, hd), F32),
            pltpu.VMEM((hd, hd), F32),
        ],
    )
    return pl.pallas_call(
        _ret_kernel,
        grid_spec=grid_spec,
        out_shape=jax.ShapeDtypeStruct((batch * n, RET_WIDTH), BF16),
        compiler_params=_params("parallel", "arbitrary"),
        name="retention",
    )(lg, proj, proj, proj, proj, cproj, cproj, cosr, sinr, cosc, sinc, gn_w.reshape(1, RET_WIDTH))


def _na_kernel(q_ref, k_ref, v_ref, kc_ref, vc_ref, bias_ref, o_ref):
    n = q_ref.shape[0]
    rows = n // GRID_W
    qb = NA_QROWS * GRID_W
    kb = NA_KROWS * GRID_W
    nb = rows // NA_QROWS
    scale = NA_HEAD_DIM ** -0.5
    kc = kc_ref[...]
    vc = vc_ref[...]

    def block(rb, carry):
        ws = jnp.clip(rb * NA_QROWS - NA_ROWS // 2, 0, rows - NA_KROWS)
        variant = jnp.where(rb == 0, 0, jnp.where(rb == nb - 1, 2, 1))
        q0 = pl.multiple_of(rb * qb, qb)
        k0 = pl.multiple_of(ws * GRID_W, GRID_W)
        q = (q_ref[pl.ds(q0, qb), :].astype(F32) * scale).astype(BF16)
        kw = k_ref[pl.ds(k0, kb), :]
        vw = v_ref[pl.ds(k0, kb), :]
        s_win = _dot_nt(q, kw) + bias_ref[0, variant]
        s_ctx = _dot_nt(q, kc)
        m = jnp.maximum(jnp.max(s_win, axis=-1, keepdims=True), jnp.max(s_ctx, axis=-1, keepdims=True))
        p_win = jnp.exp(s_win - m)
        p_ctx = jnp.exp(s_ctx - m)
        den = jnp.sum(p_win, axis=-1, keepdims=True) + jnp.sum(p_ctx, axis=-1, keepdims=True)
        o = _dot(p_win.astype(BF16), vw) + _dot(p_ctx.astype(BF16), vc)
        o_ref[pl.ds(q0, qb), :] = (o / den).astype(BF16)
        return carry

    lax.fori_loop(0, nb, block, 0)


def _na_bias(rpb, rows):
    heads = rpb.shape[0]
    c = jnp.arange(GRID_W)
    cs = jnp.clip(c - NA_COLS // 2, 0, GRID_W - NA_COLS)
    kcol = jnp.arange(GRID_W)
    vcol = (kcol[None, :] >= cs[:, None]) & (kcol[None, :] < cs[:, None] + NA_COLS)
    coff = jnp.clip(kcol[None, :] - c[:, None] + NA_COLS - 1, 0, 2 * NA_COLS - 2)

    def variant(r0):
        ws = min(max(r0 - NA_ROWS // 2, 0), rows - NA_KROWS)
        r = r0 + jnp.arange(NA_QROWS)
        rs = jnp.clip(r - NA_ROWS // 2, 0, rows - NA_ROWS)
        kr = ws + jnp.arange(NA_KROWS)
        vrow = (kr[None, :] >= rs[:, None]) & (kr[None, :] < rs[:, None] + NA_ROWS)
        roff = jnp.clip(kr[None, :] - r[:, None] + NA_ROWS - 1, 0, 2 * NA_ROWS - 2)
        b = rpb[:, roff[:, None, :, None], coff[None, :, None, :]]
        valid = vrow[:, None, :, None] & vcol[None, :, None, :]
        b = jnp.where(valid[None], b, NA_MASK)
        return b.reshape(heads, NA_QROWS * GRID_W, NA_KROWS * GRID_W)

    return jnp.stack([variant(0), variant(2 * NA_QROWS), variant(rows - NA_QROWS)], axis=1)


def _neighbourhood_attention(proj, cproj, rpb, batch, n, ctx_len):
    hd = NA_HEAD_DIM
    rows = n // GRID_W
    assert rows % NA_QROWS == 0 and rows >= NA_KROWS + NA_QROWS
    bias = _na_bias(rpb.astype(F32), rows)
    base = 4 * RET_WIDTH // hd
    col = lambda which: (lambda b, h: (b, base + which * NA_HEADS + h))
    return pl.pallas_call(
        _na_kernel,
        grid=(batch, NA_HEADS),
        in_specs=[
            pl.BlockSpec((n, hd), col(0)),
            pl.BlockSpec((n, hd), col(1)),
            pl.BlockSpec((n, hd), col(2)),
            pl.BlockSpec((ctx_len, hd), col(1)),
            pl.BlockSpec((ctx_len, hd), col(2)),
            pl.BlockSpec((1, 3, NA_QROWS * GRID_W, NA_KROWS * GRID_W), lambda b, h: (h, 0, 0, 0)),
        ],
        out_specs=pl.BlockSpec((n, hd), lambda b, h: (b, h)),
        out_shape=jax.ShapeDtypeStruct((batch * n, NA_WIDTH), BF16),
        compiler_params=_params("parallel", "arbitrary"),
        name="neighbourhood_attention",
    )(proj, proj, proj, cproj, cproj, bias)


def _outproj_kernel(n_groups, per_group, ret_ref, na_ref, w1_ref, w2_ref, x_ref, ga_ref, shf_ref,
                    scf_ref, nw_ref, wr_ref, br_ref, x1_ref, hf_ref, route_ref, cnt_ref, carry_s):
    @pl.when(pl.program_id(0) == 0)
    def _():
        carry_s[...] = jnp.zeros_like(carry_s)

    acc = _dot(ret_ref[...], w1_ref[...]) + _dot(na_ref[...], w2_ref[...])
    x1 = x_ref[...] + ga_ref[0] * acc
    x1_ref[...] = x1
    hf = _rms(x1, nw_ref[...]) * (1.0 + scf_ref[0]) + shf_ref[0]
    hf_ref[...] = hf

    logits = jnp.dot(hf, wr_ref[...], preferred_element_type=F32,
                     precision=lax.Precision.HIGHEST) + br_ref[...]
    tm = logits.shape[0]
    lane = lax.broadcasted_iota(jnp.int32, (tm, LANES), 1)
    neg = -jnp.inf

    def first_max(vals):
        top = jnp.max(vals, axis=-1, keepdims=True)
        idx = jnp.min(jnp.where(vals == top, lane, LANES), axis=-1, keepdims=True)
        return top, idx

    g_logits = jnp.where(lane < n_groups, logits, neg)
    g_top, g_sel = first_max(g_logits)
    g_w = 1.0 / jnp.sum(jnp.exp(g_logits - g_top), axis=-1, keepdims=True)
    lo = n_groups + g_sel * per_group
    e_logits = jnp.where((lane >= lo) & (lane < lo + per_group), logits, neg)
    v0, i0 = first_max(e_logits)
    v1, i1 = first_max(jnp.where(lane == i0, neg, e_logits))
    e1 = jnp.exp(v1 - v0)
    w0 = g_w / (1.0 + e1)
    w1 = g_w * e1 / (1.0 + e1)

    hit0 = lane == i0
    hit1 = lane == i1
    onehot = jnp.where(hit0 | hit1, 1.0, 0.0)
    earlier = jnp.where(lax.broadcasted_iota(jnp.int32, (tm, tm), 0)
                        > lax.broadcasted_iota(jnp.int32, (tm, tm), 1), 1.0, 0.0).astype(BF16)
    before = _dot(earlier, onehot.astype(BF16)) + carry_s[...]
    rank0 = jnp.sum(jnp.where(hit0, before, 0.0), axis=-1, keepdims=True)
    rank1 = jnp.sum(jnp.where(hit1, before, 0.0), axis=-1, keepdims=True)
    carry_s[...] = carry_s[...] + jnp.sum(onehot, axis=0, keepdims=True)
    cnt_ref[...] = carry_s[...]

    fields = {R_ID0: (i0 - n_groups).astype(F32), R_ID1: (i1 - n_groups).astype(F32),
              R_W0: w0, R_W1: w1, R_RANK0: rank0, R_RANK1: rank1}
    route = jnp.zeros((tm, LANES), F32)
    for idx, val in fields.items():
        route = jnp.where(lane == idx, val, route)
    route_ref[...] = route


def _out_projection(ret, na, w_out, x2d, mod3, norm_w, w_route, b_route, n, n_groups, per_group, tm):
    m, d = x2d.shape
    batch_of = lambda i: (i * tm) // n
    mod_spec = lambda which: pl.BlockSpec((1, 1, d), lambda i: (batch_of(i) * N_MOD + which, 0, 0))
    const2 = lambda i: (0, 0)
    return pl.pallas_call(
        functools.partial(_outproj_kernel, n_groups, per_group),
        grid=(m // tm,),
        in_specs=[
            pl.BlockSpec((tm, RET_WIDTH), lambda i: (i, 0)),
            pl.BlockSpec((tm, NA_WIDTH), lambda i: (i, 0)),
            pl.BlockSpec((RET_WIDTH, d), lambda i: (0, 0)),
            pl.BlockSpec((NA_WIDTH, d), lambda i: (RET_WIDTH // NA_WIDTH, 0)),
            pl.BlockSpec((tm, d), lambda i: (i, 0)),
            mod_spec(2), mod_spec(3), mod_spec(4),
            pl.BlockSpec((1, d), const2),
            pl.BlockSpec((d, LANES), const2),
            pl.BlockSpec((1, LANES), const2),
        ],
        out_specs=[
            pl.BlockSpec((tm, d), lambda i: (i, 0)),
            pl.BlockSpec((tm, d), lambda i: (i, 0)),
            pl.BlockSpec((tm, LANES), lambda i: (i, 0)),
            pl.BlockSpec((1, LANES), const2),
        ],
        out_shape=[
            jax.ShapeDtypeStruct((m, d), F32),
            jax.ShapeDtypeStruct((m, d), F32),
            jax.ShapeDtypeStruct((m, LANES), F32),
            jax.ShapeDtypeStruct((1, LANES), F32),
        ],
        scratch_shapes=[pltpu.VMEM((1, LANES), F32)],
        compiler_params=_params("arbitrary"),
        name="out_projection_router",
    )(ret, na, w_out, w_out, x2d, mod3, mod3, mod3, norm_w.reshape(1, d), w_route, b_route)


def _expert_kernel(tile_expert_ref, n_used_ref, slot_token_ref, hf_ref, wg_ref, wu_ref, wd_ref,
                   o_ref, xbuf, sem):
    del tile_expert_ref
    i = pl.program_id(0)
    n_used = n_used_ref[0]
    tm = xbuf.shape[1]

    def row_copy(tile, slot, j):
        tok = slot_token_ref[tile * tm + j]
        return pltpu.make_async_copy(hf_ref.at[pl.ds(tok, 1), :], xbuf.at[slot, pl.ds(j, 1), :],
                                     sem.at[slot])

    def start_tile(tile, slot):
        def body(j, carry):
            row_copy(tile, slot, j).start()
            return carry
        lax.fori_loop(0, tm, body, 0)

    def wait_tile(tile, slot):
        def body(j, carry):
            row_copy(tile, slot, j).wait()
            return carry
        lax.fori_loop(0, tm, body, 0)

    @pl.when((i == 0) & (n_used > 0))
    def _():
        start_tile(0, 0)

    @pl.when(i + 1 < n_used)
    def _():
        start_tile(i + 1, (i + 1) % 2)

    @pl.when(i < n_used)
    def _():
        slot = i % 2
        wait_tile(i, slot)
        x = xbuf[slot].astype(BF16)
        a = (_silu(_dot(x, wg_ref[0])) * _dot(x, wu_ref[0])).astype(BF16)
        o_ref[...] = _dot(a, wd_ref[0])

    @pl.when(i >= n_used)
    def _():
        o_ref[...] = jnp.zeros_like(o_ref)


def _experts(tile_expert, n_used, slot_token, hf, w_gate, w_up, w_down, n_tiles, tm):
    _, d = hf.shape
    ff = w_gate.shape[-1]
    grid_spec = pltpu.PrefetchScalarGridSpec(
        num_scalar_prefetch=3,
        grid=(n_tiles,),
        in_specs=[
            pl.BlockSpec(memory_space=pl.ANY),
            pl.BlockSpec((1, d, ff), lambda i, te, nu, st: (te[i], 0, 0)),
            pl.BlockSpec((1, d, ff), lambda i, te, nu, st: (te[i], 0, 0)),
            pl.BlockSpec((1, ff, d), lambda i, te, nu, st: (te[i], 0, 0)),
        ],
        out_specs=pl.BlockSpec((tm, d), lambda i, te, nu, st: (i, 0)),
        scratch_shapes=[pltpu.VMEM((2, tm, d), F32), pltpu.SemaphoreType.DMA((2,))],
    )
    return pl.pallas_call(
        _expert_kernel,
        grid_spec=grid_spec,
        out_shape=jax.ShapeDtypeStruct((n_tiles * tm, d), F32),
        compiler_params=_params("arbitrary"),
        name="routed_experts",
    )(tile_expert, n_used, slot_token, hf, w_gate, w_up, w_down)


def _combine_kernel(pos_ref, ys_ref, x1_ref, route_ref, gf_ref, fw_ref, o_ref, ybuf, sem):
    i = pl.program_id(0)
    n_tiles = pl.num_programs(0)
    tm = x1_ref.shape[0]

    def row_copy(tile, slot, j, choice):
        p = pos_ref[(tile * tm + j) * 2 + choice]
        return pltpu.make_async_copy(ys_ref.at[pl.ds(p, 1), :], ybuf.at[slot, choice, pl.ds(j, 1), :],
                                     sem.at[slot])

    def start_tile(tile, slot):
        def body(j, carry):
            row_copy(tile, slot, j, 0).start()
            row_copy(tile, slot, j, 1).start()
            return carry
        lax.fori_loop(0, tm, body, 0)

    def wait_tile(tile, slot):
        def body(j, carry):
            row_copy(tile, slot, j, 0).wait()
            row_copy(tile, slot, j, 1).wait()
            return carry
        lax.fori_loop(0, tm, body, 0)

    @pl.when(i == 0)
    def _():
        start_tile(0, 0)

    @pl.when(i + 1 < n_tiles)
    def _():
        start_tile(i + 1, (i + 1) % 2)

    slot = i % 2
    wait_tile(i, slot)
    route = route_ref[...]
    moe = route[:, R_W0:R_W0 + 1] * ybuf[slot, 0] + route[:, R_W1:R_W1 + 1] * ybuf[slot, 1]
    x2 = x1_ref[...] + gf_ref[0] * moe
    o_ref[...] = _rms(x2, fw_ref[...])


def _combine(pos, ys, x1, route, mod3, final_w, n, tm):
    m, d = x1.shape
    grid_spec = pltpu.PrefetchScalarGridSpec(
        num_scalar_prefetch=1,
        grid=(m // tm,),
        in_specs=[
            pl.BlockSpec(memory_space=pl.ANY),
            pl.BlockSpec((tm, d), lambda i, pos_ref: (i, 0)),
            pl.BlockSpec((tm, LANES), lambda i, pos_ref: (i, 0)),
            pl.BlockSpec((1, 1, d), lambda i, pos_ref: (((i * tm) // n) * N_MOD + 5, 0, 0)),
            pl.BlockSpec((1, d), lambda i, pos_ref: (0, 0)),
        ],
        out_specs=pl.BlockSpec((tm, d), lambda i, pos_ref: (i, 0)),
        scratch_shapes=[pltpu.VMEM((2, 2, tm, d), F32), pltpu.SemaphoreType.DMA((2,))],
    )
    return pl.pallas_call(
        _combine_kernel,
        grid_spec=grid_spec,
        out_shape=jax.ShapeDtypeStruct((m, d), F32),
        compiler_params=_params("arbitrary"),
        name="combine_final_norm",
    )(pos, ys, x1, route, mod3, final_w.reshape(1, d))


def kernel(x, c, ctx, c_ctx, w_mod, b_mod, norm_mix_w, w_in, ret_decay_f, ret_decay_b, ret_gn_w, na_rpb, w_out, norm_ffn_w, w_router_group, b_router_group, w_router_expert, b_router_expert, w_gate, w_up, w_down, final_norm_w):
    assert w_mod.shape[0] == 1, "single trunk layer"
    batch, n, d = x.shape
    ctx_len = ctx.shape[1]
    n_groups = w_router_group.shape[-1]
    per_group = w_router_expert.shape[-1]
    n_experts = w_gate.shape[1]
    assert n_groups * per_group == n_experts and n_groups + n_experts <= LANES

    mod_rows = 8
    cc = jnp.zeros((mod_rows, d), F32).at[:batch].set(c).at[batch].set(c_ctx)
    mod = _modulation(cc, w_mod[0], b_mod[0])
    mod3 = mod.reshape(mod_rows * N_MOD, 1, d)

    w_in_b = w_in[0].astype(BF16)
    tm = min(512, n)
    x2d = x.reshape(batch * n, d)
    proj = _in_projection(x2d, mod3, lambda i: (i * tm) // n, norm_mix_w[0], w_in_b, tm)
    cproj = _in_projection(ctx.reshape(batch * ctx_len, d), mod3, lambda i: batch, norm_mix_w[0],
                           w_in_b, ctx_len)

    lg = jnp.stack([jax.nn.log_sigmoid(ret_decay_f[0].astype(F32)),
                    jax.nn.log_sigmoid(ret_decay_b[0].astype(F32))])
    ret = _retention(proj, cproj, lg, ret_gn_w[0], batch, n, ctx_len)
    na = _neighbourhood_attention(proj, cproj, na_rpb[0], batch, n, ctx_len)

    w_route = jnp.concatenate(
        [w_router_group[0], jnp.moveaxis(w_router_expert[0], 0, 1).reshape(d, n_experts)], axis=1)
    w_route = jnp.pad(w_route.astype(F32), ((0, 0), (0, LANES - n_groups - n_experts)))
    b_route = jnp.concatenate([b_router_group[0], b_router_expert[0].reshape(-1)])
    b_route = jnp.pad(b_route.astype(F32), (0, LANES - n_groups - n_experts)).reshape(1, LANES)
    x1, hf, route, counts = _out_projection(ret, na, w_out[0].astype(BF16), x2d, mod3, norm_ffn_w[0],
                                            w_route, b_route, n, n_groups, per_group, tm)

    te = EXPERT_TILE
    tokens = batch * n
    n_tiles = (2 * tokens) // te + n_experts
    cnt = counts[0, n_groups:n_groups + n_experts].astype(jnp.int32)
    padded = ((cnt + te - 1) // te) * te
    ends = jnp.cumsum(padded)
    offs = ends - padded
    ids = route[:, R_ID0:R_ID1 + 1].astype(jnp.int32)
    ranks = route[:, R_RANK0:R_RANK1 + 1].astype(jnp.int32)
    pos = (offs[ids] + ranks).reshape(-1)
    n_used = (ends[-1] // te).astype(jnp.int32).reshape(1)
    tile_expert = jnp.minimum(
        jnp.searchsorted(ends, jnp.arange(n_tiles, dtype=jnp.int32) * te, side="right"),
        n_experts - 1).astype(jnp.int32)
    slot_token = jnp.zeros((n_tiles * te,), jnp.int32).at[pos].set(
        jnp.repeat(jnp.arange(tokens, dtype=jnp.int32), 2))

    ys = _experts(tile_expert, n_used, slot_token, hf, w_gate[0].astype(BF16), w_up[0].astype(BF16),
                  w_down[0].astype(BF16), n_tiles, te)
    out = _combine(pos, ys, x1, route, mod3, final_norm_w, n, min(256, n))
    return out.reshape(batch, n, d)
```

```python
import functools

import jax
import jax.numpy as jnp
import numpy as np
from jax import lax
from jax.experimental import pallas as pl
from jax.experimental.pallas import tpu as pltpu

F32 = jnp.float32
BF16 = jnp.bfloat16

GRID_W = 64
RET_HEADS = 4
RET_HEAD_DIM = 256
RET_WIDTH = RET_HEADS * RET_HEAD_DIM
NA_HEADS = 8
NA_HEAD_DIM = 128
NA_WIDTH = NA_HEADS * NA_HEAD_DIM
RET_CHUNK = 128
NA_ROWS = 8
NA_COLS = 16
ROPE_BASE = 10000.0
N_MOD = 6
NORM_EPS = 1e-6
ROPE_HALF = RET_HEAD_DIM // 2

NA_QROWS = 4
NA_KROWS = NA_QROWS + NA_ROWS
NA_MASK = -1e30

LANES = 128
SUBLANES = 8
VMEM_LIMIT_BYTES = 56 * 1024 * 1024

R_ID0, R_ID1, R_W0, R_W1, R_RANK0, R_RANK1 = 0, 1, 2, 3, 4, 5

EXPERT_TILE = 256


def _params(*sem):
    return pltpu.CompilerParams(dimension_semantics=sem, vmem_limit_bytes=VMEM_LIMIT_BYTES)


def _dot(a, b):
    return jnp.dot(a, b, preferred_element_type=F32)


def _dot_nt(a, b):
    return lax.dot_general(a, b, (((1,), (1,)), ((), ())), preferred_element_type=F32)


def _dot_tn(a, b):
    return lax.dot_general(a, b, (((0,), (0,)), ((), ())), preferred_element_type=F32)


def _rms(x, w):
    return x * lax.rsqrt(jnp.mean(x * x, axis=-1, keepdims=True) + NORM_EPS) * w


def _silu(x):
    return x * jax.nn.sigmoid(x)


def _tile_store(buf, val):
    for c in range(buf.shape[1]):
        buf[:, c] = val[:, c * LANES:(c + 1) * LANES].reshape(buf.shape[0], SUBLANES, LANES)


def _tile_load(buf):
    rows = buf.shape[0] * SUBLANES
    return jnp.concatenate([buf[:, c].reshape(rows, LANES) for c in range(buf.shape[1])], axis=1)


def _rows_out_copies(buf, hbm, first_row_tile, sem):
    return [pltpu.make_async_copy(buf.at[:, :, s, :], hbm.at[pl.ds(first_row_tile, buf.shape[0]), s], sem)
            for s in range(SUBLANES)]


def _write_rows_pipelined(buf2, hbm, sem2, fill):
    i = pl.program_id(0)
    last = pl.num_programs(0) - 1
    slot = i % 2
    nt = buf2.shape[1]

    def copies(step, s):
        return _rows_out_copies(buf2.at[s], hbm, step * nt, sem2.at[s])

    @pl.when(i >= 2)
    def _():
        for cp in copies(i - 2, slot):
            cp.wait()

    fill(buf2.at[slot])
    for cp in copies(i, slot):
        cp.start()

    @pl.when(i == last)
    def _():
        for cp in copies(i, slot):
            cp.wait()

    @pl.when((i == last) & (i >= 1))
    def _():
        for cp in copies(i - 1, 1 - slot):
            cp.wait()


def _mod_kernel(c_ref, w_ref, b_ref, o_ref):
    a = _silu(c_ref[...]).astype(BF16)
    o_ref[...] = _dot(a, w_ref[...].astype(BF16)) + b_ref[...]


def _modulation(cc, w_mod, b_mod):
    rows, d = cc.shape
    width = w_mod.shape[1]
    tn = next(t for t in (1024, 512, 256, LANES) if width % t == 0)
    return pl.pallas_call(
        _mod_kernel,
        grid=(width // tn,),
        in_specs=[
            pl.BlockSpec((rows, d), lambda j: (0, 0)),
            pl.BlockSpec((d, tn), lambda j: (0, j)),
            pl.BlockSpec((1, tn), lambda j: (0, j)),
        ],
        out_specs=pl.BlockSpec((rows, tn), lambda j: (0, j)),
        out_shape=jax.ShapeDtypeStruct((rows, width), F32),
        compiler_params=_params("arbitrary"),
        name="modulation",
    )(cc, w_mod, b_mod.reshape(1, width))


def _inproj_kernel(x_ref, sh_ref, sc_ref, nw_ref, w_ref, o_ref, h_ref):
    @pl.when(pl.program_id(1) == 0)
    def _():
        y = _rms(x_ref[...], nw_ref[...])
        h_ref[...] = (y * (1.0 + sc_ref[0]) + sh_ref[0]).astype(BF16)

    o_ref[...] = _dot(h_ref[...], w_ref[...]).astype(BF16)


def _in_projection(x2d, mod3, mod_row_of_tile, norm_w, w_in, tm):
    m, d = x2d.shape
    width = w_in.shape[1]
    tn = 1024
    return pl.pallas_call(
        _inproj_kernel,
        grid=(m // tm, width // tn),
        in_specs=[
            pl.BlockSpec((tm, d), lambda i, j: (i, 0)),
            pl.BlockSpec((1, 1, d), lambda i, j: (mod_row_of_tile(i) * N_MOD + 0, 0, 0)),
            pl.BlockSpec((1, 1, d), lambda i, j: (mod_row_of_tile(i) * N_MOD + 1, 0, 0)),
            pl.BlockSpec((1, d), lambda i, j: (0, 0)),
            pl.BlockSpec((d, tn), lambda i, j: (0, j)),
        ],
        out_specs=pl.BlockSpec((tm, tn), lambda i, j: (i, j)),
        out_shape=jax.ShapeDtypeStruct((m, width), BF16),
        scratch_shapes=[pltpu.VMEM((tm, d), BF16)],
        compiler_params=_params("parallel", "arbitrary"),
        name="in_projection",
    )(x2d, mod3, mod3, norm_w.reshape(1, d), w_in)


def _ret_kernel(lg_ref, q_ref, k_ref, v_ref, g_ref, ck_ref, cv_ref, cosr_ref, sinr_ref,
                cosc_ref, sinc_ref, gnw_ref, o_ref, qr_s, kr_s, sfh_s, sbh_s, sf_s, sb_s):
    head = pl.program_id(1)
    lgf = lg_ref[0, head]
    lgb = lg_ref[1, head]
    n = q_ref.shape[0]
    c = RET_CHUNK
    nc = n // c
    ctx_len = ck_ref.shape[0]
    k_scale = RET_HEAD_DIM ** -0.5

    posl = lax.broadcasted_iota(jnp.int32, (ctx_len, 1), 0).astype(F32)
    ck = ck_ref[...].astype(F32) * k_scale
    cv = cv_ref[...]
    sf_s[...] = _dot_tn((ck * jnp.exp(lgf * (ctx_len - 1.0 - posl))).astype(BF16), cv)
    sb_s[...] = _dot_tn((ck * jnp.exp(lgb * posl)).astype(BF16), cv)

    cosc = cosc_ref[...]
    sinc = sinc_ref[...]
    pos = lax.broadcasted_iota(jnp.int32, (c, 1), 0).astype(F32)
    qdec_f = jnp.exp(lgf * (pos + 1.0))
    kdec_f = jnp.exp(lgf * (c - 1.0 - pos))
    cdec_f = jnp.exp(lgf * c)
    qdec_b = jnp.exp(lgb * (c - pos))
    kdec_b = jnp.exp(lgb * pos)
    cdec_b = jnp.exp(lgb * c)

    def rope(x, cosr, sinr):
        xa = x[:, :ROPE_HALF]
        xb = x[:, ROPE_HALF:]
        ya = xa * cosr + pltpu.roll(xa, ROPE_HALF // 2, 1) * sinr
        yb = xb * cosc + pltpu.roll(xb, ROPE_HALF // 2, 1) * sinc
        return jnp.concatenate([ya, yb], axis=1)

    def fwd_chunk(ci, carry):
        r0 = pl.multiple_of(ci * c, c)
        cosr = cosr_ref[pl.ds(r0, c), :]
        sinr = sinr_ref[pl.ds(r0, c), :]
        qr_s[pl.ds(r0, c), :] = rope(q_ref[pl.ds(r0, c), :].astype(F32), cosr, sinr).astype(BF16)
        k = rope(k_ref[pl.ds(r0, c), :].astype(F32), cosr, sinr) * k_scale
        kr_s[pl.ds(r0, c), :] = k.astype(BF16)
        kv = _dot_tn((k * kdec_f).astype(BF16), v_ref[pl.ds(r0, c), :])
        state = sf_s[...]
        sfh_s[ci] = state.astype(BF16)
        sf_s[...] = state * cdec_f + kv
        return carry

    lax.fori_loop(0, nc, fwd_chunk, 0, unroll=2)

    def bwd_chunk(i, carry):
        ci = nc - 1 - i
        r0 = pl.multiple_of(ci * c, c)
        k = kr_s[pl.ds(r0, c), :].astype(F32)
        kv = _dot_tn((k * kdec_b).astype(BF16), v_ref[pl.ds(r0, c), :])
        state = sb_s[...]
        sbh_s[ci] = state.astype(BF16)
        sb_s[...] = state * cdec_b + kv
        return carry

    lax.fori_loop(0, nc, bwd_chunk, 0, unroll=2)

    diff = (lax.broadcasted_iota(jnp.int32, (c, c), 0)
            - lax.broadcasted_iota(jnp.int32, (c, c), 1)).astype(F32)
    intra = (jnp.where(diff >= 0, jnp.exp(lgf * jnp.maximum(diff, 0.0)), 0.0)
             + jnp.where(diff <= 0, jnp.exp(lgb * jnp.maximum(-diff, 0.0)), 0.0))
    gnw = gnw_ref[...]

    def out_chunk(ci, carry):
        r0 = pl.multiple_of(ci * c, c)
        qb = qr_s[pl.ds(r0, c), :]
        kb = kr_s[pl.ds(r0, c), :]
        q = qb.astype(F32)
        scores = _dot_nt(qb, kb) * intra
        o = (_dot(scores.astype(BF16), v_ref[pl.ds(r0, c), :])
             + _dot((q * qdec_f).astype(BF16), sfh_s[ci])
             + _dot((q * qdec_b).astype(BF16), sbh_s[ci]))
        mu = jnp.mean(o, axis=-1, keepdims=True)
        d = o - mu
        var = jnp.mean(d * d, axis=-1, keepdims=True)
        on = d * lax.rsqrt(var + NORM_EPS) * gnw
        gate = _silu(g_ref[pl.ds(r0, c), :].astype(F32))
        o_ref[pl.ds(r0, c), :] = (on * gate).astype(BF16)
        return carry

    lax.fori_loop(0, nc, out_chunk, 0, unroll=2)


def _rope_tables(n):
    inv = ROPE_BASE ** (-jnp.arange(0, ROPE_HALF, 2, dtype=F32) / ROPE_HALF)

    def tables(pos):
        ang = pos.astype(F32)[:, None] * inv[None, :]
        cos = jnp.cos(ang)
        sin = jnp.sin(ang)
        return jnp.concatenate([cos, cos], axis=1), jnp.concatenate([-sin, sin], axis=1)

    t = jnp.arange(n)
    cosr, sinr = tables(t // GRID_W)
    cosc, sinc = tables(jnp.arange(RET_CHUNK) % GRID_W)
    return cosr, sinr, cosc, sinc


def _retention(proj, cproj, lg, gn_w, batch, n, ctx_len):
    hd = RET_HEAD_DIM
    cosr, sinr, cosc, sinc = _rope_tables(n)
    col = lambda which: (lambda b, h, lg_ref: (b, which * RET_HEADS + h))
    const = lambda b, h, lg_ref: (0, 0)
    grid_spec = pltpu.PrefetchScalarGridSpec(
        num_scalar_prefetch=1,
        grid=(batch, RET_HEADS),
        in_specs=[
            pl.BlockSpec((n, hd), col(0)),
            pl.BlockSpec((n, hd), col(1)),
            pl.BlockSpec((n, hd), col(2)),
            pl.BlockSpec((n, hd), col(3)),
            pl.BlockSpec((ctx_len, hd), col(1)),
            pl.BlockSpec((ctx_len, hd), col(2)),
            pl.BlockSpec((n, ROPE_HALF), const),
            pl.BlockSpec((n, ROPE_HALF), const),
            pl.BlockSpec((RET_CHUNK, ROPE_HALF), const),
            pl.BlockSpec((RET_CHUNK, ROPE_HALF), const),
            pl.BlockSpec((1, hd), lambda b, h, lg_ref: (0, h)),
        ],
        out_specs=pl.BlockSpec((n, hd), lambda b, h, lg_ref: (b, h)),
        scratch_shapes=[
            pltpu.VMEM((n, hd), BF16),
            pltpu.VMEM((n, hd), BF16),
            pltpu.VMEM((n // RET_CHUNK, hd, hd), BF16),
            pltpu.VMEM((n // RET_CHUNK, hd, hd), BF16),
            pltpu.VMEM((hd, hd), F32),
            pltpu.VMEM((hd, hd), F32),
        ],
    )
    return pl.pallas_call(
        _ret_kernel,
        grid_spec=grid_spec,
        out_shape=jax.ShapeDtypeStruct((batch * n, RET_WIDTH), BF16),
        compiler_params=_params("parallel", "arbitrary"),
        name="retention",
    )(lg, proj, proj, proj, proj, cproj, cproj, cosr, sinr, cosc, sinc, gn_w.reshape(1, RET_WIDTH))


def _na_kernel(q_ref, k_ref, v_ref, kc_ref, vc_ref, bias_ref, o_ref):
    n = q_ref.shape[0]
    rows = n // GRID_W
    qb = NA_QROWS * GRID_W
    kb = NA_KROWS * GRID_W
    nb = rows // NA_QROWS
    scale = NA_HEAD_DIM ** -0.5
    kc = kc_ref[...]
    vc = jnp.concatenate([vc_ref[...], jnp.ones((kc.shape[0], NA_HEAD_DIM), BF16)], axis=1)
    ones_w = jnp.ones((kb, NA_HEAD_DIM), BF16)

    def block(rb, carry):
        ws = jnp.clip(rb * NA_QROWS - NA_ROWS // 2, 0, rows - NA_KROWS)
        variant = jnp.where(rb == 0, 0, jnp.where(rb == nb - 1, 2, 1))
        q0 = pl.multiple_of(rb * qb, qb)
        k0 = pl.multiple_of(ws * GRID_W, GRID_W)
        q = (q_ref[pl.ds(q0, qb), :].astype(F32) * scale).astype(BF16)
        kw = k_ref[pl.ds(k0, kb), :]
        vw = jnp.concatenate([v_ref[pl.ds(k0, kb), :], ones_w], axis=1)
        s_win = _dot_nt(q, kw) + bias_ref[0, variant]
        s_ctx = _dot_nt(q, kc)
        m = jnp.maximum(jnp.max(s_win, axis=-1, keepdims=True), jnp.max(s_ctx, axis=-1, keepdims=True))
        p_win = jnp.exp(s_win - m).astype(BF16)
        p_ctx = jnp.exp(s_ctx - m).astype(BF16)
        o = _dot(p_win, vw) + _dot(p_ctx, vc)
        o_ref[pl.ds(q0, qb), :] = (o[:, :NA_HEAD_DIM] / o[:, NA_HEAD_DIM:]).astype(BF16)
        return carry

    lax.fori_loop(0, nb, block, 0, unroll=2)


def _na_bias(rpb, rows):
    heads, n_roff, n_coff = rpb.shape
    c = np.arange(GRID_W)
    cs = np.clip(c - NA_COLS // 2, 0, GRID_W - NA_COLS)
    vcol = (c[None, :] >= cs[:, None]) & (c[None, :] < cs[:, None] + NA_COLS)
    period = 2 * GRID_W
    padded = jnp.pad(rpb, ((0, 0), (0, 0), (0, period - n_coff)))
    shifted = jnp.tile(padded, (1, 1, GRID_W))[..., :GRID_W * (period - 1)]
    shifted = shifted.reshape(heads, n_roff, GRID_W, period - 1)
    toeplitz = shifted[..., NA_COLS - 1:NA_COLS - 1 + GRID_W]
    toeplitz = jnp.where(vcol[None, None], toeplitz, NA_MASK)

    def variant(r0):
        ws = min(max(r0 - NA_ROWS // 2, 0), rows - NA_KROWS)
        r = r0 + np.arange(NA_QROWS)
        rs = np.clip(r - NA_ROWS // 2, 0, rows - NA_ROWS)
        kr = ws + np.arange(NA_KROWS)
        vrow = (kr[None, :] >= rs[:, None]) & (kr[None, :] < rs[:, None] + NA_ROWS)
        roff = np.clip(kr[None, :] - r[:, None] + NA_ROWS - 1, 0, n_roff - 1)
        b = toeplitz[:, roff.reshape(-1)].reshape(heads, NA_QROWS, NA_KROWS, GRID_W, GRID_W)
        b = jnp.where(vrow[None, :, :, None, None], b, NA_MASK)
        return b.transpose(0, 1, 3, 2, 4).reshape(heads, NA_QROWS * GRID_W, NA_KROWS * GRID_W)

    return jnp.stack([variant(0), variant(2 * NA_QROWS), variant(rows - NA_QROWS)], axis=1)


def _neighbourhood_attention(proj, cproj, rpb, batch, n, ctx_len):
    hd = NA_HEAD_DIM
    rows = n // GRID_W
    assert rows % NA_QROWS == 0 and rows >= NA_KROWS + NA_QROWS
    bias = _na_bias(rpb.astype(F32), rows)
    base = 4 * RET_WIDTH // hd
    col = lambda which: (lambda b, h: (b, base + which * NA_HEADS + h))
    return pl.pallas_call(
        _na_kernel,
        grid=(batch, NA_HEADS),
        in_specs=[
            pl.BlockSpec((n, hd), col(0)),
            pl.BlockSpec((n, hd), col(1)),
            pl.BlockSpec((n, hd), col(2)),
            pl.BlockSpec((ctx_len, hd), col(1)),
            pl.BlockSpec((ctx_len, hd), col(2)),
            pl.BlockSpec((1, 3, NA_QROWS * GRID_W, NA_KROWS * GRID_W), lambda b, h: (h, 0, 0, 0)),
        ],
        out_specs=pl.BlockSpec((n, hd), lambda b, h: (b, h)),
        out_shape=jax.ShapeDtypeStruct((batch * n, NA_WIDTH), BF16),
        compiler_params=_params("parallel", "arbitrary"),
        name="neighbourhood_attention",
    )(proj, proj, proj, cproj, cproj, bias)


def _outproj_kernel(n_groups, per_group, ret_ref, na_ref, w1_ref, w2_ref, x_ref, ga_ref, shf_ref,
                    scf_ref, nw_ref, wr_ref, br_ref, x1_ref, route_ref, cnt_ref, hf_ref, carry_s,
                    hbuf, hsem):
    @pl.when(pl.program_id(0) == 0)
    def _():
        carry_s[...] = jnp.zeros_like(carry_s)

    acc = _dot(ret_ref[...], w1_ref[...]) + _dot(na_ref[...], w2_ref[...])
    x1 = x_ref[...] + ga_ref[0] * acc
    x1_ref[...] = x1
    hf = _rms(x1, nw_ref[...]) * (1.0 + scf_ref[0]) + shf_ref[0]
    _write_rows_pipelined(hbuf, hf_ref, hsem, lambda buf: _tile_store(buf, hf))

    hf_hi = hf.astype(BF16)
    hf_lo = (hf - hf_hi.astype(F32)).astype(BF16)
    p_hi = _dot(hf_hi, wr_ref[...])
    p_lo = _dot(hf_lo, wr_ref[...])
    logits = p_hi[:, :LANES] + p_hi[:, LANES:] + p_lo[:, :LANES] + br_ref[...]
    tm = logits.shape[0]
    lane = lax.broadcasted_iota(jnp.int32, (tm, LANES), 1)
    neg = -jnp.inf

    def first_max(vals):
        top = jnp.max(vals, axis=-1, keepdims=True)
        idx = jnp.min(jnp.where(vals == top, lane, LANES), axis=-1, keepdims=True)
        return top, idx

    g_logits = jnp.where(lane < n_groups, logits, neg)
    g_top, g_sel = first_max(g_logits)
    g_w = 1.0 / jnp.sum(jnp.exp(g_logits - g_top), axis=-1, keepdims=True)
    lo = n_groups + g_sel * per_group
    e_logits = jnp.where((lane >= lo) & (lane < lo + per_group), logits, neg)
    v0, i0 = first_max(e_logits)
    v1, i1 = first_max(jnp.where(lane == i0, neg, e_logits))
    e1 = jnp.exp(v1 - v0)
    w0 = g_w / (1.0 + e1)
    w1 = g_w * e1 / (1.0 + e1)

    hit0 = lane == i0
    hit1 = lane == i1
    onehot = jnp.where(hit0 | hit1, 1.0, 0.0)
    earlier = jnp.where(lax.broadcasted_iota(jnp.int32, (tm, tm), 0)
                        > lax.broadcasted_iota(jnp.int32, (tm, tm), 1), 1.0, 0.0).astype(BF16)
    before = _dot(earlier, onehot.astype(BF16)) + carry_s[...]
    rank0 = jnp.sum(jnp.where(hit0, before, 0.0), axis=-1, keepdims=True)
    rank1 = jnp.sum(jnp.where(hit1, before, 0.0), axis=-1, keepdims=True)
    carry_s[...] = carry_s[...] + jnp.sum(onehot, axis=0, keepdims=True)
    cnt_ref[...] = carry_s[...]

    fields = {R_ID0: (i0 - n_groups).astype(F32), R_ID1: (i1 - n_groups).astype(F32),
              R_W0: w0, R_W1: w1, R_RANK0: rank0, R_RANK1: rank1}
    route = jnp.zeros((tm, LANES), F32)
    for idx, val in fields.items():
        route = jnp.where(lane == idx, val, route)
    route_ref[...] = route


def _out_projection(ret, na, w_out, x2d, mod3, norm_w, w_route, b_route, n, n_groups, per_group, tm):
    m, d = x2d.shape
    batch_of = lambda i: (i * tm) // n
    mod_spec = lambda which: pl.BlockSpec((1, 1, d), lambda i: (batch_of(i) * N_MOD + which, 0, 0))
    const2 = lambda i: (0, 0)
    return pl.pallas_call(
        functools.partial(_outproj_kernel, n_groups, per_group),
        grid=(m // tm,),
        in_specs=[
            pl.BlockSpec((tm, RET_WIDTH), lambda i: (i, 0)),
            pl.BlockSpec((tm, NA_WIDTH), lambda i: (i, 0)),
            pl.BlockSpec((RET_WIDTH, d), lambda i: (0, 0)),
            pl.BlockSpec((NA_WIDTH, d), lambda i: (RET_WIDTH // NA_WIDTH, 0)),
            pl.BlockSpec((tm, d), lambda i: (i, 0)),
            mod_spec(2), mod_spec(3), mod_spec(4),
            pl.BlockSpec((1, d), const2),
            pl.BlockSpec((d, 2 * LANES), const2),
            pl.BlockSpec((1, LANES), const2),
        ],
        out_specs=[
            pl.BlockSpec((tm, d), lambda i: (i, 0)),
            pl.BlockSpec((tm, LANES), lambda i: (i, 0)),
            pl.BlockSpec((1, LANES), const2),
            pl.BlockSpec(memory_space=pl.ANY),
        ],
        out_shape=[
            jax.ShapeDtypeStruct((m, d), F32),
            jax.ShapeDtypeStruct((m, LANES), F32),
            jax.ShapeDtypeStruct((1, LANES), F32),
            jax.ShapeDtypeStruct((m // SUBLANES, SUBLANES, d // LANES, LANES), F32),
        ],
        scratch_shapes=[
            pltpu.VMEM((1, LANES), F32),
            pltpu.VMEM((2, tm // SUBLANES, d // LANES, SUBLANES, LANES), F32),
            pltpu.SemaphoreType.DMA((2,)),
        ],
        compiler_params=_params("arbitrary"),
        name="out_projection_router",
    )(ret, na, w_out, w_out, x2d, mod3, mod3, mod3, norm_w.reshape(1, d), w_route, b_route)


def _start_row_gather(idx_ref, idx_of, src_hbm, dst_buf, sem):
    def body(jt, carry):
        for js in range(SUBLANES):
            r = idx_ref[idx_of(jt * SUBLANES + js)]
            src = src_hbm.at[lax.shift_right_logical(r, 3), jnp.bitwise_and(r, SUBLANES - 1)]
            pltpu.make_async_copy(src, dst_buf.at[jt, :, js, :], sem).start()
        return carry
    lax.fori_loop(0, dst_buf.shape[0], body, 0)


def _wait_row_gather(dst_buf, sem):
    pltpu.make_async_copy(dst_buf, dst_buf, sem).wait()


def _expert_kernel(tile_expert_ref, n_used_ref, slot_token_ref, hf_ref, wg_ref, wu_ref, wd_ref,
                   ys_ref, xbuf, gsem, obuf, osem):
    del tile_expert_ref
    i = pl.program_id(0)
    n_used = n_used_ref[0]
    tm = xbuf.shape[1] * SUBLANES

    def start_tile(tile, slot):
        _start_row_gather(slot_token_ref, lambda j: tile * tm + j, hf_ref, xbuf.at[slot], gsem.at[slot])

    @pl.when((i == 0) & (n_used > 0))
    def _():
        start_tile(0, 0)

    @pl.when(i + 1 < n_used)
    def _():
        start_tile(i + 1, (i + 1) % 2)

    def fill(buf):
        @pl.when(i < n_used)
        def _():
            slot = i % 2
            _wait_row_gather(xbuf.at[slot], gsem.at[slot])
            x = _tile_load(xbuf.at[slot]).astype(BF16)
            a = (_silu(_dot(x, wg_ref[0])) * _dot(x, wu_ref[0])).astype(BF16)
            _tile_store(buf, _dot(a, wd_ref[0]))

        @pl.when(i >= n_used)
        def _():
            buf[...] = jnp.zeros(buf.shape, F32)

    _write_rows_pipelined(obuf, ys_ref, osem, fill)


def _experts(tile_expert, n_used, slot_token, hf, w_gate, w_up, w_down, n_tiles, tm):
    chunks = hf.shape[2]
    d = chunks * LANES
    ff = w_gate.shape[-1]
    tile_buf = pltpu.VMEM((2, tm // SUBLANES, chunks, SUBLANES, LANES), F32)
    grid_spec = pltpu.PrefetchScalarGridSpec(
        num_scalar_prefetch=3,
        grid=(n_tiles,),
        in_specs=[
            pl.BlockSpec(memory_space=pl.ANY),
            pl.BlockSpec((1, d, ff), lambda i, te, nu, st: (te[i], 0, 0)),
            pl.BlockSpec((1, d, ff), lambda i, te, nu, st: (te[i], 0, 0)),
            pl.BlockSpec((1, ff, d), lambda i, te, nu, st: (te[i], 0, 0)),
        ],
        out_specs=pl.BlockSpec(memory_space=pl.ANY),
        scratch_shapes=[tile_buf, pltpu.SemaphoreType.DMA((2,)), tile_buf, pltpu.SemaphoreType.DMA((2,))],
    )
    return pl.pallas_call(
        _expert_kernel,
        grid_spec=grid_spec,
        out_shape=jax.ShapeDtypeStruct((n_tiles * tm // SUBLANES, SUBLANES, chunks, LANES), F32),
        compiler_params=_params("arbitrary"),
        name="routed_experts",
    )(tile_expert, n_used, slot_token, hf, w_gate, w_up, w_down)


def _combine_kernel(pos_ref, ys_ref, x1_ref, route_ref, gf_ref, fw_ref, o_ref, ybuf, sem):
    i = pl.program_id(0)
    n_tiles = pl.num_programs(0)
    tm = x1_ref.shape[0]

    def start_tile(tile, slot):
        for choice in range(2):
            _start_row_gather(pos_ref, lambda j: (tile * tm + j) * 2 + choice, ys_ref,
                              ybuf.at[slot, choice], sem.at[slot, choice])

    @pl.when(i == 0)
    def _():
        start_tile(0, 0)

    @pl.when(i + 1 < n_tiles)
    def _():
        start_tile(i + 1, (i + 1) % 2)

    slot = i % 2
    for choice in range(2):
        _wait_row_gather(ybuf.at[slot, choice], sem.at[slot, choice])
    route = route_ref[...]
    moe = (route[:, R_W0:R_W0 + 1] * _tile_load(ybuf.at[slot, 0])
           + route[:, R_W1:R_W1 + 1] * _tile_load(ybuf.at[slot, 1]))
    x2 = x1_ref[...] + gf_ref[0] * moe
    o_ref[...] = _rms(x2, fw_ref[...])


def _combine(pos, ys, x1, route, mod3, final_w, n, tm):
    m, d = x1.shape
    grid_spec = pltpu.PrefetchScalarGridSpec(
        num_scalar_prefetch=1,
        grid=(m // tm,),
        in_specs=[
            pl.BlockSpec(memory_space=pl.ANY),
            pl.BlockSpec((tm, d), lambda i, pos_ref: (i, 0)),
            pl.BlockSpec((tm, LANES), lambda i, pos_ref: (i, 0)),
            pl.BlockSpec((1, 1, d), lambda i, pos_ref: (((i * tm) // n) * N_MOD + 5, 0, 0)),
            pl.BlockSpec((1, d), lambda i, pos_ref: (0, 0)),
        ],
        out_specs=pl.BlockSpec((tm, d), lambda i, pos_ref: (i, 0)),
        scratch_shapes=[pltpu.VMEM((2, 2, tm // SUBLANES, d // LANES, SUBLANES, LANES), F32),
                        pltpu.SemaphoreType.DMA((2, 2))],
    )
    return pl.pallas_call(
        _combine_kernel,
        grid_spec=grid_spec,
        out_shape=jax.ShapeDtypeStruct((m, d), F32),
        compiler_params=_params("arbitrary"),
        name="combine_final_norm",
    )(pos, ys, x1, route, mod3, final_w.reshape(1, d))


def kernel(x, c, ctx, c_ctx, w_mod, b_mod, norm_mix_w, w_in, ret_decay_f, ret_decay_b, ret_gn_w, na_rpb, w_out, norm_ffn_w, w_router_group, b_router_group, w_router_expert, b_router_expert, w_gate, w_up, w_down, final_norm_w):
    assert w_mod.shape[0] == 1, "single trunk layer"
    batch, n, d = x.shape
    ctx_len = ctx.shape[1]
    n_groups = w_router_group.shape[-1]
    per_group = w_router_expert.shape[-1]
    n_experts = w_gate.shape[1]
    assert n_groups * per_group == n_experts and n_groups + n_experts <= LANES

    mod_rows = 8
    cc = jnp.zeros((mod_rows, d), F32).at[:batch].set(c).at[batch].set(c_ctx)
    mod = _modulation(cc, w_mod[0], b_mod[0])
    mod3 = mod.reshape(mod_rows * N_MOD, 1, d)

    w_in_b = w_in[0].astype(BF16)
    tm = min(512, n)
    x2d = x.reshape(batch * n, d)
    proj = _in_projection(x2d, mod3, lambda i: (i * tm) // n, norm_mix_w[0], w_in_b, tm)
    cproj = _in_projection(ctx.reshape(batch * ctx_len, d), mod3, lambda i: batch, norm_mix_w[0],
                           w_in_b, ctx_len)

    lg = jnp.stack([jax.nn.log_sigmoid(ret_decay_f[0].astype(F32)),
                    jax.nn.log_sigmoid(ret_decay_b[0].astype(F32))])
    ret = _retention(proj, cproj, lg, ret_gn_w[0], batch, n, ctx_len)
    na = _neighbourhood_attention(proj, cproj, na_rpb[0], batch, n, ctx_len)

    w_route = jnp.concatenate(
        [w_router_group[0], jnp.moveaxis(w_router_expert[0], 0, 1).reshape(d, n_experts)], axis=1)
    w_route = jnp.pad(w_route.astype(F32), ((0, 0), (0, LANES - n_groups - n_experts)))
    w_route_hi = w_route.astype(BF16)
    w_route = jnp.concatenate([w_route_hi, (w_route - w_route_hi.astype(F32)).astype(BF16)], axis=1)
    b_route = jnp.concatenate([b_router_group[0], b_router_expert[0].reshape(-1)])
    b_route = jnp.pad(b_route.astype(F32), (0, LANES - n_groups - n_experts)).reshape(1, LANES)
    x1, route, counts, hf = _out_projection(ret, na, w_out[0].astype(BF16), x2d, mod3, norm_ffn_w[0],
                                            w_route, b_route, n, n_groups, per_group, tm)

    te = EXPERT_TILE
    tokens = batch * n
    n_tiles = (2 * tokens) // te + n_experts
    cnt = counts[0, n_groups:n_groups + n_experts].astype(jnp.int32)
    padded = ((cnt + te - 1) // te) * te
    ends = jnp.cumsum(padded)
    offs = ends - padded
    ids = route[:, R_ID0:R_ID1 + 1].astype(jnp.int32)
    ranks = route[:, R_RANK0:R_RANK1 + 1].astype(jnp.int32)
    pos = (offs[ids] + ranks).reshape(-1)
    n_used = (ends[-1] // te).astype(jnp.int32).reshape(1)
    tile_start = jnp.arange(n_tiles, dtype=jnp.int32) * te
    tile_expert = jnp.minimum(
        jnp.sum((ends[None, :] <= tile_start[:, None]).astype(jnp.int32), axis=1), n_experts - 1)
    slot_token = jnp.zeros((n_tiles * te,), jnp.int32).at[pos].set(
        jnp.repeat(jnp.arange(tokens, dtype=jnp.int32), 2))

    ys = _experts(tile_expert, n_used, slot_token, hf, w_gate[0].astype(BF16), w_up[0].astype(BF16),
                  w_down[0].astype(BF16), n_tiles, te)
    out = _combine(pos, ys, x1, route, mod3, final_norm_w, n, min(256, n))
    return out.reshape(batch, n, d)
```

```python
import functools

import jax
import jax.numpy as jnp
import numpy as np
from jax import lax
from jax.experimental import pallas as pl
from jax.experimental.pallas import tpu as pltpu

F32 = jnp.float32
BF16 = jnp.bfloat16

GRID_W = 64
RET_HEADS = 4
RET_HEAD_DIM = 256
RET_WIDTH = RET_HEADS * RET_HEAD_DIM
NA_HEADS = 8
NA_HEAD_DIM = 128
NA_WIDTH = NA_HEADS * NA_HEAD_DIM
RET_CHUNK = 128
NA_ROWS = 8
NA_COLS = 16
ROPE_BASE = 10000.0
N_MOD = 6
NORM_EPS = 1e-6
ROPE_HALF = RET_HEAD_DIM // 2

NA_QROWS = 4
NA_MASK = -1e30

LANES = 128
SUBLANES = 8
VMEM_LIMIT_BYTES = 56 * 1024 * 1024

R_KEY0, R_KEY1, R_W0, R_W1 = 0, 1, 2, 3
KEY_RANK_BITS = 16
KEY_RANK_SPAN = float(1 << KEY_RANK_BITS)

EXPERT_TILE = 256


def _params(*sem):
    return pltpu.CompilerParams(dimension_semantics=sem, vmem_limit_bytes=VMEM_LIMIT_BYTES)


def _dot(a, b):
    return jnp.dot(a, b, preferred_element_type=F32)


def _dot_nt(a, b):
    return lax.dot_general(a, b, (((1,), (1,)), ((), ())), preferred_element_type=F32)


def _dot_tn(a, b):
    return lax.dot_general(a, b, (((0,), (0,)), ((), ())), preferred_element_type=F32)


def _rms(x, w):
    return x * lax.rsqrt(jnp.mean(x * x, axis=-1, keepdims=True) + NORM_EPS) * w


def _silu(x):
    return x * jax.nn.sigmoid(x)


def _store_rows(ref, val):
    for c in range(ref.shape[1]):
        ref[:, c, :] = val[:, c * LANES:(c + 1) * LANES]


ROW_PITCH = 17


def _start_row_gather(idx_ref, idx_of, src_hbm, dst_buf, sem, n_rows):
    chunks = src_hbm.shape[1]

    def body(j, carry):
        r = idx_ref[idx_of(j)]
        pltpu.make_async_copy(src_hbm.at[r], dst_buf.at[pl.ds(j * ROW_PITCH, chunks), :], sem).start()
        return carry

    lax.fori_loop(0, n_rows, body, 0, unroll=SUBLANES)


def _wait_row_gather(dst_buf, sem, n_rows, chunks):
    view = dst_buf.at[pl.ds(0, n_rows * chunks), :]
    pltpu.make_async_copy(view, view, sem).wait()


def _load_gathered(buf, n_rows, chunks):
    return jnp.concatenate([buf[pl.ds(c, n_rows, stride=ROW_PITCH), :] for c in range(chunks)], axis=1)


def _mod_kernel(c_ref, w_ref, b_ref, o_ref):
    a = _silu(c_ref[...]).astype(BF16)
    o_ref[...] = _dot(a, w_ref[...].astype(BF16)) + b_ref[...]


def _modulation(cc, w_mod, b_mod):
    rows, d = cc.shape
    width = w_mod.shape[1]
    tn = next(t for t in (1024, 512, 256, LANES) if width % t == 0)
    return pl.pallas_call(
        _mod_kernel,
        grid=(width // tn,),
        in_specs=[
            pl.BlockSpec((rows, d), lambda j: (0, 0)),
            pl.BlockSpec((d, tn), lambda j: (0, j)),
            pl.BlockSpec((1, tn), lambda j: (0, j)),
        ],
        out_specs=pl.BlockSpec((rows, tn), lambda j: (0, j)),
        out_shape=jax.ShapeDtypeStruct((rows, width), F32),
        compiler_params=_params("arbitrary"),
        name="modulation",
    )(cc, w_mod, b_mod.reshape(1, width))


def _inproj_kernel(x_ref, sh_ref, sc_ref, nw_ref, w_ref, o_ref, h_ref):
    @pl.when(pl.program_id(1) == 0)
    def _():
        y = _rms(x_ref[...], nw_ref[...])
        h_ref[...] = (y * (1.0 + sc_ref[0]) + sh_ref[0]).astype(BF16)

    o_ref[...] = _dot(h_ref[...], w_ref[...]).astype(BF16)


def _in_projection(x2d, mod3, mod_row_of_tile, norm_w, w_in, tm):
    m, d = x2d.shape
    width = w_in.shape[1]
    tn = 1024
    return pl.pallas_call(
        _inproj_kernel,
        grid=(m // tm, width // tn),
        in_specs=[
            pl.BlockSpec((tm, d), lambda i, j: (i, 0)),
            pl.BlockSpec((1, 1, d), lambda i, j: (mod_row_of_tile(i) * N_MOD + 0, 0, 0)),
            pl.BlockSpec((1, 1, d), lambda i, j: (mod_row_of_tile(i) * N_MOD + 1, 0, 0)),
            pl.BlockSpec((1, d), lambda i, j: (0, 0)),
            pl.BlockSpec((d, tn), lambda i, j: (0, j)),
        ],
        out_specs=pl.BlockSpec((tm, tn), lambda i, j: (i, j)),
        out_shape=jax.ShapeDtypeStruct((m, width), BF16),
        scratch_shapes=[pltpu.VMEM((tm, d), BF16)],
        compiler_params=_params("parallel", "arbitrary"),
        name="in_projection",
    )(x2d, mod3, mod3, norm_w.reshape(1, d), w_in)


def _ret_kernel(lg_ref, q_ref, k_ref, v_ref, g_ref, ck_ref, cv_ref, cosr_ref, sinr_ref,
                cosc_ref, sinc_ref, gnw_ref, o_ref, qr_s, kr_s, sfh_s, sbh_s, sf_s, sb_s):
    head = pl.program_id(1)
    lgf = lg_ref[0, head]
    lgb = lg_ref[1, head]
    n = q_ref.shape[0]
    c = RET_CHUNK
    nc = n // c
    ctx_len = ck_ref.shape[0]
    k_scale = RET_HEAD_DIM ** -0.5

    posl = lax.broadcasted_iota(jnp.int32, (ctx_len, 1), 0).astype(F32)
    ck = ck_ref[...].astype(F32) * k_scale
    cv = cv_ref[...]
    sf_s[...] = _dot_tn((ck * jnp.exp(lgf * (ctx_len - 1.0 - posl))).astype(BF16), cv)
    sb_s[...] = _dot_tn((ck * jnp.exp(lgb * posl)).astype(BF16), cv)

    cosc = cosc_ref[...]
    sinc = sinc_ref[...]
    pos = lax.broadcasted_iota(jnp.int32, (c, 1), 0).astype(F32)
    qdec_f = jnp.exp(lgf * (pos + 1.0))
    kdec_f = jnp.exp(lgf * (c - 1.0 - pos))
    cdec_f = jnp.exp(lgf * c)
    qdec_b = jnp.exp(lgb * (c - pos))
    kdec_b = jnp.exp(lgb * pos)
    cdec_b = jnp.exp(lgb * c)

    def rope(x, cosr, sinr):
        xa = x[:, :ROPE_HALF]
        xb = x[:, ROPE_HALF:]
        ya = xa * cosr + pltpu.roll(xa, ROPE_HALF // 2, 1) * sinr
        yb = xb * cosc + pltpu.roll(xb, ROPE_HALF // 2, 1) * sinc
        return jnp.concatenate([ya, yb], axis=1)

    def fwd_chunk(ci, carry):
        r0 = pl.multiple_of(ci * c, c)
        cosr = cosr_ref[pl.ds(r0, c), :]
        sinr = sinr_ref[pl.ds(r0, c), :]
        qr_s[pl.ds(r0, c), :] = rope(q_ref[pl.ds(r0, c), :].astype(F32), cosr, sinr).astype(BF16)
        k = rope(k_ref[pl.ds(r0, c), :].astype(F32), cosr, sinr) * k_scale
        kr_s[pl.ds(r0, c), :] = k.astype(BF16)
        kv = _dot_tn((k * kdec_f).astype(BF16), v_ref[pl.ds(r0, c), :])
        state = sf_s[...]
        sfh_s[ci] = state.astype(BF16)
        sf_s[...] = state * cdec_f + kv
        return carry

    lax.fori_loop(0, nc, fwd_chunk, 0, unroll=2)

    def bwd_chunk(i, carry):
        ci = nc - 1 - i
        r0 = pl.multiple_of(ci * c, c)
        k = kr_s[pl.ds(r0, c), :].astype(F32)
        kv = _dot_tn((k * kdec_b).astype(BF16), v_ref[pl.ds(r0, c), :])
        state = sb_s[...]
        sbh_s[ci] = state.astype(BF16)
        sb_s[...] = state * cdec_b + kv
        return carry

    lax.fori_loop(0, nc, bwd_chunk, 0, unroll=2)

    diff = (lax.broadcasted_iota(jnp.int32, (c, c), 0)
            - lax.broadcasted_iota(jnp.int32, (c, c), 1)).astype(F32)
    intra = (jnp.where(diff >= 0, jnp.exp(lgf * jnp.maximum(diff, 0.0)), 0.0)
             + jnp.where(diff <= 0, jnp.exp(lgb * jnp.maximum(-diff, 0.0)), 0.0))
    gnw = gnw_ref[...]

    def out_chunk(ci, carry):
        r0 = pl.multiple_of(ci * c, c)
        qb = qr_s[pl.ds(r0, c), :]
        kb = kr_s[pl.ds(r0, c), :]
        q = qb.astype(F32)
        scores = _dot_nt(qb, kb) * intra
        o = (_dot(scores.astype(BF16), v_ref[pl.ds(r0, c), :])
             + _dot((q * qdec_f).astype(BF16), sfh_s[ci])
             + _dot((q * qdec_b).astype(BF16), sbh_s[ci]))
        mu = jnp.mean(o, axis=-1, keepdims=True)
        d = o - mu
        var = jnp.mean(d * d, axis=-1, keepdims=True)
        on = d * lax.rsqrt(var + NORM_EPS) * gnw
        gate = _silu(g_ref[pl.ds(r0, c), :].astype(F32))
        o_ref[pl.ds(r0, c), :] = (on * gate).astype(BF16)
        return carry

    lax.fori_loop(0, nc, out_chunk, 0, unroll=2)


def _rope_tables(n):
    inv = ROPE_BASE ** (-jnp.arange(0, ROPE_HALF, 2, dtype=F32) / ROPE_HALF)

    def tables(pos):
        ang = pos.astype(F32)[:, None] * inv[None, :]
        cos = jnp.cos(ang)
        sin = jnp.sin(ang)
        return jnp.concatenate([cos, cos], axis=1), jnp.concatenate([-sin, sin], axis=1)

    t = jnp.arange(n)
    cosr, sinr = tables(t // GRID_W)
    cosc, sinc = tables(jnp.arange(RET_CHUNK) % GRID_W)
    return cosr, sinr, cosc, sinc


def _retention(proj, cproj, lg, gn_w, batch, n, ctx_len):
    hd = RET_HEAD_DIM
    cosr, sinr, cosc, sinc = _rope_tables(n)
    col = lambda which: (lambda b, h, lg_ref: (b, which * RET_HEADS + h))
    const = lambda b, h, lg_ref: (0, 0)
    grid_spec = pltpu.PrefetchScalarGridSpec(
        num_scalar_prefetch=1,
        grid=(batch, RET_HEADS),
        in_specs=[
            pl.BlockSpec((n, hd), col(0)),
            pl.BlockSpec((n, hd), col(1)),
            pl.BlockSpec((n, hd), col(2)),
            pl.BlockSpec((n, hd), col(3)),
            pl.BlockSpec((ctx_len, hd), col(1)),
            pl.BlockSpec((ctx_len, hd), col(2)),
            pl.BlockSpec((n, ROPE_HALF), const),
            pl.BlockSpec((n, ROPE_HALF), const),
            pl.BlockSpec((RET_CHUNK, ROPE_HALF), const),
            pl.BlockSpec((RET_CHUNK, ROPE_HALF), const),
            pl.BlockSpec((1, hd), lambda b, h, lg_ref: (0, h)),
        ],
        out_specs=pl.BlockSpec((n, hd), lambda b, h, lg_ref: (b, h)),
        scratch_shapes=[
            pltpu.VMEM((n, hd), BF16),
            pltpu.VMEM((n, hd), BF16),
            pltpu.VMEM((n // RET_CHUNK, hd, hd), BF16),
            pltpu.VMEM((n // RET_CHUNK, hd, hd), BF16),
            pltpu.VMEM((hd, hd), F32),
            pltpu.VMEM((hd, hd), F32),
        ],
    )
    return pl.pallas_call(
        _ret_kernel,
        grid_spec=grid_spec,
        out_shape=jax.ShapeDtypeStruct((batch * n, RET_WIDTH), BF16),
        compiler_params=_params("parallel", "arbitrary"),
        name="retention",
    )(lg, proj, proj, proj, proj, cproj, cproj, cosr, sinr, cosc, sinc, gn_w.reshape(1, RET_WIDTH))


def _na_kernel(q_ref, k_ref, v_ref, kc_ref, vc_ref, bias_ref, o_ref):
    n = q_ref.shape[0]
    rows = n // GRID_W
    kb = NA_ROWS * GRID_W
    scale = NA_HEAD_DIM ** -0.5
    kc = kc_ref[...]
    vc = jnp.concatenate([vc_ref[...], jnp.ones((kc.shape[0], NA_HEAD_DIM), BF16)], axis=1)
    ones_w = jnp.ones((kb, NA_HEAD_DIM), BF16)

    def grid_row(r):
        rs = jnp.clip(r - NA_ROWS // 2, 0, rows - NA_ROWS)
        q0 = pl.multiple_of(r * GRID_W, GRID_W)
        k0 = pl.multiple_of(rs * GRID_W, GRID_W)
        q = (q_ref[pl.ds(q0, GRID_W), :].astype(F32) * scale).astype(BF16)
        kw = k_ref[pl.ds(k0, kb), :]
        vw = jnp.concatenate([v_ref[pl.ds(k0, kb), :], ones_w], axis=1)
        s_win = _dot_nt(q, kw) + bias_ref[0, rs - r + NA_ROWS - 1]
        s_ctx = _dot_nt(q, kc)
        m = jnp.maximum(jnp.max(s_win, axis=-1, keepdims=True), jnp.max(s_ctx, axis=-1, keepdims=True))
        p_win = jnp.exp(s_win - m).astype(BF16)
        p_ctx = jnp.exp(s_ctx - m).astype(BF16)
        o = _dot(p_win, vw) + _dot(p_ctx, vc)
        o_ref[pl.ds(q0, GRID_W), :] = (o[:, :NA_HEAD_DIM] / o[:, NA_HEAD_DIM:]).astype(BF16)

    def group(g, carry):
        for u in range(NA_QROWS):
            grid_row(g * NA_QROWS + u)
        return carry

    lax.fori_loop(0, rows // NA_QROWS, group, 0)


def _na_bias(rpb):
    heads, n_roff, n_coff = rpb.shape
    n_off = n_roff - NA_ROWS + 1
    c = np.arange(GRID_W)
    cs = np.clip(c - NA_COLS // 2, 0, GRID_W - NA_COLS)
    vcol = (c[None, :] >= cs[:, None]) & (c[None, :] < cs[:, None] + NA_COLS)
    seg = 2 * GRID_W
    padded = jnp.pad(rpb, ((0, 0), (0, 0), (0, seg - n_coff)))
    vec = jnp.stack([padded[:, off:off + NA_ROWS] for off in range(n_off)], axis=1)
    period = NA_ROWS * seg
    vec = vec.reshape(heads, n_off, period)
    cut = jnp.tile(vec, (1, 1, GRID_W))[..., :GRID_W * (period - 1)]
    cut = jnp.pad(cut.reshape(heads, n_off, GRID_W, period - 1), ((0, 0), (0, 0), (0, 0), (0, 1)))
    bias = cut.reshape(heads, n_off, GRID_W, NA_ROWS, seg)[..., NA_COLS - 1:NA_COLS - 1 + GRID_W]
    bias = jnp.where(vcol[None, None, :, None, :], bias, NA_MASK)
    return bias.reshape(heads, n_off, GRID_W, NA_ROWS * GRID_W)


def _neighbourhood_attention(proj, cproj, rpb, batch, n, ctx_len):
    hd = NA_HEAD_DIM
    rows = n // GRID_W
    assert rows % NA_QROWS == 0 and rows >= NA_ROWS
    bias = _na_bias(rpb.astype(F32))
    base = 4 * RET_WIDTH // hd
    col = lambda which: (lambda b, h: (b, base + which * NA_HEADS + h))
    return pl.pallas_call(
        _na_kernel,
        grid=(batch, NA_HEADS),
        in_specs=[
            pl.BlockSpec((n, hd), col(0)),
            pl.BlockSpec((n, hd), col(1)),
            pl.BlockSpec((n, hd), col(2)),
            pl.BlockSpec((ctx_len, hd), col(1)),
            pl.BlockSpec((ctx_len, hd), col(2)),
            pl.BlockSpec((1,) + bias.shape[1:], lambda b, h: (h, 0, 0, 0)),
        ],
        out_specs=pl.BlockSpec((n, hd), lambda b, h: (b, h)),
        out_shape=jax.ShapeDtypeStruct((batch * n, NA_WIDTH), BF16),
        compiler_params=_params("parallel", "arbitrary"),
        name="neighbourhood_attention",
    )(proj, proj, proj, cproj, cproj, bias)


def _outproj_kernel(n_groups, per_group, ret_ref, na_ref, w1_ref, w2_ref, x_ref, ga_ref, shf_ref,
                    scf_ref, nw_ref, wr_ref, br_ref, x1_ref, hf_ref, route_ref, cnt_ref, carry_s):
    @pl.when(pl.program_id(0) == 0)
    def _():
        carry_s[...] = jnp.zeros_like(carry_s)

    acc = _dot(ret_ref[...], w1_ref[...]) + _dot(na_ref[...], w2_ref[...])
    x1 = x_ref[...] + ga_ref[0] * acc
    x1_ref[...] = x1
    hf = _rms(x1, nw_ref[...]) * (1.0 + scf_ref[0]) + shf_ref[0]
    _store_rows(hf_ref, hf)

    hf_hi = hf.astype(BF16)
    hf_lo = (hf - hf_hi.astype(F32)).astype(BF16)
    p_hi = _dot(hf_hi, wr_ref[...])
    p_lo = _dot(hf_lo, wr_ref[...])
    logits = p_hi[:, :LANES] + p_hi[:, LANES:] + p_lo[:, :LANES] + br_ref[...]
    tm = logits.shape[0]
    lane = lax.broadcasted_iota(jnp.int32, (tm, LANES), 1)
    neg = -jnp.inf

    def first_max(vals):
        top = jnp.max(vals, axis=-1, keepdims=True)
        idx = jnp.min(jnp.where(vals == top, lane, LANES), axis=-1, keepdims=True)
        return top, idx

    g_logits = jnp.where(lane < n_groups, logits, neg)
    g_top, g_sel = first_max(g_logits)
    g_w = 1.0 / jnp.sum(jnp.exp(g_logits - g_top), axis=-1, keepdims=True)
    lo = n_groups + g_sel * per_group
    e_logits = jnp.where((lane >= lo) & (lane < lo + per_group), logits, neg)
    v0, i0 = first_max(e_logits)
    v1, i1 = first_max(jnp.where(lane == i0, neg, e_logits))
    e1 = jnp.exp(v1 - v0)
    w0 = g_w / (1.0 + e1)
    w1 = g_w * e1 / (1.0 + e1)

    hit0 = lane == i0
    hit1 = lane == i1
    onehot = jnp.where(hit0 | hit1, 1.0, 0.0)
    earlier = jnp.where(lax.broadcasted_iota(jnp.int32, (tm, tm), 0)
                        > lax.broadcasted_iota(jnp.int32, (tm, tm), 1), 1.0, 0.0).astype(BF16)
    before = _dot(earlier, onehot.astype(BF16)) + carry_s[...]
    rank0 = jnp.sum(jnp.where(hit0, before, 0.0), axis=-1, keepdims=True)
    rank1 = jnp.sum(jnp.where(hit1, before, 0.0), axis=-1, keepdims=True)
    carry_s[...] = carry_s[...] + jnp.sum(onehot, axis=0, keepdims=True)
    cnt_ref[...] = carry_s[...]

    key0 = (i0 - n_groups).astype(F32) * KEY_RANK_SPAN + rank0
    key1 = (i1 - n_groups).astype(F32) * KEY_RANK_SPAN + rank1
    fields = {R_KEY0: key0, R_KEY1: key1, R_W0: w0, R_W1: w1}
    route = jnp.zeros((tm, LANES), F32)
    for idx, val in fields.items():
        route = jnp.where(lane == idx, val, route)
    route_ref[...] = route


def _out_projection(ret, na, w_out, x2d, mod3, norm_w, w_route, b_route, n, n_groups, per_group, tm):
    m, d = x2d.shape
    batch_of = lambda i: (i * tm) // n
    mod_spec = lambda which: pl.BlockSpec((1, 1, d), lambda i: (batch_of(i) * N_MOD + which, 0, 0))
    const2 = lambda i: (0, 0)
    return pl.pallas_call(
        functools.partial(_outproj_kernel, n_groups, per_group),
        grid=(m // tm,),
        in_specs=[
            pl.BlockSpec((tm, RET_WIDTH), lambda i: (i, 0)),
            pl.BlockSpec((tm, NA_WIDTH), lambda i: (i, 0)),
            pl.BlockSpec((RET_WIDTH, d), lambda i: (0, 0)),
            pl.BlockSpec((NA_WIDTH, d), lambda i: (RET_WIDTH // NA_WIDTH, 0)),
            pl.BlockSpec((tm, d), lambda i: (i, 0)),
            mod_spec(2), mod_spec(3), mod_spec(4),
            pl.BlockSpec((1, d), const2),
            pl.BlockSpec((d, 2 * LANES), const2),
            pl.BlockSpec((1, LANES), const2),
        ],
        out_specs=[
            pl.BlockSpec((tm, d), lambda i: (i, 0)),
            pl.BlockSpec((tm, d // LANES, LANES), lambda i: (i, 0, 0)),
            pl.BlockSpec((tm, LANES), lambda i: (i, 0)),
            pl.BlockSpec((1, LANES), const2),
        ],
        out_shape=[
            jax.ShapeDtypeStruct((m, d), F32),
            jax.ShapeDtypeStruct((m, d // LANES, LANES), F32),
            jax.ShapeDtypeStruct((m, LANES), F32),
            jax.ShapeDtypeStruct((1, LANES), F32),
        ],
        scratch_shapes=[pltpu.VMEM((1, LANES), F32)],
        compiler_params=_params("arbitrary"),
        name="out_projection_router",
    )(ret, na, w_out, w_out, x2d, mod3, mod3, mod3, norm_w.reshape(1, d), w_route, b_route)


def _slot_map_kernel(tile_rows, key_ref, cnt_ref, pos_ref, slot_ref, tile_expert_ref, n_used_ref,
                     offs_s):
    n_experts = cnt_ref.shape[0]
    shift = tile_rows.bit_length() - 1
    assert 1 << shift == tile_rows

    def fill(ref, lo, hi, val):
        def body(s, carry):
            ref[s] = val
            return carry
        lax.fori_loop(lo, hi, body, 0)

    def per_expert(e, start):
        cnt = cnt_ref[e]
        size = lax.shift_left(lax.shift_right_logical(cnt + (tile_rows - 1), shift), shift)
        offs_s[e] = start
        fill(tile_expert_ref, lax.shift_right_logical(start, shift),
             lax.shift_right_logical(start + size, shift), e)
        fill(slot_ref, start + cnt, start + size, 0)
        return start + size

    end = lax.fori_loop(0, n_experts, per_expert, 0)
    n_used_ref[0] = lax.shift_right_logical(end, shift)
    fill(tile_expert_ref, lax.shift_right_logical(end, shift), tile_expert_ref.shape[0], n_experts - 1)
    fill(slot_ref, end, slot_ref.shape[0], 0)

    def assign(a, carry):
        key = key_ref[a]
        p = offs_s[lax.shift_right_logical(key, KEY_RANK_BITS)] + jnp.bitwise_and(key, (1 << KEY_RANK_BITS) - 1)
        pos_ref[a] = p
        slot_ref[p] = lax.shift_right_logical(a, 1)
        return carry

    lax.fori_loop(0, key_ref.shape[0], assign, 0, unroll=SUBLANES)


def _slot_map(keys, cnt, n_tiles, tile_rows):
    smem = pl.BlockSpec(memory_space=pltpu.SMEM)
    i32 = lambda *shape: jax.ShapeDtypeStruct(shape, jnp.int32)
    return pl.pallas_call(
        functools.partial(_slot_map_kernel, tile_rows),
        in_specs=[smem, smem],
        out_specs=[smem, smem, smem, smem],
        out_shape=[i32(keys.shape[0]), i32(n_tiles * tile_rows), i32(n_tiles), i32(1)],
        scratch_shapes=[pltpu.SMEM((cnt.shape[0],), jnp.int32)],
        name="slot_map",
    )(keys, cnt)


def _expert_kernel(tile_expert_ref, n_used_ref, slot_token_ref, hf_ref, wg_ref, wu_ref, wd_ref,
                   o_ref, xbuf, gsem, wg_s, wu_s, wd_s):
    i = pl.program_id(0)
    n_used = n_used_ref[0]
    tm, chunks, _ = o_ref.shape

    def start_tile(tile, slot):
        _start_row_gather(slot_token_ref, lambda j: tile * tm + j, hf_ref, xbuf.at[slot],
                          gsem.at[slot], tm)

    @pl.when((i == 0) & (n_used > 0))
    def _():
        start_tile(0, 0)

    @pl.when(i + 1 < n_used)
    def _():
        start_tile(i + 1, (i + 1) % 2)

    @pl.when(i < n_used)
    def _():
        @pl.when((i == 0) | (tile_expert_ref[i] != tile_expert_ref[jnp.maximum(i - 1, 0)]))
        def _():
            wg_s[...] = wg_ref[0].astype(BF16)
            wu_s[...] = wu_ref[0].astype(BF16)
            wd_s[...] = wd_ref[0].astype(BF16)

        slot = i % 2
        _wait_row_gather(xbuf.at[slot], gsem.at[slot], tm, chunks)
        x = _load_gathered(xbuf.at[slot], tm, chunks).astype(BF16)
        a = (_silu(_dot(x, wg_s[...])) * _dot(x, wu_s[...])).astype(BF16)
        _store_rows(o_ref, _dot(a, wd_s[...]))

    @pl.when(i >= n_used)
    def _():
        o_ref[...] = jnp.zeros(o_ref.shape, F32)


def _experts(tile_expert, n_used, slot_token, hf, w_gate, w_up, w_down, n_tiles, tm):
    chunks = hf.shape[1]
    d = chunks * LANES
    ff = w_gate.shape[-1]
    grid_spec = pltpu.PrefetchScalarGridSpec(
        num_scalar_prefetch=3,
        grid=(n_tiles,),
        in_specs=[
            pl.BlockSpec(memory_space=pl.ANY),
            pl.BlockSpec((1, d, ff), lambda i, te, nu, st: (te[i], 0, 0)),
            pl.BlockSpec((1, d, ff), lambda i, te, nu, st: (te[i], 0, 0)),
            pl.BlockSpec((1, ff, d), lambda i, te, nu, st: (te[i], 0, 0)),
        ],
        out_specs=pl.BlockSpec((tm, chunks, LANES), lambda i, te, nu, st: (i, 0, 0)),
        scratch_shapes=[
            pltpu.VMEM((2, tm * ROW_PITCH, LANES), F32),
            pltpu.SemaphoreType.DMA((2,)),
            pltpu.VMEM((d, ff), BF16),
            pltpu.VMEM((d, ff), BF16),
            pltpu.VMEM((ff, d), BF16),
        ],
    )
    return pl.pallas_call(
        _expert_kernel,
        grid_spec=grid_spec,
        out_shape=jax.ShapeDtypeStruct((n_tiles * tm, chunks, LANES), F32),
        compiler_params=_params("arbitrary"),
        name="routed_experts",
    )(tile_expert, n_used, slot_token, hf, w_gate, w_up, w_down)


def _combine_kernel(pos_ref, ys_ref, x1_ref, route_ref, gf_ref, fw_ref, o_ref, ybuf, sem):
    i = pl.program_id(0)
    n_tiles = pl.num_programs(0)
    tm = x1_ref.shape[0]
    chunks = ys_ref.shape[1]

    def start_tile(tile, slot):
        for choice in range(2):
            _start_row_gather(pos_ref, lambda j: (tile * tm + j) * 2 + choice, ys_ref,
                              ybuf.at[slot, choice], sem.at[slot, choice], tm)

    @pl.when(i == 0)
    def _():
        start_tile(0, 0)

    @pl.when(i + 1 < n_tiles)
    def _():
        start_tile(i + 1, (i + 1) % 2)

    slot = i % 2
    for choice in range(2):
        _wait_row_gather(ybuf.at[slot, choice], sem.at[slot, choice], tm, chunks)
    route = route_ref[...]
    moe = (route[:, R_W0:R_W0 + 1] * _load_gathered(ybuf.at[slot, 0], tm, chunks)
           + route[:, R_W1:R_W1 + 1] * _load_gathered(ybuf.at[slot, 1], tm, chunks))
    x2 = x1_ref[...] + gf_ref[0] * moe
    o_ref[...] = _rms(x2, fw_ref[...])


def _combine(pos, ys, x1, route, mod3, final_w, n, tm):
    m, d = x1.shape
    grid_spec = pltpu.PrefetchScalarGridSpec(
        num_scalar_prefetch=1,
        grid=(m // tm,),
        in_specs=[
            pl.BlockSpec(memory_space=pl.ANY),
            pl.BlockSpec((tm, d), lambda i, pos_ref: (i, 0)),
            pl.BlockSpec((tm, LANES), lambda i, pos_ref: (i, 0)),
            pl.BlockSpec((1, 1, d), lambda i, pos_ref: (((i * tm) // n) * N_MOD + 5, 0, 0)),
            pl.BlockSpec((1, d), lambda i, pos_ref: (0, 0)),
        ],
        out_specs=pl.BlockSpec((tm, d), lambda i, pos_ref: (i, 0)),
        scratch_shapes=[pltpu.VMEM((2, 2, tm * ROW_PITCH, LANES), F32), pltpu.SemaphoreType.DMA((2, 2))],
    )
    return pl.pallas_call(
        _combine_kernel,
        grid_spec=grid_spec,
        out_shape=jax.ShapeDtypeStruct((m, d), F32),
        compiler_params=_params("arbitrary"),
        name="combine_final_norm",
    )(pos, ys, x1, route, mod3, final_w.reshape(1, d))


def kernel(x, c, ctx, c_ctx, w_mod, b_mod, norm_mix_w, w_in, ret_decay_f, ret_decay_b, ret_gn_w, na_rpb, w_out, norm_ffn_w, w_router_group, b_router_group, w_router_expert, b_router_expert, w_gate, w_up, w_down, final_norm_w):
    assert w_mod.shape[0] == 1, "single trunk layer"
    batch, n, d = x.shape
    ctx_len = ctx.shape[1]
    n_groups = w_router_group.shape[-1]
    per_group = w_router_expert.shape[-1]
    n_experts = w_gate.shape[1]
    assert n_groups * per_group == n_experts and n_groups + n_experts <= LANES

    mod_rows = 8
    cc = jnp.zeros((mod_rows, d), F32).at[:batch].set(c).at[batch].set(c_ctx)
    mod = _modulation(cc, w_mod[0], b_mod[0])
    mod3 = mod.reshape(mod_rows * N_MOD, 1, d)

    w_in_b = w_in[0].astype(BF16)
    tm = min(512, n)
    tm_in = min(1024, n)
    x2d = x.reshape(batch * n, d)
    proj = _in_projection(x2d, mod3, lambda i: (i * tm_in) // n, norm_mix_w[0], w_in_b, tm_in)
    cproj = _in_projection(ctx.reshape(batch * ctx_len, d), mod3, lambda i: batch, norm_mix_w[0],
                           w_in_b, ctx_len)

    lg = jnp.stack([jax.nn.log_sigmoid(ret_decay_f[0].astype(F32)),
                    jax.nn.log_sigmoid(ret_decay_b[0].astype(F32))])
    ret = _retention(proj, cproj, lg, ret_gn_w[0], batch, n, ctx_len)
    na = _neighbourhood_attention(proj, cproj, na_rpb[0], batch, n, ctx_len)

    w_route = jnp.concatenate(
        [w_router_group[0], jnp.moveaxis(w_router_expert[0], 0, 1).reshape(d, n_experts)], axis=1)
    w_route = jnp.pad(w_route.astype(F32), ((0, 0), (0, LANES - n_groups - n_experts)))
    w_route_hi = w_route.astype(BF16)
    w_route = jnp.concatenate([w_route_hi, (w_route - w_route_hi.astype(F32)).astype(BF16)], axis=1)
    b_route = jnp.concatenate([b_router_group[0], b_router_expert[0].reshape(-1)])
    b_route = jnp.pad(b_route.astype(F32), (0, LANES - n_groups - n_experts)).reshape(1, LANES)
    x1, hf, route, counts = _out_projection(ret, na, w_out[0].astype(BF16), x2d, mod3, norm_ffn_w[0],
                                            w_route, b_route, n, n_groups, per_group, tm)

    te = EXPERT_TILE
    tokens = batch * n
    assert tokens < 1 << KEY_RANK_BITS
    n_tiles = (2 * tokens) // te + n_experts
    cnt = counts[0, n_groups:n_groups + n_experts].astype(jnp.int32)
    keys = route[:, R_KEY0:R_KEY1 + 1].astype(jnp.int32).reshape(-1)
    pos, slot_token, tile_expert, n_used = _slot_map(keys, cnt, n_tiles, te)

    ys = _experts(tile_expert, n_used, slot_token, hf, w_gate[0], w_up[0], w_down[0], n_tiles, te)
    out = _combine(pos, ys, x1, route, mod3, final_norm_w, n, min(256, n))
    return out.reshape(batch, n, d)
```

```python
import functools

import jax
import jax.numpy as jnp
import numpy as np
from jax import lax
from jax.experimental import pallas as pl
from jax.experimental.pallas import tpu as pltpu

F32 = jnp.float32
BF16 = jnp.bfloat16

GRID_W = 64
RET_HEADS = 4
RET_HEAD_DIM = 256
RET_WIDTH = RET_HEADS * RET_HEAD_DIM
NA_HEADS = 8
NA_HEAD_DIM = 128
NA_WIDTH = NA_HEADS * NA_HEAD_DIM
RET_CHUNK = 128
NA_ROWS = 8
NA_COLS = 16
ROPE_BASE = 10000.0
N_MOD = 6
NORM_EPS = 1e-6
ROPE_HALF = RET_HEAD_DIM // 2

NA_QROWS = 4
NA_MASK = -1e30

LANES = 128
SUBLANES = 8
VMEM_LIMIT_BYTES = 56 * 1024 * 1024

R_KEY0, R_KEY1, R_W0, R_W1 = 0, 1, 2, 3
KEY_RANK_BITS = 16
KEY_RANK_SPAN = float(1 << KEY_RANK_BITS)

EXPERT_TILE = 256


def _params(*sem):
    return pltpu.CompilerParams(dimension_semantics=sem, vmem_limit_bytes=VMEM_LIMIT_BYTES)


def _dot(a, b):
    return jnp.dot(a, b, preferred_element_type=F32)


def _dot_nt(a, b):
    return lax.dot_general(a, b, (((1,), (1,)), ((), ())), preferred_element_type=F32)


def _dot_tn(a, b):
    return lax.dot_general(a, b, (((0,), (0,)), ((), ())), preferred_element_type=F32)


def _rms(x, w):
    return x * lax.rsqrt(jnp.mean(x * x, axis=-1, keepdims=True) + NORM_EPS) * w


def _silu(x):
    return x * jax.nn.sigmoid(x)


def _store_rows(ref, val):
    for c in range(ref.shape[1]):
        ref[:, c, :] = val[:, c * LANES:(c + 1) * LANES]


ROW_PITCH = 17


def _start_row_gather(idx_ref, idx_of, src_hbm, dst_buf, sem, n_rows):
    chunks = src_hbm.shape[1]

    def body(j, carry):
        r = idx_ref[idx_of(j)]
        pltpu.make_async_copy(src_hbm.at[r], dst_buf.at[pl.ds(j * ROW_PITCH, chunks), :], sem).start()
        return carry

    lax.fori_loop(0, n_rows, body, 0, unroll=SUBLANES)


def _wait_row_gather(dst_buf, sem, n_rows, chunks):
    view = dst_buf.at[pl.ds(0, n_rows * chunks), :]
    pltpu.make_async_copy(view, view, sem).wait()


def _load_gathered(buf, n_rows, chunks):
    return jnp.concatenate([buf[pl.ds(c, n_rows, stride=ROW_PITCH), :] for c in range(chunks)], axis=1)


def _mod_kernel(c_ref, w_ref, b_ref, o_ref):
    a = _silu(c_ref[...]).astype(BF16)
    o_ref[...] = _dot(a, w_ref[...].astype(BF16)) + b_ref[...]


def _modulation(cc, w_mod, b_mod):
    rows, d = cc.shape
    width = w_mod.shape[1]
    tn = next(t for t in (1024, 512, 256, LANES) if width % t == 0)
    return pl.pallas_call(
        _mod_kernel,
        grid=(width // tn,),
        in_specs=[
            pl.BlockSpec((rows, d), lambda j: (0, 0)),
            pl.BlockSpec((d, tn), lambda j: (0, j)),
            pl.BlockSpec((1, tn), lambda j: (0, j)),
        ],
        out_specs=pl.BlockSpec((rows, tn), lambda j: (0, j)),
        out_shape=jax.ShapeDtypeStruct((rows, width), F32),
        compiler_params=_params("arbitrary"),
        name="modulation",
    )(cc, w_mod, b_mod.reshape(1, width))


def _inproj_kernel(x_ref, sh_ref, sc_ref, nw_ref, w_ref, o_ref, h_ref):
    @pl.when(pl.program_id(1) == 0)
    def _():
        y = _rms(x_ref[...], nw_ref[...])
        h_ref[...] = (y * (1.0 + sc_ref[0]) + sh_ref[0]).astype(BF16)

    o_ref[...] = _dot(h_ref[...], w_ref[...]).astype(BF16)


def _in_projection(x2d, mod3, mod_row_of_tile, norm_w, w_in, tm):
    m, d = x2d.shape
    width = w_in.shape[1]
    tn = 1024
    return pl.pallas_call(
        _inproj_kernel,
        grid=(m // tm, width // tn),
        in_specs=[
            pl.BlockSpec((tm, d), lambda i, j: (i, 0)),
            pl.BlockSpec((1, 1, d), lambda i, j: (mod_row_of_tile(i) * N_MOD + 0, 0, 0)),
            pl.BlockSpec((1, 1, d), lambda i, j: (mod_row_of_tile(i) * N_MOD + 1, 0, 0)),
            pl.BlockSpec((1, d), lambda i, j: (0, 0)),
            pl.BlockSpec((d, tn), lambda i, j: (0, j)),
        ],
        out_specs=pl.BlockSpec((tm, tn), lambda i, j: (i, j)),
        out_shape=jax.ShapeDtypeStruct((m, width), BF16),
        scratch_shapes=[pltpu.VMEM((tm, d), BF16)],
        compiler_params=_params("parallel", "arbitrary"),
        name="in_projection",
    )(x2d, mod3, mod3, norm_w.reshape(1, d), w_in)


def _ret_kernel(lg_ref, q_ref, k_ref, v_ref, g_ref, ck_ref, cv_ref, cosr_ref, sinr_ref,
                cosc_ref, sinc_ref, gnw_ref, o_ref, qr_s, kr_s, sfh_s, sbh_s, sf_s, sb_s):
    head = pl.program_id(1)
    lgf = lg_ref[0, head]
    lgb = lg_ref[1, head]
    n = q_ref.shape[0]
    c = RET_CHUNK
    nc = n // c
    ctx_len = ck_ref.shape[0]
    k_scale = RET_HEAD_DIM ** -0.5

    posl = lax.broadcasted_iota(jnp.int32, (ctx_len, 1), 0).astype(F32)
    ck = ck_ref[...].astype(F32) * k_scale
    cv = cv_ref[...]
    sf_s[...] = _dot_tn((ck * jnp.exp(lgf * (ctx_len - 1.0 - posl))).astype(BF16), cv)
    sb_s[...] = _dot_tn((ck * jnp.exp(lgb * posl)).astype(BF16), cv)

    cosc = cosc_ref[...]
    sinc = sinc_ref[...]
    pos = lax.broadcasted_iota(jnp.int32, (c, 1), 0).astype(F32)
    qdec_f = jnp.exp(lgf * (pos + 1.0))
    kdec_f = jnp.exp(lgf * (c - 1.0 - pos))
    cdec_f = jnp.exp(lgf * c)
    qdec_b = jnp.exp(lgb * (c - pos))
    kdec_b = jnp.exp(lgb * pos)
    cdec_b = jnp.exp(lgb * c)

    def rope(x, cosr, sinr):
        xa = x[:, :ROPE_HALF]
        xb = x[:, ROPE_HALF:]
        ya = xa * cosr + pltpu.roll(xa, ROPE_HALF // 2, 1) * sinr
        yb = xb * cosc + pltpu.roll(xb, ROPE_HALF // 2, 1) * sinc
        return jnp.concatenate([ya, yb], axis=1)

    def fwd_chunk(ci, carry):
        r0 = pl.multiple_of(ci * c, c)
        cosr = cosr_ref[pl.ds(r0, c), :]
        sinr = sinr_ref[pl.ds(r0, c), :]
        qr_s[pl.ds(r0, c), :] = rope(q_ref[pl.ds(r0, c), :].astype(F32), cosr, sinr).astype(BF16)
        k = rope(k_ref[pl.ds(r0, c), :].astype(F32), cosr, sinr) * k_scale
        kr_s[pl.ds(r0, c), :] = k.astype(BF16)
        kv = _dot_tn((k * kdec_f).astype(BF16), v_ref[pl.ds(r0, c), :])
        state = sf_s[...]
        sfh_s[ci] = state.astype(BF16)
        sf_s[...] = state * cdec_f + kv
        return carry

    lax.fori_loop(0, nc, fwd_chunk, 0, unroll=2)

    def bwd_chunk(i, carry):
        ci = nc - 1 - i
        r0 = pl.multiple_of(ci * c, c)
        k = kr_s[pl.ds(r0, c), :].astype(F32)
        kv = _dot_tn((k * kdec_b).astype(BF16), v_ref[pl.ds(r0, c), :])
        state = sb_s[...]
        sbh_s[ci] = state.astype(BF16)
        sb_s[...] = state * cdec_b + kv
        return carry

    lax.fori_loop(0, nc, bwd_chunk, 0, unroll=2)

    diff = (lax.broadcasted_iota(jnp.int32, (c, c), 0)
            - lax.broadcasted_iota(jnp.int32, (c, c), 1)).astype(F32)
    intra = (jnp.where(diff >= 0, jnp.exp(lgf * jnp.maximum(diff, 0.0)), 0.0)
             + jnp.where(diff <= 0, jnp.exp(lgb * jnp.maximum(-diff, 0.0)), 0.0))
    gnw = gnw_ref[...]

    def out_chunk(ci, carry):
        r0 = pl.multiple_of(ci * c, c)
        qb = qr_s[pl.ds(r0, c), :]
        kb = kr_s[pl.ds(r0, c), :]
        q = qb.astype(F32)
        scores = _dot_nt(qb, kb) * intra
        o = (_dot(scores.astype(BF16), v_ref[pl.ds(r0, c), :])
             + _dot((q * qdec_f).astype(BF16), sfh_s[ci])
             + _dot((q * qdec_b).astype(BF16), sbh_s[ci]))
        mu = jnp.mean(o, axis=-1, keepdims=True)
        d = o - mu
        var = jnp.mean(d * d, axis=-1, keepdims=True)
        on = d * lax.rsqrt(var + NORM_EPS) * gnw
        gate = _silu(g_ref[pl.ds(r0, c), :].astype(F32))
        o_ref[pl.ds(r0, c), :] = (on * gate).astype(BF16)
        return carry

    lax.fori_loop(0, nc, out_chunk, 0, unroll=2)


def _rope_tables(n):
    inv = ROPE_BASE ** (-jnp.arange(0, ROPE_HALF, 2, dtype=F32) / ROPE_HALF)

    def tables(pos):
        ang = pos.astype(F32)[:, None] * inv[None, :]
        cos = jnp.cos(ang)
        sin = jnp.sin(ang)
        return jnp.concatenate([cos, cos], axis=1), jnp.concatenate([-sin, sin], axis=1)

    t = jnp.arange(n)
    cosr, sinr = tables(t // GRID_W)
    cosc, sinc = tables(jnp.arange(RET_CHUNK) % GRID_W)
    return cosr, sinr, cosc, sinc


def _retention(proj, cproj, lg, gn_w, batch, n, ctx_len):
    hd = RET_HEAD_DIM
    cosr, sinr, cosc, sinc = _rope_tables(n)
    col = lambda which: (lambda b, h, lg_ref: (b, which * RET_HEADS + h))
    const = lambda b, h, lg_ref: (0, 0)
    grid_spec = pltpu.PrefetchScalarGridSpec(
        num_scalar_prefetch=1,
        grid=(batch, RET_HEADS),
        in_specs=[
            pl.BlockSpec((n, hd), col(0)),
            pl.BlockSpec((n, hd), col(1)),
            pl.BlockSpec((n, hd), col(2)),
            pl.BlockSpec((n, hd), col(3)),
            pl.BlockSpec((ctx_len, hd), col(1)),
            pl.BlockSpec((ctx_len, hd), col(2)),
            pl.BlockSpec((n, ROPE_HALF), const),
            pl.BlockSpec((n, ROPE_HALF), const),
            pl.BlockSpec((RET_CHUNK, ROPE_HALF), const),
            pl.BlockSpec((RET_CHUNK, ROPE_HALF), const),
            pl.BlockSpec((1, hd), lambda b, h, lg_ref: (0, h)),
        ],
        out_specs=pl.BlockSpec((n, hd), lambda b, h, lg_ref: (b, h)),
        scratch_shapes=[
            pltpu.VMEM((n, hd), BF16),
            pltpu.VMEM((n, hd), BF16),
            pltpu.VMEM((n // RET_CHUNK, hd, hd), BF16),
            pltpu.VMEM((n // RET_CHUNK, hd, hd), BF16),
            pltpu.VMEM((hd, hd), F32),
            pltpu.VMEM((hd, hd), F32),
        ],
    )
    return pl.pallas_call(
        _ret_kernel,
        grid_spec=grid_spec,
        out_shape=jax.ShapeDtypeStruct((batch * n, RET_WIDTH), BF16),
        compiler_params=_params("parallel", "arbitrary"),
        name="retention",
    )(lg, proj, proj, proj, proj, cproj, cproj, cosr, sinr, cosc, sinc, gn_w.reshape(1, RET_WIDTH))


def _na_kernel(q_ref, k_ref, v_ref, kc_ref, vc_ref, bias_ref, o_ref):
    n = q_ref.shape[0]
    rows = n // GRID_W
    kb = NA_ROWS * GRID_W
    scale = NA_HEAD_DIM ** -0.5
    kc = kc_ref[...]
    vc = jnp.concatenate([vc_ref[...], jnp.ones((kc.shape[0], NA_HEAD_DIM), BF16)], axis=1)
    ones_w = jnp.ones((kb, NA_HEAD_DIM), BF16)

    def grid_row(r):
        rs = jnp.clip(r - NA_ROWS // 2, 0, rows - NA_ROWS)
        q0 = pl.multiple_of(r * GRID_W, GRID_W)
        k0 = pl.multiple_of(rs * GRID_W, GRID_W)
        q = (q_ref[pl.ds(q0, GRID_W), :].astype(F32) * scale).astype(BF16)
        kw = k_ref[pl.ds(k0, kb), :]
        vw = jnp.concatenate([v_ref[pl.ds(k0, kb), :], ones_w], axis=1)
        s_win = _dot_nt(q, kw) + bias_ref[0, rs - r + NA_ROWS - 1]
        s_ctx = _dot_nt(q, kc)
        m = jnp.maximum(jnp.max(s_win, axis=-1, keepdims=True), jnp.max(s_ctx, axis=-1, keepdims=True))
        p_win = jnp.exp(s_win - m).astype(BF16)
        p_ctx = jnp.exp(s_ctx - m).astype(BF16)
        o = _dot(p_win, vw) + _dot(p_ctx, vc)
        o_ref[pl.ds(q0, GRID_W), :] = (o[:, :NA_HEAD_DIM] / o[:, NA_HEAD_DIM:]).astype(BF16)

    def group(g, carry):
        for u in range(NA_QROWS):
            grid_row(g * NA_QROWS + u)
        return carry

    lax.fori_loop(0, rows // NA_QROWS, group, 0)


def _na_bias(rpb):
    heads, n_roff, n_coff = rpb.shape
    n_off = n_roff - NA_ROWS + 1
    c = np.arange(GRID_W)
    cs = np.clip(c - NA_COLS // 2, 0, GRID_W - NA_COLS)
    vcol = (c[None, :] >= cs[:, None]) & (c[None, :] < cs[:, None] + NA_COLS)
    seg = 2 * GRID_W
    padded = jnp.pad(rpb, ((0, 0), (0, 0), (0, seg - n_coff)))
    vec = jnp.stack([padded[:, off:off + NA_ROWS] for off in range(n_off)], axis=1)
    period = NA_ROWS * seg
    vec = vec.reshape(heads, n_off, period)
    cut = jnp.tile(vec, (1, 1, GRID_W))[..., :GRID_W * (period - 1)]
    cut = jnp.pad(cut.reshape(heads, n_off, GRID_W, period - 1), ((0, 0), (0, 0), (0, 0), (0, 1)))
    bias = cut.reshape(heads, n_off, GRID_W, NA_ROWS, seg)[..., NA_COLS - 1:NA_COLS - 1 + GRID_W]
    bias = jnp.where(vcol[None, None, :, None, :], bias, NA_MASK)
    return bias.reshape(heads, n_off, GRID_W, NA_ROWS * GRID_W)


def _neighbourhood_attention(proj, cproj, rpb, batch, n, ctx_len):
    hd = NA_HEAD_DIM
    rows = n // GRID_W
    assert rows % NA_QROWS == 0 and rows >= NA_ROWS
    bias = _na_bias(rpb.astype(F32))
    base = 4 * RET_WIDTH // hd
    col = lambda which: (lambda b, h: (b, base + which * NA_HEADS + h))
    return pl.pallas_call(
        _na_kernel,
        grid=(batch, NA_HEADS),
        in_specs=[
            pl.BlockSpec((n, hd), col(0)),
            pl.BlockSpec((n, hd), col(1)),
            pl.BlockSpec((n, hd), col(2)),
            pl.BlockSpec((ctx_len, hd), col(1)),
            pl.BlockSpec((ctx_len, hd), col(2)),
            pl.BlockSpec((1,) + bias.shape[1:], lambda b, h: (h, 0, 0, 0)),
        ],
        out_specs=pl.BlockSpec((n, hd), lambda b, h: (b, h)),
        out_shape=jax.ShapeDtypeStruct((batch * n, NA_WIDTH), BF16),
        compiler_params=_params("parallel", "arbitrary"),
        name="neighbourhood_attention",
    )(proj, proj, proj, cproj, cproj, bias)


def _outproj_kernel(n_groups, per_group, ret_ref, na_ref, w1_ref, w2_ref, x_ref, ga_ref, shf_ref,
                    scf_ref, nw_ref, wr_ref, br_ref, x1_ref, hf_ref, route_ref, cnt_ref, carry_s):
    @pl.when(pl.program_id(0) == 0)
    def _():
        carry_s[...] = jnp.zeros_like(carry_s)

    acc = _dot(ret_ref[...], w1_ref[...]) + _dot(na_ref[...], w2_ref[...])
    x1 = x_ref[...] + ga_ref[0] * acc
    x1_ref[...] = x1
    hf = _rms(x1, nw_ref[...]) * (1.0 + scf_ref[0]) + shf_ref[0]
    _store_rows(hf_ref, hf)

    hf_hi = hf.astype(BF16)
    hf_lo = (hf - hf_hi.astype(F32)).astype(BF16)
    p_hi = _dot(hf_hi, wr_ref[...])
    p_lo = _dot(hf_lo, wr_ref[...])
    logits = p_hi[:, :LANES] + p_hi[:, LANES:] + p_lo[:, :LANES] + br_ref[...]
    tm = logits.shape[0]
    lane = lax.broadcasted_iota(jnp.int32, (tm, LANES), 1)
    neg = -jnp.inf

    def first_max(vals):
        top = jnp.max(vals, axis=-1, keepdims=True)
        idx = jnp.min(jnp.where(vals == top, lane, LANES), axis=-1, keepdims=True)
        return top, idx

    g_logits = jnp.where(lane < n_groups, logits, neg)
    g_top, g_sel = first_max(g_logits)
    g_w = 1.0 / jnp.sum(jnp.exp(g_logits - g_top), axis=-1, keepdims=True)
    lo = n_groups + g_sel * per_group
    e_logits = jnp.where((lane >= lo) & (lane < lo + per_group), logits, neg)
    v0, i0 = first_max(e_logits)
    v1, i1 = first_max(jnp.where(lane == i0, neg, e_logits))
    e1 = jnp.exp(v1 - v0)
    w0 = g_w / (1.0 + e1)
    w1 = g_w * e1 / (1.0 + e1)

    hit0 = lane == i0
    hit1 = lane == i1
    onehot = jnp.where(hit0 | hit1, 1.0, 0.0)
    earlier = jnp.where(lax.broadcasted_iota(jnp.int32, (tm, tm), 0)
                        > lax.broadcasted_iota(jnp.int32, (tm, tm), 1), 1.0, 0.0).astype(BF16)
    before = _dot(earlier, onehot.astype(BF16)) + carry_s[...]
    rank0 = jnp.sum(jnp.where(hit0, before, 0.0), axis=-1, keepdims=True)
    rank1 = jnp.sum(jnp.where(hit1, before, 0.0), axis=-1, keepdims=True)
    carry_s[...] = carry_s[...] + jnp.sum(onehot, axis=0, keepdims=True)
    cnt_ref[...] = carry_s[...]

    key0 = (i0 - n_groups).astype(F32) * KEY_RANK_SPAN + rank0
    key1 = (i1 - n_groups).astype(F32) * KEY_RANK_SPAN + rank1
    fields = {R_KEY0: key0, R_KEY1: key1, R_W0: w0, R_W1: w1}
    route = jnp.zeros((tm, LANES), F32)
    for idx, val in fields.items():
        route = jnp.where(lane == idx, val, route)
    route_ref[...] = route


def _out_projection(ret, na, w_out, x2d, mod3, norm_w, w_route, b_route, n, n_groups, per_group, tm):
    m, d = x2d.shape
    batch_of = lambda i: (i * tm) // n
    mod_spec = lambda which: pl.BlockSpec((1, 1, d), lambda i: (batch_of(i) * N_MOD + which, 0, 0))
    const2 = lambda i: (0, 0)
    return pl.pallas_call(
        functools.partial(_outproj_kernel, n_groups, per_group),
        grid=(m // tm,),
        in_specs=[
            pl.BlockSpec((tm, RET_WIDTH), lambda i: (i, 0)),
            pl.BlockSpec((tm, NA_WIDTH), lambda i: (i, 0)),
            pl.BlockSpec((RET_WIDTH, d), lambda i: (0, 0)),
            pl.BlockSpec((NA_WIDTH, d), lambda i: (RET_WIDTH // NA_WIDTH, 0)),
            pl.BlockSpec((tm, d), lambda i: (i, 0)),
            mod_spec(2), mod_spec(3), mod_spec(4),
            pl.BlockSpec((1, d), const2),
            pl.BlockSpec((d, 2 * LANES), const2),
            pl.BlockSpec((1, LANES), const2),
        ],
        out_specs=[
            pl.BlockSpec((tm, d), lambda i: (i, 0)),
            pl.BlockSpec((tm, d // LANES, LANES), lambda i: (i, 0, 0)),
            pl.BlockSpec((tm, LANES), lambda i: (i, 0)),
            pl.BlockSpec((1, LANES), const2),
        ],
        out_shape=[
            jax.ShapeDtypeStruct((m, d), F32),
            jax.ShapeDtypeStruct((m, d // LANES, LANES), F32),
            jax.ShapeDtypeStruct((m, LANES), F32),
            jax.ShapeDtypeStruct((1, LANES), F32),
        ],
        scratch_shapes=[pltpu.VMEM((1, LANES), F32)],
        compiler_params=_params("arbitrary"),
        name="out_projection_router",
    )(ret, na, w_out, w_out, x2d, mod3, mod3, mod3, norm_w.reshape(1, d), w_route, b_route)


def _slot_map_kernel(tile_rows, key_ref, cnt_ref, pos_ref, slot_ref, tile_expert_ref, next_expert_ref,
                     run_parity_ref, n_used_ref, offs_s):
    n_experts = cnt_ref.shape[0]
    n_tiles = tile_expert_ref.shape[0]
    shift = tile_rows.bit_length() - 1
    assert 1 << shift == tile_rows

    def fill(ref, lo, hi, val):
        groups = lax.shift_right_logical(hi - lo, 3)

        def group(g, carry):
            for k in range(SUBLANES):
                ref[lo + g * SUBLANES + k] = val
            return carry

        def single(s, carry):
            ref[s] = val
            return carry

        lax.fori_loop(0, groups, group, 0)
        lax.fori_loop(lo + groups * SUBLANES, hi, single, 0)

    def tiles_of(e):
        size = lax.shift_left(lax.shift_right_logical(cnt_ref[e] + (tile_rows - 1), shift), shift)
        return lax.shift_right_logical(offs_s[e], shift), lax.shift_right_logical(offs_s[e] + size, shift)

    def per_expert(e, carry):
        start, runs = carry
        cnt = cnt_ref[e]
        size = lax.shift_left(lax.shift_right_logical(cnt + (tile_rows - 1), shift), shift)
        offs_s[e] = start
        t0, t1 = tiles_of(e)
        fill(tile_expert_ref, t0, t1, e)
        fill(run_parity_ref, t0, t1, jnp.bitwise_and(runs, 1))
        fill(slot_ref, start + cnt, start + size, 0)
        return start + size, runs + (size > 0).astype(jnp.int32)

    end, _ = lax.fori_loop(0, n_experts, per_expert, (0, 0))
    n_used = lax.shift_right_logical(end, shift)
    n_used_ref[0] = n_used
    fill(tile_expert_ref, n_used, n_tiles, n_experts - 1)
    fill(run_parity_ref, n_used, n_tiles, 0)
    fill(next_expert_ref, n_used, n_tiles, -1)
    fill(slot_ref, end, slot_ref.shape[0], 0)

    def per_expert_reversed(k, following):
        e = n_experts - 1 - k
        t0, t1 = tiles_of(e)
        fill(next_expert_ref, t0, t1, following)
        return jnp.where(t1 > t0, e, following)

    lax.fori_loop(0, n_experts, per_expert_reversed, -1)

    def assign(a, carry):
        key = key_ref[a]
        p = offs_s[lax.shift_right_logical(key, KEY_RANK_BITS)] + jnp.bitwise_and(key, (1 << KEY_RANK_BITS) - 1)
        pos_ref[a] = p
        slot_ref[p] = lax.shift_right_logical(a, 1)
        return carry

    lax.fori_loop(0, key_ref.shape[0], assign, 0, unroll=SUBLANES)


def _slot_map(keys, cnt, n_tiles, tile_rows):
    smem = pl.BlockSpec(memory_space=pltpu.SMEM)
    i32 = lambda *shape: jax.ShapeDtypeStruct(shape, jnp.int32)
    return pl.pallas_call(
        functools.partial(_slot_map_kernel, tile_rows),
        in_specs=[smem, smem],
        out_specs=[smem] * 6,
        out_shape=[i32(keys.shape[0]), i32(n_tiles * tile_rows), i32(n_tiles), i32(n_tiles),
                   i32(n_tiles), i32(1)],
        scratch_shapes=[pltpu.SMEM((cnt.shape[0],), jnp.int32)],
        name="slot_map",
    )(keys, cnt)


def _expert_kernel(tile_expert_ref, next_expert_ref, run_parity_ref, n_used_ref, slot_token_ref,
                   hf_ref, wg_hbm, wu_hbm, wd_hbm, o_ref, xbuf, gsem, wg_f, wu_f, wd_f, wsem,
                   wg_s, wu_s, wd_s):
    i = pl.program_id(0)
    n_used = n_used_ref[0]
    tm, chunks, _ = o_ref.shape

    def start_tile(tile, slot):
        _start_row_gather(slot_token_ref, lambda j: tile * tm + j, hf_ref, xbuf.at[slot],
                          gsem.at[slot], tm)

    def weight_copies(expert, slot):
        return [pltpu.make_async_copy(hbm.at[expert], stage.at[slot], wsem.at[slot])
                for hbm, stage in ((wg_hbm, wg_f), (wu_hbm, wu_f), (wd_hbm, wd_f))]

    @pl.when((i == 0) & (n_used > 0))
    def _():
        start_tile(0, 0)
        for cp in weight_copies(tile_expert_ref[0], run_parity_ref[0]):
            cp.start()

    @pl.when(i + 1 < n_used)
    def _():
        start_tile(i + 1, (i + 1) % 2)

    @pl.when(i < n_used)
    def _():
        @pl.when((i == 0) | (tile_expert_ref[i] != tile_expert_ref[jnp.maximum(i - 1, 0)]))
        def _():
            stage = run_parity_ref[i]
            for cp in weight_copies(tile_expert_ref[i], stage):
                cp.wait()

            @pl.when(next_expert_ref[i] >= 0)
            def _():
                for cp in weight_copies(next_expert_ref[i], 1 - stage):
                    cp.start()

            wg_s[...] = wg_f[stage].astype(BF16)
            wu_s[...] = wu_f[stage].astype(BF16)
            wd_s[...] = wd_f[stage].astype(BF16)

        slot = i % 2
        _wait_row_gather(xbuf.at[slot], gsem.at[slot], tm, chunks)
        x = _load_gathered(xbuf.at[slot], tm, chunks).astype(BF16)
        a = (_silu(_dot(x, wg_s[...])) * _dot(x, wu_s[...])).astype(BF16)
        _store_rows(o_ref, _dot(a, wd_s[...]))

    @pl.when(i >= n_used)
    def _():
        o_ref[...] = jnp.zeros(o_ref.shape, F32)


def _experts(tile_meta, slot_token, hf, w_gate, w_up, w_down, n_tiles, tm):
    chunks = hf.shape[1]
    d = chunks * LANES
    ff = w_gate.shape[-1]
    any_space = pl.BlockSpec(memory_space=pl.ANY)
    grid_spec = pltpu.PrefetchScalarGridSpec(
        num_scalar_prefetch=5,
        grid=(n_tiles,),
        in_specs=[any_space, any_space, any_space, any_space],
        out_specs=pl.BlockSpec((tm, chunks, LANES), lambda i, *prefetch: (i, 0, 0)),
        scratch_shapes=[
            pltpu.VMEM((2, tm * ROW_PITCH, LANES), F32),
            pltpu.SemaphoreType.DMA((2,)),
            pltpu.VMEM((2, d, ff), F32),
            pltpu.VMEM((2, d, ff), F32),
            pltpu.VMEM((2, ff, d), F32),
            pltpu.SemaphoreType.DMA((2,)),
            pltpu.VMEM((d, ff), BF16),
            pltpu.VMEM((d, ff), BF16),
            pltpu.VMEM((ff, d), BF16),
        ],
    )
    return pl.pallas_call(
        _expert_kernel,
        grid_spec=grid_spec,
        out_shape=jax.ShapeDtypeStruct((n_tiles * tm, chunks, LANES), F32),
        compiler_params=_params("arbitrary"),
        name="routed_experts",
    )(*tile_meta, slot_token, hf, w_gate, w_up, w_down)


def _combine_kernel(pos_ref, ys_ref, x1_ref, route_ref, gf_ref, fw_ref, o_ref, ybuf, sem):
    i = pl.program_id(0)
    n_tiles = pl.num_programs(0)
    tm = x1_ref.shape[0]
    chunks = ys_ref.shape[1]

    def start_tile(tile, slot):
        for choice in range(2):
            _start_row_gather(pos_ref, lambda j: (tile * tm + j) * 2 + choice, ys_ref,
                              ybuf.at[slot, choice], sem.at[slot, choice], tm)

    @pl.when(i == 0)
    def _():
        start_tile(0, 0)

    @pl.when(i + 1 < n_tiles)
    def _():
        start_tile(i + 1, (i + 1) % 2)

    slot = i % 2
    for choice in range(2):
        _wait_row_gather(ybuf.at[slot, choice], sem.at[slot, choice], tm, chunks)
    route = route_ref[...]
    moe = (route[:, R_W0:R_W0 + 1] * _load_gathered(ybuf.at[slot, 0], tm, chunks)
           + route[:, R_W1:R_W1 + 1] * _load_gathered(ybuf.at[slot, 1], tm, chunks))
    x2 = x1_ref[...] + gf_ref[0] * moe
    o_ref[...] = _rms(x2, fw_ref[...])


def _combine(pos, ys, x1, route, mod3, final_w, n, tm):
    m, d = x1.shape
    grid_spec = pltpu.PrefetchScalarGridSpec(
        num_scalar_prefetch=1,
        grid=(m // tm,),
        in_specs=[
            pl.BlockSpec(memory_space=pl.ANY),
            pl.BlockSpec((tm, d), lambda i, pos_ref: (i, 0)),
            pl.BlockSpec((tm, LANES), lambda i, pos_ref: (i, 0)),
            pl.BlockSpec((1, 1, d), lambda i, pos_ref: (((i * tm) // n) * N_MOD + 5, 0, 0)),
            pl.BlockSpec((1, d), lambda i, pos_ref: (0, 0)),
        ],
        out_specs=pl.BlockSpec((tm, d), lambda i, pos_ref: (i, 0)),
        scratch_shapes=[pltpu.VMEM((2, 2, tm * ROW_PITCH, LANES), F32), pltpu.SemaphoreType.DMA((2, 2))],
    )
    return pl.pallas_call(
        _combine_kernel,
        grid_spec=grid_spec,
        out_shape=jax.ShapeDtypeStruct((m, d), F32),
        compiler_params=_params("arbitrary"),
        name="combine_final_norm",
    )(pos, ys, x1, route, mod3, final_w.reshape(1, d))


def kernel(x, c, ctx, c_ctx, w_mod, b_mod, norm_mix_w, w_in, ret_decay_f, ret_decay_b, ret_gn_w, na_rpb, w_out, norm_ffn_w, w_router_group, b_router_group, w_router_expert, b_router_expert, w_gate, w_up, w_down, final_norm_w):
    assert w_mod.shape[0] == 1, "single trunk layer"
    batch, n, d = x.shape
    ctx_len = ctx.shape[1]
    n_groups = w_router_group.shape[-1]
    per_group = w_router_expert.shape[-1]
    n_experts = w_gate.shape[1]
    assert n_groups * per_group == n_experts and n_groups + n_experts <= LANES

    mod_rows = 8
    cc = jnp.zeros((mod_rows, d), F32).at[:batch].set(c).at[batch].set(c_ctx)
    mod = _modulation(cc, w_mod[0], b_mod[0])
    mod3 = mod.reshape(mod_rows * N_MOD, 1, d)

    w_in_b = w_in[0].astype(BF16)
    tm = min(512, n)
    tm_in = min(1024, n)
    x2d = x.reshape(batch * n, d)
    proj = _in_projection(x2d, mod3, lambda i: (i * tm_in) // n, norm_mix_w[0], w_in_b, tm_in)
    cproj = _in_projection(ctx.reshape(batch * ctx_len, d), mod3, lambda i: batch, norm_mix_w[0],
                           w_in_b, ctx_len)

    lg = jnp.stack([jax.nn.log_sigmoid(ret_decay_f[0].astype(F32)),
                    jax.nn.log_sigmoid(ret_decay_b[0].astype(F32))])
    ret = _retention(proj, cproj, lg, ret_gn_w[0], batch, n, ctx_len)
    na = _neighbourhood_attention(proj, cproj, na_rpb[0], batch, n, ctx_len)

    w_route = jnp.concatenate(
        [w_router_group[0], jnp.moveaxis(w_router_expert[0], 0, 1).reshape(d, n_experts)], axis=1)
    w_route = jnp.pad(w_route.astype(F32), ((0, 0), (0, LANES - n_groups - n_experts)))
    w_route_hi = w_route.astype(BF16)
    w_route = jnp.concatenate([w_route_hi, (w_route - w_route_hi.astype(F32)).astype(BF16)], axis=1)
    b_route = jnp.concatenate([b_router_group[0], b_router_expert[0].reshape(-1)])
    b_route = jnp.pad(b_route.astype(F32), (0, LANES - n_groups - n_experts)).reshape(1, LANES)
    x1, hf, route, counts = _out_projection(ret, na, w_out[0].astype(BF16), x2d, mod3, norm_ffn_w[0],
                                            w_route, b_route, n, n_groups, per_group, tm)

    te = EXPERT_TILE
    tokens = batch * n
    assert tokens < 1 << KEY_RANK_BITS
    n_tiles = (2 * tokens) // te + n_experts
    cnt = counts[0, n_groups:n_groups + n_experts].astype(jnp.int32)
    keys = route[:, R_KEY0:R_KEY1 + 1].astype(jnp.int32).reshape(-1)
    pos, slot_token, *tile_meta = _slot_map(keys, cnt, n_tiles, te)

    ys = _experts(tile_meta, slot_token, hf, w_gate[0], w_up[0], w_down[0], n_tiles, te)
    out = _combine(pos, ys, x1, route, mod3, final_norm_w, n, min(256, n))
    return out.reshape(batch, n, d)
```

```python
import functools

import jax
import jax.numpy as jnp
import numpy as np
from jax import lax
from jax.experimental import pallas as pl
from jax.experimental.pallas import tpu as pltpu

F32 = jnp.float32
BF16 = jnp.bfloat16

GRID_W = 64
RET_HEADS = 4
RET_HEAD_DIM = 256
RET_WIDTH = RET_HEADS * RET_HEAD_DIM
NA_HEADS = 8
NA_HEAD_DIM = 128
NA_WIDTH = NA_HEADS * NA_HEAD_DIM
RET_CHUNK = 128
NA_ROWS = 8
NA_COLS = 16
ROPE_BASE = 10000.0
N_MOD = 6
NORM_EPS = 1e-6
ROPE_HALF = RET_HEAD_DIM // 2
RET_UNROLL = 8

NA_QROWS = 16
NA_MASK = -1e30

LANES = 128
SUBLANES = 8
VMEM_LIMIT_BYTES = 56 * 1024 * 1024

R_KEY0, R_KEY1, R_W0, R_W1 = 0, 1, 2, 3
KEY_RANK_BITS = 16
KEY_RANK_SPAN = float(1 << KEY_RANK_BITS)

EXPERT_TILE = 256
WEIGHT_DMA_PRIORITY = 1


def _params(*sem):
    return pltpu.CompilerParams(dimension_semantics=sem, vmem_limit_bytes=VMEM_LIMIT_BYTES)


def _dot(a, b):
    return jnp.dot(a, b, preferred_element_type=F32)


def _dot_nt(a, b):
    return lax.dot_general(a, b, (((1,), (1,)), ((), ())), preferred_element_type=F32)


def _dot_tn(a, b):
    return lax.dot_general(a, b, (((0,), (0,)), ((), ())), preferred_element_type=F32)


def _rms(x, w):
    return x * lax.rsqrt(jnp.mean(x * x, axis=-1, keepdims=True) + NORM_EPS) * w


def _silu(x):
    return x * jax.nn.sigmoid(x)


def _store_rows(ref, val):
    for c in range(ref.shape[1]):
        ref[:, c, :] = val[:, c * LANES:(c + 1) * LANES]


def _row_pitch(chunks):
    return chunks + 1


def _start_row_gather(idx_ref, idx_of, src_hbm, dst_buf, sem, n_rows):
    chunks = src_hbm.shape[1]

    def body(j, carry):
        r = idx_ref[idx_of(j)]
        dst = dst_buf.at[pl.ds(j * _row_pitch(chunks), chunks), :]
        pltpu.make_async_copy(src_hbm.at[r], dst, sem).start()
        return carry

    lax.fori_loop(0, n_rows, body, 0, unroll=SUBLANES)


def _wait_row_gather(dst_buf, sem, n_rows, chunks):
    view = dst_buf.at[pl.ds(0, n_rows * chunks), :]
    pltpu.make_async_copy(view, view, sem).wait()


def _load_gathered(buf, n_rows, chunks):
    return jnp.concatenate([buf[pl.ds(c, n_rows, stride=_row_pitch(chunks)), :] for c in range(chunks)],
                           axis=1)


def _mod_kernel(c_ref, w_ref, b_ref, o_ref):
    a = _silu(c_ref[...]).astype(BF16)
    o_ref[...] = _dot(a, w_ref[...].astype(BF16)) + b_ref[...]


def _modulation(cc, w_mod, b_mod):
    rows, d = cc.shape
    width = w_mod.shape[1]
    tn = next(t for t in (1024, 512, 256, LANES) if width % t == 0)
    return pl.pallas_call(
        _mod_kernel,
        grid=(width // tn,),
        in_specs=[
            pl.BlockSpec((rows, d), lambda j: (0, 0)),
            pl.BlockSpec((d, tn), lambda j: (0, j)),
            pl.BlockSpec((1, tn), lambda j: (0, j)),
        ],
        out_specs=pl.BlockSpec((rows, tn), lambda j: (0, j)),
        out_shape=jax.ShapeDtypeStruct((rows, width), F32),
        compiler_params=_params("arbitrary"),
        name="modulation",
    )(cc, w_mod, b_mod.reshape(1, width))


def _inproj_kernel(x_ref, sh_ref, sc_ref, nw_ref, w_ref, o_ref, h_ref):
    @pl.when(pl.program_id(1) == 0)
    def _():
        y = _rms(x_ref[...], nw_ref[...])
        h_ref[...] = (y * (1.0 + sc_ref[0]) + sh_ref[0]).astype(BF16)

    o_ref[...] = _dot(h_ref[...], w_ref[...]).astype(BF16)


def _in_projection(x2d, mod3, mod_row_of_tile, norm_w, w_in, tm):
    m, d = x2d.shape
    width = w_in.shape[1]
    tn = 1024
    return pl.pallas_call(
        _inproj_kernel,
        grid=(m // tm, width // tn),
        in_specs=[
            pl.BlockSpec((tm, d), lambda i, j: (i, 0)),
            pl.BlockSpec((1, 1, d), lambda i, j: (mod_row_of_tile(i) * N_MOD + 0, 0, 0)),
            pl.BlockSpec((1, 1, d), lambda i, j: (mod_row_of_tile(i) * N_MOD + 1, 0, 0)),
            pl.BlockSpec((1, d), lambda i, j: (0, 0)),
            pl.BlockSpec((d, tn), lambda i, j: (0, j)),
        ],
        out_specs=pl.BlockSpec((tm, tn), lambda i, j: (i, j)),
        out_shape=jax.ShapeDtypeStruct((m, width), BF16),
        scratch_shapes=[pltpu.VMEM((tm, d), BF16)],
        compiler_params=_params("parallel", "arbitrary"),
        name="in_projection",
    )(x2d, mod3, mod3, norm_w.reshape(1, d), w_in)


def _ret_kernel(lg_ref, q_ref, k_ref, v_ref, g_ref, ck_ref, cv_ref, cosr_ref, sinr_ref,
                cosc_ref, sinc_ref, gnw_ref, o_ref, qr_s, kr_s, sfh_s, sbh_s, sf_s, sb_s):
    head = pl.program_id(1)
    lgf = lg_ref[0, head]
    lgb = lg_ref[1, head]
    n = q_ref.shape[0]
    c = RET_CHUNK
    nc = n // c
    ctx_len = ck_ref.shape[0]
    k_scale = RET_HEAD_DIM ** -0.5

    posl = lax.broadcasted_iota(jnp.int32, (ctx_len, 1), 0).astype(F32)
    ck = ck_ref[...].astype(F32) * k_scale
    cv = cv_ref[...]
    sf_s[...] = _dot_tn((ck * jnp.exp(lgf * (ctx_len - 1.0 - posl))).astype(BF16), cv)
    sb_s[...] = _dot_tn((ck * jnp.exp(lgb * posl)).astype(BF16), cv)

    cosc = cosc_ref[...]
    sinc = sinc_ref[...]
    pos = lax.broadcasted_iota(jnp.int32, (c, 1), 0).astype(F32)
    qdec_f = jnp.exp(lgf * (pos + 1.0))
    kdec_f = jnp.exp(lgf * (c - 1.0 - pos))
    cdec_f = jnp.exp(lgf * c)
    qdec_b = jnp.exp(lgb * (c - pos))
    kdec_b = jnp.exp(lgb * pos)
    cdec_b = jnp.exp(lgb * c)

    def rope(x, cosr, sinr):
        xa = x[:, :ROPE_HALF]
        xb = x[:, ROPE_HALF:]
        ya = xa * cosr + pltpu.roll(xa, ROPE_HALF // 2, 1) * sinr
        yb = xb * cosc + pltpu.roll(xb, ROPE_HALF // 2, 1) * sinc
        return jnp.concatenate([ya, yb], axis=1)

    def fwd_chunk(ci, carry):
        r0 = pl.multiple_of(ci * c, c)
        cosr = cosr_ref[pl.ds(r0, c), :]
        sinr = sinr_ref[pl.ds(r0, c), :]
        qr_s[pl.ds(r0, c), :] = rope(q_ref[pl.ds(r0, c), :].astype(F32), cosr, sinr).astype(BF16)
        k = rope(k_ref[pl.ds(r0, c), :].astype(F32), cosr, sinr) * k_scale
        kr_s[pl.ds(r0, c), :] = k.astype(BF16)
        kv = _dot_tn((k * kdec_f).astype(BF16), v_ref[pl.ds(r0, c), :])
        state = sf_s[...]
        sfh_s[ci] = state.astype(BF16)
        sf_s[...] = state * cdec_f + kv
        return carry

    lax.fori_loop(0, nc, fwd_chunk, 0, unroll=RET_UNROLL)

    def bwd_chunk(i, carry):
        ci = nc - 1 - i
        r0 = pl.multiple_of(ci * c, c)
        k = kr_s[pl.ds(r0, c), :].astype(F32)
        kv = _dot_tn((k * kdec_b).astype(BF16), v_ref[pl.ds(r0, c), :])
        state = sb_s[...]
        sbh_s[ci] = state.astype(BF16)
        sb_s[...] = state * cdec_b + kv
        return carry

    lax.fori_loop(0, nc, bwd_chunk, 0, unroll=RET_UNROLL)

    diff = (lax.broadcasted_iota(jnp.int32, (c, c), 0)
            - lax.broadcasted_iota(jnp.int32, (c, c), 1)).astype(F32)
    intra = (jnp.where(diff >= 0, jnp.exp(lgf * jnp.maximum(diff, 0.0)), 0.0)
             + jnp.where(diff <= 0, jnp.exp(lgb * jnp.maximum(-diff, 0.0)), 0.0))
    gnw = gnw_ref[...]

    def out_chunk(ci, carry):
        r0 = pl.multiple_of(ci * c, c)
        qb = qr_s[pl.ds(r0, c), :]
        kb = kr_s[pl.ds(r0, c), :]
        q = qb.astype(F32)
        scores = _dot_nt(qb, kb) * intra
        o = (_dot(scores.astype(BF16), v_ref[pl.ds(r0, c), :])
             + _dot((q * qdec_f).astype(BF16), sfh_s[ci])
             + _dot((q * qdec_b).astype(BF16), sbh_s[ci]))
        mu = jnp.mean(o, axis=-1, keepdims=True)
        d = o - mu
        var = jnp.mean(d * d, axis=-1, keepdims=True)
        on = d * lax.rsqrt(var + NORM_EPS) * gnw
        gate = _silu(g_ref[pl.ds(r0, c), :].astype(F32))
        o_ref[pl.ds(r0, c), :] = (on * gate).astype(BF16)
        return carry

    lax.fori_loop(0, nc, out_chunk, 0, unroll=RET_UNROLL)


def _rope_tables(n):
    inv = ROPE_BASE ** (-jnp.arange(0, ROPE_HALF, 2, dtype=F32) / ROPE_HALF)

    def tables(pos):
        ang = pos.astype(F32)[:, None] * inv[None, :]
        cos = jnp.cos(ang)
        sin = jnp.sin(ang)
        return jnp.concatenate([cos, cos], axis=1), jnp.concatenate([-sin, sin], axis=1)

    t = jnp.arange(n)
    cosr, sinr = tables(t // GRID_W)
    cosc, sinc = tables(jnp.arange(RET_CHUNK) % GRID_W)
    return cosr, sinr, cosc, sinc


def _retention(proj, cproj, lg, gn_w, batch, n, ctx_len):
    hd = RET_HEAD_DIM
    cosr, sinr, cosc, sinc = _rope_tables(n)
    col = lambda which: (lambda b, h, lg_ref: (b, which * RET_HEADS + h))
    const = lambda b, h, lg_ref: (0, 0)
    grid_spec = pltpu.PrefetchScalarGridSpec(
        num_scalar_prefetch=1,
        grid=(batch, RET_HEADS),
        in_specs=[
            pl.BlockSpec((n, hd), col(0)),
            pl.BlockSpec((n, hd), col(1)),
            pl.BlockSpec((n, hd), col(2)),
            pl.BlockSpec((n, hd), col(3)),
            pl.BlockSpec((ctx_len, hd), col(1)),
            pl.BlockSpec((ctx_len, hd), col(2)),
            pl.BlockSpec((n, ROPE_HALF), const),
            pl.BlockSpec((n, ROPE_HALF), const),
            pl.BlockSpec((RET_CHUNK, ROPE_HALF), const),
            pl.BlockSpec((RET_CHUNK, ROPE_HALF), const),
            pl.BlockSpec((1, hd), lambda b, h, lg_ref: (0, h)),
        ],
        out_specs=pl.BlockSpec((n, hd), lambda b, h, lg_ref: (b, h)),
        scratch_shapes=[
            pltpu.VMEM((n, hd), BF16),
            pltpu.VMEM((n, hd), BF16),
            pltpu.VMEM((n // RET_CHUNK, hd, hd), BF16),
            pltpu.VMEM((n // RET_CHUNK, hd, hd), BF16),
            pltpu.VMEM((hd, hd), F32),
            pltpu.VMEM((hd, hd), F32),
        ],
    )
    return pl.pallas_call(
        _ret_kernel,
        grid_spec=grid_spec,
        out_shape=jax.ShapeDtypeStruct((batch * n, RET_WIDTH), BF16),
        compiler_params=_params("parallel", "arbitrary"),
        name="retention",
    )(lg, proj, proj, proj, proj, cproj, cproj, cosr, sinr, cosc, sinc, gn_w.reshape(1, RET_WIDTH))


def _na_kernel(q_ref, k_ref, v_ref, kc_ref, vc_ref, bias_ref, o_ref, s_s, m_s):
    n = q_ref.shape[0]
    rows = n // GRID_W
    kb = NA_ROWS * GRID_W
    ctx_len = kc_ref.shape[0]
    scale = NA_HEAD_DIM ** -0.5
    kc = kc_ref[...]
    vc = jnp.concatenate([vc_ref[...], jnp.ones((ctx_len, NA_HEAD_DIM), BF16)], axis=1)
    ones_w = jnp.ones((kb, NA_HEAD_DIM), BF16)

    def window(r):
        rs = jnp.clip(r - NA_ROWS // 2, 0, rows - NA_ROWS)
        return rs, pl.multiple_of(r * GRID_W, GRID_W), pl.multiple_of(rs * GRID_W, GRID_W)

    def score_row(r):
        rs, q0, k0 = window(r)
        q = (q_ref[pl.ds(q0, GRID_W), :].astype(F32) * scale).astype(BF16)
        s_win = _dot_nt(q, k_ref[pl.ds(k0, kb), :]) + bias_ref[0, rs - r + NA_ROWS - 1]
        s_ctx = _dot_nt(q, kc)
        m = jnp.maximum(jnp.max(s_win, axis=-1, keepdims=True), jnp.max(s_ctx, axis=-1, keepdims=True))
        s_s[pl.ds(q0, GRID_W), :kb] = s_win
        s_s[pl.ds(q0, GRID_W), kb:] = s_ctx
        m_s[pl.ds(q0, GRID_W), :] = jnp.broadcast_to(m, (GRID_W, LANES))

    def value_row(r):
        _, q0, k0 = window(r)
        m = m_s[pl.ds(q0, GRID_W), :]
        p_win = jnp.exp(s_s[pl.ds(q0, GRID_W), :kb] - jnp.tile(m, (1, kb // LANES))).astype(BF16)
        p_ctx = jnp.exp(s_s[pl.ds(q0, GRID_W), kb:] - jnp.tile(m, (1, ctx_len // LANES))).astype(BF16)
        vw = jnp.concatenate([v_ref[pl.ds(k0, kb), :], ones_w], axis=1)
        o = _dot(p_win, vw) + _dot(p_ctx, vc)
        o_ref[pl.ds(q0, GRID_W), :] = (o[:, :NA_HEAD_DIM] / o[:, NA_HEAD_DIM:]).astype(BF16)

    def sweep(row_fn):
        def group(g, carry):
            for u in range(NA_QROWS):
                row_fn(g * NA_QROWS + u)
            return carry
        lax.fori_loop(0, rows // NA_QROWS, group, 0)

    sweep(score_row)
    sweep(value_row)


def _na_bias(rpb):
    heads, n_roff, n_coff = rpb.shape
    n_off = n_roff - NA_ROWS + 1
    c = np.arange(GRID_W)
    cs = np.clip(c - NA_COLS // 2, 0, GRID_W - NA_COLS)
    vcol = (c[None, :] >= cs[:, None]) & (c[None, :] < cs[:, None] + NA_COLS)
    seg = 2 * GRID_W
    padded = jnp.pad(rpb, ((0, 0), (0, 0), (0, seg - n_coff)))
    vec = jnp.stack([padded[:, off:off + NA_ROWS] for off in range(n_off)], axis=1)
    period = NA_ROWS * seg
    vec = vec.reshape(heads, n_off, period)
    cut = jnp.tile(vec, (1, 1, GRID_W))[..., :GRID_W * (period - 1)]
    cut = jnp.pad(cut.reshape(heads, n_off, GRID_W, period - 1), ((0, 0), (0, 0), (0, 0), (0, 1)))
    bias = cut.reshape(heads, n_off, GRID_W, NA_ROWS, seg)[..., NA_COLS - 1:NA_COLS - 1 + GRID_W]
    bias = jnp.where(vcol[None, None, :, None, :], bias, NA_MASK)
    return bias.reshape(heads, n_off, GRID_W, NA_ROWS * GRID_W)


def _neighbourhood_attention(proj, cproj, rpb, batch, n, ctx_len):
    hd = NA_HEAD_DIM
    rows = n // GRID_W
    assert rows % NA_QROWS == 0 and rows >= NA_ROWS
    bias = _na_bias(rpb.astype(F32))
    base = 4 * RET_WIDTH // hd
    col = lambda which: (lambda b, h: (b, base + which * NA_HEADS + h))
    return pl.pallas_call(
        _na_kernel,
        grid=(batch, NA_HEADS),
        in_specs=[
            pl.BlockSpec((n, hd), col(0)),
            pl.BlockSpec((n, hd), col(1)),
            pl.BlockSpec((n, hd), col(2)),
            pl.BlockSpec((ctx_len, hd), col(1)),
            pl.BlockSpec((ctx_len, hd), col(2)),
            pl.BlockSpec((1,) + bias.shape[1:], lambda b, h: (h, 0, 0, 0)),
        ],
        out_specs=pl.BlockSpec((n, hd), lambda b, h: (b, h)),
        out_shape=jax.ShapeDtypeStruct((batch * n, NA_WIDTH), BF16),
        scratch_shapes=[pltpu.VMEM((n, NA_ROWS * GRID_W + ctx_len), F32), pltpu.VMEM((n, LANES), F32)],
        compiler_params=_params("parallel", "arbitrary"),
        name="neighbourhood_attention",
    )(proj, proj, proj, cproj, cproj, bias)


def _outproj_kernel(n_groups, per_group, ret_ref, na_ref, w1_ref, w2_ref, x_ref, ga_ref, shf_ref,
                    scf_ref, nw_ref, wr_ref, br_ref, x1_ref, hf_ref, route_ref, cnt_ref, carry_s):
    @pl.when(pl.program_id(0) == 0)
    def _():
        carry_s[...] = jnp.zeros_like(carry_s)

    acc = _dot(ret_ref[...], w1_ref[...]) + _dot(na_ref[...], w2_ref[...])
    x1 = x_ref[...] + ga_ref[0] * acc
    x1_ref[...] = x1
    hf = _rms(x1, nw_ref[...]) * (1.0 + scf_ref[0]) + shf_ref[0]
    _store_rows(hf_ref, hf)

    hf_hi = hf.astype(BF16)
    hf_lo = (hf - hf_hi.astype(F32)).astype(BF16)
    p_hi = _dot(hf_hi, wr_ref[...])
    p_lo = _dot(hf_lo, wr_ref[...])
    logits = p_hi[:, :LANES] + p_hi[:, LANES:] + p_lo[:, :LANES] + br_ref[...]
    tm = logits.shape[0]
    lane = lax.broadcasted_iota(jnp.int32, (tm, LANES), 1)
    neg = -jnp.inf

    def first_max(vals):
        top = jnp.max(vals, axis=-1, keepdims=True)
        idx = jnp.min(jnp.where(vals == top, lane, LANES), axis=-1, keepdims=True)
        return top, idx

    g_logits = jnp.where(lane < n_groups, logits, neg)
    g_top, g_sel = first_max(g_logits)
    g_w = 1.0 / jnp.sum(jnp.exp(g_logits - g_top), axis=-1, keepdims=True)
    lo = n_groups + g_sel * per_group
    e_logits = jnp.where((lane >= lo) & (lane < lo + per_group), logits, neg)
    v0, i0 = first_max(e_logits)
    v1, i1 = first_max(jnp.where(lane == i0, neg, e_logits))
    e1 = jnp.exp(v1 - v0)
    w0 = g_w / (1.0 + e1)
    w1 = g_w * e1 / (1.0 + e1)

    hit0 = lane == i0
    hit1 = lane == i1
    onehot = jnp.where(hit0 | hit1, 1.0, 0.0)
    earlier = jnp.where(lax.broadcasted_iota(jnp.int32, (tm, tm), 0)
                        > lax.broadcasted_iota(jnp.int32, (tm, tm), 1), 1.0, 0.0).astype(BF16)
    before = _dot(earlier, onehot.astype(BF16)) + carry_s[...]
    rank0 = jnp.sum(jnp.where(hit0, before, 0.0), axis=-1, keepdims=True)
    rank1 = jnp.sum(jnp.where(hit1, before, 0.0), axis=-1, keepdims=True)
    carry_s[...] = carry_s[...] + jnp.sum(onehot, axis=0, keepdims=True)
    cnt_ref[...] = carry_s[...]

    key0 = (i0 - n_groups).astype(F32) * KEY_RANK_SPAN + rank0
    key1 = (i1 - n_groups).astype(F32) * KEY_RANK_SPAN + rank1
    fields = {R_KEY0: key0, R_KEY1: key1, R_W0: w0, R_W1: w1}
    route = jnp.zeros((tm, LANES), F32)
    for idx, val in fields.items():
        route = jnp.where(lane == idx, val, route)
    route_ref[...] = route


def _out_projection(ret, na, w_out, x2d, mod3, norm_w, w_route, b_route, n, n_groups, per_group, tm):
    m, d = x2d.shape
    chunks = d // LANES
    batch_of = lambda i: (i * tm) // n
    mod_spec = lambda which: pl.BlockSpec((1, 1, d), lambda i: (batch_of(i) * N_MOD + which, 0, 0))
    const2 = lambda i: (0, 0)
    return pl.pallas_call(
        functools.partial(_outproj_kernel, n_groups, per_group),
        grid=(m // tm,),
        in_specs=[
            pl.BlockSpec((tm, RET_WIDTH), lambda i: (i, 0)),
            pl.BlockSpec((tm, NA_WIDTH), lambda i: (i, 0)),
            pl.BlockSpec((RET_WIDTH, d), lambda i: (0, 0)),
            pl.BlockSpec((NA_WIDTH, d), lambda i: (RET_WIDTH // NA_WIDTH, 0)),
            pl.BlockSpec((tm, d), lambda i: (i, 0)),
            mod_spec(2), mod_spec(3), mod_spec(4),
            pl.BlockSpec((1, d), const2),
            pl.BlockSpec((d, 2 * LANES), const2),
            pl.BlockSpec((1, LANES), const2),
        ],
        out_specs=[
            pl.BlockSpec((tm, d), lambda i: (i, 0)),
            pl.BlockSpec((tm, chunks, LANES), lambda i: (i, 0, 0)),
            pl.BlockSpec((tm, LANES), lambda i: (i, 0)),
            pl.BlockSpec((1, LANES), const2),
        ],
        out_shape=[
            jax.ShapeDtypeStruct((m, d), F32),
            jax.ShapeDtypeStruct((m, chunks, LANES), F32),
            jax.ShapeDtypeStruct((m, LANES), F32),
            jax.ShapeDtypeStruct((1, LANES), F32),
        ],
        scratch_shapes=[pltpu.VMEM((1, LANES), F32)],
        compiler_params=_params("arbitrary"),
        name="out_projection_router",
    )(ret, na, w_out, w_out, x2d, mod3, mod3, mod3, norm_w.reshape(1, d), w_route, b_route)


def _slot_map_kernel(tile_rows, key_ref, cnt_ref, pos_ref, slot_ref, tile_expert_ref, next_expert_ref,
                     run_parity_ref, n_used_ref, offs_s):
    n_experts = cnt_ref.shape[0]
    n_tiles = tile_expert_ref.shape[0]
    shift = tile_rows.bit_length() - 1
    assert 1 << shift == tile_rows

    def fill(ref, lo, hi, val):
        groups = lax.shift_right_logical(hi - lo, 3)

        def group(g, carry):
            for k in range(SUBLANES):
                ref[lo + g * SUBLANES + k] = val
            return carry

        def single(s, carry):
            ref[s] = val
            return carry

        lax.fori_loop(0, groups, group, 0)
        lax.fori_loop(lo + groups * SUBLANES, hi, single, 0)

    def tiles_of(e):
        size = lax.shift_left(lax.shift_right_logical(cnt_ref[e] + (tile_rows - 1), shift), shift)
        return lax.shift_right_logical(offs_s[e], shift), lax.shift_right_logical(offs_s[e] + size, shift)

    def per_expert(e, carry):
        start, runs = carry
        cnt = cnt_ref[e]
        size = lax.shift_left(lax.shift_right_logical(cnt + (tile_rows - 1), shift), shift)
        offs_s[e] = start
        t0, t1 = tiles_of(e)
        fill(tile_expert_ref, t0, t1, e)
        fill(run_parity_ref, t0, t1, jnp.bitwise_and(runs, 1))
        fill(slot_ref, start + cnt, start + size, 0)
        return start + size, runs + (size > 0).astype(jnp.int32)

    end, _ = lax.fori_loop(0, n_experts, per_expert, (0, 0))
    n_used = lax.shift_right_logical(end, shift)
    n_used_ref[0] = n_used
    fill(tile_expert_ref, n_used, n_tiles, n_experts - 1)
    fill(run_parity_ref, n_used, n_tiles, 0)
    fill(next_expert_ref, n_used, n_tiles, -1)
    fill(slot_ref, end, slot_ref.shape[0], 0)

    def per_expert_reversed(k, following):
        e = n_experts - 1 - k
        t0, t1 = tiles_of(e)
        fill(next_expert_ref, t0, t1, following)
        return jnp.where(t1 > t0, e, following)

    lax.fori_loop(0, n_experts, per_expert_reversed, -1)

    def assign(a, carry):
        key = key_ref[a]
        p = offs_s[lax.shift_right_logical(key, KEY_RANK_BITS)] + jnp.bitwise_and(key, (1 << KEY_RANK_BITS) - 1)
        pos_ref[a] = p
        slot_ref[p] = lax.shift_right_logical(a, 1)
        return carry

    lax.fori_loop(0, key_ref.shape[0], assign, 0, unroll=SUBLANES)


def _slot_map(keys, cnt, n_tiles, tile_rows):
    smem = pl.BlockSpec(memory_space=pltpu.SMEM)
    i32 = lambda *shape: jax.ShapeDtypeStruct(shape, jnp.int32)
    return pl.pallas_call(
        functools.partial(_slot_map_kernel, tile_rows),
        in_specs=[smem, smem],
        out_specs=[smem] * 6,
        out_shape=[i32(keys.shape[0]), i32(n_tiles * tile_rows), i32(n_tiles), i32(n_tiles),
                   i32(n_tiles), i32(1)],
        scratch_shapes=[pltpu.SMEM((cnt.shape[0],), jnp.int32)],
        name="slot_map",
    )(keys, cnt)


def _expert_kernel(tile_expert_ref, next_expert_ref, run_parity_ref, n_used_ref, slot_token_ref,
                   hf_ref, wg_hbm, wu_hbm, wd_hbm, o_ref, xbuf, gsem, wg_f, wu_f, wd_f, wsem,
                   wg_s, wu_s, wd_s):
    i = pl.program_id(0)
    n_used = n_used_ref[0]
    tm, chunks, _ = o_ref.shape

    def start_tile(tile, slot):
        _start_row_gather(slot_token_ref, lambda j: tile * tm + j, hf_ref, xbuf.at[slot],
                          gsem.at[slot], tm)

    def weight_copies(expert, slot):
        return [pltpu.make_async_copy(hbm.at[expert], stage.at[slot], wsem.at[slot])
                for hbm, stage in ((wg_hbm, wg_f), (wu_hbm, wu_f), (wd_hbm, wd_f))]

    @pl.when((i == 0) & (n_used > 0))
    def _():
        start_tile(0, 0)
        for cp in weight_copies(tile_expert_ref[0], run_parity_ref[0]):
            cp.start(priority=WEIGHT_DMA_PRIORITY)

    @pl.when(i + 1 < n_used)
    def _():
        start_tile(i + 1, (i + 1) % 2)

    @pl.when(i < n_used)
    def _():
        @pl.when((i == 0) | (tile_expert_ref[i] != tile_expert_ref[jnp.maximum(i - 1, 0)]))
        def _():
            stage = run_parity_ref[i]
            for cp in weight_copies(tile_expert_ref[i], stage):
                cp.wait()

            @pl.when(next_expert_ref[i] >= 0)
            def _():
                for cp in weight_copies(next_expert_ref[i], 1 - stage):
                    cp.start(priority=WEIGHT_DMA_PRIORITY)

            wg_s[...] = wg_f[stage].astype(BF16)
            wu_s[...] = wu_f[stage].astype(BF16)
            wd_s[...] = wd_f[stage].astype(BF16)

        slot = i % 2
        _wait_row_gather(xbuf.at[slot], gsem.at[slot], tm, chunks)
        x = _load_gathered(xbuf.at[slot], tm, chunks).astype(BF16)
        a = (_silu(_dot(x, wg_s[...])) * _dot(x, wu_s[...])).astype(BF16)
        _store_rows(o_ref, _dot(a, wd_s[...]))

    @pl.when(i >= n_used)
    def _():
        o_ref[...] = jnp.zeros(o_ref.shape, o_ref.dtype)


def _experts(tile_meta, slot_token, hf, w_gate, w_up, w_down, n_tiles, tm):
    chunks = hf.shape[1]
    d = chunks * LANES
    ff = w_gate.shape[-1]
    any_space = pl.BlockSpec(memory_space=pl.ANY)
    grid_spec = pltpu.PrefetchScalarGridSpec(
        num_scalar_prefetch=5,
        grid=(n_tiles,),
        in_specs=[any_space, any_space, any_space, any_space],
        out_specs=pl.BlockSpec((tm, chunks, LANES), lambda i, *prefetch: (i, 0, 0)),
        scratch_shapes=[
            pltpu.VMEM((2, tm * _row_pitch(chunks), LANES), hf.dtype),
            pltpu.SemaphoreType.DMA((2,)),
            pltpu.VMEM((2, d, ff), F32),
            pltpu.VMEM((2, d, ff), F32),
            pltpu.VMEM((2, ff, d), F32),
            pltpu.SemaphoreType.DMA((2,)),
            pltpu.VMEM((d, ff), BF16),
            pltpu.VMEM((d, ff), BF16),
            pltpu.VMEM((ff, d), BF16),
        ],
    )
    return pl.pallas_call(
        _expert_kernel,
        grid_spec=grid_spec,
        out_shape=jax.ShapeDtypeStruct((n_tiles * tm, chunks, LANES), hf.dtype),
        compiler_params=_params("arbitrary"),
        name="routed_experts",
    )(*tile_meta, slot_token, hf, w_gate, w_up, w_down)


def _combine_kernel(pos_ref, ys_ref, x1_ref, route_ref, gf_ref, fw_ref, o_ref, ybuf, sem):
    i = pl.program_id(0)
    n_tiles = pl.num_programs(0)
    tm = x1_ref.shape[0]
    chunks = ys_ref.shape[1]

    def start_tile(tile, slot):
        for choice in range(2):
            _start_row_gather(pos_ref, lambda j: (tile * tm + j) * 2 + choice, ys_ref,
                              ybuf.at[slot, choice], sem.at[slot, choice], tm)

    @pl.when(i == 0)
    def _():
        start_tile(0, 0)

    @pl.when(i + 1 < n_tiles)
    def _():
        start_tile(i + 1, (i + 1) % 2)

    slot = i % 2
    for choice in range(2):
        _wait_row_gather(ybuf.at[slot, choice], sem.at[slot, choice], tm, chunks)
    route = route_ref[...]
    moe = (route[:, R_W0:R_W0 + 1] * _load_gathered(ybuf.at[slot, 0], tm, chunks)
           + route[:, R_W1:R_W1 + 1] * _load_gathered(ybuf.at[slot, 1], tm, chunks))
    x2 = x1_ref[...] + gf_ref[0] * moe
    o_ref[...] = _rms(x2, fw_ref[...])


def _combine(pos, ys, x1, route, mod3, final_w, n, tm):
    m, d = x1.shape
    chunks = ys.shape[1]
    grid_spec = pltpu.PrefetchScalarGridSpec(
        num_scalar_prefetch=1,
        grid=(m // tm,),
        in_specs=[
            pl.BlockSpec(memory_space=pl.ANY),
            pl.BlockSpec((tm, d), lambda i, pos_ref: (i, 0)),
            pl.BlockSpec((tm, LANES), lambda i, pos_ref: (i, 0)),
            pl.BlockSpec((1, 1, d), lambda i, pos_ref: (((i * tm) // n) * N_MOD + 5, 0, 0)),
            pl.BlockSpec((1, d), lambda i, pos_ref: (0, 0)),
        ],
        out_specs=pl.BlockSpec((tm, d), lambda i, pos_ref: (i, 0)),
        scratch_shapes=[pltpu.VMEM((2, 2, tm * _row_pitch(chunks), LANES), ys.dtype),
                        pltpu.SemaphoreType.DMA((2, 2))],
    )
    return pl.pallas_call(
        _combine_kernel,
        grid_spec=grid_spec,
        out_shape=jax.ShapeDtypeStruct((m, d), F32),
        compiler_params=_params("arbitrary"),
        name="combine_final_norm",
    )(pos, ys, x1, route, mod3, final_w.reshape(1, d))


def kernel(x, c, ctx, c_ctx, w_mod, b_mod, norm_mix_w, w_in, ret_decay_f, ret_decay_b, ret_gn_w, na_rpb, w_out, norm_ffn_w, w_router_group, b_router_group, w_router_expert, b_router_expert, w_gate, w_up, w_down, final_norm_w):
    assert w_mod.shape[0] == 1, "single trunk layer"
    batch, n, d = x.shape
    ctx_len = ctx.shape[1]
    n_groups = w_router_group.shape[-1]
    per_group = w_router_expert.shape[-1]
    n_experts = w_gate.shape[1]
    assert n_groups * per_group == n_experts and n_groups + n_experts <= LANES

    mod_rows = 8
    cc = jnp.zeros((mod_rows, d), F32).at[:batch].set(c).at[batch].set(c_ctx)
    mod = _modulation(cc, w_mod[0], b_mod[0])
    mod3 = mod.reshape(mod_rows * N_MOD, 1, d)

    w_in_b = w_in[0].astype(BF16)
    tm = min(512, n)
    tm_in = min(1024, n)
    x2d = x.reshape(batch * n, d)
    proj = _in_projection(x2d, mod3, lambda i: (i * tm_in) // n, norm_mix_w[0], w_in_b, tm_in)
    cproj = _in_projection(ctx.reshape(batch * ctx_len, d), mod3, lambda i: batch, norm_mix_w[0],
                           w_in_b, ctx_len)

    lg = jnp.stack([jax.nn.log_sigmoid(ret_decay_f[0].astype(F32)),
                    jax.nn.log_sigmoid(ret_decay_b[0].astype(F32))])
    ret = _retention(proj, cproj, lg, ret_gn_w[0], batch, n, ctx_len)
    na = _neighbourhood_attention(proj, cproj, na_rpb[0], batch, n, ctx_len)

    w_route = jnp.concatenate(
        [w_router_group[0], jnp.moveaxis(w_router_expert[0], 0, 1).reshape(d, n_experts)], axis=1)
    w_route = jnp.pad(w_route.astype(F32), ((0, 0), (0, LANES - n_groups - n_experts)))
    w_route_hi = w_route.astype(BF16)
    w_route = jnp.concatenate([w_route_hi, (w_route - w_route_hi.astype(F32)).astype(BF16)], axis=1)
    b_route = jnp.concatenate([b_router_group[0], b_router_expert[0].reshape(-1)])
    b_route = jnp.pad(b_route.astype(F32), (0, LANES - n_groups - n_experts)).reshape(1, LANES)
    x1, hf, route, counts = _out_projection(ret, na, w_out[0].astype(BF16), x2d, mod3, norm_ffn_w[0],
                                            w_route, b_route, n, n_groups, per_group, tm)

    te = EXPERT_TILE
    tokens = batch * n
    assert tokens < 1 << KEY_RANK_BITS
    n_tiles = (2 * tokens) // te + n_experts
    cnt = counts[0, n_groups:n_groups + n_experts].astype(jnp.int32)
    keys = route[:, R_KEY0:R_KEY1 + 1].astype(jnp.int32).reshape(-1)
    pos, slot_token, *tile_meta = _slot_map(keys, cnt, n_tiles, te)

    ys = _experts(tile_meta, slot_token, hf, w_gate[0], w_up[0], w_down[0], n_tiles, te)
    out = _combine(pos, ys, x1, route, mod3, final_norm_w, n, min(256, n))
    return out.reshape(batch, n, d)
```

```python
import functools

import jax
import jax.numpy as jnp
import numpy as np
from jax import lax
from jax.experimental import pallas as pl
from jax.experimental.pallas import tpu as pltpu

F32 = jnp.float32
BF16 = jnp.bfloat16

GRID_W = 64
RET_HEADS = 4
RET_HEAD_DIM = 256
RET_WIDTH = RET_HEADS * RET_HEAD_DIM
NA_HEADS = 8
NA_HEAD_DIM = 128
NA_WIDTH = NA_HEADS * NA_HEAD_DIM
RET_CHUNK = 128
NA_ROWS = 8
NA_COLS = 16
ROPE_BASE = 10000.0
N_MOD = 6
NORM_EPS = 1e-6
ROPE_HALF = RET_HEAD_DIM // 2
RET_UNROLL = 8

NA_QROWS = 16
NA_MASK = -1e30

LANES = 128
SUBLANES = 8
VMEM_LIMIT_BYTES = 56 * 1024 * 1024

R_KEY0, R_KEY1, R_W0, R_W1 = 0, 1, 2, 3
KEY_RANK_BITS = 16
KEY_RANK_SPAN = float(1 << KEY_RANK_BITS)

EXPERT_TILE = 256
WEIGHT_DMA_PRIORITY = 1


def _params(*sem):
    return pltpu.CompilerParams(dimension_semantics=sem, vmem_limit_bytes=VMEM_LIMIT_BYTES)


def _dot(a, b):
    return jnp.dot(a, b, preferred_element_type=F32)


def _dot_nt(a, b):
    return lax.dot_general(a, b, (((1,), (1,)), ((), ())), preferred_element_type=F32)


def _dot_tn(a, b):
    return lax.dot_general(a, b, (((0,), (0,)), ((), ())), preferred_element_type=F32)


def _rms(x, w):
    return x * lax.rsqrt(jnp.mean(x * x, axis=-1, keepdims=True) + NORM_EPS) * w


def _silu(x):
    return x * jax.nn.sigmoid(x)


def _store_rows(ref, val):
    for c in range(ref.shape[1]):
        ref[:, c, :] = val[:, c * LANES:(c + 1) * LANES]


def _row_pitch(chunks):
    return chunks + 1


def _start_row_gather(src_row_of, src_hbm, dst_buf, sem, n_rows):
    chunks = src_hbm.shape[1]

    def body(j, carry):
        dst = dst_buf.at[pl.ds(j * _row_pitch(chunks), chunks), :]
        pltpu.make_async_copy(src_hbm.at[src_row_of(j)], dst, sem).start()
        return carry

    lax.fori_loop(0, n_rows, body, 0, unroll=SUBLANES)


def _wait_row_gather(dst_buf, sem, n_rows, chunks):
    view = dst_buf.at[pl.ds(0, n_rows * chunks), :]
    pltpu.make_async_copy(view, view, sem).wait()


def _load_gathered(buf, n_rows, chunks):
    return jnp.concatenate([buf[pl.ds(c, n_rows, stride=_row_pitch(chunks)), :] for c in range(chunks)],
                           axis=1)


def _mod_kernel(c_ref, w_ref, b_ref, o_ref):
    a = _silu(c_ref[...]).astype(BF16)
    o_ref[...] = _dot(a, w_ref[...].astype(BF16)) + b_ref[...]


def _modulation(cc, w_mod, b_mod):
    rows, d = cc.shape
    width = w_mod.shape[1]
    tn = next(t for t in (1024, 512, 256, LANES) if width % t == 0)
    return pl.pallas_call(
        _mod_kernel,
        grid=(width // tn,),
        in_specs=[
            pl.BlockSpec((rows, d), lambda j: (0, 0)),
            pl.BlockSpec((d, tn), lambda j: (0, j)),
            pl.BlockSpec((1, tn), lambda j: (0, j)),
        ],
        out_specs=pl.BlockSpec((rows, tn), lambda j: (0, j)),
        out_shape=jax.ShapeDtypeStruct((rows, width), F32),
        compiler_params=_params("arbitrary"),
        name="modulation",
    )(cc, w_mod, b_mod.reshape(1, width))


def _inproj_kernel(x_ref, sh_ref, sc_ref, nw_ref, w_ref, o_ref, h_ref):
    @pl.when(pl.program_id(1) == 0)
    def _():
        y = _rms(x_ref[...], nw_ref[...])
        h_ref[...] = (y * (1.0 + sc_ref[0]) + sh_ref[0]).astype(BF16)

    o_ref[...] = _dot(h_ref[...], w_ref[...]).astype(BF16)


def _in_projection(x2d, mod3, mod_row_of_tile, norm_w, w_in, tm):
    m, d = x2d.shape
    width = w_in.shape[1]
    tn = 1024
    return pl.pallas_call(
        _inproj_kernel,
        grid=(m // tm, width // tn),
        in_specs=[
            pl.BlockSpec((tm, d), lambda i, j: (i, 0)),
            pl.BlockSpec((1, 1, d), lambda i, j: (mod_row_of_tile(i) * N_MOD + 0, 0, 0)),
            pl.BlockSpec((1, 1, d), lambda i, j: (mod_row_of_tile(i) * N_MOD + 1, 0, 0)),
            pl.BlockSpec((1, d), lambda i, j: (0, 0)),
            pl.BlockSpec((d, tn), lambda i, j: (0, j)),
        ],
        out_specs=pl.BlockSpec((tm, tn), lambda i, j: (i, j)),
        out_shape=jax.ShapeDtypeStruct((m, width), BF16),
        scratch_shapes=[pltpu.VMEM((tm, d), BF16)],
        compiler_params=_params("parallel", "arbitrary"),
        name="in_projection",
    )(x2d, mod3, mod3, norm_w.reshape(1, d), w_in)


def _ret_kernel(lg_ref, q_ref, k_ref, v_ref, g_ref, ck_ref, cv_ref, cosr_ref, sinr_ref,
                cosc_ref, sinc_ref, gnw_ref, o_ref, qr_s, kr_s, sfh_s, sbh_s, sf_s, sb_s):
    head = pl.program_id(1)
    lgf = lg_ref[0, head]
    lgb = lg_ref[1, head]
    n = q_ref.shape[0]
    c = RET_CHUNK
    nc = n // c
    ctx_len = ck_ref.shape[0]
    k_scale = RET_HEAD_DIM ** -0.5

    posl = lax.broadcasted_iota(jnp.int32, (ctx_len, 1), 0).astype(F32)
    ck = ck_ref[...].astype(F32) * k_scale
    cv = cv_ref[...]
    sf_s[...] = _dot_tn((ck * jnp.exp(lgf * (ctx_len - 1.0 - posl))).astype(BF16), cv)
    sb_s[...] = _dot_tn((ck * jnp.exp(lgb * posl)).astype(BF16), cv)

    cosc = cosc_ref[...]
    sinc = sinc_ref[...]
    pos = lax.broadcasted_iota(jnp.int32, (c, 1), 0).astype(F32)
    qdec_f = jnp.exp(lgf * (pos + 1.0))
    kdec_f = jnp.exp(lgf * (c - 1.0 - pos))
    cdec_f = jnp.exp(lgf * c)
    qdec_b = jnp.exp(lgb * (c - pos))
    kdec_b = jnp.exp(lgb * pos)
    cdec_b = jnp.exp(lgb * c)

    def rope(x, cosr, sinr):
        xa = x[:, :ROPE_HALF]
        xb = x[:, ROPE_HALF:]
        ya = xa * cosr + pltpu.roll(xa, ROPE_HALF // 2, 1) * sinr
        yb = xb * cosc + pltpu.roll(xb, ROPE_HALF // 2, 1) * sinc
        return jnp.concatenate([ya, yb], axis=1)

    def fwd_chunk(ci, carry):
        r0 = pl.multiple_of(ci * c, c)
        cosr = cosr_ref[pl.ds(r0, c), :]
        sinr = sinr_ref[pl.ds(r0, c), :]
        qr_s[pl.ds(r0, c), :] = rope(q_ref[pl.ds(r0, c), :].astype(F32), cosr, sinr).astype(BF16)
        k = rope(k_ref[pl.ds(r0, c), :].astype(F32), cosr, sinr) * k_scale
        kr_s[pl.ds(r0, c), :] = k.astype(BF16)
        kv = _dot_tn((k * kdec_f).astype(BF16), v_ref[pl.ds(r0, c), :])
        state = sf_s[...]
        sfh_s[ci] = state.astype(BF16)
        sf_s[...] = state * cdec_f + kv
        return carry

    lax.fori_loop(0, nc, fwd_chunk, 0, unroll=RET_UNROLL)

    def bwd_chunk(i, carry):
        ci = nc - 1 - i
        r0 = pl.multiple_of(ci * c, c)
        k = kr_s[pl.ds(r0, c), :].astype(F32)
        kv = _dot_tn((k * kdec_b).astype(BF16), v_ref[pl.ds(r0, c), :])
        state = sb_s[...]
        sbh_s[ci] = state.astype(BF16)
        sb_s[...] = state * cdec_b + kv
        return carry

    lax.fori_loop(0, nc, bwd_chunk, 0, unroll=RET_UNROLL)

    diff = (lax.broadcasted_iota(jnp.int32, (c, c), 0)
            - lax.broadcasted_iota(jnp.int32, (c, c), 1)).astype(F32)
    intra = (jnp.where(diff >= 0, jnp.exp(lgf * jnp.maximum(diff, 0.0)), 0.0)
             + jnp.where(diff <= 0, jnp.exp(lgb * jnp.maximum(-diff, 0.0)), 0.0))
    gnw = gnw_ref[...]

    def out_chunk(ci, carry):
        r0 = pl.multiple_of(ci * c, c)
        qb = qr_s[pl.ds(r0, c), :]
        kb = kr_s[pl.ds(r0, c), :]
        q = qb.astype(F32)
        scores = _dot_nt(qb, kb) * intra
        o = (_dot(scores.astype(BF16), v_ref[pl.ds(r0, c), :])
             + _dot((q * qdec_f).astype(BF16), sfh_s[ci])
             + _dot((q * qdec_b).astype(BF16), sbh_s[ci]))
        mu = jnp.mean(o, axis=-1, keepdims=True)
        d = o - mu
        var = jnp.mean(d * d, axis=-1, keepdims=True)
        on = d * lax.rsqrt(var + NORM_EPS) * gnw
        gate = _silu(g_ref[pl.ds(r0, c), :].astype(F32))
        o_ref[pl.ds(r0, c), :] = (on * gate).astype(BF16)
        return carry

    lax.fori_loop(0, nc, out_chunk, 0, unroll=RET_UNROLL)


def _rope_tables(n):
    inv = ROPE_BASE ** (-jnp.arange(0, ROPE_HALF, 2, dtype=F32) / ROPE_HALF)

    def tables(pos):
        ang = pos.astype(F32)[:, None] * inv[None, :]
        cos = jnp.cos(ang)
        sin = jnp.sin(ang)
        return jnp.concatenate([cos, cos], axis=1), jnp.concatenate([-sin, sin], axis=1)

    t = jnp.arange(n)
    cosr, sinr = tables(t // GRID_W)
    cosc, sinc = tables(jnp.arange(RET_CHUNK) % GRID_W)
    return cosr, sinr, cosc, sinc


def _retention(proj, cproj, lg, gn_w, batch, n, ctx_len):
    hd = RET_HEAD_DIM
    cosr, sinr, cosc, sinc = _rope_tables(n)
    col = lambda which: (lambda b, h, lg_ref: (b, which * RET_HEADS + h))
    const = lambda b, h, lg_ref: (0, 0)
    grid_spec = pltpu.PrefetchScalarGridSpec(
        num_scalar_prefetch=1,
        grid=(batch, RET_HEADS),
        in_specs=[
            pl.BlockSpec((n, hd), col(0)),
            pl.BlockSpec((n, hd), col(1)),
            pl.BlockSpec((n, hd), col(2)),
            pl.BlockSpec((n, hd), col(3)),
            pl.BlockSpec((ctx_len, hd), col(1)),
            pl.BlockSpec((ctx_len, hd), col(2)),
            pl.BlockSpec((n, ROPE_HALF), const),
            pl.BlockSpec((n, ROPE_HALF), const),
            pl.BlockSpec((RET_CHUNK, ROPE_HALF), const),
            pl.BlockSpec((RET_CHUNK, ROPE_HALF), const),
            pl.BlockSpec((1, hd), lambda b, h, lg_ref: (0, h)),
        ],
        out_specs=pl.BlockSpec((n, hd), lambda b, h, lg_ref: (b, h)),
        scratch_shapes=[
            pltpu.VMEM((n, hd), BF16),
            pltpu.VMEM((n, hd), BF16),
            pltpu.VMEM((n // RET_CHUNK, hd, hd), BF16),
            pltpu.VMEM((n // RET_CHUNK, hd, hd), BF16),
            pltpu.VMEM((hd, hd), F32),
            pltpu.VMEM((hd, hd), F32),
        ],
    )
    return pl.pallas_call(
        _ret_kernel,
        grid_spec=grid_spec,
        out_shape=jax.ShapeDtypeStruct((batch * n, RET_WIDTH), BF16),
        compiler_params=_params("parallel", "arbitrary"),
        name="retention",
    )(lg, proj, proj, proj, proj, cproj, cproj, cosr, sinr, cosc, sinc, gn_w.reshape(1, RET_WIDTH))


def _na_kernel(q_ref, k_ref, v_ref, kc_ref, vc_ref, bias_ref, o_ref, s_s, m_s):
    n = q_ref.shape[0]
    rows = n // GRID_W
    kb = NA_ROWS * GRID_W
    ctx_len = kc_ref.shape[0]
    scale = NA_HEAD_DIM ** -0.5
    kc = kc_ref[...]
    vc = jnp.concatenate([vc_ref[...], jnp.ones((ctx_len, NA_HEAD_DIM), BF16)], axis=1)
    ones_w = jnp.ones((kb, NA_HEAD_DIM), BF16)

    def window(r):
        rs = jnp.clip(r - NA_ROWS // 2, 0, rows - NA_ROWS)
        return rs, pl.multiple_of(r * GRID_W, GRID_W), pl.multiple_of(rs * GRID_W, GRID_W)

    def score_row(r):
        rs, q0, k0 = window(r)
        q = (q_ref[pl.ds(q0, GRID_W), :].astype(F32) * scale).astype(BF16)
        s_win = _dot_nt(q, k_ref[pl.ds(k0, kb), :]) + bias_ref[0, rs - r + NA_ROWS - 1]
        s_ctx = _dot_nt(q, kc)
        m = jnp.maximum(jnp.max(s_win, axis=-1, keepdims=True), jnp.max(s_ctx, axis=-1, keepdims=True))
        s_s[pl.ds(q0, GRID_W), :kb] = s_win
        s_s[pl.ds(q0, GRID_W), kb:] = s_ctx
        m_s[pl.ds(q0, GRID_W), :] = jnp.broadcast_to(m, (GRID_W, LANES))

    def value_row(r):
        _, q0, k0 = window(r)
        m = m_s[pl.ds(q0, GRID_W), :]
        p_win = jnp.exp(s_s[pl.ds(q0, GRID_W), :kb] - jnp.tile(m, (1, kb // LANES))).astype(BF16)
        p_ctx = jnp.exp(s_s[pl.ds(q0, GRID_W), kb:] - jnp.tile(m, (1, ctx_len // LANES))).astype(BF16)
        vw = jnp.concatenate([v_ref[pl.ds(k0, kb), :], ones_w], axis=1)
        o = _dot(p_win, vw) + _dot(p_ctx, vc)
        o_ref[pl.ds(q0, GRID_W), :] = (o[:, :NA_HEAD_DIM] / o[:, NA_HEAD_DIM:]).astype(BF16)

    def sweep(row_fn):
        def group(g, carry):
            for u in range(NA_QROWS):
                row_fn(g * NA_QROWS + u)
            return carry
        lax.fori_loop(0, rows // NA_QROWS, group, 0)

    sweep(score_row)
    sweep(value_row)


def _na_bias(rpb):
    heads, n_roff, n_coff = rpb.shape
    n_off = n_roff - NA_ROWS + 1
    c = np.arange(GRID_W)
    cs = np.clip(c - NA_COLS // 2, 0, GRID_W - NA_COLS)
    vcol = (c[None, :] >= cs[:, None]) & (c[None, :] < cs[:, None] + NA_COLS)
    seg = 2 * GRID_W
    padded = jnp.pad(rpb, ((0, 0), (0, 0), (0, seg - n_coff)))
    vec = jnp.stack([padded[:, off:off + NA_ROWS] for off in range(n_off)], axis=1)
    period = NA_ROWS * seg
    vec = vec.reshape(heads, n_off, period)
    cut = jnp.tile(vec, (1, 1, GRID_W))[..., :GRID_W * (period - 1)]
    cut = jnp.pad(cut.reshape(heads, n_off, GRID_W, period - 1), ((0, 0), (0, 0), (0, 0), (0, 1)))
    bias = cut.reshape(heads, n_off, GRID_W, NA_ROWS, seg)[..., NA_COLS - 1:NA_COLS - 1 + GRID_W]
    bias = jnp.where(vcol[None, None, :, None, :], bias, NA_MASK)
    return bias.reshape(heads, n_off, GRID_W, NA_ROWS * GRID_W)


def _neighbourhood_attention(proj, cproj, rpb, batch, n, ctx_len):
    hd = NA_HEAD_DIM
    rows = n // GRID_W
    assert rows % NA_QROWS == 0 and rows >= NA_ROWS
    bias = _na_bias(rpb.astype(F32))
    base = 4 * RET_WIDTH // hd
    col = lambda which: (lambda b, h: (b, base + which * NA_HEADS + h))
    return pl.pallas_call(
        _na_kernel,
        grid=(batch, NA_HEADS),
        in_specs=[
            pl.BlockSpec((n, hd), col(0)),
            pl.BlockSpec((n, hd), col(1)),
            pl.BlockSpec((n, hd), col(2)),
            pl.BlockSpec((ctx_len, hd), col(1)),
            pl.BlockSpec((ctx_len, hd), col(2)),
            pl.BlockSpec((1,) + bias.shape[1:], lambda b, h: (h, 0, 0, 0)),
        ],
        out_specs=pl.BlockSpec((n, hd), lambda b, h: (b, h)),
        out_shape=jax.ShapeDtypeStruct((batch * n, NA_WIDTH), BF16),
        scratch_shapes=[pltpu.VMEM((n, NA_ROWS * GRID_W + ctx_len), F32), pltpu.VMEM((n, LANES), F32)],
        compiler_params=_params("parallel", "arbitrary"),
        name="neighbourhood_attention",
    )(proj, proj, proj, cproj, cproj, bias)


def _outproj_kernel(n_groups, per_group, ret_ref, na_ref, w1_ref, w2_ref, x_ref, ga_ref, shf_ref,
                    scf_ref, nw_ref, wr_ref, br_ref, x1_ref, hf_ref, route_ref, cnt_ref, carry_s):
    @pl.when(pl.program_id(0) == 0)
    def _():
        carry_s[...] = jnp.zeros_like(carry_s)

    acc = _dot(ret_ref[...], w1_ref[...]) + _dot(na_ref[...], w2_ref[...])
    x1 = x_ref[...] + ga_ref[0] * acc
    x1_ref[...] = x1
    hf = _rms(x1, nw_ref[...]) * (1.0 + scf_ref[0]) + shf_ref[0]
    _store_rows(hf_ref, hf)

    hf_hi = hf.astype(BF16)
    hf_lo = (hf - hf_hi.astype(F32)).astype(BF16)
    p_hi = _dot(hf_hi, wr_ref[...])
    p_lo = _dot(hf_lo, wr_ref[...])
    logits = p_hi[:, :LANES] + p_hi[:, LANES:] + p_lo[:, :LANES] + br_ref[...]
    tm = logits.shape[0]
    lane = lax.broadcasted_iota(jnp.int32, (tm, LANES), 1)
    neg = -jnp.inf

    def first_max(vals):
        top = jnp.max(vals, axis=-1, keepdims=True)
        idx = jnp.min(jnp.where(vals == top, lane, LANES), axis=-1, keepdims=True)
        return top, idx

    g_logits = jnp.where(lane < n_groups, logits, neg)
    g_top, g_sel = first_max(g_logits)
    g_w = 1.0 / jnp.sum(jnp.exp(g_logits - g_top), axis=-1, keepdims=True)
    lo = n_groups + g_sel * per_group
    e_logits = jnp.where((lane >= lo) & (lane < lo + per_group), logits, neg)
    v0, i0 = first_max(e_logits)
    v1, i1 = first_max(jnp.where(lane == i0, neg, e_logits))
    e1 = jnp.exp(v1 - v0)
    w0 = g_w / (1.0 + e1)
    w1 = g_w * e1 / (1.0 + e1)

    hit0 = lane == i0
    hit1 = lane == i1
    onehot = jnp.where(hit0 | hit1, 1.0, 0.0)
    earlier = jnp.where(lax.broadcasted_iota(jnp.int32, (tm, tm), 0)
                        > lax.broadcasted_iota(jnp.int32, (tm, tm), 1), 1.0, 0.0).astype(BF16)
    before = _dot(earlier, onehot.astype(BF16)) + carry_s[...]
    rank0 = jnp.sum(jnp.where(hit0, before, 0.0), axis=-1, keepdims=True)
    rank1 = jnp.sum(jnp.where(hit1, before, 0.0), axis=-1, keepdims=True)
    carry_s[...] = carry_s[...] + jnp.sum(onehot, axis=0, keepdims=True)
    cnt_ref[...] = carry_s[...]

    key0 = (i0 - n_groups).astype(F32) * KEY_RANK_SPAN + rank0
    key1 = (i1 - n_groups).astype(F32) * KEY_RANK_SPAN + rank1
    fields = {R_KEY0: key0, R_KEY1: key1, R_W0: w0, R_W1: w1}
    route = jnp.zeros((tm, LANES), F32)
    for idx, val in fields.items():
        route = jnp.where(lane == idx, val, route)
    route_ref[...] = route


def _out_projection(ret, na, w_out, x2d, mod3, norm_w, w_route, b_route, n, n_groups, per_group, tm):
    m, d = x2d.shape
    chunks = d // LANES
    batch_of = lambda i: (i * tm) // n
    mod_spec = lambda which: pl.BlockSpec((1, 1, d), lambda i: (batch_of(i) * N_MOD + which, 0, 0))
    const2 = lambda i: (0, 0)
    return pl.pallas_call(
        functools.partial(_outproj_kernel, n_groups, per_group),
        grid=(m // tm,),
        in_specs=[
            pl.BlockSpec((tm, RET_WIDTH), lambda i: (i, 0)),
            pl.BlockSpec((tm, NA_WIDTH), lambda i: (i, 0)),
            pl.BlockSpec((RET_WIDTH, d), lambda i: (0, 0)),
            pl.BlockSpec((NA_WIDTH, d), lambda i: (RET_WIDTH // NA_WIDTH, 0)),
            pl.BlockSpec((tm, d), lambda i: (i, 0)),
            mod_spec(2), mod_spec(3), mod_spec(4),
            pl.BlockSpec((1, d), const2),
            pl.BlockSpec((d, 2 * LANES), const2),
            pl.BlockSpec((1, LANES), const2),
        ],
        out_specs=[
            pl.BlockSpec((tm, d), lambda i: (i, 0)),
            pl.BlockSpec((tm, chunks, LANES), lambda i: (i, 0, 0)),
            pl.BlockSpec((tm, LANES), lambda i: (i, 0)),
            pl.BlockSpec((1, LANES), const2),
        ],
        out_shape=[
            jax.ShapeDtypeStruct((m, d), F32),
            jax.ShapeDtypeStruct((m, chunks, LANES), F32),
            jax.ShapeDtypeStruct((m, LANES), F32),
            jax.ShapeDtypeStruct((1, LANES), F32),
        ],
        scratch_shapes=[pltpu.VMEM((1, LANES), F32)],
        compiler_params=_params("arbitrary"),
        name="out_projection_router",
    )(ret, na, w_out, w_out, x2d, mod3, mod3, mod3, norm_w.reshape(1, d), w_route, b_route)


def _slot_map_kernel(tile_rows, key_ref, cnt_ref, pos_ref, slot_ref, tile_expert_ref, next_expert_ref,
                     run_parity_ref, n_used_ref, offs_s):
    n_experts = cnt_ref.shape[0]
    n_tiles = tile_expert_ref.shape[0]
    shift = tile_rows.bit_length() - 1
    assert 1 << shift == tile_rows

    def fill(ref, lo, hi, val):
        value_at = val if callable(val) else (lambda s: val)
        groups = lax.shift_right_logical(hi - lo, 3)

        def group(g, carry):
            for k in range(SUBLANES):
                s = lo + g * SUBLANES + k
                ref[s] = value_at(s)
            return carry

        def single(s, carry):
            ref[s] = value_at(s)
            return carry

        lax.fori_loop(0, groups, group, 0)
        lax.fori_loop(lo + groups * SUBLANES, hi, single, 0)

    n_tokens = key_ref.shape[0] // 2
    padding_row = lambda s: lax.rem(s, n_tokens)

    def tiles_of(e):
        size = lax.shift_left(lax.shift_right_logical(cnt_ref[e] + (tile_rows - 1), shift), shift)
        return lax.shift_right_logical(offs_s[e], shift), lax.shift_right_logical(offs_s[e] + size, shift)

    def per_expert(e, carry):
        start, runs = carry
        cnt = cnt_ref[e]
        size = lax.shift_left(lax.shift_right_logical(cnt + (tile_rows - 1), shift), shift)
        offs_s[e] = start
        t0, t1 = tiles_of(e)
        fill(tile_expert_ref, t0, t1, e)
        fill(run_parity_ref, t0, t1, jnp.bitwise_and(runs, 1))
        fill(slot_ref, start + cnt, start + size, padding_row)
        return start + size, runs + (size > 0).astype(jnp.int32)

    end, _ = lax.fori_loop(0, n_experts, per_expert, (0, 0))
    n_used = lax.shift_right_logical(end, shift)
    n_used_ref[0] = n_used
    fill(tile_expert_ref, n_used, n_tiles, n_experts - 1)
    fill(run_parity_ref, n_used, n_tiles, 0)
    fill(next_expert_ref, n_used, n_tiles, -1)
    fill(slot_ref, end, slot_ref.shape[0], padding_row)

    def per_expert_reversed(k, following):
        e = n_experts - 1 - k
        t0, t1 = tiles_of(e)
        fill(next_expert_ref, t0, t1, following)
        return jnp.where(t1 > t0, e, following)

    lax.fori_loop(0, n_experts, per_expert_reversed, -1)

    def assign(a, carry):
        key = key_ref[a]
        p = offs_s[lax.shift_right_logical(key, KEY_RANK_BITS)] + jnp.bitwise_and(key, (1 << KEY_RANK_BITS) - 1)
        pos_ref[a] = p
        slot_ref[p] = lax.shift_right_logical(a, 1)
        return carry

    lax.fori_loop(0, key_ref.shape[0], assign, 0, unroll=SUBLANES)


def _slot_map(keys, cnt, n_tiles, tile_rows):
    smem = pl.BlockSpec(memory_space=pltpu.SMEM)
    i32 = lambda *shape: jax.ShapeDtypeStruct(shape, jnp.int32)
    return pl.pallas_call(
        functools.partial(_slot_map_kernel, tile_rows),
        in_specs=[smem, smem],
        out_specs=[smem] * 6,
        out_shape=[i32(keys.shape[0]), i32(n_tiles * tile_rows), i32(n_tiles), i32(n_tiles),
                   i32(n_tiles), i32(1)],
        scratch_shapes=[pltpu.SMEM((cnt.shape[0],), jnp.int32)],
        name="slot_map",
    )(keys, cnt)


def _expert_kernel(tile_expert_ref, next_expert_ref, run_parity_ref, n_used_ref, tokens_ref,
                   next_tokens_ref, hf_ref, wg_hbm, wu_hbm, wd_hbm, o_ref, xbuf, gsem, wg_f, wu_f,
                   wd_f, wsem, wg_s, wu_s, wd_s):
    i = pl.program_id(0)
    n_used = n_used_ref[0]
    tm, chunks, _ = o_ref.shape

    def start_tile(ids_ref, slot):
        _start_row_gather(lambda j: ids_ref[0, 0, j], hf_ref, xbuf.at[slot], gsem.at[slot], tm)

    def weight_copies(expert, slot):
        return [pltpu.make_async_copy(hbm.at[expert], stage.at[slot], wsem.at[slot])
                for hbm, stage in ((wg_hbm, wg_f), (wu_hbm, wu_f), (wd_hbm, wd_f))]

    @pl.when((i == 0) & (n_used > 0))
    def _():
        start_tile(tokens_ref, 0)
        for cp in weight_copies(tile_expert_ref[0], run_parity_ref[0]):
            cp.start(priority=WEIGHT_DMA_PRIORITY)

    @pl.when(i + 1 < n_used)
    def _():
        start_tile(next_tokens_ref, (i + 1) % 2)

    @pl.when(i < n_used)
    def _():
        @pl.when((i == 0) | (tile_expert_ref[i] != tile_expert_ref[jnp.maximum(i - 1, 0)]))
        def _():
            stage = run_parity_ref[i]
            for cp in weight_copies(tile_expert_ref[i], stage):
                cp.wait()

            @pl.when(next_expert_ref[i] >= 0)
            def _():
                for cp in weight_copies(next_expert_ref[i], 1 - stage):
                    cp.start(priority=WEIGHT_DMA_PRIORITY)

            wg_s[...] = wg_f[stage].astype(BF16)
            wu_s[...] = wu_f[stage].astype(BF16)
            wd_s[...] = wd_f[stage].astype(BF16)

        slot = i % 2
        _wait_row_gather(xbuf.at[slot], gsem.at[slot], tm, chunks)
        x = _load_gathered(xbuf.at[slot], tm, chunks).astype(BF16)
        a = (_silu(_dot(x, wg_s[...])) * _dot(x, wu_s[...])).astype(BF16)
        _store_rows(o_ref, _dot(a, wd_s[...]))

    @pl.when(i >= n_used)
    def _():
        o_ref[...] = jnp.zeros(o_ref.shape, o_ref.dtype)


def _experts(tile_meta, slot_token, hf, w_gate, w_up, w_down, n_tiles, tm):
    chunks = hf.shape[1]
    d = chunks * LANES
    ff = w_gate.shape[-1]
    slot_tiles = slot_token.reshape(n_tiles, 1, tm)
    any_space = pl.BlockSpec(memory_space=pl.ANY)
    ids_block = lambda ahead: pl.BlockSpec(
        (1, 1, tm), lambda i, *prefetch: (jnp.minimum(i + ahead, n_tiles - 1), 0, 0),
        memory_space=pltpu.SMEM)
    grid_spec = pltpu.PrefetchScalarGridSpec(
        num_scalar_prefetch=4,
        grid=(n_tiles,),
        in_specs=[ids_block(0), ids_block(1), any_space, any_space, any_space, any_space],
        out_specs=pl.BlockSpec((tm, chunks, LANES), lambda i, *prefetch: (i, 0, 0)),
        scratch_shapes=[
            pltpu.VMEM((2, tm * _row_pitch(chunks), LANES), hf.dtype),
            pltpu.SemaphoreType.DMA((2,)),
            pltpu.VMEM((2, d, ff), F32),
            pltpu.VMEM((2, d, ff), F32),
            pltpu.VMEM((2, ff, d), F32),
            pltpu.SemaphoreType.DMA((2,)),
            pltpu.VMEM((d, ff), BF16),
            pltpu.VMEM((d, ff), BF16),
            pltpu.VMEM((ff, d), BF16),
        ],
    )
    return pl.pallas_call(
        _expert_kernel,
        grid_spec=grid_spec,
        out_shape=jax.ShapeDtypeStruct((n_tiles * tm, chunks, LANES), hf.dtype),
        compiler_params=_params("arbitrary"),
        name="routed_experts",
    )(*tile_meta, slot_tiles, slot_tiles, hf, w_gate, w_up, w_down)


def _combine_kernel(pos_ref, ys_ref, x1_ref, route_ref, gf_ref, fw_ref, o_ref, ybuf, sem):
    i = pl.program_id(0)
    n_tiles = pl.num_programs(0)
    tm = x1_ref.shape[0]
    chunks = ys_ref.shape[1]

    def start_tile(tile, slot):
        for choice in range(2):
            _start_row_gather(lambda j: pos_ref[(tile * tm + j) * 2 + choice], ys_ref,
                              ybuf.at[slot, choice], sem.at[slot, choice], tm)

    @pl.when(i == 0)
    def _():
        start_tile(0, 0)

    @pl.when(i + 1 < n_tiles)
    def _():
        start_tile(i + 1, (i + 1) % 2)

    slot = i % 2
    for choice in range(2):
        _wait_row_gather(ybuf.at[slot, choice], sem.at[slot, choice], tm, chunks)
    route = route_ref[...]
    moe = (route[:, R_W0:R_W0 + 1] * _load_gathered(ybuf.at[slot, 0], tm, chunks)
           + route[:, R_W1:R_W1 + 1] * _load_gathered(ybuf.at[slot, 1], tm, chunks))
    x2 = x1_ref[...] + gf_ref[0] * moe
    o_ref[...] = _rms(x2, fw_ref[...])


def _combine(pos, ys, x1, route, mod3, final_w, n, tm):
    m, d = x1.shape
    chunks = ys.shape[1]
    grid_spec = pltpu.PrefetchScalarGridSpec(
        num_scalar_prefetch=1,
        grid=(m // tm,),
        in_specs=[
            pl.BlockSpec(memory_space=pl.ANY),
            pl.BlockSpec((tm, d), lambda i, pos_ref: (i, 0)),
            pl.BlockSpec((tm, LANES), lambda i, pos_ref: (i, 0)),
            pl.BlockSpec((1, 1, d), lambda i, pos_ref: (((i * tm) // n) * N_MOD + 5, 0, 0)),
            pl.BlockSpec((1, d), lambda i, pos_ref: (0, 0)),
        ],
        out_specs=pl.BlockSpec((tm, d), lambda i, pos_ref: (i, 0)),
        scratch_shapes=[pltpu.VMEM((2, 2, tm * _row_pitch(chunks), LANES), ys.dtype),
                        pltpu.SemaphoreType.DMA((2, 2))],
    )
    return pl.pallas_call(
        _combine_kernel,
        grid_spec=grid_spec,
        out_shape=jax.ShapeDtypeStruct((m, d), F32),
        compiler_params=_params("arbitrary"),
        name="combine_final_norm",
    )(pos, ys, x1, route, mod3, final_w.reshape(1, d))


def kernel(x, c, ctx, c_ctx, w_mod, b_mod, norm_mix_w, w_in, ret_decay_f, ret_decay_b, ret_gn_w, na_rpb, w_out, norm_ffn_w, w_router_group, b_router_group, w_router_expert, b_router_expert, w_gate, w_up, w_down, final_norm_w):
    assert w_mod.shape[0] == 1, "single trunk layer"
    batch, n, d = x.shape
    ctx_len = ctx.shape[1]
    n_groups = w_router_group.shape[-1]
    per_group = w_router_expert.shape[-1]
    n_experts = w_gate.shape[1]
    assert n_groups * per_group == n_experts and n_groups + n_experts <= LANES

    mod_rows = 8
    cc = jnp.zeros((mod_rows, d), F32).at[:batch].set(c).at[batch].set(c_ctx)
    mod = _modulation(cc, w_mod[0], b_mod[0])
    mod3 = mod.reshape(mod_rows * N_MOD, 1, d)

    w_in_b = w_in[0].astype(BF16)
    tm = min(512, n)
    tm_in = min(1024, n)
    x2d = x.reshape(batch * n, d)
    proj = _in_projection(x2d, mod3, lambda i: (i * tm_in) // n, norm_mix_w[0], w_in_b, tm_in)
    cproj = _in_projection(ctx.reshape(batch * ctx_len, d), mod3, lambda i: batch, norm_mix_w[0],
                           w_in_b, ctx_len)

    lg = jnp.stack([jax.nn.log_sigmoid(ret_decay_f[0].astype(F32)),
                    jax.nn.log_sigmoid(ret_decay_b[0].astype(F32))])
    ret = _retention(proj, cproj, lg, ret_gn_w[0], batch, n, ctx_len)
    na = _neighbourhood_attention(proj, cproj, na_rpb[0], batch, n, ctx_len)

    w_route = jnp.concatenate(
        [w_router_group[0], jnp.moveaxis(w_router_expert[0], 0, 1).reshape(d, n_experts)], axis=1)
    w_route = jnp.pad(w_route.astype(F32), ((0, 0), (0, LANES - n_groups - n_experts)))
    w_route_hi = w_route.astype(BF16)
    w_route = jnp.concatenate([w_route_hi, (w_route - w_route_hi.astype(F32)).astype(BF16)], axis=1)
    b_route = jnp.concatenate([b_router_group[0], b_router_expert[0].reshape(-1)])
    b_route = jnp.pad(b_route.astype(F32), (0, LANES - n_groups - n_experts)).reshape(1, LANES)
    x1, hf, route, counts = _out_projection(ret, na, w_out[0].astype(BF16), x2d, mod3, norm_ffn_w[0],
                                            w_route, b_route, n, n_groups, per_group, tm)

    te = EXPERT_TILE
    tokens = batch * n
    assert tokens < 1 << KEY_RANK_BITS
    n_tiles = (2 * tokens) // te + n_experts
    cnt = counts[0, n_groups:n_groups + n_experts].astype(jnp.int32)
    keys = route[:, R_KEY0:R_KEY1 + 1].astype(jnp.int32).reshape(-1)
    pos, slot_token, *tile_meta = _slot_map(keys, cnt, n_tiles, te)

    ys = _experts(tile_meta, slot_token, hf, w_gate[0], w_up[0], w_down[0], n_tiles, te)
    out = _combine(pos, ys, x1, route, mod3, final_norm_w, n, min(256, n))
    return out.reshape(batch, n, d)
```

```python
import functools

import jax
import jax.numpy as jnp
from jax import lax
from jax.experimental import pallas as pl
from jax.experimental.pallas import tpu as pltpu

F32 = jnp.float32
BF16 = jnp.bfloat16

GRID_W = 64
RET_HEADS = 4
RET_HEAD_DIM = 256
RET_WIDTH = RET_HEADS * RET_HEAD_DIM
NA_HEADS = 8
NA_HEAD_DIM = 128
NA_WIDTH = NA_HEADS * NA_HEAD_DIM
RET_CHUNK = 128
NA_ROWS = 8
NA_COLS = 16
ROPE_BASE = 10000.0
N_MOD = 6
NORM_EPS = 1e-6
ROPE_HALF = RET_HEAD_DIM // 2
RET_UNROLL = 8

INPROJ_TN = 1024
assert INPROJ_TN == RET_WIDTH == NA_WIDTH
CTX_COL_BLOCKS = (1, 2, 5, 6)
CTX_RET_K, CTX_RET_V, CTX_NA_K, CTX_NA_V = 0, 1, 2, 3

NA_QROWS = 16
NA_MASK = -1e30

LANES = 128
SUBLANES = 8
VMEM_LIMIT_BYTES = 56 * 1024 * 1024

R_KEY0, R_KEY1, R_W0, R_W1 = 0, 1, 2, 3
KEY_RANK_BITS = 16
KEY_RANK_SPAN = float(1 << KEY_RANK_BITS)

EXPERT_TILE = 256
WEIGHT_DMA_PRIORITY = 1
EXPERT_DOWN_SPLIT = 4


def _params(*sem):
    return pltpu.CompilerParams(dimension_semantics=sem, vmem_limit_bytes=VMEM_LIMIT_BYTES)


def _dot(a, b):
    return jnp.dot(a, b, preferred_element_type=F32)


def _dot_nt(a, b):
    return lax.dot_general(a, b, (((1,), (1,)), ((), ())), preferred_element_type=F32)


def _dot_tn(a, b):
    return lax.dot_general(a, b, (((0,), (0,)), ((), ())), preferred_element_type=F32)


def _rms(x, w):
    return x * lax.rsqrt(jnp.mean(x * x, axis=-1, keepdims=True) + NORM_EPS) * w


def _silu(x):
    return x * jax.nn.sigmoid(x)


def _store_rows(ref, val):
    for c in range(ref.shape[1]):
        ref[:, c, :] = val[:, c * LANES:(c + 1) * LANES]


def _tile_store(buf, val):
    for c in range(buf.shape[1]):
        buf[:, c] = val[:, c * LANES:(c + 1) * LANES].reshape(buf.shape[0], SUBLANES, LANES)


def _row_major_copies(buf, hbm, first_row_tile, sem):
    return [pltpu.make_async_copy(buf.at[:, :, s, :], hbm.at[pl.ds(first_row_tile, buf.shape[0]), s], sem)
            for s in range(SUBLANES)]


def _write_rows_pipelined(buf2, hbm, sem2, val):
    i = pl.program_id(0)
    last = pl.num_programs(0) - 1
    slot = i % 2
    tiles = buf2.shape[1]

    def copies(step, s):
        return _row_major_copies(buf2.at[s], hbm, step * tiles, sem2.at[s])

    @pl.when(i >= 2)
    def _():
        for cp in copies(i - 2, slot):
            cp.wait()

    _tile_store(buf2.at[slot], val)
    for cp in copies(i, slot):
        cp.start()

    @pl.when(i == last)
    def _():
        for cp in copies(i, slot):
            cp.wait()

    @pl.when((i == last) & (i >= 1))
    def _():
        for cp in copies(i - 1, 1 - slot):
            cp.wait()


def _row_pitch(chunks):
    return chunks + 1


def _start_row_gather(src_row, chunks, dst_buf, sem, n_rows):
    def body(j, carry):
        dst = dst_buf.at[pl.ds(j * _row_pitch(chunks), chunks), :]
        pltpu.make_async_copy(src_row(j), dst, sem).start()
        return carry

    lax.fori_loop(0, n_rows, body, 0, unroll=SUBLANES)


def _wait_row_gather(dst_buf, sem, n_rows, chunks):
    view = dst_buf.at[pl.ds(0, n_rows * chunks), :]
    pltpu.make_async_copy(view, view, sem).wait()


def _load_gathered(buf, n_rows, chunks):
    return jnp.concatenate([buf[pl.ds(c, n_rows, stride=_row_pitch(chunks)), :] for c in range(chunks)],
                           axis=1)


def _mod_kernel(c_ref, w_ref, b_ref, o_ref):
    a = _silu(c_ref[...]).astype(BF16)
    o_ref[...] = _dot(a, w_ref[...].astype(BF16)) + b_ref[...]


def _modulation(cc, w_mod, b_mod):
    rows, d = cc.shape
    width = w_mod.shape[1]
    tn = next(t for t in (1024, 512, 256, LANES) if width % t == 0)
    return pl.pallas_call(
        _mod_kernel,
        grid=(width // tn,),
        in_specs=[
            pl.BlockSpec((rows, d), lambda j: (0, 0)),
            pl.BlockSpec((d, tn), lambda j: (0, j)),
            pl.BlockSpec((1, tn), lambda j: (0, j)),
        ],
        out_specs=pl.BlockSpec((rows, tn), lambda j: (0, j)),
        out_shape=jax.ShapeDtypeStruct((rows, width), F32),
        compiler_params=_params("arbitrary"),
        name="modulation",
    )(cc, w_mod, b_mod.reshape(1, width))


def _inproj_kernel(x_ref, sh_ref, sc_ref, nw_ref, w_ref, o_ref, h_ref):
    @pl.when(pl.program_id(1) == 0)
    def _():
        y = _rms(x_ref[...], nw_ref[...])
        h_ref[...] = (y * (1.0 + sc_ref[0]) + sh_ref[0]).astype(BF16)

    o_ref[...] = _dot(h_ref[...], w_ref[...]).astype(BF16)


def _in_projection(x2d, mod3, mod_row_of_tile, norm_w, w_in, tm, col_blocks=None):
    m, d = x2d.shape
    tn = INPROJ_TN
    if col_blocks is None:
        col_blocks = tuple(range(w_in.shape[1] // tn))
    n_blocks = len(col_blocks)
    w_block = lambda j: sum(jnp.where(j == k, blk, 0) for k, blk in enumerate(col_blocks))
    return pl.pallas_call(
        _inproj_kernel,
        grid=(m // tm, n_blocks),
        in_specs=[
            pl.BlockSpec((tm, d), lambda i, j: (i, 0)),
            pl.BlockSpec((1, 1, d), lambda i, j: (mod_row_of_tile(i) * N_MOD + 0, 0, 0)),
            pl.BlockSpec((1, 1, d), lambda i, j: (mod_row_of_tile(i) * N_MOD + 1, 0, 0)),
            pl.BlockSpec((1, d), lambda i, j: (0, 0)),
            pl.BlockSpec((d, tn), lambda i, j: (0, w_block(j))),
        ],
        out_specs=pl.BlockSpec((tm, tn), lambda i, j: (i, j)),
        out_shape=jax.ShapeDtypeStruct((m, n_blocks * tn), BF16),
        scratch_shapes=[pltpu.VMEM((tm, d), BF16)],
        compiler_params=_params("parallel", "arbitrary"),
        name="in_projection",
    )(x2d, mod3, mod3, norm_w.reshape(1, d), w_in)


def _ret_kernel(lg_ref, q_ref, k_ref, v_ref, g_ref, ck_ref, cv_ref, cosr_ref, sinr_ref,
                cosc_ref, sinc_ref, gnw_ref, o_ref, qr_s, kr_s, sfh_s, sbh_s, sf_s, sb_s):
    head = pl.program_id(1)
    lgf = lg_ref[0, head]
    lgb = lg_ref[1, head]
    n = q_ref.shape[0]
    c = RET_CHUNK
    nc = n // c
    ctx_len = ck_ref.shape[0]
    k_scale = RET_HEAD_DIM ** -0.5

    posl = lax.broadcasted_iota(jnp.int32, (ctx_len, 1), 0).astype(F32)
    ck = ck_ref[...].astype(F32) * k_scale
    cv = cv_ref[...]
    sf_s[...] = _dot_tn((ck * jnp.exp(lgf * (ctx_len - 1.0 - posl))).astype(BF16), cv)
    sb_s[...] = _dot_tn((ck * jnp.exp(lgb * posl)).astype(BF16), cv)

    cosc = cosc_ref[...]
    sinc = sinc_ref[...]
    pos = lax.broadcasted_iota(jnp.int32, (c, 1), 0).astype(F32)
    qdec_f = jnp.exp(lgf * (pos + 1.0))
    kdec_f = jnp.exp(lgf * (c - 1.0 - pos))
    cdec_f = jnp.exp(lgf * c)
    qdec_b = jnp.exp(lgb * (c - pos))
    kdec_b = jnp.exp(lgb * pos)
    cdec_b = jnp.exp(lgb * c)

    def rope(x, cosr, sinr):
        xa = x[:, :ROPE_HALF]
        xb = x[:, ROPE_HALF:]
        ya = xa * cosr + pltpu.roll(xa, ROPE_HALF // 2, 1) * sinr
        yb = xb * cosc + pltpu.roll(xb, ROPE_HALF // 2, 1) * sinc
        return jnp.concatenate([ya, yb], axis=1)

    def fwd_chunk(ci, carry):
        r0 = pl.multiple_of(ci * c, c)
        cosr = cosr_ref[pl.ds(r0, c), :]
        sinr = sinr_ref[pl.ds(r0, c), :]
        qr_s[pl.ds(r0, c), :] = rope(q_ref[pl.ds(r0, c), :].astype(F32), cosr, sinr).astype(BF16)
        k = rope(k_ref[pl.ds(r0, c), :].astype(F32), cosr, sinr) * k_scale
        kr_s[pl.ds(r0, c), :] = k.astype(BF16)
        kv = _dot_tn((k * kdec_f).astype(BF16), v_ref[pl.ds(r0, c), :])
        state = sf_s[...]
        sfh_s[ci] = state.astype(BF16)
        sf_s[...] = state * cdec_f + kv
        return carry

    lax.fori_loop(0, nc, fwd_chunk, 0, unroll=RET_UNROLL)

    def bwd_chunk(i, carry):
        ci = nc - 1 - i
        r0 = pl.multiple_of(ci * c, c)
        k = kr_s[pl.ds(r0, c), :].astype(F32)
        kv = _dot_tn((k * kdec_b).astype(BF16), v_ref[pl.ds(r0, c), :])
        state = sb_s[...]
        sbh_s[ci] = state.astype(BF16)
        sb_s[...] = state * cdec_b + kv
        return carry

    lax.fori_loop(0, nc, bwd_chunk, 0, unroll=RET_UNROLL)

    diff = (lax.broadcasted_iota(jnp.int32, (c, c), 0)
            - lax.broadcasted_iota(jnp.int32, (c, c), 1)).astype(F32)
    intra = (jnp.where(diff >= 0, jnp.exp(lgf * jnp.maximum(diff, 0.0)), 0.0)
             + jnp.where(diff <= 0, jnp.exp(lgb * jnp.maximum(-diff, 0.0)), 0.0))
    gnw = gnw_ref[...]

    def out_chunk(ci, carry):
        r0 = pl.multiple_of(ci * c, c)
        qb = qr_s[pl.ds(r0, c), :]
        kb = kr_s[pl.ds(r0, c), :]
        q = qb.astype(F32)
        scores = _dot_nt(qb, kb) * intra
        o = (_dot(scores.astype(BF16), v_ref[pl.ds(r0, c), :])
             + _dot((q * qdec_f).astype(BF16), sfh_s[ci])
             + _dot((q * qdec_b).astype(BF16), sbh_s[ci]))
        mu = jnp.mean(o, axis=-1, keepdims=True)
        d = o - mu
        var = jnp.mean(d * d, axis=-1, keepdims=True)
        on = d * lax.rsqrt(var + NORM_EPS) * gnw
        gate = _silu(g_ref[pl.ds(r0, c), :].astype(F32))
        o_ref[pl.ds(r0, c), :] = (on * gate).astype(BF16)
        return carry

    lax.fori_loop(0, nc, out_chunk, 0, unroll=RET_UNROLL)


def _rope_tables(n):
    inv = ROPE_BASE ** (-jnp.arange(0, ROPE_HALF, 2, dtype=F32) / ROPE_HALF)

    def tables(pos):
        ang = pos.astype(F32)[:, None] * inv[None, :]
        cos = jnp.cos(ang)
        sin = jnp.sin(ang)
        return jnp.concatenate([cos, cos], axis=1), jnp.concatenate([-sin, sin], axis=1)

    t = jnp.arange(n)
    cosr, sinr = tables(t // GRID_W)
    cosc, sinc = tables(jnp.arange(RET_CHUNK) % GRID_W)
    return cosr, sinr, cosc, sinc


def _retention(proj, cproj, lg, gn_w, batch, n, ctx_len):
    hd = RET_HEAD_DIM
    cosr, sinr, cosc, sinc = _rope_tables(n)
    col = lambda which: (lambda b, h, lg_ref: (b, which * RET_HEADS + h))
    const = lambda b, h, lg_ref: (0, 0)
    grid_spec = pltpu.PrefetchScalarGridSpec(
        num_scalar_prefetch=1,
        grid=(batch, RET_HEADS),
        in_specs=[
            pl.BlockSpec((n, hd), col(0)),
            pl.BlockSpec((n, hd), col(1)),
            pl.BlockSpec((n, hd), col(2)),
            pl.BlockSpec((n, hd), col(3)),
            pl.BlockSpec((ctx_len, hd), lambda b, h, lg_ref: (b, CTX_RET_K * RET_HEADS + h)),
            pl.BlockSpec((ctx_len, hd), lambda b, h, lg_ref: (b, CTX_RET_V * RET_HEADS + h)),
            pl.BlockSpec((n, ROPE_HALF), const),
            pl.BlockSpec((n, ROPE_HALF), const),
            pl.BlockSpec((RET_CHUNK, ROPE_HALF), const),
            pl.BlockSpec((RET_CHUNK, ROPE_HALF), const),
            pl.BlockSpec((1, hd), lambda b, h, lg_ref: (0, h)),
        ],
        out_specs=pl.BlockSpec((n, hd), lambda b, h, lg_ref: (b, h)),
        scratch_shapes=[
            pltpu.VMEM((n, hd), BF16),
            pltpu.VMEM((n, hd), BF16),
            pltpu.VMEM((n // RET_CHUNK, hd, hd), BF16),
            pltpu.VMEM((n // RET_CHUNK, hd, hd), BF16),
            pltpu.VMEM((hd, hd), F32),
            pltpu.VMEM((hd, hd), F32),
        ],
    )
    return pl.pallas_call(
        _ret_kernel,
        grid_spec=grid_spec,
        out_shape=jax.ShapeDtypeStruct((batch * n, RET_WIDTH), BF16),
        compiler_params=_params("parallel", "arbitrary"),
        name="retention",
    )(lg, proj, proj, proj, proj, cproj, cproj, cosr, sinr, cosc, sinc, gn_w.reshape(1, RET_WIDTH))


def _na_build_bias(rpb_ref, head, pair_s):
    n_roff, n_coff = 2 * NA_ROWS - 1, 2 * NA_COLS - 1
    c = lax.broadcasted_iota(jnp.int32, (GRID_W, LANES), 0)
    lane = lax.broadcasted_iota(jnp.int32, (GRID_W, LANES), 1)
    kc = jnp.bitwise_and(lane, GRID_W - 1)
    cs = jnp.clip(c - NA_COLS // 2, 0, GRID_W - NA_COLS)
    d = jnp.where((kc >= cs) & (kc < cs + NA_COLS), kc - c + (NA_COLS - 1), -1)

    def table(i):
        acc = jnp.full((GRID_W, LANES), NA_MASK, F32)
        base = (head * n_roff + i) * n_coff
        for j in range(n_coff):
            acc = jnp.where(d == j, rpb_ref[base + j], acc)
        return acc

    prev = table(0)
    for i in range(1, n_roff):
        cur = table(i)
        pair_s[i - 1] = jnp.where(lane < GRID_W, prev, cur)
        prev = cur


def _na_kernel(rpb_ref, q_ref, k_ref, v_ref, kc_ref, vc_ref, o_ref, s_s, m_s, pair_s):
    @pl.when(pl.program_id(1) == 0)
    def _():
        _na_build_bias(rpb_ref, pl.program_id(0), pair_s)

    n = q_ref.shape[0]
    rows = n // GRID_W
    kb = NA_ROWS * GRID_W
    ctx_len = kc_ref.shape[0]
    scale = NA_HEAD_DIM ** -0.5
    kc = kc_ref[...]
    vc = jnp.concatenate([vc_ref[...], jnp.ones((ctx_len, NA_HEAD_DIM), BF16)], axis=1)
    ones_w = jnp.ones((kb, NA_HEAD_DIM), BF16)

    def window(r):
        rs = jnp.clip(r - NA_ROWS // 2, 0, rows - NA_ROWS)
        return rs, pl.multiple_of(r * GRID_W, GRID_W), pl.multiple_of(rs * GRID_W, GRID_W)

    def score_row(r):
        rs, q0, k0 = window(r)
        q = (q_ref[pl.ds(q0, GRID_W), :].astype(F32) * scale).astype(BF16)
        off = rs - r + NA_ROWS - 1
        bias = jnp.concatenate([pair_s[off + 2 * t] for t in range(NA_ROWS // 2)], axis=1)
        s_win = _dot_nt(q, k_ref[pl.ds(k0, kb), :]) + bias
        s_ctx = _dot_nt(q, kc)
        m = jnp.maximum(jnp.max(s_win, axis=-1, keepdims=True), jnp.max(s_ctx, axis=-1, keepdims=True))
        s_s[pl.ds(q0, GRID_W), :kb] = s_win
        s_s[pl.ds(q0, GRID_W), kb:] = s_ctx
        m_s[pl.ds(q0, GRID_W), :] = jnp.broadcast_to(m, (GRID_W, LANES))

    def value_row(r):
        _, q0, k0 = window(r)
        m = m_s[pl.ds(q0, GRID_W), :]
        p_win = jnp.exp(s_s[pl.ds(q0, GRID_W), :kb] - jnp.tile(m, (1, kb // LANES))).astype(BF16)
        p_ctx = jnp.exp(s_s[pl.ds(q0, GRID_W), kb:] - jnp.tile(m, (1, ctx_len // LANES))).astype(BF16)
        vw = jnp.concatenate([v_ref[pl.ds(k0, kb), :], ones_w], axis=1)
        o = _dot(p_win, vw) + _dot(p_ctx, vc)
        o_ref[pl.ds(q0, GRID_W), :] = (o[:, :NA_HEAD_DIM] / o[:, NA_HEAD_DIM:]).astype(BF16)

    def sweep(row_fn):
        def group(g, carry):
            for u in range(NA_QROWS):
                row_fn(g * NA_QROWS + u)
            return carry
        lax.fori_loop(0, rows // NA_QROWS, group, 0)

    sweep(score_row)
    sweep(value_row)


def _neighbourhood_attention(proj, cproj, rpb, batch, n, ctx_len):
    hd = NA_HEAD_DIM
    rows = n // GRID_W
    assert rows % NA_QROWS == 0 and rows >= NA_ROWS
    assert rpb.shape == (NA_HEADS, 2 * NA_ROWS - 1, 2 * NA_COLS - 1)
    base = 4 * RET_WIDTH // hd
    col = lambda which: (lambda h, b, rpb_ref: (b, base + which * NA_HEADS + h))
    grid_spec = pltpu.PrefetchScalarGridSpec(
        num_scalar_prefetch=1,
        grid=(NA_HEADS, batch),
        in_specs=[
            pl.BlockSpec((n, hd), col(0)),
            pl.BlockSpec((n, hd), col(1)),
            pl.BlockSpec((n, hd), col(2)),
            pl.BlockSpec((ctx_len, hd), lambda h, b, rpb_ref: (b, CTX_NA_K * NA_HEADS + h)),
            pl.BlockSpec((ctx_len, hd), lambda h, b, rpb_ref: (b, CTX_NA_V * NA_HEADS + h)),
        ],
        out_specs=pl.BlockSpec((n, hd), lambda h, b, rpb_ref: (b, h)),
        scratch_shapes=[
            pltpu.VMEM((n, NA_ROWS * GRID_W + ctx_len), F32),
            pltpu.VMEM((n, LANES), F32),
            pltpu.VMEM((2 * NA_ROWS - 2, GRID_W, LANES), F32),
        ],
    )
    return pl.pallas_call(
        _na_kernel,
        grid_spec=grid_spec,
        out_shape=jax.ShapeDtypeStruct((batch * n, NA_WIDTH), BF16),
        compiler_params=_params("parallel", "arbitrary"),
        name="neighbourhood_attention",
    )(rpb.astype(F32).reshape(-1), proj, proj, proj, cproj, cproj)


def _outproj_kernel(n_groups, per_group, ret_ref, na_ref, w1_ref, w2_ref, x_ref, ga_ref, shf_ref,
                    scf_ref, nw_ref, wr_ref, br_ref, x1_ref, route_ref, cnt_ref, hf_ref, carry_s,
                    earlier_s, hbuf, hsem):
    tm = x_ref.shape[0]

    @pl.when(pl.program_id(0) == 0)
    def _():
        carry_s[...] = jnp.zeros_like(carry_s)
        earlier_s[...] = jnp.where(lax.broadcasted_iota(jnp.int32, (tm, tm), 0)
                                   > lax.broadcasted_iota(jnp.int32, (tm, tm), 1), 1.0, 0.0).astype(BF16)

    acc = _dot(ret_ref[...], w1_ref[...]) + _dot(na_ref[...], w2_ref[...])
    x1 = x_ref[...] + ga_ref[0] * acc
    x1_ref[...] = x1
    hf = _rms(x1, nw_ref[...]) * (1.0 + scf_ref[0]) + shf_ref[0]
    _write_rows_pipelined(hbuf, hf_ref, hsem, hf)

    hf_hi = hf.astype(BF16)
    hf_lo = (hf - hf_hi.astype(F32)).astype(BF16)
    p_hi = _dot(hf_hi, wr_ref[...])
    p_lo = _dot(hf_lo, wr_ref[...])
    logits = p_hi[:, :LANES] + p_hi[:, LANES:] + p_lo[:, :LANES] + br_ref[...]
    lane = lax.broadcasted_iota(jnp.int32, (tm, LANES), 1)
    neg = -jnp.inf

    def first_max(vals):
        top = jnp.max(vals, axis=-1, keepdims=True)
        idx = jnp.min(jnp.where(vals == top, lane, LANES), axis=-1, keepdims=True)
        return top, idx

    g_logits = jnp.where(lane < n_groups, logits, neg)
    g_top, g_sel = first_max(g_logits)
    g_w = 1.0 / jnp.sum(jnp.exp(g_logits - g_top), axis=-1, keepdims=True)
    lo = n_groups + g_sel * per_group
    e_logits = jnp.where((lane >= lo) & (lane < lo + per_group), logits, neg)
    v0, i0 = first_max(e_logits)
    v1, i1 = first_max(jnp.where(lane == i0, neg, e_logits))
    e1 = jnp.exp(v1 - v0)
    w0 = g_w / (1.0 + e1)
    w1 = g_w * e1 / (1.0 + e1)

    hit0 = lane == i0
    hit1 = lane == i1
    onehot = jnp.where(hit0 | hit1, 1.0, 0.0)
    before = _dot(earlier_s[...], onehot.astype(BF16)) + carry_s[...]
    rank0 = jnp.sum(jnp.where(hit0, before, 0.0), axis=-1, keepdims=True)
    rank1 = jnp.sum(jnp.where(hit1, before, 0.0), axis=-1, keepdims=True)
    carry_s[...] = carry_s[...] + jnp.sum(onehot, axis=0, keepdims=True)
    cnt_ref[...] = carry_s[...]

    key0 = (i0 - n_groups).astype(F32) * KEY_RANK_SPAN + rank0
    key1 = (i1 - n_groups).astype(F32) * KEY_RANK_SPAN + rank1
    fields = {R_KEY0: key0, R_KEY1: key1, R_W0: w0, R_W1: w1}
    route = jnp.zeros((tm, LANES), F32)
    for idx, val in fields.items():
        route = jnp.where(lane == idx, val, route)
    route_ref[...] = route


def _out_projection(ret, na, w_out, x2d, mod3, norm_w, w_route, b_route, n, n_groups, per_group, tm):
    m, d = x2d.shape
    chunks = d // LANES
    batch_of = lambda i: (i * tm) // n
    mod_spec = lambda which: pl.BlockSpec((1, 1, d), lambda i: (batch_of(i) * N_MOD + which, 0, 0))
    const2 = lambda i: (0, 0)
    return pl.pallas_call(
        functools.partial(_outproj_kernel, n_groups, per_group),
        grid=(m // tm,),
        in_specs=[
            pl.BlockSpec((tm, RET_WIDTH), lambda i: (i, 0)),
            pl.BlockSpec((tm, NA_WIDTH), lambda i: (i, 0)),
            pl.BlockSpec((RET_WIDTH, d), lambda i: (0, 0)),
            pl.BlockSpec((NA_WIDTH, d), lambda i: (RET_WIDTH // NA_WIDTH, 0)),
            pl.BlockSpec((tm, d), lambda i: (i, 0)),
            mod_spec(2), mod_spec(3), mod_spec(4),
            pl.BlockSpec((1, d), const2),
            pl.BlockSpec((d, 2 * LANES), const2),
            pl.BlockSpec((1, LANES), const2),
        ],
        out_specs=[
            pl.BlockSpec((tm, d), lambda i: (i, 0)),
            pl.BlockSpec((tm, LANES), lambda i: (i, 0)),
            pl.BlockSpec((1, LANES), const2),
            pl.BlockSpec(memory_space=pl.ANY),
        ],
        out_shape=[
            jax.ShapeDtypeStruct((m, d), F32),
            jax.ShapeDtypeStruct((m, LANES), F32),
            jax.ShapeDtypeStruct((1, LANES), F32),
            jax.ShapeDtypeStruct((m // SUBLANES, SUBLANES, chunks, LANES), F32),
        ],
        scratch_shapes=[
            pltpu.VMEM((1, LANES), F32),
            pltpu.VMEM((tm, tm), BF16),
            pltpu.VMEM((2, tm // SUBLANES, chunks, SUBLANES, LANES), F32),
            pltpu.SemaphoreType.DMA((2,)),
        ],
        compiler_params=_params("arbitrary"),
        name="out_projection_router",
    )(ret, na, w_out, w_out, x2d, mod3, mod3, mod3, norm_w.reshape(1, d), w_route, b_route)


def _slot_map_kernel(tile_rows, pos_ref, cnt_ref, slot_ref, tile_expert_ref, next_expert_ref,
                     run_parity_ref, n_used_ref, offs_s):
    n_experts = cnt_ref.shape[0]
    n_tiles = tile_expert_ref.shape[0]
    shift = tile_rows.bit_length() - 1
    assert 1 << shift == tile_rows

    def fill(ref, lo, hi, val):
        value_at = val if callable(val) else (lambda s: val)
        groups = lax.shift_right_logical(hi - lo, 3)

        def group(g, carry):
            for k in range(SUBLANES):
                s = lo + g * SUBLANES + k
                ref[s] = value_at(s)
            return carry

        def single(s, carry):
            ref[s] = value_at(s)
            return carry

        lax.fori_loop(0, groups, group, 0)
        lax.fori_loop(lo + groups * SUBLANES, hi, single, 0)

    assert tile_rows <= pos_ref.shape[0] // 2
    padding_row = lambda s: jnp.bitwise_and(s, tile_rows - 1)

    def tiles_of(e):
        size = lax.shift_left(lax.shift_right_logical(cnt_ref[e] + (tile_rows - 1), shift), shift)
        return lax.shift_right_logical(offs_s[e], shift), lax.shift_right_logical(offs_s[e] + size, shift)

    def per_expert(e, carry):
        start, runs = carry
        cnt = cnt_ref[e]
        size = lax.shift_left(lax.shift_right_logical(cnt + (tile_rows - 1), shift), shift)
        offs_s[e] = start
        t0, t1 = tiles_of(e)
        fill(tile_expert_ref, t0, t1, e)
        fill(run_parity_ref, t0, t1, jnp.bitwise_and(runs, 1))
        fill(slot_ref, start + cnt, start + size, padding_row)
        return start + size, runs + (size > 0).astype(jnp.int32)

    end, _ = lax.fori_loop(0, n_experts, per_expert, (0, 0))
    n_used = lax.shift_right_logical(end, shift)
    n_used_ref[0] = n_used
    fill(tile_expert_ref, n_used, n_tiles, n_experts - 1)
    fill(run_parity_ref, n_used, n_tiles, 0)
    fill(next_expert_ref, n_used, n_tiles, -1)
    fill(slot_ref, end, slot_ref.shape[0], padding_row)

    def per_expert_reversed(k, following):
        e = n_experts - 1 - k
        t0, t1 = tiles_of(e)
        fill(next_expert_ref, t0, t1, following)
        return jnp.where(t1 > t0, e, following)

    lax.fori_loop(0, n_experts, per_expert_reversed, -1)

    def assign(a, carry):
        slot_ref[pos_ref[a]] = lax.shift_right_logical(a, 1)
        return carry

    lax.fori_loop(0, pos_ref.shape[0], assign, 0, unroll=2 * SUBLANES)


def _slot_map(pos, cnt, n_tiles, tile_rows):
    smem = pl.BlockSpec(memory_space=pltpu.SMEM)
    i32 = lambda *shape: jax.ShapeDtypeStruct(shape, jnp.int32)
    return pl.pallas_call(
        functools.partial(_slot_map_kernel, tile_rows),
        in_specs=[smem, smem],
        out_specs=[smem] * 5,
        out_shape=[i32(n_tiles * tile_rows), i32(n_tiles), i32(n_tiles), i32(n_tiles), i32(1)],
        scratch_shapes=[pltpu.SMEM((cnt.shape[0],), jnp.int32)],
        name="slot_map",
    )(pos, cnt)


def _expert_kernel(tile_expert_ref, next_expert_ref, run_parity_ref, n_used_ref, tokens_ref,
                   next_tokens_ref, hf_ref, wg_hbm, wu_hbm, wd_hbm, o_ref, xbuf, gsem, wg_f, wu_f,
                   wd_f, wsem, wg_s, wu_s, wd_s):
    i = pl.program_id(0)
    last = pl.num_programs(0) - 1
    n_used = n_used_ref[0]
    tm, chunks, _ = o_ref.shape
    ff = wg_s.shape[1]

    def row_copy(ids_ref, slot, j):
        dst = xbuf.at[slot, pl.ds(j * _row_pitch(chunks), chunks), :]
        return pltpu.make_async_copy(hf_ref.at[ids_ref[0, 0, j]], dst, gsem.at[slot])

    def weight_copies(expert, slot):
        return [pltpu.make_async_copy(hbm.at[expert], stage.at[slot], wsem.at[slot])
                for hbm, stage in ((wg_hbm, wg_f), (wu_hbm, wu_f), (wd_hbm, wd_f))]

    @pl.when((i == 0) & (n_used > 0))
    def _():
        _start_row_gather(lambda j: hf_ref.at[tokens_ref[0, 0, j]], chunks, xbuf.at[0], gsem.at[0], tm)
        for cp in weight_copies(tile_expert_ref[0], run_parity_ref[0]):
            cp.start(priority=WEIGHT_DMA_PRIORITY)

    @pl.when(i < n_used)
    def _():
        @pl.when((i == 0) | (tile_expert_ref[i] != tile_expert_ref[jnp.maximum(i - 1, 0)]))
        def _():
            stage = run_parity_ref[i]
            for cp in weight_copies(tile_expert_ref[i], stage):
                cp.wait()

            @pl.when(next_expert_ref[i] >= 0)
            def _():
                for cp in weight_copies(next_expert_ref[i], 1 - stage):
                    cp.start(priority=WEIGHT_DMA_PRIORITY)

            wg_s[...] = wg_f[stage].astype(BF16)
            wu_s[...] = wu_f[stage].astype(BF16)
            wd_s[...] = wd_f[stage].astype(BF16)

        slot = i % 2
        _wait_row_gather(xbuf.at[slot], gsem.at[slot], tm, chunks)
        x = _load_gathered(xbuf.at[slot], tm, chunks).astype(BF16)

        down_split = min(EXPERT_DOWN_SPLIT, chunks)
        n_stages = 2 + down_split
        bounds = [tm * s // n_stages for s in range(n_stages + 1)]

        def start_next_rows(stage):
            for j in range(bounds[stage], bounds[stage + 1]):
                row_copy(next_tokens_ref, 1 - slot, j).start()

        halves = []
        for half in range(2):
            start_next_rows(half)
            cols = slice(half * ff // 2, (half + 1) * ff // 2)
            halves.append((_silu(_dot(x, wg_s[:, cols])) * _dot(x, wu_s[:, cols])).astype(BF16))
        a = jnp.concatenate(halves, axis=1)
        per_part = chunks // down_split
        for part in range(down_split):
            start_next_rows(2 + part)
            y = _dot(a, wd_s[:, part * per_part * LANES:(part + 1) * per_part * LANES])
            for c in range(per_part):
                o_ref[:, part * per_part + c, :] = y[:, c * LANES:(c + 1) * LANES]

        @pl.when(i == last)
        def _():
            _wait_row_gather(xbuf.at[1 - slot], gsem.at[1 - slot], tm, chunks)

    @pl.when(i >= n_used)
    def _():
        @pl.when((i == n_used) & (i > 0))
        def _():
            _wait_row_gather(xbuf.at[i % 2], gsem.at[i % 2], tm, chunks)

        o_ref[...] = jnp.zeros(o_ref.shape, o_ref.dtype)


def _experts(tile_meta, slot_token, hf, w_gate, w_up, w_down, n_tiles, tm):
    chunks = hf.shape[1]
    d = chunks * LANES
    ff = w_gate.shape[-1]
    slot_tiles = slot_token.reshape(n_tiles, 1, tm)
    any_space = pl.BlockSpec(memory_space=pl.ANY)
    ids_block = lambda ahead: pl.BlockSpec(
        (1, 1, tm), lambda i, *prefetch: (jnp.minimum(i + ahead, n_tiles - 1), 0, 0),
        memory_space=pltpu.SMEM)
    grid_spec = pltpu.PrefetchScalarGridSpec(
        num_scalar_prefetch=4,
        grid=(n_tiles,),
        in_specs=[ids_block(0), ids_block(1), any_space, any_space, any_space, any_space],
        out_specs=pl.BlockSpec((tm, chunks, LANES), lambda i, *prefetch: (i, 0, 0)),
        scratch_shapes=[
            pltpu.VMEM((2, tm * _row_pitch(chunks), LANES), hf.dtype),
            pltpu.SemaphoreType.DMA((2,)),
            pltpu.VMEM((2, d, ff), F32),
            pltpu.VMEM((2, d, ff), F32),
            pltpu.VMEM((2, ff, d), F32),
            pltpu.SemaphoreType.DMA((2,)),
            pltpu.VMEM((d, ff), BF16),
            pltpu.VMEM((d, ff), BF16),
            pltpu.VMEM((ff, d), BF16),
        ],
    )
    return pl.pallas_call(
        _expert_kernel,
        grid_spec=grid_spec,
        out_shape=jax.ShapeDtypeStruct((n_tiles * tm, chunks, LANES), hf.dtype),
        compiler_params=_params("arbitrary"),
        name="routed_experts",
    )(*tile_meta, slot_tiles, slot_tiles, hf, w_gate, w_up, w_down)


def _combine_kernel(pos_ref, ys_ref, x1_ref, route_ref, gf_ref, fw_ref, o_ref, ybuf, sem):
    i = pl.program_id(0)
    n_tiles = pl.num_programs(0)
    tm = x1_ref.shape[0]
    chunks = ys_ref.shape[1]

    def start_tile(tile, slot):
        for choice in range(2):
            _start_row_gather(lambda j: ys_ref.at[pos_ref[(tile * tm + j) * 2 + choice]], chunks,
                              ybuf.at[slot, choice], sem.at[slot, choice], tm)

    @pl.when(i == 0)
    def _():
        start_tile(0, 0)

    @pl.when(i + 1 < n_tiles)
    def _():
        start_tile(i + 1, (i + 1) % 2)

    slot = i % 2
    for choice in range(2):
        _wait_row_gather(ybuf.at[slot, choice], sem.at[slot, choice], tm, chunks)
    route = route_ref[...]
    moe = (route[:, R_W0:R_W0 + 1] * _load_gathered(ybuf.at[slot, 0], tm, chunks)
           + route[:, R_W1:R_W1 + 1] * _load_gathered(ybuf.at[slot, 1], tm, chunks))
    x2 = x1_ref[...] + gf_ref[0] * moe
    o_ref[...] = _rms(x2, fw_ref[...])


def _combine(pos, ys, x1, route, mod3, final_w, n, tm):
    m, d = x1.shape
    chunks = ys.shape[1]
    grid_spec = pltpu.PrefetchScalarGridSpec(
        num_scalar_prefetch=1,
        grid=(m // tm,),
        in_specs=[
            pl.BlockSpec(memory_space=pl.ANY),
            pl.BlockSpec((tm, d), lambda i, pos_ref: (i, 0)),
            pl.BlockSpec((tm, LANES), lambda i, pos_ref: (i, 0)),
            pl.BlockSpec((1, 1, d), lambda i, pos_ref: (((i * tm) // n) * N_MOD + 5, 0, 0)),
            pl.BlockSpec((1, d), lambda i, pos_ref: (0, 0)),
        ],
        out_specs=pl.BlockSpec((tm, d), lambda i, pos_ref: (i, 0)),
        scratch_shapes=[pltpu.VMEM((2, 2, tm * _row_pitch(chunks), LANES), ys.dtype),
                        pltpu.SemaphoreType.DMA((2, 2))],
    )
    return pl.pallas_call(
        _combine_kernel,
        grid_spec=grid_spec,
        out_shape=jax.ShapeDtypeStruct((m, d), F32),
        compiler_params=_params("arbitrary"),
        name="combine_final_norm",
    )(pos, ys, x1, route, mod3, final_w.reshape(1, d))


def kernel(x, c, ctx, c_ctx, w_mod, b_mod, norm_mix_w, w_in, ret_decay_f, ret_decay_b, ret_gn_w, na_rpb, w_out, norm_ffn_w, w_router_group, b_router_group, w_router_expert, b_router_expert, w_gate, w_up, w_down, final_norm_w):
    assert w_mod.shape[0] == 1, "single trunk layer"
    batch, n, d = x.shape
    ctx_len = ctx.shape[1]
    n_groups = w_router_group.shape[-1]
    per_group = w_router_expert.shape[-1]
    n_experts = w_gate.shape[1]
    assert n_groups * per_group == n_experts and n_groups + n_experts <= LANES

    mod_rows = 8
    cc = jnp.zeros((mod_rows, d), F32).at[:batch].set(c).at[batch].set(c_ctx)
    mod = _modulation(cc, w_mod[0], b_mod[0])
    mod3 = mod.reshape(mod_rows * N_MOD, 1, d)

    w_in_b = w_in[0].astype(BF16)
    tm = min(512, n)
    tm_in = min(1024, n)
    x2d = x.reshape(batch * n, d)
    proj = _in_projection(x2d, mod3, lambda i: (i * tm_in) // n, norm_mix_w[0], w_in_b, tm_in)
    cproj = _in_projection(ctx.reshape(batch * ctx_len, d), mod3, lambda i: batch, norm_mix_w[0],
                           w_in_b, batch * ctx_len, CTX_COL_BLOCKS)

    lg = jnp.stack([jax.nn.log_sigmoid(ret_decay_f[0].astype(F32)),
                    jax.nn.log_sigmoid(ret_decay_b[0].astype(F32))])
    ret = _retention(proj, cproj, lg, ret_gn_w[0], batch, n, ctx_len)
    na = _neighbourhood_attention(proj, cproj, na_rpb[0], batch, n, ctx_len)

    w_route = jnp.concatenate(
        [w_router_group[0], jnp.moveaxis(w_router_expert[0], 0, 1).reshape(d, n_experts)], axis=1)
    w_route = jnp.pad(w_route.astype(F32), ((0, 0), (0, LANES - n_groups - n_experts)))
    w_route_hi = w_route.astype(BF16)
    w_route = jnp.concatenate([w_route_hi, (w_route - w_route_hi.astype(F32)).astype(BF16)], axis=1)
    b_route = jnp.concatenate([b_router_group[0], b_router_expert[0].reshape(-1)])
    b_route = jnp.pad(b_route.astype(F32), (0, LANES - n_groups - n_experts)).reshape(1, LANES)
    x1, route, counts, hf = _out_projection(ret, na, w_out[0].astype(BF16), x2d, mod3, norm_ffn_w[0],
                                            w_route, b_route, n, n_groups, per_group, tm)

    te = EXPERT_TILE
    tokens = batch * n
    assert tokens < 1 << KEY_RANK_BITS
    n_tiles = (2 * tokens) // te + n_experts
    cnt = counts[0, n_groups:n_groups + n_experts].astype(jnp.int32)
    padded = (cnt + te - 1) // te * te
    starts = jnp.cumsum(padded) - padded
    keys = route[:, R_KEY0:R_KEY1 + 1].astype(jnp.int32).reshape(-1)
    expert = lax.shift_right_logical(keys, KEY_RANK_BITS)
    start_of = jnp.sum(jnp.where(expert[:, None] == jnp.arange(n_experts)[None, :], starts[None, :], 0), axis=1)
    pos = start_of + jnp.bitwise_and(keys, (1 << KEY_RANK_BITS) - 1)
    slot_token, *tile_meta = _slot_map(pos, cnt, n_tiles, te)

    hf_rows = hf.reshape(tokens, d // LANES, LANES)
    ys = _experts(tile_meta, slot_token, hf_rows, w_gate[0], w_up[0], w_down[0], n_tiles, te)
    out = _combine(pos, ys, x1, route, mod3, final_norm_w, n, min(256, n))
    return out.reshape(batch, n, d)
```

```python
import functools

import jax
import jax.numpy as jnp
from jax import lax
from jax.experimental import pallas as pl
from jax.experimental.pallas import tpu as pltpu

F32 = jnp.float32
BF16 = jnp.bfloat16

GRID_W = 64
RET_HEADS = 4
RET_HEAD_DIM = 256
RET_WIDTH = RET_HEADS * RET_HEAD_DIM
NA_HEADS = 8
NA_HEAD_DIM = 128
NA_WIDTH = NA_HEADS * NA_HEAD_DIM
RET_CHUNK = 128
NA_ROWS = 8
NA_COLS = 16
ROPE_BASE = 10000.0
N_MOD = 6
NORM_EPS = 1e-6
ROPE_HALF = RET_HEAD_DIM // 2
RET_UNROLL = 8

INPROJ_TN = 1024
assert INPROJ_TN == RET_WIDTH == NA_WIDTH
CTX_COL_BLOCKS = (1, 2, 5, 6)
CTX_RET_K, CTX_RET_V, CTX_NA_K, CTX_NA_V = 0, 1, 2, 3

NA_QROWS = 16
NA_MASK = -1e30

LANES = 128
SUBLANES = 8
VMEM_LIMIT_BYTES = 56 * 1024 * 1024

R_KEY0, R_KEY1, R_W0, R_W1 = 0, 1, 2, 3
KEY_RANK_BITS = 16
KEY_RANK_SPAN = float(1 << KEY_RANK_BITS)

EXPERT_TILE = 256
WEIGHT_DMA_PRIORITY = 1
EXPERT_DOWN_SPLIT = 4


def _params(*sem):
    return pltpu.CompilerParams(dimension_semantics=sem, vmem_limit_bytes=VMEM_LIMIT_BYTES)


def _dot(a, b):
    return jnp.dot(a, b, preferred_element_type=F32)


def _dot_nt(a, b):
    return lax.dot_general(a, b, (((1,), (1,)), ((), ())), preferred_element_type=F32)


def _dot_tn(a, b):
    return lax.dot_general(a, b, (((0,), (0,)), ((), ())), preferred_element_type=F32)


def _rms(x, w):
    return x * lax.rsqrt(jnp.mean(x * x, axis=-1, keepdims=True) + NORM_EPS) * w


def _silu(x):
    return x * jax.nn.sigmoid(x)


def _store_rows(ref, val):
    for c in range(ref.shape[1]):
        ref[:, c, :] = val[:, c * LANES:(c + 1) * LANES]


def _tile_store(buf, val):
    for c in range(buf.shape[1]):
        buf[:, c] = val[:, c * LANES:(c + 1) * LANES].reshape(buf.shape[0], SUBLANES, LANES)


def _row_major_copies(buf, hbm, first_row_tile, sem):
    return [pltpu.make_async_copy(buf.at[:, :, s, :], hbm.at[pl.ds(first_row_tile, buf.shape[0]), s], sem)
            for s in range(SUBLANES)]


def _write_rows_pipelined(buf2, hbm, sem2, val):
    i = pl.program_id(0)
    last = pl.num_programs(0) - 1
    slot = i % 2
    tiles = buf2.shape[1]

    def copies(step, s):
        return _row_major_copies(buf2.at[s], hbm, step * tiles, sem2.at[s])

    @pl.when(i >= 2)
    def _():
        for cp in copies(i - 2, slot):
            cp.wait()

    _tile_store(buf2.at[slot], val)
    for cp in copies(i, slot):
        cp.start()

    @pl.when(i == last)
    def _():
        for cp in copies(i, slot):
            cp.wait()

    @pl.when((i == last) & (i >= 1))
    def _():
        for cp in copies(i - 1, 1 - slot):
            cp.wait()


def _row_pitch(chunks):
    return chunks + 1


def _start_row_gather(src_row, chunks, dst_buf, sem, n_rows):
    def body(j, carry):
        dst = dst_buf.at[pl.ds(j * _row_pitch(chunks), chunks), :]
        pltpu.make_async_copy(src_row(j), dst, sem).start()
        return carry

    lax.fori_loop(0, n_rows, body, 0, unroll=SUBLANES)


def _wait_row_gather(dst_buf, sem, n_rows, chunks):
    view = dst_buf.at[pl.ds(0, n_rows * chunks), :]
    pltpu.make_async_copy(view, view, sem).wait()


def _load_gathered(buf, n_rows, chunks):
    return jnp.concatenate([buf[pl.ds(c, n_rows, stride=_row_pitch(chunks)), :] for c in range(chunks)],
                           axis=1)


def _mod_kernel(c_ref, w_ref, b_ref, o_ref):
    a = _silu(c_ref[...]).astype(BF16)
    o_ref[...] = _dot(a, w_ref[...].astype(BF16)) + b_ref[...]


def _modulation(cc, w_mod, b_mod):
    rows, d = cc.shape
    width = w_mod.shape[1]
    tn = next(t for t in (1024, 512, 256, LANES) if width % t == 0)
    return pl.pallas_call(
        _mod_kernel,
        grid=(width // tn,),
        in_specs=[
            pl.BlockSpec((rows, d), lambda j: (0, 0)),
            pl.BlockSpec((d, tn), lambda j: (0, j)),
            pl.BlockSpec((1, tn), lambda j: (0, j)),
        ],
        out_specs=pl.BlockSpec((rows, tn), lambda j: (0, j)),
        out_shape=jax.ShapeDtypeStruct((rows, width), F32),
        compiler_params=_params("arbitrary"),
        name="modulation",
    )(cc, w_mod, b_mod.reshape(1, width))


def _inproj_kernel(x_ref, sh_ref, sc_ref, nw_ref, w_ref, o_ref, h_ref):
    @pl.when(pl.program_id(1) == 0)
    def _():
        y = _rms(x_ref[...], nw_ref[...])
        h_ref[...] = (y * (1.0 + sc_ref[0]) + sh_ref[0]).astype(BF16)

    o_ref[...] = _dot(h_ref[...], w_ref[...].astype(BF16)).astype(BF16)


def _in_projection(x2d, mod3, mod_row_of_tile, norm_w, w_in, tm, col_blocks=None):
    m, d = x2d.shape
    tn = INPROJ_TN
    if col_blocks is None:
        col_blocks = tuple(range(w_in.shape[1] // tn))
    n_blocks = len(col_blocks)
    w_block = lambda j: sum(jnp.where(j == k, blk, 0) for k, blk in enumerate(col_blocks))
    return pl.pallas_call(
        _inproj_kernel,
        grid=(m // tm, n_blocks),
        in_specs=[
            pl.BlockSpec((tm, d), lambda i, j: (i, 0)),
            pl.BlockSpec((1, 1, d), lambda i, j: (mod_row_of_tile(i) * N_MOD + 0, 0, 0)),
            pl.BlockSpec((1, 1, d), lambda i, j: (mod_row_of_tile(i) * N_MOD + 1, 0, 0)),
            pl.BlockSpec((1, d), lambda i, j: (0, 0)),
            pl.BlockSpec((d, tn), lambda i, j: (0, w_block(j))),
        ],
        out_specs=pl.BlockSpec((tm, tn), lambda i, j: (i, j)),
        out_shape=jax.ShapeDtypeStruct((m, n_blocks * tn), BF16),
        scratch_shapes=[pltpu.VMEM((tm, d), BF16)],
        compiler_params=_params("parallel", "arbitrary"),
        name="in_projection",
    )(x2d, mod3, mod3, norm_w.reshape(1, d), w_in)


def _ret_kernel(lg_ref, q_ref, k_ref, v_ref, g_ref, ck_ref, cv_ref, cosr_ref, sinr_ref,
                cosc_ref, sinc_ref, gnw_ref, o_ref, qr_s, kr_s, sfh_s, sbh_s, sf_s, sb_s):
    head = pl.program_id(1)
    lgf = lg_ref[0, head]
    lgb = lg_ref[1, head]
    n = q_ref.shape[0]
    c = RET_CHUNK
    nc = n // c
    ctx_len = ck_ref.shape[0]
    k_scale = RET_HEAD_DIM ** -0.5

    posl = lax.broadcasted_iota(jnp.int32, (ctx_len, 1), 0).astype(F32)
    ck = ck_ref[...].astype(F32) * k_scale
    cv = cv_ref[...]
    sf_s[...] = _dot_tn((ck * jnp.exp(lgf * (ctx_len - 1.0 - posl))).astype(BF16), cv)
    sb_s[...] = _dot_tn((ck * jnp.exp(lgb * posl)).astype(BF16), cv)

    cosc = cosc_ref[...]
    sinc = sinc_ref[...]
    pos = lax.broadcasted_iota(jnp.int32, (c, 1), 0).astype(F32)
    qdec_f = jnp.exp(lgf * (pos + 1.0))
    kdec_f = jnp.exp(lgf * (c - 1.0 - pos))
    cdec_f = jnp.exp(lgf * c)
    qdec_b = jnp.exp(lgb * (c - pos))
    kdec_b = jnp.exp(lgb * pos)
    cdec_b = jnp.exp(lgb * c)

    def rope(x, cosr, sinr):
        xa = x[:, :ROPE_HALF]
        xb = x[:, ROPE_HALF:]
        ya = xa * cosr + pltpu.roll(xa, ROPE_HALF // 2, 1) * sinr
        yb = xb * cosc + pltpu.roll(xb, ROPE_HALF // 2, 1) * sinc
        return jnp.concatenate([ya, yb], axis=1)

    def fwd_chunk(ci, carry):
        r0 = pl.multiple_of(ci * c, c)
        cosr = cosr_ref[pl.ds(r0, c), :]
        sinr = sinr_ref[pl.ds(r0, c), :]
        qr_s[pl.ds(r0, c), :] = rope(q_ref[pl.ds(r0, c), :].astype(F32), cosr, sinr).astype(BF16)
        k = rope(k_ref[pl.ds(r0, c), :].astype(F32), cosr, sinr) * k_scale
        kr_s[pl.ds(r0, c), :] = k.astype(BF16)
        kv = _dot_tn((k * kdec_f).astype(BF16), v_ref[pl.ds(r0, c), :])
        state = sf_s[...]
        sfh_s[ci] = state.astype(BF16)
        sf_s[...] = state * cdec_f + kv
        return carry

    lax.fori_loop(0, nc, fwd_chunk, 0, unroll=RET_UNROLL)

    def bwd_chunk(i, carry):
        ci = nc - 1 - i
        r0 = pl.multiple_of(ci * c, c)
        k = kr_s[pl.ds(r0, c), :].astype(F32)
        kv = _dot_tn((k * kdec_b).astype(BF16), v_ref[pl.ds(r0, c), :])
        state = sb_s[...]
        sbh_s[ci] = state.astype(BF16)
        sb_s[...] = state * cdec_b + kv
        return carry

    lax.fori_loop(0, nc, bwd_chunk, 0, unroll=RET_UNROLL)

    diff = (lax.broadcasted_iota(jnp.int32, (c, c), 0)
            - lax.broadcasted_iota(jnp.int32, (c, c), 1)).astype(F32)
    intra = (jnp.where(diff >= 0, jnp.exp(lgf * jnp.maximum(diff, 0.0)), 0.0)
             + jnp.where(diff <= 0, jnp.exp(lgb * jnp.maximum(-diff, 0.0)), 0.0))
    gnw = gnw_ref[...]

    def out_chunk(ci, carry):
        r0 = pl.multiple_of(ci * c, c)
        qb = qr_s[pl.ds(r0, c), :]
        kb = kr_s[pl.ds(r0, c), :]
        q = qb.astype(F32)
        scores = _dot_nt(qb, kb) * intra
        o = (_dot(scores.astype(BF16), v_ref[pl.ds(r0, c), :])
             + _dot((q * qdec_f).astype(BF16), sfh_s[ci])
             + _dot((q * qdec_b).astype(BF16), sbh_s[ci]))
        mu = jnp.mean(o, axis=-1, keepdims=True)
        d = o - mu
        var = jnp.mean(d * d, axis=-1, keepdims=True)
        on = d * lax.rsqrt(var + NORM_EPS) * gnw
        gate = _silu(g_ref[pl.ds(r0, c), :].astype(F32))
        o_ref[pl.ds(r0, c), :] = (on * gate).astype(BF16)
        return carry

    lax.fori_loop(0, nc, out_chunk, 0, unroll=RET_UNROLL)


def _rope_tables(n):
    inv = ROPE_BASE ** (-jnp.arange(0, ROPE_HALF, 2, dtype=F32) / ROPE_HALF)

    def tables(pos):
        ang = pos.astype(F32)[:, None] * inv[None, :]
        cos = jnp.cos(ang)
        sin = jnp.sin(ang)
        return jnp.concatenate([cos, cos], axis=1), jnp.concatenate([-sin, sin], axis=1)

    t = jnp.arange(n)
    cosr, sinr = tables(t // GRID_W)
    cosc, sinc = tables(jnp.arange(RET_CHUNK) % GRID_W)
    return cosr, sinr, cosc, sinc


def _retention(proj, cproj, lg, gn_w, batch, n, ctx_len):
    hd = RET_HEAD_DIM
    cosr, sinr, cosc, sinc = _rope_tables(n)
    col = lambda which: (lambda b, h, lg_ref: (b, which * RET_HEADS + h))
    const = lambda b, h, lg_ref: (0, 0)
    grid_spec = pltpu.PrefetchScalarGridSpec(
        num_scalar_prefetch=1,
        grid=(batch, RET_HEADS),
        in_specs=[
            pl.BlockSpec((n, hd), col(0)),
            pl.BlockSpec((n, hd), col(1)),
            pl.BlockSpec((n, hd), col(2)),
            pl.BlockSpec((n, hd), col(3)),
            pl.BlockSpec((ctx_len, hd), lambda b, h, lg_ref: (b, CTX_RET_K * RET_HEADS + h)),
            pl.BlockSpec((ctx_len, hd), lambda b, h, lg_ref: (b, CTX_RET_V * RET_HEADS + h)),
            pl.BlockSpec((n, ROPE_HALF), const),
            pl.BlockSpec((n, ROPE_HALF), const),
            pl.BlockSpec((RET_CHUNK, ROPE_HALF), const),
            pl.BlockSpec((RET_CHUNK, ROPE_HALF), const),
            pl.BlockSpec((1, hd), lambda b, h, lg_ref: (0, h)),
        ],
        out_specs=pl.BlockSpec((n, hd), lambda b, h, lg_ref: (b, h)),
        scratch_shapes=[
            pltpu.VMEM((n, hd), BF16),
            pltpu.VMEM((n, hd), BF16),
            pltpu.VMEM((n // RET_CHUNK, hd, hd), BF16),
            pltpu.VMEM((n // RET_CHUNK, hd, hd), BF16),
            pltpu.VMEM((hd, hd), F32),
            pltpu.VMEM((hd, hd), F32),
        ],
    )
    return pl.pallas_call(
        _ret_kernel,
        grid_spec=grid_spec,
        out_shape=jax.ShapeDtypeStruct((batch * n, RET_WIDTH), BF16),
        compiler_params=_params("parallel", "arbitrary"),
        name="retention",
    )(lg, proj, proj, proj, proj, cproj, cproj, cosr, sinr, cosc, sinc, gn_w.reshape(1, RET_WIDTH))


def _na_build_bias(rpb_ref, head, pair_s):
    n_roff, n_coff = 2 * NA_ROWS - 1, 2 * NA_COLS - 1
    c = lax.broadcasted_iota(jnp.int32, (GRID_W, LANES), 0)
    lane = lax.broadcasted_iota(jnp.int32, (GRID_W, LANES), 1)
    kc = jnp.bitwise_and(lane, GRID_W - 1)
    cs = jnp.clip(c - NA_COLS // 2, 0, GRID_W - NA_COLS)
    d = jnp.where((kc >= cs) & (kc < cs + NA_COLS), kc - c + (NA_COLS - 1), -1)

    def table(i):
        acc = jnp.full((GRID_W, LANES), NA_MASK, F32)
        base = (head * n_roff + i) * n_coff
        for j in range(n_coff):
            acc = jnp.where(d == j, rpb_ref[base + j], acc)
        return acc

    prev = table(0)
    for i in range(1, n_roff):
        cur = table(i)
        pair_s[i - 1] = jnp.where(lane < GRID_W, prev, cur)
        prev = cur


def _na_kernel(rpb_ref, q_ref, k_ref, v_ref, kc_ref, vc_ref, o_ref, s_s, m_s, pair_s):
    @pl.when(pl.program_id(1) == 0)
    def _():
        _na_build_bias(rpb_ref, pl.program_id(0), pair_s)

    n = q_ref.shape[0]
    rows = n // GRID_W
    kb = NA_ROWS * GRID_W
    ctx_len = kc_ref.shape[0]
    scale = NA_HEAD_DIM ** -0.5
    kc = kc_ref[...]
    vc = jnp.concatenate([vc_ref[...], jnp.ones((ctx_len, NA_HEAD_DIM), BF16)], axis=1)
    ones_w = jnp.ones((kb, NA_HEAD_DIM), BF16)

    def window(r):
        rs = jnp.clip(r - NA_ROWS // 2, 0, rows - NA_ROWS)
        return rs, pl.multiple_of(r * GRID_W, GRID_W), pl.multiple_of(rs * GRID_W, GRID_W)

    def score_row(r):
        rs, q0, k0 = window(r)
        q = (q_ref[pl.ds(q0, GRID_W), :].astype(F32) * scale).astype(BF16)
        off = rs - r + NA_ROWS - 1
        bias = jnp.concatenate([pair_s[off + 2 * t] for t in range(NA_ROWS // 2)], axis=1)
        s_win = _dot_nt(q, k_ref[pl.ds(k0, kb), :]) + bias
        s_ctx = _dot_nt(q, kc)
        m = jnp.maximum(jnp.max(s_win, axis=-1, keepdims=True), jnp.max(s_ctx, axis=-1, keepdims=True))
        s_s[pl.ds(q0, GRID_W), :kb] = s_win
        s_s[pl.ds(q0, GRID_W), kb:] = s_ctx
        m_s[pl.ds(q0, GRID_W), :] = jnp.broadcast_to(m, (GRID_W, LANES))

    def value_row(r):
        _, q0, k0 = window(r)
        m = m_s[pl.ds(q0, GRID_W), :]
        p_win = jnp.exp(s_s[pl.ds(q0, GRID_W), :kb] - jnp.tile(m, (1, kb // LANES))).astype(BF16)
        p_ctx = jnp.exp(s_s[pl.ds(q0, GRID_W), kb:] - jnp.tile(m, (1, ctx_len // LANES))).astype(BF16)
        vw = jnp.concatenate([v_ref[pl.ds(k0, kb), :], ones_w], axis=1)
        o = _dot(p_win, vw) + _dot(p_ctx, vc)
        o_ref[pl.ds(q0, GRID_W), :] = (o[:, :NA_HEAD_DIM] / o[:, NA_HEAD_DIM:]).astype(BF16)

    def sweep(row_fn):
        def group(g, carry):
            for u in range(NA_QROWS):
                row_fn(g * NA_QROWS + u)
            return carry
        lax.fori_loop(0, rows // NA_QROWS, group, 0)

    sweep(score_row)
    sweep(value_row)


def _neighbourhood_attention(proj, cproj, rpb, batch, n, ctx_len):
    hd = NA_HEAD_DIM
    rows = n // GRID_W
    assert rows % NA_QROWS == 0 and rows >= NA_ROWS
    assert rpb.shape == (NA_HEADS, 2 * NA_ROWS - 1, 2 * NA_COLS - 1)
    base = 4 * RET_WIDTH // hd
    col = lambda which: (lambda h, b, rpb_ref: (b, base + which * NA_HEADS + h))
    grid_spec = pltpu.PrefetchScalarGridSpec(
        num_scalar_prefetch=1,
        grid=(NA_HEADS, batch),
        in_specs=[
            pl.BlockSpec((n, hd), col(0)),
            pl.BlockSpec((n, hd), col(1)),
            pl.BlockSpec((n, hd), col(2)),
            pl.BlockSpec((ctx_len, hd), lambda h, b, rpb_ref: (b, CTX_NA_K * NA_HEADS + h)),
            pl.BlockSpec((ctx_len, hd), lambda h, b, rpb_ref: (b, CTX_NA_V * NA_HEADS + h)),
        ],
        out_specs=pl.BlockSpec((n, hd), lambda h, b, rpb_ref: (b, h)),
        scratch_shapes=[
            pltpu.VMEM((n, NA_ROWS * GRID_W + ctx_len), F32),
            pltpu.VMEM((n, LANES), F32),
            pltpu.VMEM((2 * NA_ROWS - 2, GRID_W, LANES), F32),
        ],
    )
    return pl.pallas_call(
        _na_kernel,
        grid_spec=grid_spec,
        out_shape=jax.ShapeDtypeStruct((batch * n, NA_WIDTH), BF16),
        compiler_params=_params("parallel", "arbitrary"),
        name="neighbourhood_attention",
    )(rpb.astype(F32).reshape(-1), proj, proj, proj, cproj, cproj)


def _outproj_kernel(n_groups, per_group, ret_ref, na_ref, w1_ref, w2_ref, x_ref, ga_ref, shf_ref,
                    scf_ref, nw_ref, wr_ref, br_ref, x1_ref, route_ref, cnt_ref, hf_ref, carry_s,
                    earlier_s, hbuf, hsem):
    tm = x_ref.shape[0]

    @pl.when(pl.program_id(0) == 0)
    def _():
        carry_s[...] = jnp.zeros_like(carry_s)
        earlier_s[...] = jnp.where(lax.broadcasted_iota(jnp.int32, (tm, tm), 0)
                                   > lax.broadcasted_iota(jnp.int32, (tm, tm), 1), 1.0, 0.0).astype(BF16)

    acc = _dot(ret_ref[...], w1_ref[...]) + _dot(na_ref[...], w2_ref[...])
    x1 = x_ref[...] + ga_ref[0] * acc
    x1_ref[...] = x1
    hf = _rms(x1, nw_ref[...]) * (1.0 + scf_ref[0]) + shf_ref[0]
    _write_rows_pipelined(hbuf, hf_ref, hsem, hf)

    hf_hi = hf.astype(BF16)
    hf_lo = (hf - hf_hi.astype(F32)).astype(BF16)
    p_hi = _dot(hf_hi, wr_ref[...])
    p_lo = _dot(hf_lo, wr_ref[...])
    logits = p_hi[:, :LANES] + p_hi[:, LANES:] + p_lo[:, :LANES] + br_ref[...]
    lane = lax.broadcasted_iota(jnp.int32, (tm, LANES), 1)
    neg = -jnp.inf

    def first_max(vals):
        top = jnp.max(vals, axis=-1, keepdims=True)
        idx = jnp.min(jnp.where(vals == top, lane, LANES), axis=-1, keepdims=True)
        return top, idx

    g_logits = jnp.where(lane < n_groups, logits, neg)
    g_top, g_sel = first_max(g_logits)
    g_w = 1.0 / jnp.sum(jnp.exp(g_logits - g_top), axis=-1, keepdims=True)
    lo = n_groups + g_sel * per_group
    e_logits = jnp.where((lane >= lo) & (lane < lo + per_group), logits, neg)
    v0, i0 = first_max(e_logits)
    v1, i1 = first_max(jnp.where(lane == i0, neg, e_logits))
    e1 = jnp.exp(v1 - v0)
    w0 = g_w / (1.0 + e1)
    w1 = g_w * e1 / (1.0 + e1)

    hit0 = lane == i0
    hit1 = lane == i1
    onehot = jnp.where(hit0 | hit1, 1.0, 0.0)
    before = _dot(earlier_s[...], onehot.astype(BF16)) + carry_s[...]
    rank0 = jnp.sum(jnp.where(hit0, before, 0.0), axis=-1, keepdims=True)
    rank1 = jnp.sum(jnp.where(hit1, before, 0.0), axis=-1, keepdims=True)
    carry_s[...] = carry_s[...] + jnp.sum(onehot, axis=0, keepdims=True)
    cnt_ref[...] = carry_s[...]

    key0 = (i0 - n_groups).astype(F32) * KEY_RANK_SPAN + rank0
    key1 = (i1 - n_groups).astype(F32) * KEY_RANK_SPAN + rank1
    fields = {R_KEY0: key0, R_KEY1: key1, R_W0: w0, R_W1: w1}
    route = jnp.zeros((tm, LANES), F32)
    for idx, val in fields.items():
        route = jnp.where(lane == idx, val, route)
    route_ref[...] = route


def _out_projection(ret, na, w_out, x2d, mod3, norm_w, w_route, b_route, n, n_groups, per_group, tm):
    m, d = x2d.shape
    chunks = d // LANES
    batch_of = lambda i: (i * tm) // n
    mod_spec = lambda which: pl.BlockSpec((1, 1, d), lambda i: (batch_of(i) * N_MOD + which, 0, 0))
    const2 = lambda i: (0, 0)
    return pl.pallas_call(
        functools.partial(_outproj_kernel, n_groups, per_group),
        grid=(m // tm,),
        in_specs=[
            pl.BlockSpec((tm, RET_WIDTH), lambda i: (i, 0)),
            pl.BlockSpec((tm, NA_WIDTH), lambda i: (i, 0)),
            pl.BlockSpec((RET_WIDTH, d), lambda i: (0, 0)),
            pl.BlockSpec((NA_WIDTH, d), lambda i: (RET_WIDTH // NA_WIDTH, 0)),
            pl.BlockSpec((tm, d), lambda i: (i, 0)),
            mod_spec(2), mod_spec(3), mod_spec(4),
            pl.BlockSpec((1, d), const2),
            pl.BlockSpec((d, 2 * LANES), const2),
            pl.BlockSpec((1, LANES), const2),
        ],
        out_specs=[
            pl.BlockSpec((tm, d), lambda i: (i, 0)),
            pl.BlockSpec((tm, LANES), lambda i: (i, 0)),
            pl.BlockSpec((1, LANES), const2),
            pl.BlockSpec(memory_space=pl.ANY),
        ],
        out_shape=[
            jax.ShapeDtypeStruct((m, d), F32),
            jax.ShapeDtypeStruct((m, LANES), F32),
            jax.ShapeDtypeStruct((1, LANES), F32),
            jax.ShapeDtypeStruct((m // SUBLANES, SUBLANES, chunks, LANES), F32),
        ],
        scratch_shapes=[
            pltpu.VMEM((1, LANES), F32),
            pltpu.VMEM((tm, tm), BF16),
            pltpu.VMEM((2, tm // SUBLANES, chunks, SUBLANES, LANES), F32),
            pltpu.SemaphoreType.DMA((2,)),
        ],
        compiler_params=_params("arbitrary"),
        name="out_projection_router",
    )(ret, na, w_out, w_out, x2d, mod3, mod3, mod3, norm_w.reshape(1, d), w_route, b_route)


def _slot_map_kernel(tile_rows, pos_ref, cnt_ref, slot_ref, tile_expert_ref, next_expert_ref,
                     run_parity_ref, n_used_ref, offs_s):
    n_experts = cnt_ref.shape[0]
    n_tiles = tile_expert_ref.shape[0]
    shift = tile_rows.bit_length() - 1
    assert 1 << shift == tile_rows

    def fill(ref, lo, hi, val):
        value_at = val if callable(val) else (lambda s: val)
        groups = lax.shift_right_logical(hi - lo, 3)

        def group(g, carry):
            for k in range(SUBLANES):
                s = lo + g * SUBLANES + k
                ref[s] = value_at(s)
            return carry

        def single(s, carry):
            ref[s] = value_at(s)
            return carry

        lax.fori_loop(0, groups, group, 0)
        lax.fori_loop(lo + groups * SUBLANES, hi, single, 0)

    assert tile_rows <= pos_ref.shape[0] // 2
    padding_row = lambda s: jnp.bitwise_and(s, tile_rows - 1)

    def tiles_of(e):
        size = lax.shift_left(lax.shift_right_logical(cnt_ref[e] + (tile_rows - 1), shift), shift)
        return lax.shift_right_logical(offs_s[e], shift), lax.shift_right_logical(offs_s[e] + size, shift)

    def per_expert(e, carry):
        start, runs = carry
        cnt = cnt_ref[e]
        size = lax.shift_left(lax.shift_right_logical(cnt + (tile_rows - 1), shift), shift)
        offs_s[e] = start
        t0, t1 = tiles_of(e)
        fill(tile_expert_ref, t0, t1, e)
        fill(run_parity_ref, t0, t1, jnp.bitwise_and(runs, 1))
        fill(slot_ref, start + cnt, start + size, padding_row)
        return start + size, runs + (size > 0).astype(jnp.int32)

    end, _ = lax.fori_loop(0, n_experts, per_expert, (0, 0))
    n_used = lax.shift_right_logical(end, shift)
    n_used_ref[0] = n_used
    fill(tile_expert_ref, n_used, n_tiles, n_experts - 1)
    fill(run_parity_ref, n_used, n_tiles, 0)
    fill(next_expert_ref, n_used, n_tiles, -1)
    fill(slot_ref, end, slot_ref.shape[0], padding_row)

    def per_expert_reversed(k, following):
        e = n_experts - 1 - k
        t0, t1 = tiles_of(e)
        fill(next_expert_ref, t0, t1, following)
        return jnp.where(t1 > t0, e, following)

    lax.fori_loop(0, n_experts, per_expert_reversed, -1)

    def assign(a, carry):
        slot_ref[pos_ref[a]] = lax.shift_right_logical(a, 1)
        return carry

    lax.fori_loop(0, pos_ref.shape[0], assign, 0, unroll=2 * SUBLANES)


def _slot_map(pos, cnt, n_tiles, tile_rows):
    smem = pl.BlockSpec(memory_space=pltpu.SMEM)
    i32 = lambda *shape: jax.ShapeDtypeStruct(shape, jnp.int32)
    return pl.pallas_call(
        functools.partial(_slot_map_kernel, tile_rows),
        in_specs=[smem, smem],
        out_specs=[smem] * 5,
        out_shape=[i32(n_tiles * tile_rows), i32(n_tiles), i32(n_tiles), i32(n_tiles), i32(1)],
        scratch_shapes=[pltpu.SMEM((cnt.shape[0],), jnp.int32)],
        name="slot_map",
    )(pos, cnt)


def _expert_kernel(tile_expert_ref, next_expert_ref, run_parity_ref, n_used_ref, tokens_ref,
                   next_tokens_ref, hf_ref, wg_hbm, wu_hbm, wd_hbm, o_ref, xbuf, gsem, wg_f, wu_f,
                   wd_f, wsem, wg_s, wu_s, wd_s):
    i = pl.program_id(0)
    last = pl.num_programs(0) - 1
    n_used = n_used_ref[0]
    tm, chunks, _ = o_ref.shape
    ff = wg_s.shape[1]

    def row_copy(ids_ref, slot, j):
        dst = xbuf.at[slot, pl.ds(j * _row_pitch(chunks), chunks), :]
        return pltpu.make_async_copy(hf_ref.at[ids_ref[0, 0, j]], dst, gsem.at[slot])

    def weight_copies(expert, slot):
        return [pltpu.make_async_copy(hbm.at[expert], stage.at[slot], wsem.at[slot])
                for hbm, stage in ((wg_hbm, wg_f), (wu_hbm, wu_f), (wd_hbm, wd_f))]

    @pl.when((i == 0) & (n_used > 0))
    def _():
        _start_row_gather(lambda j: hf_ref.at[tokens_ref[0, 0, j]], chunks, xbuf.at[0], gsem.at[0], tm)
        for cp in weight_copies(tile_expert_ref[0], run_parity_ref[0]):
            cp.start(priority=WEIGHT_DMA_PRIORITY)

    @pl.when(i < n_used)
    def _():
        @pl.when((i == 0) | (tile_expert_ref[i] != tile_expert_ref[jnp.maximum(i - 1, 0)]))
        def _():
            stage = run_parity_ref[i]
            for cp in weight_copies(tile_expert_ref[i], stage):
                cp.wait()

            @pl.when(next_expert_ref[i] >= 0)
            def _():
                for cp in weight_copies(next_expert_ref[i], 1 - stage):
                    cp.start(priority=WEIGHT_DMA_PRIORITY)

            wg_s[...] = wg_f[stage].astype(BF16)
            wu_s[...] = wu_f[stage].astype(BF16)
            wd_s[...] = wd_f[stage].astype(BF16)

        slot = i % 2
        _wait_row_gather(xbuf.at[slot], gsem.at[slot], tm, chunks)
        x = _load_gathered(xbuf.at[slot], tm, chunks).astype(BF16)

        down_split = min(EXPERT_DOWN_SPLIT, chunks)
        n_stages = 2 + down_split
        bounds = [tm * s // n_stages for s in range(n_stages + 1)]

        def start_next_rows(stage):
            for j in range(bounds[stage], bounds[stage + 1]):
                row_copy(next_tokens_ref, 1 - slot, j).start()

        halves = []
        for half in range(2):
            start_next_rows(half)
            cols = slice(half * ff // 2, (half + 1) * ff // 2)
            halves.append((_silu(_dot(x, wg_s[:, cols])) * _dot(x, wu_s[:, cols])).astype(BF16))
        a = jnp.concatenate(halves, axis=1)
        per_part = chunks // down_split
        for part in range(down_split):
            start_next_rows(2 + part)
            y = _dot(a, wd_s[:, part * per_part * LANES:(part + 1) * per_part * LANES])
            for c in range(per_part):
                o_ref[:, part * per_part + c, :] = y[:, c * LANES:(c + 1) * LANES]

        @pl.when(i == last)
        def _():
            _wait_row_gather(xbuf.at[1 - slot], gsem.at[1 - slot], tm, chunks)

    @pl.when(i >= n_used)
    def _():
        @pl.when((i == n_used) & (i > 0))
        def _():
            _wait_row_gather(xbuf.at[i % 2], gsem.at[i % 2], tm, chunks)

        o_ref[...] = jnp.zeros(o_ref.shape, o_ref.dtype)


def _experts(tile_meta, slot_token, hf, w_gate, w_up, w_down, n_tiles, tm):
    chunks = hf.shape[1]
    d = chunks * LANES
    ff = w_gate.shape[-1]
    slot_tiles = slot_token.reshape(n_tiles, 1, tm)
    any_space = pl.BlockSpec(memory_space=pl.ANY)
    ids_block = lambda ahead: pl.BlockSpec(
        (1, 1, tm), lambda i, *prefetch: (jnp.minimum(i + ahead, n_tiles - 1), 0, 0),
        memory_space=pltpu.SMEM)
    grid_spec = pltpu.PrefetchScalarGridSpec(
        num_scalar_prefetch=4,
        grid=(n_tiles,),
        in_specs=[ids_block(0), ids_block(1), any_space, any_space, any_space, any_space],
        out_specs=pl.BlockSpec((tm, chunks, LANES), lambda i, *prefetch: (i, 0, 0)),
        scratch_shapes=[
            pltpu.VMEM((2, tm * _row_pitch(chunks), LANES), hf.dtype),
            pltpu.SemaphoreType.DMA((2,)),
            pltpu.VMEM((2, d, ff), F32),
            pltpu.VMEM((2, d, ff), F32),
            pltpu.VMEM((2, ff, d), F32),
            pltpu.SemaphoreType.DMA((2,)),
            pltpu.VMEM((d, ff), BF16),
            pltpu.VMEM((d, ff), BF16),
            pltpu.VMEM((ff, d), BF16),
        ],
    )
    return pl.pallas_call(
        _expert_kernel,
        grid_spec=grid_spec,
        out_shape=jax.ShapeDtypeStruct((n_tiles * tm, chunks, LANES), hf.dtype),
        compiler_params=_params("arbitrary"),
        name="routed_experts",
    )(*tile_meta, slot_tiles, slot_tiles, hf, w_gate, w_up, w_down)


def _combine_kernel(pos_ref, ys_ref, x1_ref, route_ref, gf_ref, fw_ref, o_ref, ybuf, sem):
    i = pl.program_id(0)
    n_tiles = pl.num_programs(0)
    tm = x1_ref.shape[0]
    chunks = ys_ref.shape[1]

    def start_tile(tile, slot):
        for choice in range(2):
            _start_row_gather(lambda j: ys_ref.at[pos_ref[(tile * tm + j) * 2 + choice]], chunks,
                              ybuf.at[slot, choice], sem.at[slot, choice], tm)

    @pl.when(i == 0)
    def _():
        start_tile(0, 0)

    @pl.when(i + 1 < n_tiles)
    def _():
        start_tile(i + 1, (i + 1) % 2)

    slot = i % 2
    for choice in range(2):
        _wait_row_gather(ybuf.at[slot, choice], sem.at[slot, choice], tm, chunks)
    route = route_ref[...]
    moe = (route[:, R_W0:R_W0 + 1] * _load_gathered(ybuf.at[slot, 0], tm, chunks)
           + route[:, R_W1:R_W1 + 1] * _load_gathered(ybuf.at[slot, 1], tm, chunks))
    x2 = x1_ref[...] + gf_ref[0] * moe
    o_ref[...] = _rms(x2, fw_ref[...])


def _combine(pos, ys, x1, route, mod3, final_w, n, tm):
    m, d = x1.shape
    chunks = ys.shape[1]
    grid_spec = pltpu.PrefetchScalarGridSpec(
        num_scalar_prefetch=1,
        grid=(m // tm,),
        in_specs=[
            pl.BlockSpec(memory_space=pl.ANY),
            pl.BlockSpec((tm, d), lambda i, pos_ref: (i, 0)),
            pl.BlockSpec((tm, LANES), lambda i, pos_ref: (i, 0)),
            pl.BlockSpec((1, 1, d), lambda i, pos_ref: (((i * tm) // n) * N_MOD + 5, 0, 0)),
            pl.BlockSpec((1, d), lambda i, pos_ref: (0, 0)),
        ],
        out_specs=pl.BlockSpec((tm, d), lambda i, pos_ref: (i, 0)),
        scratch_shapes=[pltpu.VMEM((2, 2, tm * _row_pitch(chunks), LANES), ys.dtype),
                        pltpu.SemaphoreType.DMA((2, 2))],
    )
    return pl.pallas_call(
        _combine_kernel,
        grid_spec=grid_spec,
        out_shape=jax.ShapeDtypeStruct((m, d), F32),
        compiler_params=_params("arbitrary"),
        name="combine_final_norm",
    )(pos, ys, x1, route, mod3, final_w.reshape(1, d))


def kernel(x, c, ctx, c_ctx, w_mod, b_mod, norm_mix_w, w_in, ret_decay_f, ret_decay_b, ret_gn_w, na_rpb, w_out, norm_ffn_w, w_router_group, b_router_group, w_router_expert, b_router_expert, w_gate, w_up, w_down, final_norm_w):
    assert w_mod.shape[0] == 1, "single trunk layer"
    batch, n, d = x.shape
    ctx_len = ctx.shape[1]
    n_groups = w_router_group.shape[-1]
    per_group = w_router_expert.shape[-1]
    n_experts = w_gate.shape[1]
    assert n_groups * per_group == n_experts and n_groups + n_experts <= LANES

    mod_rows = 8
    cc = jnp.zeros((mod_rows, d), F32).at[:batch].set(c).at[batch].set(c_ctx)
    mod = _modulation(cc, w_mod[0], b_mod[0])
    mod3 = mod.reshape(mod_rows * N_MOD, 1, d)

    w_in_b = w_in[0]
    tm = min(512, n)
    tm_in = min(1024, n)
    x2d = x.reshape(batch * n, d)
    proj = _in_projection(x2d, mod3, lambda i: (i * tm_in) // n, norm_mix_w[0], w_in_b, tm_in)
    cproj = _in_projection(ctx.reshape(batch * ctx_len, d), mod3, lambda i: batch, norm_mix_w[0],
                           w_in_b, batch * ctx_len, CTX_COL_BLOCKS)

    lg = jnp.stack([jax.nn.log_sigmoid(ret_decay_f[0].astype(F32)),
                    jax.nn.log_sigmoid(ret_decay_b[0].astype(F32))])
    ret = _retention(proj, cproj, lg, ret_gn_w[0], batch, n, ctx_len)
    na = _neighbourhood_attention(proj, cproj, na_rpb[0], batch, n, ctx_len)

    w_route = jnp.concatenate(
        [w_router_group[0], jnp.moveaxis(w_router_expert[0], 0, 1).reshape(d, n_experts)], axis=1)
    w_route = jnp.pad(w_route.astype(F32), ((0, 0), (0, LANES - n_groups - n_experts)))
    w_route_hi = w_route.astype(BF16)
    w_route = jnp.concatenate([w_route_hi, (w_route - w_route_hi.astype(F32)).astype(BF16)], axis=1)
    b_route = jnp.concatenate([b_router_group[0], b_router_expert[0].reshape(-1)])
    b_route = jnp.pad(b_route.astype(F32), (0, LANES - n_groups - n_experts)).reshape(1, LANES)
    x1, route, counts, hf = _out_projection(ret, na, w_out[0].astype(BF16), x2d, mod3, norm_ffn_w[0],
                                            w_route, b_route, n, n_groups, per_group, tm)

    te = EXPERT_TILE
    tokens = batch * n
    assert tokens < 1 << KEY_RANK_BITS
    n_tiles = (2 * tokens) // te + n_experts
    cnt = counts[0, n_groups:n_groups + n_experts].astype(jnp.int32)
    padded = (cnt + te - 1) // te * te
    starts = jnp.cumsum(padded) - padded
    keys = route[:, R_KEY0:R_KEY1 + 1].astype(jnp.int32).reshape(-1)
    expert = lax.shift_right_logical(keys, KEY_RANK_BITS)
    start_of = jnp.sum(jnp.where(expert[:, None] == jnp.arange(n_experts)[None, :], starts[None, :], 0), axis=1)
    pos = start_of + jnp.bitwise_and(keys, (1 << KEY_RANK_BITS) - 1)
    slot_token, *tile_meta = _slot_map(pos, cnt, n_tiles, te)

    hf_rows = hf.reshape(tokens, d // LANES, LANES)
    ys = _experts(tile_meta, slot_token, hf_rows, w_gate[0], w_up[0], w_down[0], n_tiles, te)
    out = _combine(pos, ys, x1, route, mod3, final_norm_w, n, min(256, n))
    return out.reshape(batch, n, d)
```

```python
import functools
import math

import jax
import jax.numpy as jnp
from jax import lax
from jax.experimental import pallas as pl
from jax.experimental.pallas import tpu as pltpu

F32 = jnp.float32
BF16 = jnp.bfloat16

GRID_W = 64
RET_HEADS = 4
RET_HEAD_DIM = 256
RET_WIDTH = RET_HEADS * RET_HEAD_DIM
NA_HEADS = 8
NA_HEAD_DIM = 128
NA_WIDTH = NA_HEADS * NA_HEAD_DIM
RET_CHUNK = 128
NA_ROWS = 8
NA_COLS = 16
ROPE_BASE = 10000.0
N_MOD = 6
NORM_EPS = 1e-6
ROPE_HALF = RET_HEAD_DIM // 2
RET_UNROLL = 8

INPROJ_TN = 1024
assert INPROJ_TN == RET_WIDTH == NA_WIDTH
CTX_COL_BLOCKS = (1, 2, 5, 6)
CTX_RET_K, CTX_RET_V, CTX_NA_K, CTX_NA_V = 0, 1, 2, 3

NA_QROWS = 64
NA_MASK = -1e30

LANES = 128
SUBLANES = 8
VMEM_LIMIT_BYTES = 56 * 1024 * 1024

R_KEY0, R_KEY1, R_W0, R_W1 = 0, 1, 2, 3
KEY_RANK_BITS = 16
KEY_RANK_SPAN = float(1 << KEY_RANK_BITS)

EXPERT_TILE = 256
WEIGHT_DMA_PRIORITY = 1
EXPERT_DOWN_SPLIT = 4


def _params(*sem):
    return pltpu.CompilerParams(dimension_semantics=sem, vmem_limit_bytes=VMEM_LIMIT_BYTES)


def _dot(a, b):
    return jnp.dot(a, b, preferred_element_type=F32)


def _dot_nt(a, b):
    return lax.dot_general(a, b, (((1,), (1,)), ((), ())), preferred_element_type=F32)


def _dot_tn(a, b):
    return lax.dot_general(a, b, (((0,), (0,)), ((), ())), preferred_element_type=F32)


def _rms(x, w):
    return x * lax.rsqrt(jnp.mean(x * x, axis=-1, keepdims=True) + NORM_EPS) * w


def _silu(x):
    return x * jax.nn.sigmoid(x)


def _store_rows(ref, val):
    for c in range(ref.shape[1]):
        ref[:, c, :] = val[:, c * LANES:(c + 1) * LANES]


def _tile_store(buf, val):
    for c in range(buf.shape[1]):
        buf[:, c] = val[:, c * LANES:(c + 1) * LANES].reshape(buf.shape[0], SUBLANES, LANES)


def _row_major_copies(buf, hbm, first_row_tile, sem):
    return [pltpu.make_async_copy(buf.at[:, :, s, :], hbm.at[pl.ds(first_row_tile, buf.shape[0]), s], sem)
            for s in range(SUBLANES)]


def _write_rows_pipelined(buf2, hbm, sem2, val):
    i = pl.program_id(0)
    last = pl.num_programs(0) - 1
    slot = i % 2
    tiles = buf2.shape[1]

    def copies(step, s):
        return _row_major_copies(buf2.at[s], hbm, step * tiles, sem2.at[s])

    @pl.when(i >= 2)
    def _():
        for cp in copies(i - 2, slot):
            cp.wait()

    _tile_store(buf2.at[slot], val)
    for cp in copies(i, slot):
        cp.start()

    @pl.when(i == last)
    def _():
        for cp in copies(i, slot):
            cp.wait()

    @pl.when((i == last) & (i >= 1))
    def _():
        for cp in copies(i - 1, 1 - slot):
            cp.wait()


def _row_pitch(chunks):
    return chunks + 1


def _start_row_gather(src_row, chunks, dst_buf, sem, n_rows):
    def body(j, carry):
        dst = dst_buf.at[pl.ds(j * _row_pitch(chunks), chunks), :]
        pltpu.make_async_copy(src_row(j), dst, sem).start()
        return carry

    lax.fori_loop(0, n_rows, body, 0, unroll=SUBLANES)


def _wait_row_gather(dst_buf, sem, n_rows, chunks):
    view = dst_buf.at[pl.ds(0, n_rows * chunks), :]
    pltpu.make_async_copy(view, view, sem).wait()


def _load_gathered(buf, n_rows, chunks):
    return jnp.concatenate([buf[pl.ds(c, n_rows, stride=_row_pitch(chunks)), :] for c in range(chunks)],
                           axis=1)


def _mod_kernel(c_ref, w_ref, b_ref, o_ref):
    a = _silu(c_ref[...]).astype(BF16)
    o_ref[...] = _dot(a, w_ref[...].astype(BF16)) + b_ref[...]


def _modulation(cc, w_mod, b_mod):
    rows, d = cc.shape
    width = w_mod.shape[1]
    tn = next(t for t in (1024, 512, 256, LANES) if width % t == 0)
    return pl.pallas_call(
        _mod_kernel,
        grid=(width // tn,),
        in_specs=[
            pl.BlockSpec((rows, d), lambda j: (0, 0)),
            pl.BlockSpec((d, tn), lambda j: (0, j)),
            pl.BlockSpec((1, tn), lambda j: (0, j)),
        ],
        out_specs=pl.BlockSpec((rows, tn), lambda j: (0, j)),
        out_shape=jax.ShapeDtypeStruct((rows, width), F32),
        compiler_params=_params("arbitrary"),
        name="modulation",
    )(cc, w_mod, b_mod.reshape(1, width))


def _inproj_kernel(x_ref, sh_ref, sc_ref, nw_ref, w_ref, o_ref, h_ref):
    @pl.when(pl.program_id(1) == 0)
    def _():
        y = _rms(x_ref[...], nw_ref[...])
        h_ref[...] = (y * (1.0 + sc_ref[0]) + sh_ref[0]).astype(BF16)

    o_ref[...] = _dot(h_ref[...], w_ref[...].astype(BF16)).astype(BF16)


def _in_projection(x2d, mod3, mod_row_of_tile, norm_w, w_in, tm, col_blocks=None):
    m, d = x2d.shape
    tn = INPROJ_TN
    if col_blocks is None:
        col_blocks = tuple(range(w_in.shape[1] // tn))
    n_blocks = len(col_blocks)
    w_block = lambda j: sum(jnp.where(j == k, blk, 0) for k, blk in enumerate(col_blocks))
    return pl.pallas_call(
        _inproj_kernel,
        grid=(m // tm, n_blocks),
        in_specs=[
            pl.BlockSpec((tm, d), lambda i, j: (i, 0)),
            pl.BlockSpec((1, 1, d), lambda i, j: (mod_row_of_tile(i) * N_MOD + 0, 0, 0)),
            pl.BlockSpec((1, 1, d), lambda i, j: (mod_row_of_tile(i) * N_MOD + 1, 0, 0)),
            pl.BlockSpec((1, d), lambda i, j: (0, 0)),
            pl.BlockSpec((d, tn), lambda i, j: (0, w_block(j))),
        ],
        out_specs=pl.BlockSpec((tm, tn), lambda i, j: (i, j)),
        out_shape=jax.ShapeDtypeStruct((m, n_blocks * tn), BF16),
        scratch_shapes=[pltpu.VMEM((tm, d), BF16)],
        compiler_params=_params("parallel", "arbitrary"),
        name="in_projection",
    )(x2d, mod3, mod3, norm_w.reshape(1, d), w_in)


def _ret_kernel(lg_ref, q_ref, k_ref, v_ref, g_ref, ck_ref, cv_ref, cosr_ref, sinr_ref,
                cosc_ref, sinc_ref, gnw_ref, o_ref, qr_s, kr_s, sfh_s, sbh_s, sf_s, sb_s):
    head = pl.program_id(1)
    lgf = lg_ref[0, head]
    lgb = lg_ref[1, head]
    n = q_ref.shape[0]
    c = RET_CHUNK
    nc = n // c
    ctx_len = ck_ref.shape[0]
    k_scale = RET_HEAD_DIM ** -0.5

    posl = lax.broadcasted_iota(jnp.int32, (ctx_len, 1), 0).astype(F32)
    ck = ck_ref[...].astype(F32) * k_scale
    cv = cv_ref[...]
    sf_s[...] = _dot_tn((ck * jnp.exp(lgf * (ctx_len - 1.0 - posl))).astype(BF16), cv)
    sb_s[...] = _dot_tn((ck * jnp.exp(lgb * posl)).astype(BF16), cv)

    cosc = cosc_ref[...]
    sinc = sinc_ref[...]
    pos = lax.broadcasted_iota(jnp.int32, (c, 1), 0).astype(F32)
    qdec_f = jnp.exp(lgf * (pos + 1.0))
    kdec_f = jnp.exp(lgf * (c - 1.0 - pos))
    cdec_f = jnp.exp(lgf * c)
    qdec_b = jnp.exp(lgb * (c - pos))
    kdec_b = jnp.exp(lgb * pos)
    cdec_b = jnp.exp(lgb * c)

    def rope(x, cosr, sinr):
        xa = x[:, :ROPE_HALF]
        xb = x[:, ROPE_HALF:]
        ya = xa * cosr + pltpu.roll(xa, ROPE_HALF // 2, 1) * sinr
        yb = xb * cosc + pltpu.roll(xb, ROPE_HALF // 2, 1) * sinc
        return jnp.concatenate([ya, yb], axis=1)

    def fwd_chunk(ci, carry):
        r0 = pl.multiple_of(ci * c, c)
        cosr = cosr_ref[pl.ds(r0, c), :]
        sinr = sinr_ref[pl.ds(r0, c), :]
        qr_s[pl.ds(r0, c), :] = rope(q_ref[pl.ds(r0, c), :].astype(F32), cosr, sinr).astype(BF16)
        k = rope(k_ref[pl.ds(r0, c), :].astype(F32), cosr, sinr) * k_scale
        kr_s[pl.ds(r0, c), :] = k.astype(BF16)
        kv = _dot_tn((k * kdec_f).astype(BF16), v_ref[pl.ds(r0, c), :])
        state = sf_s[...]
        sfh_s[ci] = state.astype(BF16)
        sf_s[...] = state * cdec_f + kv
        return carry

    lax.fori_loop(0, nc, fwd_chunk, 0, unroll=RET_UNROLL)

    def bwd_chunk(i, carry):
        ci = nc - 1 - i
        r0 = pl.multiple_of(ci * c, c)
        k = kr_s[pl.ds(r0, c), :].astype(F32)
        kv = _dot_tn((k * kdec_b).astype(BF16), v_ref[pl.ds(r0, c), :])
        state = sb_s[...]
        sbh_s[ci] = state.astype(BF16)
        sb_s[...] = state * cdec_b + kv
        return carry

    lax.fori_loop(0, nc, bwd_chunk, 0, unroll=RET_UNROLL)

    diff = (lax.broadcasted_iota(jnp.int32, (c, c), 0)
            - lax.broadcasted_iota(jnp.int32, (c, c), 1)).astype(F32)
    intra = (jnp.where(diff >= 0, jnp.exp(lgf * jnp.maximum(diff, 0.0)), 0.0)
             + jnp.where(diff <= 0, jnp.exp(lgb * jnp.maximum(-diff, 0.0)), 0.0))
    gnw = gnw_ref[...]

    def out_chunk(ci, carry):
        r0 = pl.multiple_of(ci * c, c)
        qb = qr_s[pl.ds(r0, c), :]
        kb = kr_s[pl.ds(r0, c), :]
        q = qb.astype(F32)
        scores = _dot_nt(qb, kb) * intra
        o = (_dot(scores.astype(BF16), v_ref[pl.ds(r0, c), :])
             + _dot((q * qdec_f).astype(BF16), sfh_s[ci])
             + _dot((q * qdec_b).astype(BF16), sbh_s[ci]))
        mu = jnp.mean(o, axis=-1, keepdims=True)
        d = o - mu
        var = jnp.mean(d * d, axis=-1, keepdims=True)
        on = d * lax.rsqrt(var + NORM_EPS) * gnw
        gate = _silu(g_ref[pl.ds(r0, c), :].astype(F32))
        o_ref[pl.ds(r0, c), :] = (on * gate).astype(BF16)
        return carry

    lax.fori_loop(0, nc, out_chunk, 0, unroll=RET_UNROLL)


def _rope_tables(n):
    inv = ROPE_BASE ** (-jnp.arange(0, ROPE_HALF, 2, dtype=F32) / ROPE_HALF)

    def tables(pos):
        ang = pos.astype(F32)[:, None] * inv[None, :]
        cos = jnp.cos(ang)
        sin = jnp.sin(ang)
        return jnp.concatenate([cos, cos], axis=1), jnp.concatenate([-sin, sin], axis=1)

    t = jnp.arange(n)
    cosr, sinr = tables(t // GRID_W)
    cosc, sinc = tables(jnp.arange(RET_CHUNK) % GRID_W)
    return cosr, sinr, cosc, sinc


def _retention(proj, cproj, lg, gn_w, batch, n, ctx_len):
    hd = RET_HEAD_DIM
    cosr, sinr, cosc, sinc = _rope_tables(n)
    col = lambda which: (lambda b, h, lg_ref: (b, which * RET_HEADS + h))
    const = lambda b, h, lg_ref: (0, 0)
    grid_spec = pltpu.PrefetchScalarGridSpec(
        num_scalar_prefetch=1,
        grid=(batch, RET_HEADS),
        in_specs=[
            pl.BlockSpec((n, hd), col(0)),
            pl.BlockSpec((n, hd), col(1)),
            pl.BlockSpec((n, hd), col(2)),
            pl.BlockSpec((n, hd), col(3)),
            pl.BlockSpec((ctx_len, hd), lambda b, h, lg_ref: (b, CTX_RET_K * RET_HEADS + h)),
            pl.BlockSpec((ctx_len, hd), lambda b, h, lg_ref: (b, CTX_RET_V * RET_HEADS + h)),
            pl.BlockSpec((n, ROPE_HALF), const),
            pl.BlockSpec((n, ROPE_HALF), const),
            pl.BlockSpec((RET_CHUNK, ROPE_HALF), const),
            pl.BlockSpec((RET_CHUNK, ROPE_HALF), const),
            pl.BlockSpec((1, hd), lambda b, h, lg_ref: (0, h)),
        ],
        out_specs=pl.BlockSpec((n, hd), lambda b, h, lg_ref: (b, h)),
        scratch_shapes=[
            pltpu.VMEM((n, hd), BF16),
            pltpu.VMEM((n, hd), BF16),
            pltpu.VMEM((n // RET_CHUNK, hd, hd), BF16),
            pltpu.VMEM((n // RET_CHUNK, hd, hd), BF16),
            pltpu.VMEM((hd, hd), F32),
            pltpu.VMEM((hd, hd), F32),
        ],
    )
    return pl.pallas_call(
        _ret_kernel,
        grid_spec=grid_spec,
        out_shape=jax.ShapeDtypeStruct((batch * n, RET_WIDTH), BF16),
        compiler_params=_params("parallel", "arbitrary"),
        name="retention",
    )(lg, proj, proj, proj, proj, cproj, cproj, cosr, sinr, cosc, sinc, gn_w.reshape(1, RET_WIDTH))


def _na_build_bias(rpb_ref, head, pair_s):
    n_roff, n_coff = 2 * NA_ROWS - 1, 2 * NA_COLS - 1
    c = lax.broadcasted_iota(jnp.int32, (GRID_W, LANES), 0)
    lane = lax.broadcasted_iota(jnp.int32, (GRID_W, LANES), 1)
    kc = jnp.bitwise_and(lane, GRID_W - 1)
    cs = jnp.clip(c - NA_COLS // 2, 0, GRID_W - NA_COLS)
    d = jnp.where((kc >= cs) & (kc < cs + NA_COLS), kc - c + (NA_COLS - 1), -1)

    def table(i):
        acc = jnp.full((GRID_W, LANES), NA_MASK, F32)
        base = (head * n_roff + i) * n_coff
        for j in range(n_coff):
            acc = jnp.where(d == j, rpb_ref[base + j], acc)
        return acc

    prev = table(0)
    for i in range(1, n_roff):
        cur = table(i)
        pair_s[i - 1] = jnp.where(lane < GRID_W, prev, cur)
        prev = cur


def _na_kernel(rpb_ref, q_ref, k_ref, v_ref, kc_ref, vc_ref, o_ref, s_s, m_s, pair_s):
    @pl.when(pl.program_id(1) == 0)
    def _():
        _na_build_bias(rpb_ref, pl.program_id(0), pair_s)

    n = q_ref.shape[0]
    rows = n // GRID_W
    kb = NA_ROWS * GRID_W
    ctx_len = kc_ref.shape[0]
    scale = NA_HEAD_DIM ** -0.5
    kc = kc_ref[...]
    vc = jnp.concatenate([vc_ref[...], jnp.ones((ctx_len, NA_HEAD_DIM), BF16)], axis=1)
    ones_w = jnp.ones((kb, NA_HEAD_DIM), BF16)

    def window(r):
        rs = jnp.clip(r - NA_ROWS // 2, 0, rows - NA_ROWS)
        return rs, pl.multiple_of(r * GRID_W, GRID_W), pl.multiple_of(rs * GRID_W, GRID_W)

    def score_row(r):
        rs, q0, k0 = window(r)
        q = (q_ref[pl.ds(q0, GRID_W), :].astype(F32) * scale).astype(BF16)
        off = rs - r + NA_ROWS - 1
        bias = jnp.concatenate([pair_s[off + 2 * t] for t in range(NA_ROWS // 2)], axis=1)
        s_win = _dot_nt(q, k_ref[pl.ds(k0, kb), :]) + bias
        s_ctx = _dot_nt(q, kc)
        m = jnp.maximum(jnp.max(s_win, axis=-1, keepdims=True), jnp.max(s_ctx, axis=-1, keepdims=True))
        s_s[pl.ds(q0, GRID_W), :kb] = s_win
        s_s[pl.ds(q0, GRID_W), kb:] = s_ctx
        m_s[pl.ds(q0, GRID_W), :] = jnp.broadcast_to(m, (GRID_W, LANES))

    def value_row(r):
        _, q0, k0 = window(r)
        m = m_s[pl.ds(q0, GRID_W), :]
        p_win = jnp.exp(s_s[pl.ds(q0, GRID_W), :kb] - jnp.tile(m, (1, kb // LANES))).astype(BF16)
        p_ctx = jnp.exp(s_s[pl.ds(q0, GRID_W), kb:] - jnp.tile(m, (1, ctx_len // LANES))).astype(BF16)
        vw = jnp.concatenate([v_ref[pl.ds(k0, kb), :], ones_w], axis=1)
        o = _dot(p_win, vw) + _dot(p_ctx, vc)
        o_ref[pl.ds(q0, GRID_W), :] = (o[:, :NA_HEAD_DIM] / o[:, NA_HEAD_DIM:]).astype(BF16)

    qrows = math.gcd(rows, NA_QROWS)

    def sweep(row_fn):
        def group(g, carry):
            for u in range(qrows):
                row_fn(g * qrows + u)
            return carry
        lax.fori_loop(0, rows // qrows, group, 0)

    sweep(score_row)
    sweep(value_row)


def _neighbourhood_attention(proj, cproj, rpb, batch, n, ctx_len):
    hd = NA_HEAD_DIM
    rows = n // GRID_W
    assert rows >= NA_ROWS
    assert rpb.shape == (NA_HEADS, 2 * NA_ROWS - 1, 2 * NA_COLS - 1)
    base = 4 * RET_WIDTH // hd
    col = lambda which: (lambda h, b, rpb_ref: (b, base + which * NA_HEADS + h))
    grid_spec = pltpu.PrefetchScalarGridSpec(
        num_scalar_prefetch=1,
        grid=(NA_HEADS, batch),
        in_specs=[
            pl.BlockSpec((n, hd), col(0)),
            pl.BlockSpec((n, hd), col(1)),
            pl.BlockSpec((n, hd), col(2)),
            pl.BlockSpec((ctx_len, hd), lambda h, b, rpb_ref: (b, CTX_NA_K * NA_HEADS + h)),
            pl.BlockSpec((ctx_len, hd), lambda h, b, rpb_ref: (b, CTX_NA_V * NA_HEADS + h)),
        ],
        out_specs=pl.BlockSpec((n, hd), lambda h, b, rpb_ref: (b, h)),
        scratch_shapes=[
            pltpu.VMEM((n, NA_ROWS * GRID_W + ctx_len), F32),
            pltpu.VMEM((n, LANES), F32),
            pltpu.VMEM((2 * NA_ROWS - 2, GRID_W, LANES), F32),
        ],
    )
    return pl.pallas_call(
        _na_kernel,
        grid_spec=grid_spec,
        out_shape=jax.ShapeDtypeStruct((batch * n, NA_WIDTH), BF16),
        compiler_params=_params("parallel", "arbitrary"),
        name="neighbourhood_attention",
    )(rpb.astype(F32).reshape(-1), proj, proj, proj, cproj, cproj)


def _outproj_kernel(n_groups, per_group, ret_ref, na_ref, w1_ref, w2_ref, x_ref, ga_ref, shf_ref,
                    scf_ref, nw_ref, wr_ref, br_ref, x1_ref, route_ref, cnt_ref, hf_ref, carry_s,
                    earlier_s, hbuf, hsem):
    tm = x_ref.shape[0]

    @pl.when(pl.program_id(0) == 0)
    def _():
        carry_s[...] = jnp.zeros_like(carry_s)
        earlier_s[...] = jnp.where(lax.broadcasted_iota(jnp.int32, (tm, tm), 0)
                                   > lax.broadcasted_iota(jnp.int32, (tm, tm), 1), 1.0, 0.0).astype(BF16)

    acc = _dot(ret_ref[...], w1_ref[...]) + _dot(na_ref[...], w2_ref[...])
    x1 = x_ref[...] + ga_ref[0] * acc
    x1_ref[...] = x1
    hf = _rms(x1, nw_ref[...]) * (1.0 + scf_ref[0]) + shf_ref[0]
    _write_rows_pipelined(hbuf, hf_ref, hsem, hf)

    hf_hi = hf.astype(BF16)
    hf_lo = (hf - hf_hi.astype(F32)).astype(BF16)
    p_hi = _dot(hf_hi, wr_ref[...])
    p_lo = _dot(hf_lo, wr_ref[...])
    logits = p_hi[:, :LANES] + p_hi[:, LANES:] + p_lo[:, :LANES] + br_ref[...]
    lane = lax.broadcasted_iota(jnp.int32, (tm, LANES), 1)
    neg = -jnp.inf

    def first_max(vals):
        top = jnp.max(vals, axis=-1, keepdims=True)
        idx = jnp.min(jnp.where(vals == top, lane, LANES), axis=-1, keepdims=True)
        return top, idx

    g_logits = jnp.where(lane < n_groups, logits, neg)
    g_top, g_sel = first_max(g_logits)
    g_w = 1.0 / jnp.sum(jnp.exp(g_logits - g_top), axis=-1, keepdims=True)
    lo = n_groups + g_sel * per_group
    e_logits = jnp.where((lane >= lo) & (lane < lo + per_group), logits, neg)
    v0, i0 = first_max(e_logits)
    v1, i1 = first_max(jnp.where(lane == i0, neg, e_logits))
    e1 = jnp.exp(v1 - v0)
    w0 = g_w / (1.0 + e1)
    w1 = g_w * e1 / (1.0 + e1)

    hit0 = lane == i0
    hit1 = lane == i1
    onehot = jnp.where(hit0 | hit1, 1.0, 0.0)
    before = _dot(earlier_s[...], onehot.astype(BF16)) + carry_s[...]
    rank0 = jnp.sum(jnp.where(hit0, before, 0.0), axis=-1, keepdims=True)
    rank1 = jnp.sum(jnp.where(hit1, before, 0.0), axis=-1, keepdims=True)
    carry_s[...] = carry_s[...] + jnp.sum(onehot, axis=0, keepdims=True)
    cnt_ref[...] = carry_s[...]

    key0 = (i0 - n_groups).astype(F32) * KEY_RANK_SPAN + rank0
    key1 = (i1 - n_groups).astype(F32) * KEY_RANK_SPAN + rank1
    fields = {R_KEY0: key0, R_KEY1: key1, R_W0: w0, R_W1: w1}
    route = jnp.zeros((tm, LANES), F32)
    for idx, val in fields.items():
        route = jnp.where(lane == idx, val, route)
    route_ref[...] = route


def _out_projection(ret, na, w_out, x2d, mod3, norm_w, w_route, b_route, n, n_groups, per_group, tm):
    m, d = x2d.shape
    chunks = d // LANES
    batch_of = lambda i: (i * tm) // n
    mod_spec = lambda which: pl.BlockSpec((1, 1, d), lambda i: (batch_of(i) * N_MOD + which, 0, 0))
    const2 = lambda i: (0, 0)
    return pl.pallas_call(
        functools.partial(_outproj_kernel, n_groups, per_group),
        grid=(m // tm,),
        in_specs=[
            pl.BlockSpec((tm, RET_WIDTH), lambda i: (i, 0)),
            pl.BlockSpec((tm, NA_WIDTH), lambda i: (i, 0)),
            pl.BlockSpec((RET_WIDTH, d), lambda i: (0, 0)),
            pl.BlockSpec((NA_WIDTH, d), lambda i: (RET_WIDTH // NA_WIDTH, 0)),
            pl.BlockSpec((tm, d), lambda i: (i, 0)),
            mod_spec(2), mod_spec(3), mod_spec(4),
            pl.BlockSpec((1, d), const2),
            pl.BlockSpec((d, 2 * LANES), const2),
            pl.BlockSpec((1, LANES), const2),
        ],
        out_specs=[
            pl.BlockSpec((tm, d), lambda i: (i, 0)),
            pl.BlockSpec((tm, LANES), lambda i: (i, 0)),
            pl.BlockSpec((1, LANES), const2),
            pl.BlockSpec(memory_space=pl.ANY),
        ],
        out_shape=[
            jax.ShapeDtypeStruct((m, d), F32),
            jax.ShapeDtypeStruct((m, LANES), F32),
            jax.ShapeDtypeStruct((1, LANES), F32),
            jax.ShapeDtypeStruct((m // SUBLANES, SUBLANES, chunks, LANES), F32),
        ],
        scratch_shapes=[
            pltpu.VMEM((1, LANES), F32),
            pltpu.VMEM((tm, tm), BF16),
            pltpu.VMEM((2, tm // SUBLANES, chunks, SUBLANES, LANES), F32),
            pltpu.SemaphoreType.DMA((2,)),
        ],
        compiler_params=_params("arbitrary"),
        name="out_projection_router",
    )(ret, na, w_out, w_out, x2d, mod3, mod3, mod3, norm_w.reshape(1, d), w_route, b_route)


def _slot_map_kernel(tile_rows, pos_ref, cnt_ref, slot_ref, tile_expert_ref, next_expert_ref,
                     run_parity_ref, n_used_ref, offs_s):
    n_experts = cnt_ref.shape[0]
    n_tiles = tile_expert_ref.shape[0]
    shift = tile_rows.bit_length() - 1
    assert 1 << shift == tile_rows

    def fill(ref, lo, hi, val):
        value_at = val if callable(val) else (lambda s: val)
        groups = lax.div(hi - lo, SUBLANES)

        def group(g, carry):
            for k in range(SUBLANES):
                s = lo + g * SUBLANES + k
                ref[s] = value_at(s)
            return carry

        def single(s, carry):
            ref[s] = value_at(s)
            return carry

        lax.fori_loop(0, groups, group, 0)
        lax.fori_loop(lo + groups * SUBLANES, hi, single, 0)

    assert tile_rows <= pos_ref.shape[0] // 2
    padding_row = lambda s: jnp.bitwise_and(s, tile_rows - 1)

    def tiles_of(e):
        size = lax.shift_left(lax.shift_right_logical(cnt_ref[e] + (tile_rows - 1), shift), shift)
        return lax.shift_right_logical(offs_s[e], shift), lax.shift_right_logical(offs_s[e] + size, shift)

    def per_expert(e, carry):
        start, runs = carry
        cnt = cnt_ref[e]
        size = lax.shift_left(lax.shift_right_logical(cnt + (tile_rows - 1), shift), shift)
        offs_s[e] = start
        t0, t1 = tiles_of(e)
        fill(tile_expert_ref, t0, t1, e)
        fill(run_parity_ref, t0, t1, jnp.bitwise_and(runs, 1))
        fill(slot_ref, start + cnt, start + size, padding_row)
        return start + size, runs + (size > 0).astype(jnp.int32)

    end, _ = lax.fori_loop(0, n_experts, per_expert, (0, 0))
    n_used = lax.shift_right_logical(end, shift)
    n_used_ref[0] = n_used
    fill(tile_expert_ref, n_used, n_tiles, n_experts - 1)
    fill(run_parity_ref, n_used, n_tiles, 0)
    fill(next_expert_ref, n_used, n_tiles, -1)
    fill(slot_ref, end, slot_ref.shape[0], padding_row)

    def per_expert_reversed(k, following):
        e = n_experts - 1 - k
        t0, t1 = tiles_of(e)
        fill(next_expert_ref, t0, t1, following)
        return jnp.where(t1 > t0, e, following)

    lax.fori_loop(0, n_experts, per_expert_reversed, -1)

    def assign(a, carry):
        slot_ref[pos_ref[a]] = lax.shift_right_logical(a, 1)
        return carry

    lax.fori_loop(0, pos_ref.shape[0], assign, 0, unroll=2 * SUBLANES)


def _slot_map(pos, cnt, n_tiles, tile_rows):
    smem = pl.BlockSpec(memory_space=pltpu.SMEM)
    i32 = lambda *shape: jax.ShapeDtypeStruct(shape, jnp.int32)
    return pl.pallas_call(
        functools.partial(_slot_map_kernel, tile_rows),
        in_specs=[smem, smem],
        out_specs=[smem] * 5,
        out_shape=[i32(n_tiles * tile_rows), i32(n_tiles), i32(n_tiles), i32(n_tiles), i32(1)],
        scratch_shapes=[pltpu.SMEM((cnt.shape[0],), jnp.int32)],
        name="slot_map",
    )(pos, cnt)


def _expert_kernel(tile_expert_ref, next_expert_ref, run_parity_ref, n_used_ref, tokens_ref,
                   next_tokens_ref, hf_ref, wg_hbm, wu_hbm, wd_hbm, o_ref, xbuf, gsem, wg_f, wu_f,
                   wd_f, wsem, wg_s, wu_s, wd_s):
    i = pl.program_id(0)
    last = pl.num_programs(0) - 1
    n_used = n_used_ref[0]
    tm, chunks, _ = o_ref.shape
    ff = wg_s.shape[1]

    def row_copy(ids_ref, slot, j):
        dst = xbuf.at[slot, pl.ds(j * _row_pitch(chunks), chunks), :]
        return pltpu.make_async_copy(hf_ref.at[ids_ref[0, 0, j]], dst, gsem.at[slot])

    def weight_copies(expert, slot):
        return [pltpu.make_async_copy(hbm.at[expert], stage.at[slot], wsem.at[slot])
                for hbm, stage in ((wg_hbm, wg_f), (wu_hbm, wu_f), (wd_hbm, wd_f))]

    @pl.when((i == 0) & (n_used > 0))
    def _():
        _start_row_gather(lambda j: hf_ref.at[tokens_ref[0, 0, j]], chunks, xbuf.at[0], gsem.at[0], tm)
        for cp in weight_copies(tile_expert_ref[0], run_parity_ref[0]):
            cp.start(priority=WEIGHT_DMA_PRIORITY)

    @pl.when(i < n_used)
    def _():
        @pl.when((i == 0) | (tile_expert_ref[i] != tile_expert_ref[jnp.maximum(i - 1, 0)]))
        def _():
            stage = run_parity_ref[i]
            for cp in weight_copies(tile_expert_ref[i], stage):
                cp.wait()

            @pl.when(next_expert_ref[i] >= 0)
            def _():
                for cp in weight_copies(next_expert_ref[i], 1 - stage):
                    cp.start(priority=WEIGHT_DMA_PRIORITY)

            wg_s[...] = wg_f[stage].astype(BF16)
            wu_s[...] = wu_f[stage].astype(BF16)
            wd_s[...] = wd_f[stage].astype(BF16)

        slot = i % 2
        _wait_row_gather(xbuf.at[slot], gsem.at[slot], tm, chunks)
        x = _load_gathered(xbuf.at[slot], tm, chunks).astype(BF16)

        down_split = min(EXPERT_DOWN_SPLIT, chunks)
        n_stages = 2 + down_split
        bounds = [tm * s // n_stages for s in range(n_stages + 1)]

        def start_next_rows(stage):
            for j in range(bounds[stage], bounds[stage + 1]):
                row_copy(next_tokens_ref, 1 - slot, j).start()

        halves = []
        for half in range(2):
            start_next_rows(half)
            cols = slice(half * ff // 2, (half + 1) * ff // 2)
            halves.append((_silu(_dot(x, wg_s[:, cols])) * _dot(x, wu_s[:, cols])).astype(BF16))
        a = jnp.concatenate(halves, axis=1)
        per_part = chunks // down_split
        for part in range(down_split):
            start_next_rows(2 + part)
            y = _dot(a, wd_s[:, part * per_part * LANES:(part + 1) * per_part * LANES])
            for c in range(per_part):
                o_ref[:, part * per_part + c, :] = y[:, c * LANES:(c + 1) * LANES]

        @pl.when(i == last)
        def _():
            _wait_row_gather(xbuf.at[1 - slot], gsem.at[1 - slot], tm, chunks)

    @pl.when(i >= n_used)
    def _():
        @pl.when((i == n_used) & (i > 0))
        def _():
            _wait_row_gather(xbuf.at[i % 2], gsem.at[i % 2], tm, chunks)

        o_ref[...] = jnp.zeros(o_ref.shape, o_ref.dtype)


def _experts(tile_meta, slot_token, hf, w_gate, w_up, w_down, n_tiles, tm):
    chunks = hf.shape[1]
    d = chunks * LANES
    ff = w_gate.shape[-1]
    slot_tiles = slot_token.reshape(n_tiles, 1, tm)
    any_space = pl.BlockSpec(memory_space=pl.ANY)
    ids_block = lambda ahead: pl.BlockSpec(
        (1, 1, tm), lambda i, *prefetch: (jnp.minimum(i + ahead, n_tiles - 1), 0, 0),
        memory_space=pltpu.SMEM)
    grid_spec = pltpu.PrefetchScalarGridSpec(
        num_scalar_prefetch=4,
        grid=(n_tiles,),
        in_specs=[ids_block(0), ids_block(1), any_space, any_space, any_space, any_space],
        out_specs=pl.BlockSpec((tm, chunks, LANES), lambda i, *prefetch: (i, 0, 0)),
        scratch_shapes=[
            pltpu.VMEM((2, tm * _row_pitch(chunks), LANES), hf.dtype),
            pltpu.SemaphoreType.DMA((2,)),
            pltpu.VMEM((2, d, ff), F32),
            pltpu.VMEM((2, d, ff), F32),
            pltpu.VMEM((2, ff, d), F32),
            pltpu.SemaphoreType.DMA((2,)),
            pltpu.VMEM((d, ff), BF16),
            pltpu.VMEM((d, ff), BF16),
            pltpu.VMEM((ff, d), BF16),
        ],
    )
    return pl.pallas_call(
        _expert_kernel,
        grid_spec=grid_spec,
        out_shape=jax.ShapeDtypeStruct((n_tiles * tm, chunks, LANES), hf.dtype),
        compiler_params=_params("arbitrary"),
        name="routed_experts",
    )(*tile_meta, slot_tiles, slot_tiles, hf, w_gate, w_up, w_down)


def _combine_kernel(pos_ref, ys_ref, x1_ref, route_ref, gf_ref, fw_ref, o_ref, ybuf, sem):
    i = pl.program_id(0)
    n_tiles = pl.num_programs(0)
    tm = x1_ref.shape[0]
    chunks = ys_ref.shape[1]

    def start_tile(tile, slot):
        for choice in range(2):
            _start_row_gather(lambda j: ys_ref.at[pos_ref[(tile * tm + j) * 2 + choice]], chunks,
                              ybuf.at[slot, choice], sem.at[slot, choice], tm)

    @pl.when(i == 0)
    def _():
        start_tile(0, 0)

    @pl.when(i + 1 < n_tiles)
    def _():
        start_tile(i + 1, (i + 1) % 2)

    slot = i % 2
    for choice in range(2):
        _wait_row_gather(ybuf.at[slot, choice], sem.at[slot, choice], tm, chunks)
    route = route_ref[...]
    moe = (route[:, R_W0:R_W0 + 1] * _load_gathered(ybuf.at[slot, 0], tm, chunks)
           + route[:, R_W1:R_W1 + 1] * _load_gathered(ybuf.at[slot, 1], tm, chunks))
    x2 = x1_ref[...] + gf_ref[0] * moe
    o_ref[...] = _rms(x2, fw_ref[...])


def _combine(pos, ys, x1, route, mod3, final_w, n, tm):
    m, d = x1.shape
    chunks = ys.shape[1]
    grid_spec = pltpu.PrefetchScalarGridSpec(
        num_scalar_prefetch=1,
        grid=(m // tm,),
        in_specs=[
            pl.BlockSpec(memory_space=pl.ANY),
            pl.BlockSpec((tm, d), lambda i, pos_ref: (i, 0)),
            pl.BlockSpec((tm, LANES), lambda i, pos_ref: (i, 0)),
            pl.BlockSpec((1, 1, d), lambda i, pos_ref: (((i * tm) // n) * N_MOD + 5, 0, 0)),
            pl.BlockSpec((1, d), lambda i, pos_ref: (0, 0)),
        ],
        out_specs=pl.BlockSpec((tm, d), lambda i, pos_ref: (i, 0)),
        scratch_shapes=[pltpu.VMEM((2, 2, tm * _row_pitch(chunks), LANES), ys.dtype),
                        pltpu.SemaphoreType.DMA((2, 2))],
    )
    return pl.pallas_call(
        _combine_kernel,
        grid_spec=grid_spec,
        out_shape=jax.ShapeDtypeStruct((m, d), F32),
        compiler_params=_params("arbitrary"),
        name="combine_final_norm",
    )(pos, ys, x1, route, mod3, final_w.reshape(1, d))


def kernel(x, c, ctx, c_ctx, w_mod, b_mod, norm_mix_w, w_in, ret_decay_f, ret_decay_b, ret_gn_w, na_rpb, w_out, norm_ffn_w, w_router_group, b_router_group, w_router_expert, b_router_expert, w_gate, w_up, w_down, final_norm_w):
    assert w_mod.shape[0] == 1, "single trunk layer"
    batch, n, d = x.shape
    ctx_len = ctx.shape[1]
    n_groups = w_router_group.shape[-1]
    per_group = w_router_expert.shape[-1]
    n_experts = w_gate.shape[1]
    assert n_groups * per_group == n_experts and n_groups + n_experts <= LANES

    mod_rows = 8
    cc = jnp.zeros((mod_rows, d), F32).at[:batch].set(c).at[batch].set(c_ctx)
    mod = _modulation(cc, w_mod[0], b_mod[0])
    mod3 = mod.reshape(mod_rows * N_MOD, 1, d)

    w_in_b = w_in[0]
    tm = min(512, n)
    tm_in = min(1024, n)
    x2d = x.reshape(batch * n, d)
    proj = _in_projection(x2d, mod3, lambda i: (i * tm_in) // n, norm_mix_w[0], w_in_b, tm_in)
    cproj = _in_projection(ctx.reshape(batch * ctx_len, d), mod3, lambda i: batch, norm_mix_w[0],
                           w_in_b, batch * ctx_len, CTX_COL_BLOCKS)

    lg = jnp.stack([jax.nn.log_sigmoid(ret_decay_f[0].astype(F32)),
                    jax.nn.log_sigmoid(ret_decay_b[0].astype(F32))])
    ret = _retention(proj, cproj, lg, ret_gn_w[0], batch, n, ctx_len)
    na = _neighbourhood_attention(proj, cproj, na_rpb[0], batch, n, ctx_len)

    w_route = jnp.concatenate(
        [w_router_group[0], jnp.moveaxis(w_router_expert[0], 0, 1).reshape(d, n_experts)], axis=1)
    w_route = jnp.pad(w_route.astype(F32), ((0, 0), (0, LANES - n_groups - n_experts)))
    w_route_hi = w_route.astype(BF16)
    w_route = jnp.concatenate([w_route_hi, (w_route - w_route_hi.astype(F32)).astype(BF16)], axis=1)
    b_route = jnp.concatenate([b_router_group[0], b_router_expert[0].reshape(-1)])
    b_route = jnp.pad(b_route.astype(F32), (0, LANES - n_groups - n_experts)).reshape(1, LANES)
    x1, route, counts, hf = _out_projection(ret, na, w_out[0].astype(BF16), x2d, mod3, norm_ffn_w[0],
                                            w_route, b_route, n, n_groups, per_group, tm)

    te = EXPERT_TILE
    tokens = batch * n
    assert tokens < 1 << KEY_RANK_BITS and (2 * tokens) % te == 0
    n_tiles = (2 * tokens) // te + n_experts
    cnt = counts[0, n_groups:n_groups + n_experts].astype(jnp.int32)
    padded = (cnt + te - 1) // te * te
    starts = jnp.cumsum(padded) - padded
    keys = route[:, R_KEY0:R_KEY1 + 1].astype(jnp.int32).reshape(-1)
    expert = lax.shift_right_logical(keys, KEY_RANK_BITS)
    start_of = jnp.sum(jnp.where(expert[:, None] == jnp.arange(n_experts)[None, :], starts[None, :], 0), axis=1)
    pos = start_of + jnp.bitwise_and(keys, (1 << KEY_RANK_BITS) - 1)
    slot_token, *tile_meta = _slot_map(pos, cnt, n_tiles, te)

    hf_rows = hf.reshape(tokens, d // LANES, LANES)
    ys = _experts(tile_meta, slot_token, hf_rows, w_gate[0], w_up[0], w_down[0], n_tiles, te)
    out = _combine(pos, ys, x1, route, mod3, final_norm_w, n, min(256, n))
    return out.reshape(batch, n, d)
```

```python
import functools
import math

import jax
import jax.numpy as jnp
from jax import lax
from jax.experimental import pallas as pl
from jax.experimental.pallas import tpu as pltpu

F32 = jnp.float32
BF16 = jnp.bfloat16

GRID_W = 64
RET_HEADS = 4
RET_HEAD_DIM = 256
RET_WIDTH = RET_HEADS * RET_HEAD_DIM
NA_HEADS = 8
NA_HEAD_DIM = 128
NA_WIDTH = NA_HEADS * NA_HEAD_DIM
RET_CHUNK = 128
NA_ROWS = 8
NA_COLS = 16
ROPE_BASE = 10000.0
N_MOD = 6
NORM_EPS = 1e-6
ROPE_HALF = RET_HEAD_DIM // 2
RET_UNROLL = 8

INPROJ_TN = 1024
assert INPROJ_TN == RET_WIDTH == NA_WIDTH
CTX_COL_BLOCKS = (1, 2, 5, 6)
CTX_RET_K, CTX_RET_V, CTX_NA_K, CTX_NA_V = 0, 1, 2, 3

NA_QROWS = 16
NA_MASK = -1e30

LANES = 128
SUBLANES = 8
BF16_ROWS = 2 * SUBLANES
VMEM_LIMIT_BYTES = 56 * 1024 * 1024

R_KEY0, R_KEY1, R_W0, R_W1 = 0, 1, 2, 3
KEY_RANK_BITS = 16
KEY_RANK_SPAN = float(1 << KEY_RANK_BITS)

EXPERT_TILE = 256
WEIGHT_DMA_PRIORITY = 1
EXPERT_DOWN_SPLIT = 4


def _params(*sem):
    return pltpu.CompilerParams(dimension_semantics=sem, vmem_limit_bytes=VMEM_LIMIT_BYTES)


def _dot(a, b):
    return jnp.dot(a, b, preferred_element_type=F32)


def _dot_nt(a, b):
    return lax.dot_general(a, b, (((1,), (1,)), ((), ())), preferred_element_type=F32)


def _dot_tn(a, b):
    return lax.dot_general(a, b, (((0,), (0,)), ((), ())), preferred_element_type=F32)


def _rms(x, w):
    return x * lax.rsqrt(jnp.mean(x * x, axis=-1, keepdims=True) + NORM_EPS) * w


def _silu(x):
    return x * jax.nn.sigmoid(x)


def _store_rows(ref, val):
    for c in range(ref.shape[1]):
        ref[:, c, :] = val[:, c * LANES:(c + 1) * LANES]


def _tile_store(buf, val):
    for c in range(buf.shape[1]):
        buf[:, c] = val[:, c * LANES:(c + 1) * LANES].reshape(buf.shape[0], SUBLANES, LANES)


def _row_major_copies(buf, hbm, first_row_tile, sem):
    return [pltpu.make_async_copy(buf.at[:, :, s, :], hbm.at[pl.ds(first_row_tile, buf.shape[0]), s], sem)
            for s in range(SUBLANES)]


def _write_rows_pipelined(buf2, hbm, sem2, val):
    i = pl.program_id(0)
    last = pl.num_programs(0) - 1
    slot = i % 2
    tiles = buf2.shape[1]

    def copies(step, s):
        return _row_major_copies(buf2.at[s], hbm, step * tiles, sem2.at[s])

    @pl.when(i >= 2)
    def _():
        for cp in copies(i - 2, slot):
            cp.wait()

    _tile_store(buf2.at[slot], val)
    for cp in copies(i, slot):
        cp.start()

    @pl.when(i == last)
    def _():
        for cp in copies(i, slot):
            cp.wait()

    @pl.when((i == last) & (i >= 1))
    def _():
        for cp in copies(i - 1, 1 - slot):
            cp.wait()


def _row_pitch(chunks):
    return chunks + 1


def _start_row_gather(src_row, chunks, dst_buf, sem, n_rows):
    def body(j, carry):
        dst = dst_buf.at[pl.ds(j * _row_pitch(chunks), chunks), :]
        pltpu.make_async_copy(src_row(j), dst, sem).start()
        return carry

    lax.fori_loop(0, n_rows, body, 0, unroll=SUBLANES)


def _wait_row_gather(dst_buf, sem, n_rows, chunks):
    view = dst_buf.at[pl.ds(0, n_rows * chunks), :]
    pltpu.make_async_copy(view, view, sem).wait()


def _load_gathered(buf, n_rows, chunks):
    return jnp.concatenate([buf[pl.ds(c, n_rows, stride=_row_pitch(chunks)), :] for c in range(chunks)],
                           axis=1)


def _mod_kernel(c_ref, w_ref, b_ref, o_ref):
    a = _silu(c_ref[...]).astype(BF16)
    o_ref[...] = _dot(a, w_ref[...].astype(BF16)) + b_ref[...]


def _modulation(cc, w_mod, b_mod):
    rows, d = cc.shape
    width = w_mod.shape[1]
    tn = next(t for t in (1024, 512, 256, LANES) if width % t == 0)
    return pl.pallas_call(
        _mod_kernel,
        grid=(width // tn,),
        in_specs=[
            pl.BlockSpec((rows, d), lambda j: (0, 0)),
            pl.BlockSpec((d, tn), lambda j: (0, j)),
            pl.BlockSpec((1, tn), lambda j: (0, j)),
        ],
        out_specs=pl.BlockSpec((rows, tn), lambda j: (0, j)),
        out_shape=jax.ShapeDtypeStruct((rows, width), F32),
        compiler_params=_params("arbitrary"),
        name="modulation",
    )(cc, w_mod, b_mod.reshape(1, width))


def _inproj_kernel(rows_per_step, x_ref, sh_ref, sc_ref, nw_ref, w_ref, o_ref, h_s):
    i = pl.program_id(0)
    j = pl.program_id(1)
    tm = o_ref.shape[0]

    def normalise(rows, slot):
        y = _rms(x_ref[rows, :], nw_ref[...])
        h_s[slot, rows, :] = (y * (1.0 + sc_ref[0]) + sh_ref[0]).astype(BF16)

    @pl.when((i == 0) & (j == 0))
    def _():
        normalise(pl.ds(0, tm), 0)

    o_ref[...] = _dot(h_s[i % 2], w_ref[...].astype(BF16)).astype(BF16)
    start = jnp.clip((j - 1) * rows_per_step, 0, tm - rows_per_step)
    normalise(pl.ds(pl.multiple_of(start, BF16_ROWS), rows_per_step), (i + 1) % 2)


def _in_projection(x2d, mod3, mod_row_of_tile, norm_w, w_in, tm, col_blocks=None):
    m, d = x2d.shape
    tn = INPROJ_TN
    if col_blocks is None:
        col_blocks = tuple(range(w_in.shape[1] // tn))
    n_blocks = len(col_blocks)
    n_tiles = m // tm
    assert n_blocks >= 2 and tm % BF16_ROWS == 0
    rows_per_step = -(-tm // ((n_blocks - 1) * BF16_ROWS)) * BF16_ROWS
    w_block = lambda j: sum(jnp.where(j == k, blk, 0) for k, blk in enumerate(col_blocks))
    src = lambda i, j: jnp.where((i == 0) & (j == 0), 0, jnp.minimum(i + 1, n_tiles - 1))
    return pl.pallas_call(
        functools.partial(_inproj_kernel, rows_per_step),
        grid=(n_tiles, n_blocks),
        in_specs=[
            pl.BlockSpec((tm, d), lambda i, j: (src(i, j), 0)),
            pl.BlockSpec((1, 1, d), lambda i, j: (mod_row_of_tile(src(i, j)) * N_MOD + 0, 0, 0)),
            pl.BlockSpec((1, 1, d), lambda i, j: (mod_row_of_tile(src(i, j)) * N_MOD + 1, 0, 0)),
            pl.BlockSpec((1, d), lambda i, j: (0, 0)),
            pl.BlockSpec((d, tn), lambda i, j: (0, w_block(j))),
        ],
        out_specs=pl.BlockSpec((tm, tn), lambda i, j: (i, j)),
        out_shape=jax.ShapeDtypeStruct((m, n_blocks * tn), BF16),
        scratch_shapes=[pltpu.VMEM((2, tm, d), BF16)],
        compiler_params=_params("arbitrary", "arbitrary"),
        name="in_projection",
    )(x2d, mod3, mod3, norm_w.reshape(1, d), w_in)


def _ret_kernel(lg_ref, q_ref, k_ref, v_ref, g_ref, ck_ref, cv_ref, cosr_ref, sinr_ref,
                cosc_ref, sinc_ref, gnw_ref, o_ref, qr_s, kr_s, sfh_s, sbh_s, sf_s, sb_s):
    head = pl.program_id(1)
    lgf = lg_ref[0, head]
    lgb = lg_ref[1, head]
    n = q_ref.shape[0]
    c = RET_CHUNK
    nc = n // c
    ctx_len = ck_ref.shape[0]
    k_scale = RET_HEAD_DIM ** -0.5

    posl = lax.broadcasted_iota(jnp.int32, (ctx_len, 1), 0).astype(F32)
    ck = ck_ref[...].astype(F32) * k_scale
    cv = cv_ref[...]
    sf_s[...] = _dot_tn((ck * jnp.exp(lgf * (ctx_len - 1.0 - posl))).astype(BF16), cv)
    sb_s[...] = _dot_tn((ck * jnp.exp(lgb * posl)).astype(BF16), cv)

    cosc = cosc_ref[...]
    sinc = sinc_ref[...]
    pos = lax.broadcasted_iota(jnp.int32, (c, 1), 0).astype(F32)
    qdec_f = jnp.exp(lgf * (pos + 1.0))
    kdec_f = jnp.exp(lgf * (c - 1.0 - pos))
    cdec_f = jnp.exp(lgf * c)
    qdec_b = jnp.exp(lgb * (c - pos))
    kdec_b = jnp.exp(lgb * pos)
    cdec_b = jnp.exp(lgb * c)

    def rope(x, cosr, sinr):
        xa = x[:, :ROPE_HALF]
        xb = x[:, ROPE_HALF:]
        ya = xa * cosr + pltpu.roll(xa, ROPE_HALF // 2, 1) * sinr
        yb = xb * cosc + pltpu.roll(xb, ROPE_HALF // 2, 1) * sinc
        return jnp.concatenate([ya, yb], axis=1)

    def fwd_chunk(ci, carry):
        r0 = pl.multiple_of(ci * c, c)
        cosr = cosr_ref[pl.ds(r0, c), :]
        sinr = sinr_ref[pl.ds(r0, c), :]
        qr_s[pl.ds(r0, c), :] = rope(q_ref[pl.ds(r0, c), :].astype(F32), cosr, sinr).astype(BF16)
        k = rope(k_ref[pl.ds(r0, c), :].astype(F32), cosr, sinr) * k_scale
        kr_s[pl.ds(r0, c), :] = k.astype(BF16)
        kv = _dot_tn((k * kdec_f).astype(BF16), v_ref[pl.ds(r0, c), :])
        state = sf_s[...]
        sfh_s[ci] = state.astype(BF16)
        sf_s[...] = state * cdec_f + kv
        return carry

    lax.fori_loop(0, nc, fwd_chunk, 0, unroll=RET_UNROLL)

    def bwd_chunk(i, carry):
        ci = nc - 1 - i
        r0 = pl.multiple_of(ci * c, c)
        k = kr_s[pl.ds(r0, c), :].astype(F32)
        kv = _dot_tn((k * kdec_b).astype(BF16), v_ref[pl.ds(r0, c), :])
        state = sb_s[...]
        sbh_s[ci] = state.astype(BF16)
        sb_s[...] = state * cdec_b + kv
        return carry

    lax.fori_loop(0, nc, bwd_chunk, 0, unroll=RET_UNROLL)

    diff = (lax.broadcasted_iota(jnp.int32, (c, c), 0)
            - lax.broadcasted_iota(jnp.int32, (c, c), 1)).astype(F32)
    intra = (jnp.where(diff >= 0, jnp.exp(lgf * jnp.maximum(diff, 0.0)), 0.0)
             + jnp.where(diff <= 0, jnp.exp(lgb * jnp.maximum(-diff, 0.0)), 0.0))
    gnw = gnw_ref[...]

    def out_chunk(ci, carry):
        r0 = pl.multiple_of(ci * c, c)
        qb = qr_s[pl.ds(r0, c), :]
        kb = kr_s[pl.ds(r0, c), :]
        q = qb.astype(F32)
        scores = _dot_nt(qb, kb) * intra
        o = (_dot(scores.astype(BF16), v_ref[pl.ds(r0, c), :])
             + _dot((q * qdec_f).astype(BF16), sfh_s[ci])
             + _dot((q * qdec_b).astype(BF16), sbh_s[ci]))
        mu = jnp.mean(o, axis=-1, keepdims=True)
        d = o - mu
        var = jnp.mean(d * d, axis=-1, keepdims=True)
        on = d * lax.rsqrt(var + NORM_EPS) * gnw
        gate = _silu(g_ref[pl.ds(r0, c), :].astype(F32))
        o_ref[pl.ds(r0, c), :] = (on * gate).astype(BF16)
        return carry

    lax.fori_loop(0, nc, out_chunk, 0, unroll=RET_UNROLL)


def _rope_tables(n):
    inv = ROPE_BASE ** (-jnp.arange(0, ROPE_HALF, 2, dtype=F32) / ROPE_HALF)

    def tables(pos):
        ang = pos.astype(F32)[:, None] * inv[None, :]
        cos = jnp.cos(ang)
        sin = jnp.sin(ang)
        return jnp.concatenate([cos, cos], axis=1), jnp.concatenate([-sin, sin], axis=1)

    t = jnp.arange(n)
    cosr, sinr = tables(t // GRID_W)
    cosc, sinc = tables(jnp.arange(RET_CHUNK) % GRID_W)
    return cosr, sinr, cosc, sinc


def _retention(proj, cproj, lg, gn_w, batch, n, ctx_len):
    hd = RET_HEAD_DIM
    cosr, sinr, cosc, sinc = _rope_tables(n)
    col = lambda which: (lambda b, h, lg_ref: (b, which * RET_HEADS + h))
    const = lambda b, h, lg_ref: (0, 0)
    grid_spec = pltpu.PrefetchScalarGridSpec(
        num_scalar_prefetch=1,
        grid=(batch, RET_HEADS),
        in_specs=[
            pl.BlockSpec((n, hd), col(0)),
            pl.BlockSpec((n, hd), col(1)),
            pl.BlockSpec((n, hd), col(2)),
            pl.BlockSpec((n, hd), col(3)),
            pl.BlockSpec((ctx_len, hd), lambda b, h, lg_ref: (b, CTX_RET_K * RET_HEADS + h)),
            pl.BlockSpec((ctx_len, hd), lambda b, h, lg_ref: (b, CTX_RET_V * RET_HEADS + h)),
            pl.BlockSpec((n, ROPE_HALF), const),
            pl.BlockSpec((n, ROPE_HALF), const),
            pl.BlockSpec((RET_CHUNK, ROPE_HALF), const),
            pl.BlockSpec((RET_CHUNK, ROPE_HALF), const),
            pl.BlockSpec((1, hd), lambda b, h, lg_ref: (0, h)),
        ],
        out_specs=pl.BlockSpec((n, hd), lambda b, h, lg_ref: (b, h)),
        scratch_shapes=[
            pltpu.VMEM((n, hd), BF16),
            pltpu.VMEM((n, hd), BF16),
            pltpu.VMEM((n // RET_CHUNK, hd, hd), BF16),
            pltpu.VMEM((n // RET_CHUNK, hd, hd), BF16),
            pltpu.VMEM((hd, hd), F32),
            pltpu.VMEM((hd, hd), F32),
        ],
    )
    return pl.pallas_call(
        _ret_kernel,
        grid_spec=grid_spec,
        out_shape=jax.ShapeDtypeStruct((batch * n, RET_WIDTH), BF16),
        compiler_params=_params("parallel", "arbitrary"),
        name="retention",
    )(lg, proj, proj, proj, proj, cproj, cproj, cosr, sinr, cosc, sinc, gn_w.reshape(1, RET_WIDTH))


def _na_build_bias(rpb_ref, head, pair_s):
    n_roff, n_coff = 2 * NA_ROWS - 1, 2 * NA_COLS - 1
    c = lax.broadcasted_iota(jnp.int32, (GRID_W, LANES), 0)
    lane = lax.broadcasted_iota(jnp.int32, (GRID_W, LANES), 1)
    kc = jnp.bitwise_and(lane, GRID_W - 1)
    cs = jnp.clip(c - NA_COLS // 2, 0, GRID_W - NA_COLS)
    d = jnp.where((kc >= cs) & (kc < cs + NA_COLS), kc - c + (NA_COLS - 1), -1)

    def table(i):
        acc = jnp.full((GRID_W, LANES), NA_MASK, F32)
        base = (head * n_roff + i) * n_coff
        for j in range(n_coff):
            acc = jnp.where(d == j, rpb_ref[base + j], acc)
        return acc

    prev = table(0)
    for i in range(1, n_roff):
        cur = table(i)
        pair_s[i - 1] = jnp.where(lane < GRID_W, prev, cur)
        prev = cur


def _na_kernel(rpb_ref, q_ref, k_ref, v_ref, kc_ref, vc_ref, o_ref, s_s, m_s, pair_s):
    @pl.when(pl.program_id(1) == 0)
    def _():
        _na_build_bias(rpb_ref, pl.program_id(0), pair_s)

    n = q_ref.shape[0]
    rows = n // GRID_W
    kb = NA_ROWS * GRID_W
    ctx_len = kc_ref.shape[0]
    scale = NA_HEAD_DIM ** -0.5
    kc = kc_ref[...]
    vc = jnp.concatenate([vc_ref[...], jnp.ones((ctx_len, NA_HEAD_DIM), BF16)], axis=1)
    ones_w = jnp.ones((kb, NA_HEAD_DIM), BF16)

    def window(r):
        rs = jnp.clip(r - NA_ROWS // 2, 0, rows - NA_ROWS)
        return rs, pl.multiple_of(r * GRID_W, GRID_W), pl.multiple_of(rs * GRID_W, GRID_W)

    def score_row(r):
        rs, q0, k0 = window(r)
        q = (q_ref[pl.ds(q0, GRID_W), :].astype(F32) * scale).astype(BF16)
        off = rs - r + NA_ROWS - 1
        bias = jnp.concatenate([pair_s[off + 2 * t] for t in range(NA_ROWS // 2)], axis=1)
        s_win = _dot_nt(q, k_ref[pl.ds(k0, kb), :]) + bias
        s_ctx = _dot_nt(q, kc)
        m = jnp.maximum(jnp.max(s_win, axis=-1, keepdims=True), jnp.max(s_ctx, axis=-1, keepdims=True))
        s_s[pl.ds(q0, GRID_W), :kb] = s_win
        s_s[pl.ds(q0, GRID_W), kb:] = s_ctx
        m_s[pl.ds(q0, GRID_W), :] = jnp.broadcast_to(m, (GRID_W, LANES))

    def value_row(r):
        _, q0, k0 = window(r)
        m = m_s[pl.ds(q0, GRID_W), :]
        p_win = jnp.exp(s_s[pl.ds(q0, GRID_W), :kb] - jnp.tile(m, (1, kb // LANES))).astype(BF16)
        p_ctx = jnp.exp(s_s[pl.ds(q0, GRID_W), kb:] - jnp.tile(m, (1, ctx_len // LANES))).astype(BF16)
        vw = jnp.concatenate([v_ref[pl.ds(k0, kb), :], ones_w], axis=1)
        o = _dot(p_win, vw) + _dot(p_ctx, vc)
        o_ref[pl.ds(q0, GRID_W), :] = (o[:, :NA_HEAD_DIM] / o[:, NA_HEAD_DIM:]).astype(BF16)

    qrows = math.gcd(rows, NA_QROWS)

    def sweep(row_fn):
        def group(g, carry):
            for u in range(qrows):
                row_fn(g * qrows + u)
            return carry
        lax.fori_loop(0, rows // qrows, group, 0)

    sweep(score_row)
    sweep(value_row)


def _neighbourhood_attention(proj, cproj, rpb, batch, n, ctx_len):
    hd = NA_HEAD_DIM
    rows = n // GRID_W
    assert rows >= NA_ROWS
    assert rpb.shape == (NA_HEADS, 2 * NA_ROWS - 1, 2 * NA_COLS - 1)
    base = 4 * RET_WIDTH // hd
    col = lambda which: (lambda h, b, rpb_ref: (b, base + which * NA_HEADS + h))
    grid_spec = pltpu.PrefetchScalarGridSpec(
        num_scalar_prefetch=1,
        grid=(NA_HEADS, batch),
        in_specs=[
            pl.BlockSpec((n, hd), col(0)),
            pl.BlockSpec((n, hd), col(1)),
            pl.BlockSpec((n, hd), col(2)),
            pl.BlockSpec((ctx_len, hd), lambda h, b, rpb_ref: (b, CTX_NA_K * NA_HEADS + h)),
            pl.BlockSpec((ctx_len, hd), lambda h, b, rpb_ref: (b, CTX_NA_V * NA_HEADS + h)),
        ],
        out_specs=pl.BlockSpec((n, hd), lambda h, b, rpb_ref: (b, h)),
        scratch_shapes=[
            pltpu.VMEM((n, NA_ROWS * GRID_W + ctx_len), F32),
            pltpu.VMEM((n, LANES), F32),
            pltpu.VMEM((2 * NA_ROWS - 2, GRID_W, LANES), F32),
        ],
    )
    return pl.pallas_call(
        _na_kernel,
        grid_spec=grid_spec,
        out_shape=jax.ShapeDtypeStruct((batch * n, NA_WIDTH), BF16),
        compiler_params=_params("parallel", "arbitrary"),
        name="neighbourhood_attention",
    )(rpb.astype(F32).reshape(-1), proj, proj, proj, cproj, cproj)


def _outproj_kernel(n_groups, per_group, ret_ref, na_ref, w1_ref, w2_ref, x_ref, ga_ref, shf_ref,
                    scf_ref, nw_ref, wr_ref, br_ref, x1_ref, route_ref, cnt_ref, hf_ref, carry_s,
                    earlier_s, hbuf, hsem):
    tm = x_ref.shape[0]

    @pl.when(pl.program_id(0) == 0)
    def _():
        carry_s[...] = jnp.zeros_like(carry_s)
        earlier_s[...] = jnp.where(lax.broadcasted_iota(jnp.int32, (tm, tm), 0)
                                   > lax.broadcasted_iota(jnp.int32, (tm, tm), 1), 1.0, 0.0).astype(BF16)

    acc = _dot(ret_ref[...], w1_ref[...]) + _dot(na_ref[...], w2_ref[...])
    x1 = x_ref[...] + ga_ref[0] * acc
    x1_ref[...] = x1
    hf = _rms(x1, nw_ref[...]) * (1.0 + scf_ref[0]) + shf_ref[0]
    _write_rows_pipelined(hbuf, hf_ref, hsem, hf)

    hf_hi = hf.astype(BF16)
    hf_lo = (hf - hf_hi.astype(F32)).astype(BF16)
    p_hi = _dot(hf_hi, wr_ref[...])
    p_lo = _dot(hf_lo, wr_ref[...])
    logits = p_hi[:, :LANES] + p_hi[:, LANES:] + p_lo[:, :LANES] + br_ref[...]
    lane = lax.broadcasted_iota(jnp.int32, (tm, LANES), 1)
    neg = -jnp.inf

    def first_max(vals):
        top = jnp.max(vals, axis=-1, keepdims=True)
        idx = jnp.min(jnp.where(vals == top, lane, LANES), axis=-1, keepdims=True)
        return top, idx

    g_logits = jnp.where(lane < n_groups, logits, neg)
    g_top, g_sel = first_max(g_logits)
    g_w = 1.0 / jnp.sum(jnp.exp(g_logits - g_top), axis=-1, keepdims=True)
    lo = n_groups + g_sel * per_group
    e_logits = jnp.where((lane >= lo) & (lane < lo + per_group), logits, neg)
    v0, i0 = first_max(e_logits)
    v1, i1 = first_max(jnp.where(lane == i0, neg, e_logits))
    e1 = jnp.exp(v1 - v0)
    w0 = g_w / (1.0 + e1)
    w1 = g_w * e1 / (1.0 + e1)

    hit0 = lane == i0
    hit1 = lane == i1
    onehot = jnp.where(hit0 | hit1, 1.0, 0.0)
    before = _dot(earlier_s[...], onehot.astype(BF16)) + carry_s[...]
    rank0 = jnp.sum(jnp.where(hit0, before, 0.0), axis=-1, keepdims=True)
    rank1 = jnp.sum(jnp.where(hit1, before, 0.0), axis=-1, keepdims=True)
    carry_s[...] = carry_s[...] + jnp.sum(onehot, axis=0, keepdims=True)
    cnt_ref[...] = carry_s[...]

    key0 = (i0 - n_groups).astype(F32) * KEY_RANK_SPAN + rank0
    key1 = (i1 - n_groups).astype(F32) * KEY_RANK_SPAN + rank1
    fields = {R_KEY0: key0, R_KEY1: key1, R_W0: w0, R_W1: w1}
    route = jnp.zeros((tm, LANES), F32)
    for idx, val in fields.items():
        route = jnp.where(lane == idx, val, route)
    route_ref[...] = route


def _out_projection(ret, na, w_out, x2d, mod3, norm_w, w_route, b_route, n, n_groups, per_group, tm):
    m, d = x2d.shape
    chunks = d // LANES
    batch_of = lambda i: (i * tm) // n
    mod_spec = lambda which: pl.BlockSpec((1, 1, d), lambda i: (batch_of(i) * N_MOD + which, 0, 0))
    const2 = lambda i: (0, 0)
    return pl.pallas_call(
        functools.partial(_outproj_kernel, n_groups, per_group),
        grid=(m // tm,),
        in_specs=[
            pl.BlockSpec((tm, RET_WIDTH), lambda i: (i, 0)),
            pl.BlockSpec((tm, NA_WIDTH), lambda i: (i, 0)),
            pl.BlockSpec((RET_WIDTH, d), lambda i: (0, 0)),
            pl.BlockSpec((NA_WIDTH, d), lambda i: (RET_WIDTH // NA_WIDTH, 0)),
            pl.BlockSpec((tm, d), lambda i: (i, 0)),
            mod_spec(2), mod_spec(3), mod_spec(4),
            pl.BlockSpec((1, d), const2),
            pl.BlockSpec((d, 2 * LANES), const2),
            pl.BlockSpec((1, LANES), const2),
        ],
        out_specs=[
            pl.BlockSpec((tm, d), lambda i: (i, 0)),
            pl.BlockSpec((tm, LANES), lambda i: (i, 0)),
            pl.BlockSpec((1, LANES), const2),
            pl.BlockSpec(memory_space=pl.ANY),
        ],
        out_shape=[
            jax.ShapeDtypeStruct((m, d), F32),
            jax.ShapeDtypeStruct((m, LANES), F32),
            jax.ShapeDtypeStruct((1, LANES), F32),
            jax.ShapeDtypeStruct((m // SUBLANES, SUBLANES, chunks, LANES), F32),
        ],
        scratch_shapes=[
            pltpu.VMEM((1, LANES), F32),
            pltpu.VMEM((tm, tm), BF16),
            pltpu.VMEM((2, tm // SUBLANES, chunks, SUBLANES, LANES), F32),
            pltpu.SemaphoreType.DMA((2,)),
        ],
        compiler_params=_params("arbitrary"),
        name="out_projection_router",
    )(ret, na, w_out, w_out, x2d, mod3, mod3, mod3, norm_w.reshape(1, d), w_route, b_route)


def _slot_map_kernel(tile_rows, pos_ref, cnt_ref, slot_ref, tile_expert_ref, next_expert_ref,
                     run_parity_ref, n_used_ref, offs_s):
    n_experts = cnt_ref.shape[0]
    n_tiles = tile_expert_ref.shape[0]
    shift = tile_rows.bit_length() - 1
    assert 1 << shift == tile_rows

    def fill(ref, lo, hi, val):
        value_at = val if callable(val) else (lambda s: val)
        groups = lax.div(hi - lo, SUBLANES)

        def group(g, carry):
            for k in range(SUBLANES):
                s = lo + g * SUBLANES + k
                ref[s] = value_at(s)
            return carry

        def single(s, carry):
            ref[s] = value_at(s)
            return carry

        lax.fori_loop(0, groups, group, 0)
        lax.fori_loop(lo + groups * SUBLANES, hi, single, 0)

    assert tile_rows <= pos_ref.shape[0] // 2
    padding_row = lambda s: jnp.bitwise_and(s, tile_rows - 1)

    def tiles_of(e):
        size = lax.shift_left(lax.shift_right_logical(cnt_ref[e] + (tile_rows - 1), shift), shift)
        return lax.shift_right_logical(offs_s[e], shift), lax.shift_right_logical(offs_s[e] + size, shift)

    def per_expert(e, carry):
        start, runs = carry
        cnt = cnt_ref[e]
        size = lax.shift_left(lax.shift_right_logical(cnt + (tile_rows - 1), shift), shift)
        offs_s[e] = start
        t0, t1 = tiles_of(e)
        fill(tile_expert_ref, t0, t1, e)
        fill(run_parity_ref, t0, t1, jnp.bitwise_and(runs, 1))
        fill(slot_ref, start + cnt, start + size, padding_row)
        return start + size, runs + (size > 0).astype(jnp.int32)

    end, _ = lax.fori_loop(0, n_experts, per_expert, (0, 0))
    n_used = lax.shift_right_logical(end, shift)
    n_used_ref[0] = n_used
    fill(tile_expert_ref, n_used, n_tiles, n_experts - 1)
    fill(run_parity_ref, n_used, n_tiles, 0)
    fill(next_expert_ref, n_used, n_tiles, -1)
    fill(slot_ref, end, slot_ref.shape[0], padding_row)

    def per_expert_reversed(k, following):
        e = n_experts - 1 - k
        t0, t1 = tiles_of(e)
        fill(next_expert_ref, t0, t1, following)
        return jnp.where(t1 > t0, e, following)

    lax.fori_loop(0, n_experts, per_expert_reversed, -1)

    def assign(a, carry):
        slot_ref[pos_ref[a]] = lax.shift_right_logical(a, 1)
        return carry

    lax.fori_loop(0, pos_ref.shape[0], assign, 0, unroll=2 * SUBLANES)


def _slot_map(pos, cnt, n_tiles, tile_rows):
    smem = pl.BlockSpec(memory_space=pltpu.SMEM)
    i32 = lambda *shape: jax.ShapeDtypeStruct(shape, jnp.int32)
    return pl.pallas_call(
        functools.partial(_slot_map_kernel, tile_rows),
        in_specs=[smem, smem],
        out_specs=[smem] * 5,
        out_shape=[i32(n_tiles * tile_rows), i32(n_tiles), i32(n_tiles), i32(n_tiles), i32(1)],
        scratch_shapes=[pltpu.SMEM((cnt.shape[0],), jnp.int32)],
        name="slot_map",
    )(pos, cnt)


def _expert_kernel(tile_expert_ref, next_expert_ref, run_parity_ref, n_used_ref, tokens_ref,
                   next_tokens_ref, hf_ref, wg_hbm, wu_hbm, wd_hbm, o_ref, xbuf, gsem, wg_f, wu_f,
                   wd_f, wsem, wg_s, wu_s, wd_s):
    i = pl.program_id(0)
    last = pl.num_programs(0) - 1
    n_used = n_used_ref[0]
    tm, chunks, _ = o_ref.shape
    ff = wg_s.shape[1]

    def row_copy(ids_ref, slot, j):
        dst = xbuf.at[slot, pl.ds(j * _row_pitch(chunks), chunks), :]
        return pltpu.make_async_copy(hf_ref.at[ids_ref[0, 0, j]], dst, gsem.at[slot])

    def weight_copies(expert, slot):
        return [pltpu.make_async_copy(hbm.at[expert], stage.at[slot], wsem.at[slot])
                for hbm, stage in ((wg_hbm, wg_f), (wu_hbm, wu_f), (wd_hbm, wd_f))]

    @pl.when((i == 0) & (n_used > 0))
    def _():
        _start_row_gather(lambda j: hf_ref.at[tokens_ref[0, 0, j]], chunks, xbuf.at[0], gsem.at[0], tm)
        for cp in weight_copies(tile_expert_ref[0], run_parity_ref[0]):
            cp.start(priority=WEIGHT_DMA_PRIORITY)

    @pl.when(i < n_used)
    def _():
        @pl.when((i == 0) | (tile_expert_ref[i] != tile_expert_ref[jnp.maximum(i - 1, 0)]))
        def _():
            stage = run_parity_ref[i]
            for cp in weight_copies(tile_expert_ref[i], stage):
                cp.wait()

            @pl.when(next_expert_ref[i] >= 0)
            def _():
                for cp in weight_copies(next_expert_ref[i], 1 - stage):
                    cp.start(priority=WEIGHT_DMA_PRIORITY)

            wg_s[...] = wg_f[stage].astype(BF16)
            wu_s[...] = wu_f[stage].astype(BF16)
            wd_s[...] = wd_f[stage].astype(BF16)

        slot = i % 2
        _wait_row_gather(xbuf.at[slot], gsem.at[slot], tm, chunks)
        x = _load_gathered(xbuf.at[slot], tm, chunks).astype(BF16)

        down_split = min(EXPERT_DOWN_SPLIT, chunks)
        n_stages = 2 + down_split
        bounds = [tm * s // n_stages for s in range(n_stages + 1)]

        def start_next_rows(stage):
            for j in range(bounds[stage], bounds[stage + 1]):
                row_copy(next_tokens_ref, 1 - slot, j).start()

        halves = []
        for half in range(2):
            start_next_rows(half)
            cols = slice(half * ff // 2, (half + 1) * ff // 2)
            halves.append((_silu(_dot(x, wg_s[:, cols])) * _dot(x, wu_s[:, cols])).astype(BF16))
        a = jnp.concatenate(halves, axis=1)
        per_part = chunks // down_split
        for part in range(down_split):
            start_next_rows(2 + part)
            y = _dot(a, wd_s[:, part * per_part * LANES:(part + 1) * per_part * LANES])
            for c in range(per_part):
                o_ref[:, part * per_part + c, :] = y[:, c * LANES:(c + 1) * LANES]

        @pl.when(i == last)
        def _():
            _wait_row_gather(xbuf.at[1 - slot], gsem.at[1 - slot], tm, chunks)

    @pl.when(i >= n_used)
    def _():
        @pl.when((i == n_used) & (i > 0))
        def _():
            _wait_row_gather(xbuf.at[i % 2], gsem.at[i % 2], tm, chunks)

        o_ref[...] = jnp.zeros(o_ref.shape, o_ref.dtype)


def _experts(tile_meta, slot_token, hf, w_gate, w_up, w_down, n_tiles, tm):
    chunks = hf.shape[1]
    d = chunks * LANES
    ff = w_gate.shape[-1]
    slot_tiles = slot_token.reshape(n_tiles, 1, tm)
    any_space = pl.BlockSpec(memory_space=pl.ANY)
    ids_block = lambda ahead: pl.BlockSpec(
        (1, 1, tm), lambda i, *prefetch: (jnp.minimum(i + ahead, n_tiles - 1), 0, 0),
        memory_space=pltpu.SMEM)
    grid_spec = pltpu.PrefetchScalarGridSpec(
        num_scalar_prefetch=4,
        grid=(n_tiles,),
        in_specs=[ids_block(0), ids_block(1), any_space, any_space, any_space, any_space],
        out_specs=pl.BlockSpec((tm, chunks, LANES), lambda i, *prefetch: (i, 0, 0)),
        scratch_shapes=[
            pltpu.VMEM((2, tm * _row_pitch(chunks), LANES), hf.dtype),
            pltpu.SemaphoreType.DMA((2,)),
            pltpu.VMEM((2, d, ff), F32),
            pltpu.VMEM((2, d, ff), F32),
            pltpu.VMEM((2, ff, d), F32),
            pltpu.SemaphoreType.DMA((2,)),
            pltpu.VMEM((d, ff), BF16),
            pltpu.VMEM((d, ff), BF16),
            pltpu.VMEM((ff, d), BF16),
        ],
    )
    return pl.pallas_call(
        _expert_kernel,
        grid_spec=grid_spec,
        out_shape=jax.ShapeDtypeStruct((n_tiles * tm, chunks, LANES), hf.dtype),
        compiler_params=_params("arbitrary"),
        name="routed_experts",
    )(*tile_meta, slot_tiles, slot_tiles, hf, w_gate, w_up, w_down)


def _combine_kernel(pos_ref, ys_ref, x1_ref, route_ref, gf_ref, fw_ref, o_ref, ybuf, sem):
    i = pl.program_id(0)
    n_tiles = pl.num_programs(0)
    tm = x1_ref.shape[0]
    chunks = ys_ref.shape[1]

    def start_tile(tile, slot):
        for choice in range(2):
            _start_row_gather(lambda j: ys_ref.at[pos_ref[(tile * tm + j) * 2 + choice]], chunks,
                              ybuf.at[slot, choice], sem.at[slot, choice], tm)

    @pl.when(i == 0)
    def _():
        start_tile(0, 0)

    @pl.when(i + 1 < n_tiles)
    def _():
        start_tile(i + 1, (i + 1) % 2)

    slot = i % 2
    for choice in range(2):
        _wait_row_gather(ybuf.at[slot, choice], sem.at[slot, choice], tm, chunks)
    route = route_ref[...]
    moe = (route[:, R_W0:R_W0 + 1] * _load_gathered(ybuf.at[slot, 0], tm, chunks)
           + route[:, R_W1:R_W1 + 1] * _load_gathered(ybuf.at[slot, 1], tm, chunks))
    x2 = x1_ref[...] + gf_ref[0] * moe
    o_ref[...] = _rms(x2, fw_ref[...])


def _combine(pos, ys, x1, route, mod3, final_w, n, tm):
    m, d = x1.shape
    chunks = ys.shape[1]
    grid_spec = pltpu.PrefetchScalarGridSpec(
        num_scalar_prefetch=1,
        grid=(m // tm,),
        in_specs=[
            pl.BlockSpec(memory_space=pl.ANY),
            pl.BlockSpec((tm, d), lambda i, pos_ref: (i, 0)),
            pl.BlockSpec((tm, LANES), lambda i, pos_ref: (i, 0)),
            pl.BlockSpec((1, 1, d), lambda i, pos_ref: (((i * tm) // n) * N_MOD + 5, 0, 0)),
            pl.BlockSpec((1, d), lambda i, pos_ref: (0, 0)),
        ],
        out_specs=pl.BlockSpec((tm, d), lambda i, pos_ref: (i, 0)),
        scratch_shapes=[pltpu.VMEM((2, 2, tm * _row_pitch(chunks), LANES), ys.dtype),
                        pltpu.SemaphoreType.DMA((2, 2))],
    )
    return pl.pallas_call(
        _combine_kernel,
        grid_spec=grid_spec,
        out_shape=jax.ShapeDtypeStruct((m, d), F32),
        compiler_params=_params("arbitrary"),
        name="combine_final_norm",
    )(pos, ys, x1, route, mod3, final_w.reshape(1, d))


def kernel(x, c, ctx, c_ctx, w_mod, b_mod, norm_mix_w, w_in, ret_decay_f, ret_decay_b, ret_gn_w, na_rpb, w_out, norm_ffn_w, w_router_group, b_router_group, w_router_expert, b_router_expert, w_gate, w_up, w_down, final_norm_w):
    assert w_mod.shape[0] == 1, "single trunk layer"
    batch, n, d = x.shape
    ctx_len = ctx.shape[1]
    n_groups = w_router_group.shape[-1]
    per_group = w_router_expert.shape[-1]
    n_experts = w_gate.shape[1]
    assert n_groups * per_group == n_experts and n_groups + n_experts <= LANES

    mod_rows = 8
    cc = jnp.zeros((mod_rows, d), F32).at[:batch].set(c).at[batch].set(c_ctx)
    mod = _modulation(cc, w_mod[0], b_mod[0])
    mod3 = mod.reshape(mod_rows * N_MOD, 1, d)

    w_in_b = w_in[0]
    tm = min(512, n)
    tm_in = min(1024, n)
    x2d = x.reshape(batch * n, d)
    proj = _in_projection(x2d, mod3, lambda i: (i * tm_in) // n, norm_mix_w[0], w_in_b, tm_in)
    cproj = _in_projection(ctx.reshape(batch * ctx_len, d), mod3, lambda i: batch, norm_mix_w[0],
                           w_in_b, batch * ctx_len, CTX_COL_BLOCKS)

    lg = jnp.stack([jax.nn.log_sigmoid(ret_decay_f[0].astype(F32)),
                    jax.nn.log_sigmoid(ret_decay_b[0].astype(F32))])
    ret = _retention(proj, cproj, lg, ret_gn_w[0], batch, n, ctx_len)
    na = _neighbourhood_attention(proj, cproj, na_rpb[0], batch, n, ctx_len)

    w_route = jnp.concatenate(
        [w_router_group[0], jnp.moveaxis(w_router_expert[0], 0, 1).reshape(d, n_experts)], axis=1)
    w_route = jnp.pad(w_route.astype(F32), ((0, 0), (0, LANES - n_groups - n_experts)))
    w_route_hi = w_route.astype(BF16)
    w_route = jnp.concatenate([w_route_hi, (w_route - w_route_hi.astype(F32)).astype(BF16)], axis=1)
    b_route = jnp.concatenate([b_router_group[0], b_router_expert[0].reshape(-1)])
    b_route = jnp.pad(b_route.astype(F32), (0, LANES - n_groups - n_experts)).reshape(1, LANES)
    x1, route, counts, hf = _out_projection(ret, na, w_out[0].astype(BF16), x2d, mod3, norm_ffn_w[0],
                                            w_route, b_route, n, n_groups, per_group, tm)

    te = EXPERT_TILE
    tokens = batch * n
    assert tokens < 1 << KEY_RANK_BITS and (2 * tokens) % te == 0
    n_tiles = (2 * tokens) // te + n_experts
    cnt = counts[0, n_groups:n_groups + n_experts].astype(jnp.int32)
    padded = (cnt + te - 1) // te * te
    starts = jnp.cumsum(padded) - padded
    keys = route[:, R_KEY0:R_KEY1 + 1].astype(jnp.int32).reshape(-1)
    expert = lax.shift_right_logical(keys, KEY_RANK_BITS)
    start_of = jnp.sum(jnp.where(expert[:, None] == jnp.arange(n_experts)[None, :], starts[None, :], 0), axis=1)
    pos = start_of + jnp.bitwise_and(keys, (1 << KEY_RANK_BITS) - 1)
    slot_token, *tile_meta = _slot_map(pos, cnt, n_tiles, te)

    hf_rows = hf.reshape(tokens, d // LANES, LANES)
    ys = _experts(tile_meta, slot_token, hf_rows, w_gate[0], w_up[0], w_down[0], n_tiles, te)
    out = _combine(pos, ys, x1, route, mod3, final_norm_w, n, min(256, n))
    return out.reshape(batch, n, d)
```

```python
import functools
import math

import jax
import jax.numpy as jnp
from jax import lax
from jax.experimental import pallas as pl
from jax.experimental.pallas import tpu as pltpu

F32 = jnp.float32
BF16 = jnp.bfloat16

GRID_W = 64
RET_HEADS = 4
RET_HEAD_DIM = 256
RET_WIDTH = RET_HEADS * RET_HEAD_DIM
NA_HEADS = 8
NA_HEAD_DIM = 128
NA_WIDTH = NA_HEADS * NA_HEAD_DIM
RET_CHUNK = 128
NA_ROWS = 8
NA_COLS = 16
ROPE_BASE = 10000.0
N_MOD = 6
NORM_EPS = 1e-6
ROPE_HALF = RET_HEAD_DIM // 2
RET_UNROLL = 8

INPROJ_TN = 1024
assert INPROJ_TN == RET_WIDTH == NA_WIDTH
CTX_COL_BLOCKS = (1, 2, 5, 6)
CTX_RET_K, CTX_RET_V, CTX_NA_K, CTX_NA_V = 0, 1, 2, 3

NA_QROWS = 16
NA_MASK = -1e30

LANES = 128
SUBLANES = 8
BF16_ROWS = 2 * SUBLANES
VMEM_LIMIT_BYTES = 56 * 1024 * 1024

R_KEY0, R_KEY1, R_W0, R_W1 = 0, 1, 2, 3
KEY_RANK_BITS = 16
KEY_RANK_SPAN = float(1 << KEY_RANK_BITS)

EXPERT_TILE = 256
WEIGHT_DMA_PRIORITY = 1
EXPERT_DOWN_SPLIT = 4


def _params(*sem):
    return pltpu.CompilerParams(dimension_semantics=sem, vmem_limit_bytes=VMEM_LIMIT_BYTES)


def _dot(a, b):
    return jnp.dot(a, b, preferred_element_type=F32)


def _dot_nt(a, b):
    return lax.dot_general(a, b, (((1,), (1,)), ((), ())), preferred_element_type=F32)


def _dot_tn(a, b):
    return lax.dot_general(a, b, (((0,), (0,)), ((), ())), preferred_element_type=F32)


def _rms(x, w):
    return x * lax.rsqrt(jnp.mean(x * x, axis=-1, keepdims=True) + NORM_EPS) * w


def _silu(x):
    return x * jax.nn.sigmoid(x)


def _row_pitch(chunks):
    return chunks + 1


def _store_rows(ref, val):
    n_rows, d = val.shape
    chunks = d // LANES
    pitch = _row_pitch(chunks)
    for c in range(chunks):
        ref[pl.ds(c, n_rows, stride=pitch), :] = val[:, c * LANES:(c + 1) * LANES]
    ref[pl.ds(chunks, n_rows, stride=pitch), :] = jnp.zeros((n_rows, LANES), ref.dtype)


def _row_ref(hbm, row, chunks):
    return hbm.at[pl.ds(row * _row_pitch(chunks), chunks), :]


def _start_row_gather(src_row, chunks, dst_buf, sem, n_rows):
    def body(j, carry):
        pltpu.make_async_copy(src_row(j), _row_ref(dst_buf, j, chunks), sem).start()
        return carry

    lax.fori_loop(0, n_rows, body, 0, unroll=SUBLANES)


def _wait_row_gather(dst_buf, sem, n_rows, chunks):
    view = dst_buf.at[pl.ds(0, n_rows * chunks), :]
    pltpu.make_async_copy(view, view, sem).wait()


def _load_gathered(buf, n_rows, chunks):
    return jnp.concatenate([buf[pl.ds(c, n_rows, stride=_row_pitch(chunks)), :] for c in range(chunks)],
                           axis=1)


def _mod_kernel(c_ref, w_ref, b_ref, o_ref):
    a = _silu(c_ref[...]).astype(BF16)
    o_ref[...] = _dot(a, w_ref[...].astype(BF16)) + b_ref[...]


def _modulation(cc, w_mod, b_mod):
    rows, d = cc.shape
    width = w_mod.shape[1]
    tn = next(t for t in (1024, 512, 256, LANES) if width % t == 0)
    return pl.pallas_call(
        _mod_kernel,
        grid=(width // tn,),
        in_specs=[
            pl.BlockSpec((rows, d), lambda j: (0, 0)),
            pl.BlockSpec((d, tn), lambda j: (0, j)),
            pl.BlockSpec((1, tn), lambda j: (0, j)),
        ],
        out_specs=pl.BlockSpec((rows, tn), lambda j: (0, j)),
        out_shape=jax.ShapeDtypeStruct((rows, width), F32),
        compiler_params=_params("arbitrary"),
        name="modulation",
    )(cc, w_mod, b_mod.reshape(1, width))


def _inproj_kernel(rows_per_step, x_ref, sh_ref, sc_ref, nw_ref, w_ref, o_ref, h_s):
    i = pl.program_id(0)
    j = pl.program_id(1)
    tm = o_ref.shape[0]

    def normalise(rows, slot):
        y = _rms(x_ref[rows, :], nw_ref[...])
        h_s[slot, rows, :] = (y * (1.0 + sc_ref[0]) + sh_ref[0]).astype(BF16)

    @pl.when((i == 0) & (j == 0))
    def _():
        normalise(pl.ds(0, tm), 0)

    o_ref[...] = _dot(h_s[i % 2], w_ref[...].astype(BF16)).astype(BF16)
    start = jnp.clip((j - 1) * rows_per_step, 0, tm - rows_per_step)
    normalise(pl.ds(pl.multiple_of(start, BF16_ROWS), rows_per_step), (i + 1) % 2)


def _in_projection(x2d, mod3, mod_row_of_tile, norm_w, w_in, tm, col_blocks=None):
    m, d = x2d.shape
    tn = INPROJ_TN
    if col_blocks is None:
        col_blocks = tuple(range(w_in.shape[1] // tn))
    n_blocks = len(col_blocks)
    n_tiles = m // tm
    assert n_blocks >= 2 and tm % BF16_ROWS == 0
    rows_per_step = -(-tm // ((n_blocks - 1) * BF16_ROWS)) * BF16_ROWS
    w_block = lambda j: sum(jnp.where(j == k, blk, 0) for k, blk in enumerate(col_blocks))
    src = lambda i, j: jnp.where((i == 0) & (j == 0), 0, jnp.minimum(i + 1, n_tiles - 1))
    return pl.pallas_call(
        functools.partial(_inproj_kernel, rows_per_step),
        grid=(n_tiles, n_blocks),
        in_specs=[
            pl.BlockSpec((tm, d), lambda i, j: (src(i, j), 0)),
            pl.BlockSpec((1, 1, d), lambda i, j: (mod_row_of_tile(src(i, j)) * N_MOD + 0, 0, 0)),
            pl.BlockSpec((1, 1, d), lambda i, j: (mod_row_of_tile(src(i, j)) * N_MOD + 1, 0, 0)),
            pl.BlockSpec((1, d), lambda i, j: (0, 0)),
            pl.BlockSpec((d, tn), lambda i, j: (0, w_block(j))),
        ],
        out_specs=pl.BlockSpec((tm, tn), lambda i, j: (i, j)),
        out_shape=jax.ShapeDtypeStruct((m, n_blocks * tn), BF16),
        scratch_shapes=[pltpu.VMEM((2, tm, d), BF16)],
        compiler_params=_params("arbitrary", "arbitrary"),
        name="in_projection",
    )(x2d, mod3, mod3, norm_w.reshape(1, d), w_in)


def _ret_kernel(lg_ref, q_ref, k_ref, v_ref, g_ref, ck_ref, cv_ref, cosr_ref, sinr_ref,
                cosc_ref, sinc_ref, gnw_ref, o_ref, qr_s, kr_s, sfh_s, sbh_s, sf_s, sb_s):
    head = pl.program_id(1)
    lgf = lg_ref[0, head]
    lgb = lg_ref[1, head]
    n = q_ref.shape[0]
    c = RET_CHUNK
    nc = n // c
    ctx_len = ck_ref.shape[0]
    k_scale = RET_HEAD_DIM ** -0.5

    posl = lax.broadcasted_iota(jnp.int32, (ctx_len, 1), 0).astype(F32)
    ck = ck_ref[...].astype(F32) * k_scale
    cv = cv_ref[...]
    sf_s[...] = _dot_tn((ck * jnp.exp(lgf * (ctx_len - 1.0 - posl))).astype(BF16), cv)
    sb_s[...] = _dot_tn((ck * jnp.exp(lgb * posl)).astype(BF16), cv)

    cosc = cosc_ref[...]
    sinc = sinc_ref[...]
    pos = lax.broadcasted_iota(jnp.int32, (c, 1), 0).astype(F32)
    qdec_f = jnp.exp(lgf * (pos + 1.0))
    kdec_f = jnp.exp(lgf * (c - 1.0 - pos))
    cdec_f = jnp.exp(lgf * c)
    qdec_b = jnp.exp(lgb * (c - pos))
    kdec_b = jnp.exp(lgb * pos)
    cdec_b = jnp.exp(lgb * c)

    def rope(x, cosr, sinr):
        xa = x[:, :ROPE_HALF]
        xb = x[:, ROPE_HALF:]
        ya = xa * cosr + pltpu.roll(xa, ROPE_HALF // 2, 1) * sinr
        yb = xb * cosc + pltpu.roll(xb, ROPE_HALF // 2, 1) * sinc
        return jnp.concatenate([ya, yb], axis=1)

    def fwd_chunk(ci, carry):
        r0 = pl.multiple_of(ci * c, c)
        cosr = cosr_ref[pl.ds(r0, c), :]
        sinr = sinr_ref[pl.ds(r0, c), :]
        qr_s[pl.ds(r0, c), :] = rope(q_ref[pl.ds(r0, c), :].astype(F32), cosr, sinr).astype(BF16)
        k = rope(k_ref[pl.ds(r0, c), :].astype(F32), cosr, sinr) * k_scale
        kr_s[pl.ds(r0, c), :] = k.astype(BF16)
        kv = _dot_tn((k * kdec_f).astype(BF16), v_ref[pl.ds(r0, c), :])
        state = sf_s[...]
        sfh_s[ci] = state.astype(BF16)
        sf_s[...] = state * cdec_f + kv
        return carry

    lax.fori_loop(0, nc, fwd_chunk, 0, unroll=RET_UNROLL)

    def bwd_chunk(i, carry):
        ci = nc - 1 - i
        r0 = pl.multiple_of(ci * c, c)
        k = kr_s[pl.ds(r0, c), :].astype(F32)
        kv = _dot_tn((k * kdec_b).astype(BF16), v_ref[pl.ds(r0, c), :])
        state = sb_s[...]
        sbh_s[ci] = state.astype(BF16)
        sb_s[...] = state * cdec_b + kv
        return carry

    lax.fori_loop(0, nc, bwd_chunk, 0, unroll=RET_UNROLL)

    diff = (lax.broadcasted_iota(jnp.int32, (c, c), 0)
            - lax.broadcasted_iota(jnp.int32, (c, c), 1)).astype(F32)
    intra = (jnp.where(diff >= 0, jnp.exp(lgf * jnp.maximum(diff, 0.0)), 0.0)
             + jnp.where(diff <= 0, jnp.exp(lgb * jnp.maximum(-diff, 0.0)), 0.0))
    gnw = gnw_ref[...]

    def out_chunk(ci, carry):
        r0 = pl.multiple_of(ci * c, c)
        qb = qr_s[pl.ds(r0, c), :]
        kb = kr_s[pl.ds(r0, c), :]
        q = qb.astype(F32)
        scores = _dot_nt(qb, kb) * intra
        o = (_dot(scores.astype(BF16), v_ref[pl.ds(r0, c), :])
             + _dot((q * qdec_f).astype(BF16), sfh_s[ci])
             + _dot((q * qdec_b).astype(BF16), sbh_s[ci]))
        mu = jnp.mean(o, axis=-1, keepdims=True)
        d = o - mu
        var = jnp.mean(d * d, axis=-1, keepdims=True)
        on = d * lax.rsqrt(var + NORM_EPS) * gnw
        gate = _silu(g_ref[pl.ds(r0, c), :].astype(F32))
        o_ref[pl.ds(r0, c), :] = (on * gate).astype(BF16)
        return carry

    lax.fori_loop(0, nc, out_chunk, 0, unroll=RET_UNROLL)


def _rope_tables(n):
    inv = ROPE_BASE ** (-jnp.arange(0, ROPE_HALF, 2, dtype=F32) / ROPE_HALF)

    def tables(pos):
        ang = pos.astype(F32)[:, None] * inv[None, :]
        cos = jnp.cos(ang)
        sin = jnp.sin(ang)
        return jnp.concatenate([cos, cos], axis=1), jnp.concatenate([-sin, sin], axis=1)

    t = jnp.arange(n)
    cosr, sinr = tables(t // GRID_W)
    cosc, sinc = tables(jnp.arange(RET_CHUNK) % GRID_W)
    return cosr, sinr, cosc, sinc


def _retention(proj, cproj, lg, gn_w, batch, n, ctx_len):
    hd = RET_HEAD_DIM
    cosr, sinr, cosc, sinc = _rope_tables(n)
    col = lambda which: (lambda b, h, lg_ref: (b, which * RET_HEADS + h))
    const = lambda b, h, lg_ref: (0, 0)
    grid_spec = pltpu.PrefetchScalarGridSpec(
        num_scalar_prefetch=1,
        grid=(batch, RET_HEADS),
        in_specs=[
            pl.BlockSpec((n, hd), col(0)),
            pl.BlockSpec((n, hd), col(1)),
            pl.BlockSpec((n, hd), col(2)),
            pl.BlockSpec((n, hd), col(3)),
            pl.BlockSpec((ctx_len, hd), lambda b, h, lg_ref: (b, CTX_RET_K * RET_HEADS + h)),
            pl.BlockSpec((ctx_len, hd), lambda b, h, lg_ref: (b, CTX_RET_V * RET_HEADS + h)),
            pl.BlockSpec((n, ROPE_HALF), const),
            pl.BlockSpec((n, ROPE_HALF), const),
            pl.BlockSpec((RET_CHUNK, ROPE_HALF), const),
            pl.BlockSpec((RET_CHUNK, ROPE_HALF), const),
            pl.BlockSpec((1, hd), lambda b, h, lg_ref: (0, h)),
        ],
        out_specs=pl.BlockSpec((n, hd), lambda b, h, lg_ref: (b, h)),
        scratch_shapes=[
            pltpu.VMEM((n, hd), BF16),
            pltpu.VMEM((n, hd), BF16),
            pltpu.VMEM((n // RET_CHUNK, hd, hd), BF16),
            pltpu.VMEM((n // RET_CHUNK, hd, hd), BF16),
            pltpu.VMEM((hd, hd), F32),
            pltpu.VMEM((hd, hd), F32),
        ],
    )
    return pl.pallas_call(
        _ret_kernel,
        grid_spec=grid_spec,
        out_shape=jax.ShapeDtypeStruct((batch * n, RET_WIDTH), BF16),
        compiler_params=_params("parallel", "arbitrary"),
        name="retention",
    )(lg, proj, proj, proj, proj, cproj, cproj, cosr, sinr, cosc, sinc, gn_w.reshape(1, RET_WIDTH))


def _na_build_bias(rpb_ref, head, pair_s):
    n_roff, n_coff = 2 * NA_ROWS - 1, 2 * NA_COLS - 1
    c = lax.broadcasted_iota(jnp.int32, (GRID_W, LANES), 0)
    lane = lax.broadcasted_iota(jnp.int32, (GRID_W, LANES), 1)
    kc = jnp.bitwise_and(lane, GRID_W - 1)
    cs = jnp.clip(c - NA_COLS // 2, 0, GRID_W - NA_COLS)
    d = jnp.where((kc >= cs) & (kc < cs + NA_COLS), kc - c + (NA_COLS - 1), -1)

    def table(i):
        acc = jnp.full((GRID_W, LANES), NA_MASK, F32)
        base = (head * n_roff + i) * n_coff
        for j in range(n_coff):
            acc = jnp.where(d == j, rpb_ref[base + j], acc)
        return acc

    prev = table(0)
    for i in range(1, n_roff):
        cur = table(i)
        pair_s[i - 1] = jnp.where(lane < GRID_W, prev, cur)
        prev = cur


def _na_kernel(rpb_ref, q_ref, k_ref, v_ref, kc_ref, vc_ref, o_ref, s_s, m_s, pair_s):
    @pl.when(pl.program_id(1) == 0)
    def _():
        _na_build_bias(rpb_ref, pl.program_id(0), pair_s)

    n = q_ref.shape[0]
    rows = n // GRID_W
    kb = NA_ROWS * GRID_W
    ctx_len = kc_ref.shape[0]
    scale = NA_HEAD_DIM ** -0.5
    kc = kc_ref[...]
    vc = jnp.concatenate([vc_ref[...], jnp.ones((ctx_len, NA_HEAD_DIM), BF16)], axis=1)
    ones_w = jnp.ones((kb, NA_HEAD_DIM), BF16)

    def window(r):
        rs = jnp.clip(r - NA_ROWS // 2, 0, rows - NA_ROWS)
        return rs, pl.multiple_of(r * GRID_W, GRID_W), pl.multiple_of(rs * GRID_W, GRID_W)

    def score_row(r):
        rs, q0, k0 = window(r)
        q = (q_ref[pl.ds(q0, GRID_W), :].astype(F32) * scale).astype(BF16)
        off = rs - r + NA_ROWS - 1
        bias = jnp.concatenate([pair_s[off + 2 * t] for t in range(NA_ROWS // 2)], axis=1)
        s_win = _dot_nt(q, k_ref[pl.ds(k0, kb), :]) + bias
        s_ctx = _dot_nt(q, kc)
        m = jnp.maximum(jnp.max(s_win, axis=-1, keepdims=True), jnp.max(s_ctx, axis=-1, keepdims=True))
        s_s[pl.ds(q0, GRID_W), :kb] = s_win
        s_s[pl.ds(q0, GRID_W), kb:] = s_ctx
        m_s[pl.ds(q0, GRID_W), :] = jnp.broadcast_to(m, (GRID_W, LANES))

    def value_row(r):
        _, q0, k0 = window(r)
        m = m_s[pl.ds(q0, GRID_W), :]
        p_win = jnp.exp(s_s[pl.ds(q0, GRID_W), :kb] - jnp.tile(m, (1, kb // LANES))).astype(BF16)
        p_ctx = jnp.exp(s_s[pl.ds(q0, GRID_W), kb:] - jnp.tile(m, (1, ctx_len // LANES))).astype(BF16)
        vw = jnp.concatenate([v_ref[pl.ds(k0, kb), :], ones_w], axis=1)
        o = _dot(p_win, vw) + _dot(p_ctx, vc)
        o_ref[pl.ds(q0, GRID_W), :] = (o[:, :NA_HEAD_DIM] / o[:, NA_HEAD_DIM:]).astype(BF16)

    qrows = math.gcd(rows, NA_QROWS)

    def sweep(row_fn):
        def group(g, carry):
            for u in range(qrows):
                row_fn(g * qrows + u)
            return carry
        lax.fori_loop(0, rows // qrows, group, 0)

    sweep(score_row)
    sweep(value_row)


def _neighbourhood_attention(proj, cproj, rpb, batch, n, ctx_len):
    hd = NA_HEAD_DIM
    rows = n // GRID_W
    assert rows >= NA_ROWS
    assert rpb.shape == (NA_HEADS, 2 * NA_ROWS - 1, 2 * NA_COLS - 1)
    base = 4 * RET_WIDTH // hd
    col = lambda which: (lambda h, b, rpb_ref: (b, base + which * NA_HEADS + h))
    grid_spec = pltpu.PrefetchScalarGridSpec(
        num_scalar_prefetch=1,
        grid=(NA_HEADS, batch),
        in_specs=[
            pl.BlockSpec((n, hd), col(0)),
            pl.BlockSpec((n, hd), col(1)),
            pl.BlockSpec((n, hd), col(2)),
            pl.BlockSpec((ctx_len, hd), lambda h, b, rpb_ref: (b, CTX_NA_K * NA_HEADS + h)),
            pl.BlockSpec((ctx_len, hd), lambda h, b, rpb_ref: (b, CTX_NA_V * NA_HEADS + h)),
        ],
        out_specs=pl.BlockSpec((n, hd), lambda h, b, rpb_ref: (b, h)),
        scratch_shapes=[
            pltpu.VMEM((n, NA_ROWS * GRID_W + ctx_len), F32),
            pltpu.VMEM((n, LANES), F32),
            pltpu.VMEM((2 * NA_ROWS - 2, GRID_W, LANES), F32),
        ],
    )
    return pl.pallas_call(
        _na_kernel,
        grid_spec=grid_spec,
        out_shape=jax.ShapeDtypeStruct((batch * n, NA_WIDTH), BF16),
        compiler_params=_params("parallel", "arbitrary"),
        name="neighbourhood_attention",
    )(rpb.astype(F32).reshape(-1), proj, proj, proj, cproj, cproj)


def _outproj_kernel(n_groups, per_group, ret_ref, na_ref, w1_ref, w2_ref, x_ref, ga_ref, shf_ref,
                    scf_ref, nw_ref, wr_ref, br_ref, x1_ref, hf_ref, route_ref, cnt_ref, carry_s,
                    earlier_s):
    tm = x_ref.shape[0]

    @pl.when(pl.program_id(0) == 0)
    def _():
        carry_s[...] = jnp.zeros_like(carry_s)
        earlier_s[...] = jnp.where(lax.broadcasted_iota(jnp.int32, (tm, tm), 0)
                                   > lax.broadcasted_iota(jnp.int32, (tm, tm), 1), 1.0, 0.0).astype(BF16)

    acc = _dot(ret_ref[...], w1_ref[...]) + _dot(na_ref[...], w2_ref[...])
    x1 = x_ref[...] + ga_ref[0] * acc
    x1_ref[...] = x1
    hf = _rms(x1, nw_ref[...]) * (1.0 + scf_ref[0]) + shf_ref[0]
    _store_rows(hf_ref, hf)

    hf_hi = hf.astype(BF16)
    hf_lo = (hf - hf_hi.astype(F32)).astype(BF16)
    p_hi = _dot(hf_hi, wr_ref[...])
    p_lo = _dot(hf_lo, wr_ref[...])
    logits = p_hi[:, :LANES] + p_hi[:, LANES:] + p_lo[:, :LANES] + br_ref[...]
    lane = lax.broadcasted_iota(jnp.int32, (tm, LANES), 1)
    neg = -jnp.inf

    def first_max(vals):
        top = jnp.max(vals, axis=-1, keepdims=True)
        idx = jnp.min(jnp.where(vals == top, lane, LANES), axis=-1, keepdims=True)
        return top, idx

    g_logits = jnp.where(lane < n_groups, logits, neg)
    g_top, g_sel = first_max(g_logits)
    g_w = 1.0 / jnp.sum(jnp.exp(g_logits - g_top), axis=-1, keepdims=True)
    lo = n_groups + g_sel * per_group
    e_logits = jnp.where((lane >= lo) & (lane < lo + per_group), logits, neg)
    v0, i0 = first_max(e_logits)
    v1, i1 = first_max(jnp.where(lane == i0, neg, e_logits))
    e1 = jnp.exp(v1 - v0)
    w0 = g_w / (1.0 + e1)
    w1 = g_w * e1 / (1.0 + e1)

    hit0 = lane == i0
    hit1 = lane == i1
    onehot = jnp.where(hit0 | hit1, 1.0, 0.0)
    before = _dot(earlier_s[...], onehot.astype(BF16)) + carry_s[...]
    rank0 = jnp.sum(jnp.where(hit0, before, 0.0), axis=-1, keepdims=True)
    rank1 = jnp.sum(jnp.where(hit1, before, 0.0), axis=-1, keepdims=True)
    carry_s[...] = carry_s[...] + jnp.sum(onehot, axis=0, keepdims=True)
    cnt_ref[...] = carry_s[...]

    key0 = (i0 - n_groups).astype(F32) * KEY_RANK_SPAN + rank0
    key1 = (i1 - n_groups).astype(F32) * KEY_RANK_SPAN + rank1
    fields = {R_KEY0: key0, R_KEY1: key1, R_W0: w0, R_W1: w1}
    route = jnp.zeros((tm, LANES), F32)
    for idx, val in fields.items():
        route = jnp.where(lane == idx, val, route)
    route_ref[...] = route


def _out_projection(ret, na, w_out, x2d, mod3, norm_w, w_route, b_route, n, n_groups, per_group, tm):
    m, d = x2d.shape
    pitch = _row_pitch(d // LANES)
    batch_of = lambda i: (i * tm) // n
    mod_spec = lambda which: pl.BlockSpec((1, 1, d), lambda i: (batch_of(i) * N_MOD + which, 0, 0))
    const2 = lambda i: (0, 0)
    return pl.pallas_call(
        functools.partial(_outproj_kernel, n_groups, per_group),
        grid=(m // tm,),
        in_specs=[
            pl.BlockSpec((tm, RET_WIDTH), lambda i: (i, 0)),
            pl.BlockSpec((tm, NA_WIDTH), lambda i: (i, 0)),
            pl.BlockSpec((RET_WIDTH, d), lambda i: (0, 0)),
            pl.BlockSpec((NA_WIDTH, d), lambda i: (RET_WIDTH // NA_WIDTH, 0)),
            pl.BlockSpec((tm, d), lambda i: (i, 0)),
            mod_spec(2), mod_spec(3), mod_spec(4),
            pl.BlockSpec((1, d), const2),
            pl.BlockSpec((d, 2 * LANES), const2),
            pl.BlockSpec((1, LANES), const2),
        ],
        out_specs=[
            pl.BlockSpec((tm, d), lambda i: (i, 0)),
            pl.BlockSpec((tm * pitch, LANES), lambda i: (i, 0)),
            pl.BlockSpec((tm, LANES), lambda i: (i, 0)),
            pl.BlockSpec((1, LANES), const2),
        ],
        out_shape=[
            jax.ShapeDtypeStruct((m, d), F32),
            jax.ShapeDtypeStruct((m * pitch, LANES), F32),
            jax.ShapeDtypeStruct((m, LANES), F32),
            jax.ShapeDtypeStruct((1, LANES), F32),
        ],
        scratch_shapes=[pltpu.VMEM((1, LANES), F32), pltpu.VMEM((tm, tm), BF16)],
        compiler_params=_params("arbitrary"),
        name="out_projection_router",
    )(ret, na, w_out, w_out, x2d, mod3, mod3, mod3, norm_w.reshape(1, d), w_route, b_route)


def _slot_map_kernel(tile_rows, pos_ref, cnt_ref, slot_ref, tile_expert_ref, next_expert_ref,
                     run_parity_ref, n_used_ref, offs_s):
    n_experts = cnt_ref.shape[0]
    n_tiles = tile_expert_ref.shape[0]
    shift = tile_rows.bit_length() - 1
    assert 1 << shift == tile_rows

    def fill(ref, lo, hi, val):
        value_at = val if callable(val) else (lambda s: val)
        groups = lax.div(hi - lo, SUBLANES)

        def group(g, carry):
            for k in range(SUBLANES):
                s = lo + g * SUBLANES + k
                ref[s] = value_at(s)
            return carry

        def single(s, carry):
            ref[s] = value_at(s)
            return carry

        lax.fori_loop(0, groups, group, 0)
        lax.fori_loop(lo + groups * SUBLANES, hi, single, 0)

    assert tile_rows <= pos_ref.shape[0] // 2
    padding_row = lambda s: jnp.bitwise_and(s, tile_rows - 1)

    def tiles_of(e):
        size = lax.shift_left(lax.shift_right_logical(cnt_ref[e] + (tile_rows - 1), shift), shift)
        return lax.shift_right_logical(offs_s[e], shift), lax.shift_right_logical(offs_s[e] + size, shift)

    def per_expert(e, carry):
        start, runs = carry
        cnt = cnt_ref[e]
        size = lax.shift_left(lax.shift_right_logical(cnt + (tile_rows - 1), shift), shift)
        offs_s[e] = start
        t0, t1 = tiles_of(e)
        fill(tile_expert_ref, t0, t1, e)
        fill(run_parity_ref, t0, t1, jnp.bitwise_and(runs, 1))
        fill(slot_ref, start + cnt, start + size, padding_row)
        return start + size, runs + (size > 0).astype(jnp.int32)

    end, _ = lax.fori_loop(0, n_experts, per_expert, (0, 0))
    n_used = lax.shift_right_logical(end, shift)
    n_used_ref[0] = n_used
    fill(tile_expert_ref, n_used, n_tiles, n_experts - 1)
    fill(run_parity_ref, n_used, n_tiles, 0)
    fill(next_expert_ref, n_used, n_tiles, -1)
    fill(slot_ref, end, slot_ref.shape[0], padding_row)

    def per_expert_reversed(k, following):
        e = n_experts - 1 - k
        t0, t1 = tiles_of(e)
        fill(next_expert_ref, t0, t1, following)
        return jnp.where(t1 > t0, e, following)

    lax.fori_loop(0, n_experts, per_expert_reversed, -1)

    def assign(a, carry):
        slot_ref[pos_ref[a]] = lax.shift_right_logical(a, 1)
        return carry

    lax.fori_loop(0, pos_ref.shape[0], assign, 0, unroll=2 * SUBLANES)


def _slot_map(pos, cnt, n_tiles, tile_rows):
    smem = pl.BlockSpec(memory_space=pltpu.SMEM)
    i32 = lambda *shape: jax.ShapeDtypeStruct(shape, jnp.int32)
    return pl.pallas_call(
        functools.partial(_slot_map_kernel, tile_rows),
        in_specs=[smem, smem],
        out_specs=[smem] * 5,
        out_shape=[i32(n_tiles * tile_rows), i32(n_tiles), i32(n_tiles), i32(n_tiles), i32(1)],
        scratch_shapes=[pltpu.SMEM((cnt.shape[0],), jnp.int32)],
        name="slot_map",
    )(pos, cnt)


def _expert_kernel(tile_expert_ref, next_expert_ref, run_parity_ref, n_used_ref, tokens_ref,
                   next_tokens_ref, hf_ref, wg_hbm, wu_hbm, wd_hbm, o_ref, xbuf, gsem, wg_f, wu_f,
                   wd_f, wsem, wg_s, wu_s, wd_s):
    i = pl.program_id(0)
    last = pl.num_programs(0) - 1
    n_used = n_used_ref[0]
    d, ff = wg_s.shape
    chunks = d // LANES
    pitch = _row_pitch(chunks)
    tm = o_ref.shape[0] // pitch

    def row_copy(ids_ref, slot, j):
        return pltpu.make_async_copy(_row_ref(hf_ref, ids_ref[0, 0, j], chunks),
                                     _row_ref(xbuf.at[slot], j, chunks), gsem.at[slot])

    def weight_copies(expert, slot):
        return [pltpu.make_async_copy(hbm.at[expert], stage.at[slot], wsem.at[slot])
                for hbm, stage in ((wg_hbm, wg_f), (wu_hbm, wu_f), (wd_hbm, wd_f))]

    @pl.when((i == 0) & (n_used > 0))
    def _():
        _start_row_gather(lambda j: _row_ref(hf_ref, tokens_ref[0, 0, j], chunks), chunks, xbuf.at[0],
                          gsem.at[0], tm)
        for cp in weight_copies(tile_expert_ref[0], run_parity_ref[0]):
            cp.start(priority=WEIGHT_DMA_PRIORITY)

    @pl.when(i < n_used)
    def _():
        @pl.when((i == 0) | (tile_expert_ref[i] != tile_expert_ref[jnp.maximum(i - 1, 0)]))
        def _():
            stage = run_parity_ref[i]
            for cp in weight_copies(tile_expert_ref[i], stage):
                cp.wait()

            @pl.when(next_expert_ref[i] >= 0)
            def _():
                for cp in weight_copies(next_expert_ref[i], 1 - stage):
                    cp.start(priority=WEIGHT_DMA_PRIORITY)

            wg_s[...] = wg_f[stage].astype(BF16)
            wu_s[...] = wu_f[stage].astype(BF16)
            wd_s[...] = wd_f[stage].astype(BF16)

        slot = i % 2
        _wait_row_gather(xbuf.at[slot], gsem.at[slot], tm, chunks)
        x = _load_gathered(xbuf.at[slot], tm, chunks).astype(BF16)

        down_split = min(EXPERT_DOWN_SPLIT, chunks)
        n_stages = 2 + down_split
        bounds = [tm * s // n_stages for s in range(n_stages + 1)]

        def start_next_rows(stage):
            for j in range(bounds[stage], bounds[stage + 1]):
                row_copy(next_tokens_ref, 1 - slot, j).start()

        halves = []
        for half in range(2):
            start_next_rows(half)
            cols = slice(half * ff // 2, (half + 1) * ff // 2)
            halves.append((_silu(_dot(x, wg_s[:, cols])) * _dot(x, wu_s[:, cols])).astype(BF16))
        a = jnp.concatenate(halves, axis=1)
        per_part = chunks // down_split
        for part in range(down_split):
            start_next_rows(2 + part)
            y = _dot(a, wd_s[:, part * per_part * LANES:(part + 1) * per_part * LANES])
            for c in range(per_part):
                o_ref[pl.ds(part * per_part + c, tm, stride=pitch), :] = y[:, c * LANES:(c + 1) * LANES]
        o_ref[pl.ds(chunks, tm, stride=pitch), :] = jnp.zeros((tm, LANES), o_ref.dtype)

        @pl.when(i == last)
        def _():
            _wait_row_gather(xbuf.at[1 - slot], gsem.at[1 - slot], tm, chunks)

    @pl.when(i >= n_used)
    def _():
        @pl.when((i == n_used) & (i > 0))
        def _():
            _wait_row_gather(xbuf.at[i % 2], gsem.at[i % 2], tm, chunks)

        o_ref[...] = jnp.zeros(o_ref.shape, o_ref.dtype)


def _experts(tile_meta, slot_token, hf, w_gate, w_up, w_down, n_tiles, tm):
    d, ff = w_gate.shape[-2:]
    chunks = d // LANES
    pitch = _row_pitch(chunks)
    slot_tiles = slot_token.reshape(n_tiles, 1, tm)
    any_space = pl.BlockSpec(memory_space=pl.ANY)
    ids_block = lambda ahead: pl.BlockSpec(
        (1, 1, tm), lambda i, *prefetch: (jnp.minimum(i + ahead, n_tiles - 1), 0, 0),
        memory_space=pltpu.SMEM)
    grid_spec = pltpu.PrefetchScalarGridSpec(
        num_scalar_prefetch=4,
        grid=(n_tiles,),
        in_specs=[ids_block(0), ids_block(1), any_space, any_space, any_space, any_space],
        out_specs=pl.BlockSpec((tm * pitch, LANES), lambda i, *prefetch: (i, 0)),
        scratch_shapes=[
            pltpu.VMEM((2, tm * pitch, LANES), hf.dtype),
            pltpu.SemaphoreType.DMA((2,)),
            pltpu.VMEM((2, d, ff), F32),
            pltpu.VMEM((2, d, ff), F32),
            pltpu.VMEM((2, ff, d), F32),
            pltpu.SemaphoreType.DMA((2,)),
            pltpu.VMEM((d, ff), BF16),
            pltpu.VMEM((d, ff), BF16),
            pltpu.VMEM((ff, d), BF16),
        ],
    )
    return pl.pallas_call(
        _expert_kernel,
        grid_spec=grid_spec,
        out_shape=jax.ShapeDtypeStruct((n_tiles * tm * pitch, LANES), hf.dtype),
        compiler_params=_params("arbitrary"),
        name="routed_experts",
    )(*tile_meta, slot_tiles, slot_tiles, hf, w_gate, w_up, w_down)


def _combine_kernel(pos_ref, ys_ref, x1_ref, route_ref, gf_ref, fw_ref, o_ref, ybuf, sem):
    i = pl.program_id(0)
    n_tiles = pl.num_programs(0)
    tm, d = x1_ref.shape
    chunks = d // LANES

    def start_tile(tile, slot):
        for choice in range(2):
            _start_row_gather(lambda j: _row_ref(ys_ref, pos_ref[(tile * tm + j) * 2 + choice], chunks),
                              chunks, ybuf.at[slot, choice], sem.at[slot, choice], tm)

    @pl.when(i == 0)
    def _():
        start_tile(0, 0)

    @pl.when(i + 1 < n_tiles)
    def _():
        start_tile(i + 1, (i + 1) % 2)

    slot = i % 2
    for choice in range(2):
        _wait_row_gather(ybuf.at[slot, choice], sem.at[slot, choice], tm, chunks)
    route = route_ref[...]
    moe = (route[:, R_W0:R_W0 + 1] * _load_gathered(ybuf.at[slot, 0], tm, chunks)
           + route[:, R_W1:R_W1 + 1] * _load_gathered(ybuf.at[slot, 1], tm, chunks))
    x2 = x1_ref[...] + gf_ref[0] * moe
    o_ref[...] = _rms(x2, fw_ref[...])


def _combine(pos, ys, x1, route, mod3, final_w, n, tm):
    m, d = x1.shape
    chunks = d // LANES
    grid_spec = pltpu.PrefetchScalarGridSpec(
        num_scalar_prefetch=1,
        grid=(m // tm,),
        in_specs=[
            pl.BlockSpec(memory_space=pl.ANY),
            pl.BlockSpec((tm, d), lambda i, pos_ref: (i, 0)),
            pl.BlockSpec((tm, LANES), lambda i, pos_ref: (i, 0)),
            pl.BlockSpec((1, 1, d), lambda i, pos_ref: (((i * tm) // n) * N_MOD + 5, 0, 0)),
            pl.BlockSpec((1, d), lambda i, pos_ref: (0, 0)),
        ],
        out_specs=pl.BlockSpec((tm, d), lambda i, pos_ref: (i, 0)),
        scratch_shapes=[pltpu.VMEM((2, 2, tm * _row_pitch(chunks), LANES), ys.dtype),
                        pltpu.SemaphoreType.DMA((2, 2))],
    )
    return pl.pallas_call(
        _combine_kernel,
        grid_spec=grid_spec,
        out_shape=jax.ShapeDtypeStruct((m, d), F32),
        compiler_params=_params("arbitrary"),
        name="combine_final_norm",
    )(pos, ys, x1, route, mod3, final_w.reshape(1, d))


def kernel(x, c, ctx, c_ctx, w_mod, b_mod, norm_mix_w, w_in, ret_decay_f, ret_decay_b, ret_gn_w, na_rpb, w_out, norm_ffn_w, w_router_group, b_router_group, w_router_expert, b_router_expert, w_gate, w_up, w_down, final_norm_w):
    assert w_mod.shape[0] == 1, "single trunk layer"
    batch, n, d = x.shape
    ctx_len = ctx.shape[1]
    n_groups = w_router_group.shape[-1]
    per_group = w_router_expert.shape[-1]
    n_experts = w_gate.shape[1]
    assert n_groups * per_group == n_experts and n_groups + n_experts <= LANES

    mod_rows = 8
    cc = jnp.zeros((mod_rows, d), F32).at[:batch].set(c).at[batch].set(c_ctx)
    mod = _modulation(cc, w_mod[0], b_mod[0])
    mod3 = mod.reshape(mod_rows * N_MOD, 1, d)

    w_in_b = w_in[0]
    tm = min(512, n)
    tm_in = min(1024, n)
    x2d = x.reshape(batch * n, d)
    proj = _in_projection(x2d, mod3, lambda i: (i * tm_in) // n, norm_mix_w[0], w_in_b, tm_in)
    cproj = _in_projection(ctx.reshape(batch * ctx_len, d), mod3, lambda i: batch, norm_mix_w[0],
                           w_in_b, batch * ctx_len, CTX_COL_BLOCKS)

    lg = jnp.stack([jax.nn.log_sigmoid(ret_decay_f[0].astype(F32)),
                    jax.nn.log_sigmoid(ret_decay_b[0].astype(F32))])
    ret = _retention(proj, cproj, lg, ret_gn_w[0], batch, n, ctx_len)
    na = _neighbourhood_attention(proj, cproj, na_rpb[0], batch, n, ctx_len)

    w_route = jnp.concatenate(
        [w_router_group[0], jnp.moveaxis(w_router_expert[0], 0, 1).reshape(d, n_experts)], axis=1)
    w_route = jnp.pad(w_route.astype(F32), ((0, 0), (0, LANES - n_groups - n_experts)))
    w_route_hi = w_route.astype(BF16)
    w_route = jnp.concatenate([w_route_hi, (w_route - w_route_hi.astype(F32)).astype(BF16)], axis=1)
    b_route = jnp.concatenate([b_router_group[0], b_router_expert[0].reshape(-1)])
    b_route = jnp.pad(b_route.astype(F32), (0, LANES - n_groups - n_experts)).reshape(1, LANES)
    x1, hf, route, counts = _out_projection(ret, na, w_out[0].astype(BF16), x2d, mod3, norm_ffn_w[0],
                                            w_route, b_route, n, n_groups, per_group, tm)

    te = EXPERT_TILE
    tokens = batch * n
    assert tokens < 1 << KEY_RANK_BITS and (2 * tokens) % te == 0
    n_tiles = (2 * tokens) // te + n_experts
    cnt = counts[0, n_groups:n_groups + n_experts].astype(jnp.int32)
    padded = (cnt + te - 1) // te * te
    starts = jnp.cumsum(padded) - padded
    keys = route[:, R_KEY0:R_KEY1 + 1].astype(jnp.int32).reshape(-1)
    expert = lax.shift_right_logical(keys, KEY_RANK_BITS)
    start_of = jnp.sum(jnp.where(expert[:, None] == jnp.arange(n_experts)[None, :], starts[None, :], 0), axis=1)
    pos = start_of + jnp.bitwise_and(keys, (1 << KEY_RANK_BITS) - 1)
    slot_token, *tile_meta = _slot_map(pos, cnt, n_tiles, te)

    ys = _experts(tile_meta, slot_token, hf, w_gate[0], w_up[0], w_down[0], n_tiles, te)
    out = _combine(pos, ys, x1, route, mod3, final_norm_w, n, min(256, n))
    return out.reshape(batch, n, d)
```

```python
import functools
import math

import jax
import jax.numpy as jnp
from jax import lax
from jax.experimental import pallas as pl
from jax.experimental.pallas import tpu as pltpu

F32 = jnp.float32
BF16 = jnp.bfloat16

GRID_W = 64
RET_HEADS = 4
RET_HEAD_DIM = 256
RET_WIDTH = RET_HEADS * RET_HEAD_DIM
NA_HEADS = 8
NA_HEAD_DIM = 128
NA_WIDTH = NA_HEADS * NA_HEAD_DIM
RET_CHUNK = 128
NA_ROWS = 8
NA_COLS = 16
ROPE_BASE = 10000.0
N_MOD = 6
NORM_EPS = 1e-6
ROPE_HALF = RET_HEAD_DIM // 2
RET_UNROLL = 8

INPROJ_TN = 1024
assert INPROJ_TN == RET_WIDTH == NA_WIDTH
CTX_COL_BLOCKS = (1, 2, 5, 6)
CTX_RET_K, CTX_RET_V, CTX_NA_K, CTX_NA_V = 0, 1, 2, 3

NA_QROWS = 16
NA_MASK = -1e30

LANES = 128
SUBLANES = 8
BF16_ROWS = 2 * SUBLANES
VMEM_LIMIT_BYTES = 56 * 1024 * 1024

R_KEY0, R_KEY1, R_W0, R_W1 = 0, 1, 2, 3
KEY_RANK_BITS = 16
KEY_RANK_SPAN = float(1 << KEY_RANK_BITS)

EXPERT_TILE = 256
WEIGHT_DMA_PRIORITY = 1
EXPERT_DOWN_SPLIT = 4


def _params(*sem):
    return pltpu.CompilerParams(dimension_semantics=sem, vmem_limit_bytes=VMEM_LIMIT_BYTES)


def _dot(a, b):
    return jnp.dot(a, b, preferred_element_type=F32)


def _dot_nt(a, b):
    return lax.dot_general(a, b, (((1,), (1,)), ((), ())), preferred_element_type=F32)


def _dot_tn(a, b):
    return lax.dot_general(a, b, (((0,), (0,)), ((), ())), preferred_element_type=F32)


def _rms(x, w):
    return x * lax.rsqrt(jnp.mean(x * x, axis=-1, keepdims=True) + NORM_EPS) * w


def _silu(x):
    return x * jax.nn.sigmoid(x)


def _row_pitch(chunks):
    return chunks + 1


def _store_rows(ref, val):
    n_rows, d = val.shape
    chunks = d // LANES
    pitch = _row_pitch(chunks)
    for c in range(chunks):
        ref[pl.ds(c, n_rows, stride=pitch), :] = val[:, c * LANES:(c + 1) * LANES]
    ref[pl.ds(chunks, n_rows, stride=pitch), :] = jnp.zeros((n_rows, LANES), ref.dtype)


def _row_ref(hbm, row, chunks):
    return hbm.at[pl.ds(row * _row_pitch(chunks), chunks), :]


def _start_row_gather(src_row, chunks, dst_buf, sem, n_rows):
    def body(j, carry):
        pltpu.make_async_copy(src_row(j), _row_ref(dst_buf, j, chunks), sem).start()
        return carry

    lax.fori_loop(0, n_rows, body, 0, unroll=SUBLANES)


def _wait_row_gather(dst_buf, sem, n_rows, chunks):
    view = dst_buf.at[pl.ds(0, n_rows * chunks), :]
    pltpu.make_async_copy(view, view, sem).wait()


def _load_gathered(buf, n_rows, chunks):
    return jnp.concatenate([buf[pl.ds(c, n_rows, stride=_row_pitch(chunks)), :] for c in range(chunks)],
                           axis=1)


def _mod_kernel(c_ref, w_ref, b_ref, o_ref):
    a = _silu(c_ref[...]).astype(BF16)
    o_ref[...] = _dot(a, w_ref[...].astype(BF16)) + b_ref[...]


def _modulation(cc, w_mod, b_mod):
    rows, d = cc.shape
    width = w_mod.shape[1]
    tn = next(t for t in (1024, 512, 256, LANES) if width % t == 0)
    return pl.pallas_call(
        _mod_kernel,
        grid=(width // tn,),
        in_specs=[
            pl.BlockSpec((rows, d), lambda j: (0, 0)),
            pl.BlockSpec((d, tn), lambda j: (0, j)),
            pl.BlockSpec((1, tn), lambda j: (0, j)),
        ],
        out_specs=pl.BlockSpec((rows, tn), lambda j: (0, j)),
        out_shape=jax.ShapeDtypeStruct((rows, width), F32),
        compiler_params=_params("arbitrary"),
        name="modulation",
    )(cc, w_mod, b_mod.reshape(1, width))


def _inproj_kernel(rows_per_step, x_ref, sh_ref, sc_ref, nw_ref, w_ref, o_ref, h_s):
    i = pl.program_id(0)
    j = pl.program_id(1)
    tm = o_ref.shape[0]

    def normalise(rows, slot):
        y = _rms(x_ref[rows, :], nw_ref[...])
        h_s[slot, rows, :] = (y * (1.0 + sc_ref[0]) + sh_ref[0]).astype(BF16)

    @pl.when((i == 0) & (j == 0))
    def _():
        normalise(pl.ds(0, tm), 0)

    o_ref[...] = _dot(h_s[i % 2], w_ref[...].astype(BF16)).astype(BF16)
    start = jnp.clip((j - 1) * rows_per_step, 0, tm - rows_per_step)
    normalise(pl.ds(pl.multiple_of(start, BF16_ROWS), rows_per_step), (i + 1) % 2)


def _in_projection(x2d, mod3, mod_row_of_tile, norm_w, w_in, tm, col_blocks=None):
    m, d = x2d.shape
    tn = INPROJ_TN
    if col_blocks is None:
        col_blocks = tuple(range(w_in.shape[1] // tn))
    n_blocks = len(col_blocks)
    n_tiles = m // tm
    assert n_blocks >= 2 and tm % BF16_ROWS == 0
    rows_per_step = -(-tm // ((n_blocks - 1) * BF16_ROWS)) * BF16_ROWS
    w_block = lambda j: sum(jnp.where(j == k, blk, 0) for k, blk in enumerate(col_blocks))
    src = lambda i, j: jnp.where((i == 0) & (j == 0), 0, jnp.minimum(i + 1, n_tiles - 1))
    return pl.pallas_call(
        functools.partial(_inproj_kernel, rows_per_step),
        grid=(n_tiles, n_blocks),
        in_specs=[
            pl.BlockSpec((tm, d), lambda i, j: (src(i, j), 0)),
            pl.BlockSpec((1, 1, d), lambda i, j: (mod_row_of_tile(src(i, j)) * N_MOD + 0, 0, 0)),
            pl.BlockSpec((1, 1, d), lambda i, j: (mod_row_of_tile(src(i, j)) * N_MOD + 1, 0, 0)),
            pl.BlockSpec((1, d), lambda i, j: (0, 0)),
            pl.BlockSpec((d, tn), lambda i, j: (0, w_block(j))),
        ],
        out_specs=pl.BlockSpec((tm, tn), lambda i, j: (i, j)),
        out_shape=jax.ShapeDtypeStruct((m, n_blocks * tn), BF16),
        scratch_shapes=[pltpu.VMEM((2, tm, d), BF16)],
        compiler_params=_params("arbitrary", "arbitrary"),
        name="in_projection",
    )(x2d, mod3, mod3, norm_w.reshape(1, d), w_in)


def _ret_kernel(lg_ref, q_ref, k_ref, v_ref, g_ref, ck_ref, cv_ref, cosr_ref, sinr_ref,
                cosc_ref, sinc_ref, gnw_ref, o_ref, qr_s, kr_s, sfh_s, sbh_s, sf_s, sb_s):
    head = pl.program_id(1)
    lgf = lg_ref[0, head]
    lgb = lg_ref[1, head]
    n = q_ref.shape[0]
    c = RET_CHUNK
    nc = n // c
    ctx_len = ck_ref.shape[0]
    k_scale = RET_HEAD_DIM ** -0.5

    posl = lax.broadcasted_iota(jnp.int32, (ctx_len, 1), 0).astype(F32)
    ck = ck_ref[...].astype(F32) * k_scale
    cv = cv_ref[...]
    sf_s[...] = _dot_tn((ck * jnp.exp(lgf * (ctx_len - 1.0 - posl))).astype(BF16), cv)
    sb_s[...] = _dot_tn((ck * jnp.exp(lgb * posl)).astype(BF16), cv)

    cosc = cosc_ref[...]
    sinc = sinc_ref[...]
    pos = lax.broadcasted_iota(jnp.int32, (c, 1), 0).astype(F32)
    qdec_f = jnp.exp(lgf * (pos + 1.0))
    kdec_f = jnp.exp(lgf * (c - 1.0 - pos))
    cdec_f = jnp.exp(lgf * c)
    qdec_b = jnp.exp(lgb * (c - pos))
    kdec_b = jnp.exp(lgb * pos)
    cdec_b = jnp.exp(lgb * c)

    def rope(x, cosr, sinr):
        xa = x[:, :ROPE_HALF]
        xb = x[:, ROPE_HALF:]
        ya = xa * cosr + pltpu.roll(xa, ROPE_HALF // 2, 1) * sinr
        yb = xb * cosc + pltpu.roll(xb, ROPE_HALF // 2, 1) * sinc
        return jnp.concatenate([ya, yb], axis=1)

    def fwd_chunk(ci, carry):
        r0 = pl.multiple_of(ci * c, c)
        cosr = cosr_ref[pl.ds(r0, c), :]
        sinr = sinr_ref[pl.ds(r0, c), :]
        qr_s[pl.ds(r0, c), :] = rope(q_ref[pl.ds(r0, c), :].astype(F32), cosr, sinr).astype(BF16)
        k = rope(k_ref[pl.ds(r0, c), :].astype(F32), cosr, sinr) * k_scale
        kr_s[pl.ds(r0, c), :] = k.astype(BF16)
        kv = _dot_tn((k * kdec_f).astype(BF16), v_ref[pl.ds(r0, c), :])
        state = sf_s[...]
        sfh_s[ci] = state.astype(BF16)
        sf_s[...] = state * cdec_f + kv
        return carry

    lax.fori_loop(0, nc, fwd_chunk, 0, unroll=RET_UNROLL)

    def bwd_chunk(i, carry):
        ci = nc - 1 - i
        r0 = pl.multiple_of(ci * c, c)
        k = kr_s[pl.ds(r0, c), :].astype(F32)
        kv = _dot_tn((k * kdec_b).astype(BF16), v_ref[pl.ds(r0, c), :])
        state = sb_s[...]
        sbh_s[ci] = state.astype(BF16)
        sb_s[...] = state * cdec_b + kv
        return carry

    lax.fori_loop(0, nc, bwd_chunk, 0, unroll=RET_UNROLL)

    diff = (lax.broadcasted_iota(jnp.int32, (c, c), 0)
            - lax.broadcasted_iota(jnp.int32, (c, c), 1)).astype(F32)
    intra = (jnp.where(diff >= 0, jnp.exp(lgf * jnp.maximum(diff, 0.0)), 0.0)
             + jnp.where(diff <= 0, jnp.exp(lgb * jnp.maximum(-diff, 0.0)), 0.0))
    gnw = gnw_ref[...]

    def out_chunk(ci, carry):
        r0 = pl.multiple_of(ci * c, c)
        qb = qr_s[pl.ds(r0, c), :]
        kb = kr_s[pl.ds(r0, c), :]
        q = qb.astype(F32)
        scores = _dot_nt(qb, kb) * intra
        o = (_dot(scores.astype(BF16), v_ref[pl.ds(r0, c), :])
             + _dot((q * qdec_f).astype(BF16), sfh_s[ci])
             + _dot((q * qdec_b).astype(BF16), sbh_s[ci]))
        mu = jnp.mean(o, axis=-1, keepdims=True)
        d = o - mu
        var = jnp.mean(d * d, axis=-1, keepdims=True)
        on = d * lax.rsqrt(var + NORM_EPS) * gnw
        gate = _silu(g_ref[pl.ds(r0, c), :].astype(F32))
        o_ref[pl.ds(r0, c), :] = (on * gate).astype(BF16)
        return carry

    lax.fori_loop(0, nc, out_chunk, 0, unroll=RET_UNROLL)


def _rope_tables(n):
    inv = ROPE_BASE ** (-jnp.arange(0, ROPE_HALF, 2, dtype=F32) / ROPE_HALF)

    def tables(pos):
        ang = pos.astype(F32)[:, None] * inv[None, :]
        cos = jnp.cos(ang)
        sin = jnp.sin(ang)
        return jnp.concatenate([cos, cos], axis=1), jnp.concatenate([-sin, sin], axis=1)

    t = jnp.arange(n)
    cosr, sinr = tables(t // GRID_W)
    cosc, sinc = tables(jnp.arange(RET_CHUNK) % GRID_W)
    return cosr, sinr, cosc, sinc


def _retention(proj, cproj, lg, gn_w, batch, n, ctx_len):
    hd = RET_HEAD_DIM
    cosr, sinr, cosc, sinc = _rope_tables(n)
    col = lambda which: (lambda b, h, lg_ref: (b, which * RET_HEADS + h))
    const = lambda b, h, lg_ref: (0, 0)
    grid_spec = pltpu.PrefetchScalarGridSpec(
        num_scalar_prefetch=1,
        grid=(batch, RET_HEADS),
        in_specs=[
            pl.BlockSpec((n, hd), col(0)),
            pl.BlockSpec((n, hd), col(1)),
            pl.BlockSpec((n, hd), col(2)),
            pl.BlockSpec((n, hd), col(3)),
            pl.BlockSpec((ctx_len, hd), lambda b, h, lg_ref: (b, CTX_RET_K * RET_HEADS + h)),
            pl.BlockSpec((ctx_len, hd), lambda b, h, lg_ref: (b, CTX_RET_V * RET_HEADS + h)),
            pl.BlockSpec((n, ROPE_HALF), const),
            pl.BlockSpec((n, ROPE_HALF), const),
            pl.BlockSpec((RET_CHUNK, ROPE_HALF), const),
            pl.BlockSpec((RET_CHUNK, ROPE_HALF), const),
            pl.BlockSpec((1, hd), lambda b, h, lg_ref: (0, h)),
        ],
        out_specs=pl.BlockSpec((n, hd), lambda b, h, lg_ref: (b, h)),
        scratch_shapes=[
            pltpu.VMEM((n, hd), BF16),
            pltpu.VMEM((n, hd), BF16),
            pltpu.VMEM((n // RET_CHUNK, hd, hd), BF16),
            pltpu.VMEM((n // RET_CHUNK, hd, hd), BF16),
            pltpu.VMEM((hd, hd), F32),
            pltpu.VMEM((hd, hd), F32),
        ],
    )
    return pl.pallas_call(
        _ret_kernel,
        grid_spec=grid_spec,
        out_shape=jax.ShapeDtypeStruct((batch * n, RET_WIDTH), BF16),
        compiler_params=_params("parallel", "arbitrary"),
        name="retention",
    )(lg, proj, proj, proj, proj, cproj, cproj, cosr, sinr, cosc, sinc, gn_w.reshape(1, RET_WIDTH))


def _na_build_bias(rpb_ref, head, pair_s):
    n_roff, n_coff = 2 * NA_ROWS - 1, 2 * NA_COLS - 1
    c = lax.broadcasted_iota(jnp.int32, (GRID_W, LANES), 0)
    lane = lax.broadcasted_iota(jnp.int32, (GRID_W, LANES), 1)
    kc = jnp.bitwise_and(lane, GRID_W - 1)
    cs = jnp.clip(c - NA_COLS // 2, 0, GRID_W - NA_COLS)
    d = jnp.where((kc >= cs) & (kc < cs + NA_COLS), kc - c + (NA_COLS - 1), -1)

    def table(i):
        acc = jnp.full((GRID_W, LANES), NA_MASK, F32)
        base = (head * n_roff + i) * n_coff
        for j in range(n_coff):
            acc = jnp.where(d == j, rpb_ref[base + j], acc)
        return acc

    prev = table(0)
    for i in range(1, n_roff):
        cur = table(i)
        pair_s[i - 1] = jnp.where(lane < GRID_W, prev, cur)
        prev = cur


def _na_kernel(rpb_ref, q_ref, k_ref, v_ref, kc_ref, vc_ref, o_ref, s_s, m_s, pair_s):
    @pl.when(pl.program_id(1) == 0)
    def _():
        _na_build_bias(rpb_ref, pl.program_id(0), pair_s)

    n = q_ref.shape[0]
    rows = n // GRID_W
    kb = NA_ROWS * GRID_W
    ctx_len = kc_ref.shape[0]
    scale = NA_HEAD_DIM ** -0.5
    kc = kc_ref[...]
    vc = jnp.concatenate([vc_ref[...], jnp.ones((ctx_len, NA_HEAD_DIM), BF16)], axis=1)
    ones_w = jnp.ones((kb, NA_HEAD_DIM), BF16)

    def window(r):
        rs = jnp.clip(r - NA_ROWS // 2, 0, rows - NA_ROWS)
        return rs, pl.multiple_of(r * GRID_W, GRID_W), pl.multiple_of(rs * GRID_W, GRID_W)

    def score_row(r):
        rs, q0, k0 = window(r)
        q = (q_ref[pl.ds(q0, GRID_W), :].astype(F32) * scale).astype(BF16)
        off = rs - r + NA_ROWS - 1
        bias = jnp.concatenate([pair_s[off + 2 * t] for t in range(NA_ROWS // 2)], axis=1)
        s_win = _dot_nt(q, k_ref[pl.ds(k0, kb), :]) + bias
        s_ctx = _dot_nt(q, kc)
        m = jnp.maximum(jnp.max(s_win, axis=-1, keepdims=True), jnp.max(s_ctx, axis=-1, keepdims=True))
        s_s[pl.ds(q0, GRID_W), :kb] = s_win
        s_s[pl.ds(q0, GRID_W), kb:] = s_ctx
        m_s[pl.ds(q0, GRID_W), :] = jnp.broadcast_to(m, (GRID_W, LANES))

    def value_row(r):
        _, q0, k0 = window(r)
        m = m_s[pl.ds(q0, GRID_W), :]
        p_win = jnp.exp(s_s[pl.ds(q0, GRID_W), :kb] - jnp.tile(m, (1, kb // LANES))).astype(BF16)
        p_ctx = jnp.exp(s_s[pl.ds(q0, GRID_W), kb:] - jnp.tile(m, (1, ctx_len // LANES))).astype(BF16)
        vw = jnp.concatenate([v_ref[pl.ds(k0, kb), :], ones_w], axis=1)
        o = _dot(p_win, vw) + _dot(p_ctx, vc)
        o_ref[pl.ds(q0, GRID_W), :] = (o[:, :NA_HEAD_DIM] / o[:, NA_HEAD_DIM:]).astype(BF16)

    qrows = math.gcd(rows, NA_QROWS)

    def sweep(row_fn):
        def group(g, carry):
            for u in range(qrows):
                row_fn(g * qrows + u)
            return carry
        lax.fori_loop(0, rows // qrows, group, 0)

    sweep(score_row)
    sweep(value_row)


def _neighbourhood_attention(proj, cproj, rpb, batch, n, ctx_len):
    hd = NA_HEAD_DIM
    rows = n // GRID_W
    assert rows >= NA_ROWS
    assert rpb.shape == (NA_HEADS, 2 * NA_ROWS - 1, 2 * NA_COLS - 1)
    base = 4 * RET_WIDTH // hd
    col = lambda which: (lambda h, b, rpb_ref: (b, base + which * NA_HEADS + h))
    grid_spec = pltpu.PrefetchScalarGridSpec(
        num_scalar_prefetch=1,
        grid=(NA_HEADS, batch),
        in_specs=[
            pl.BlockSpec((n, hd), col(0)),
            pl.BlockSpec((n, hd), col(1)),
            pl.BlockSpec((n, hd), col(2)),
            pl.BlockSpec((ctx_len, hd), lambda h, b, rpb_ref: (b, CTX_NA_K * NA_HEADS + h)),
            pl.BlockSpec((ctx_len, hd), lambda h, b, rpb_ref: (b, CTX_NA_V * NA_HEADS + h)),
        ],
        out_specs=pl.BlockSpec((n, hd), lambda h, b, rpb_ref: (b, h)),
        scratch_shapes=[
            pltpu.VMEM((n, NA_ROWS * GRID_W + ctx_len), F32),
            pltpu.VMEM((n, LANES), F32),
            pltpu.VMEM((2 * NA_ROWS - 2, GRID_W, LANES), F32),
        ],
    )
    return pl.pallas_call(
        _na_kernel,
        grid_spec=grid_spec,
        out_shape=jax.ShapeDtypeStruct((batch * n, NA_WIDTH), BF16),
        compiler_params=_params("parallel", "arbitrary"),
        name="neighbourhood_attention",
    )(rpb.astype(F32).reshape(-1), proj, proj, proj, cproj, cproj)


def _outproj_kernel(n_groups, per_group, ret_ref, na_ref, w1_ref, w2_ref, x_ref, ga_ref, shf_ref,
                    scf_ref, nw_ref, wr_ref, br_ref, x1_ref, hf_ref, route_ref, cnt_ref, carry_s,
                    earlier_s):
    tm = x_ref.shape[0]

    @pl.when(pl.program_id(0) == 0)
    def _():
        carry_s[...] = jnp.zeros_like(carry_s)
        earlier_s[...] = jnp.where(lax.broadcasted_iota(jnp.int32, (tm, tm), 0)
                                   > lax.broadcasted_iota(jnp.int32, (tm, tm), 1), 1.0, 0.0).astype(BF16)

    acc = _dot(ret_ref[...], w1_ref[...]) + _dot(na_ref[...], w2_ref[...])
    x1 = x_ref[...] + ga_ref[0] * acc
    x1_ref[...] = x1
    hf = _rms(x1, nw_ref[...]) * (1.0 + scf_ref[0]) + shf_ref[0]
    _store_rows(hf_ref, hf)

    hf_hi = hf.astype(BF16)
    hf_lo = (hf - hf_hi.astype(F32)).astype(BF16)
    p_hi = _dot(hf_hi, wr_ref[...])
    p_lo = _dot(hf_lo, wr_ref[...])
    logits = p_hi[:, :LANES] + p_hi[:, LANES:] + p_lo[:, :LANES] + br_ref[...]
    lane = lax.broadcasted_iota(jnp.int32, (tm, LANES), 1)
    neg = -jnp.inf

    def first_max(vals):
        top = jnp.max(vals, axis=-1, keepdims=True)
        idx = jnp.min(jnp.where(vals == top, lane, LANES), axis=-1, keepdims=True)
        return top, idx

    g_logits = jnp.where(lane < n_groups, logits, neg)
    g_top, g_sel = first_max(g_logits)
    g_w = 1.0 / jnp.sum(jnp.exp(g_logits - g_top), axis=-1, keepdims=True)
    lo = n_groups + g_sel * per_group
    e_logits = jnp.where((lane >= lo) & (lane < lo + per_group), logits, neg)
    v0, i0 = first_max(e_logits)
    v1, i1 = first_max(jnp.where(lane == i0, neg, e_logits))
    e1 = jnp.exp(v1 - v0)
    w0 = g_w / (1.0 + e1)
    w1 = g_w * e1 / (1.0 + e1)

    hit0 = lane == i0
    hit1 = lane == i1
    onehot = jnp.where(hit0 | hit1, 1.0, 0.0)
    before = _dot(earlier_s[...], onehot.astype(BF16)) + carry_s[...]
    rank0 = jnp.sum(jnp.where(hit0, before, 0.0), axis=-1, keepdims=True)
    rank1 = jnp.sum(jnp.where(hit1, before, 0.0), axis=-1, keepdims=True)
    carry_s[...] = carry_s[...] + jnp.sum(onehot, axis=0, keepdims=True)
    cnt_ref[...] = carry_s[...]

    key0 = (i0 - n_groups).astype(F32) * KEY_RANK_SPAN + rank0
    key1 = (i1 - n_groups).astype(F32) * KEY_RANK_SPAN + rank1
    fields = {R_KEY0: key0, R_KEY1: key1, R_W0: w0, R_W1: w1}
    route = jnp.zeros((tm, LANES), F32)
    for idx, val in fields.items():
        route = jnp.where(lane == idx, val, route)
    route_ref[...] = route


def _out_projection(ret, na, w_out, x2d, mod3, norm_w, w_route, b_route, n, n_groups, per_group, tm):
    m, d = x2d.shape
    pitch = _row_pitch(d // LANES)
    batch_of = lambda i: (i * tm) // n
    mod_spec = lambda which: pl.BlockSpec((1, 1, d), lambda i: (batch_of(i) * N_MOD + which, 0, 0))
    const2 = lambda i: (0, 0)
    return pl.pallas_call(
        functools.partial(_outproj_kernel, n_groups, per_group),
        grid=(m // tm,),
        in_specs=[
            pl.BlockSpec((tm, RET_WIDTH), lambda i: (i, 0)),
            pl.BlockSpec((tm, NA_WIDTH), lambda i: (i, 0)),
            pl.BlockSpec((RET_WIDTH, d), lambda i: (0, 0)),
            pl.BlockSpec((NA_WIDTH, d), lambda i: (RET_WIDTH // NA_WIDTH, 0)),
            pl.BlockSpec((tm, d), lambda i: (i, 0)),
            mod_spec(2), mod_spec(3), mod_spec(4),
            pl.BlockSpec((1, d), const2),
            pl.BlockSpec((d, 2 * LANES), const2),
            pl.BlockSpec((1, LANES), const2),
        ],
        out_specs=[
            pl.BlockSpec((tm, d), lambda i: (i, 0)),
            pl.BlockSpec((tm * pitch, LANES), lambda i: (i, 0)),
            pl.BlockSpec((tm, LANES), lambda i: (i, 0)),
            pl.BlockSpec((1, LANES), const2),
        ],
        out_shape=[
            jax.ShapeDtypeStruct((m, d), F32),
            jax.ShapeDtypeStruct((m * pitch, LANES), F32),
            jax.ShapeDtypeStruct((m, LANES), F32),
            jax.ShapeDtypeStruct((1, LANES), F32),
        ],
        scratch_shapes=[pltpu.VMEM((1, LANES), F32), pltpu.VMEM((tm, tm), BF16)],
        compiler_params=_params("arbitrary"),
        name="out_projection_router",
    )(ret, na, w_out, w_out, x2d, mod3, mod3, mod3, norm_w.reshape(1, d), w_route, b_route)


def _slot_map_kernel(tile_rows, pos_ref, cnt_ref, slot_ref, tile_expert_ref, next_expert_ref,
                     run_parity_ref, n_used_ref, offs_s):
    n_experts = cnt_ref.shape[0]
    n_tiles = tile_expert_ref.shape[0]
    shift = tile_rows.bit_length() - 1
    assert 1 << shift == tile_rows

    def fill(ref, lo, hi, val):
        value_at = val if callable(val) else (lambda s: val)
        groups = lax.div(hi - lo, SUBLANES)

        def group(g, carry):
            for k in range(SUBLANES):
                s = lo + g * SUBLANES + k
                ref[s] = value_at(s)
            return carry

        def single(s, carry):
            ref[s] = value_at(s)
            return carry

        lax.fori_loop(0, groups, group, 0)
        lax.fori_loop(lo + groups * SUBLANES, hi, single, 0)

    assert tile_rows <= pos_ref.shape[0] // 2
    padding_row = lambda s: jnp.bitwise_and(s, tile_rows - 1)

    def tiles_of(e):
        size = lax.shift_left(lax.shift_right_logical(cnt_ref[e] + (tile_rows - 1), shift), shift)
        return lax.shift_right_logical(offs_s[e], shift), lax.shift_right_logical(offs_s[e] + size, shift)

    def per_expert(e, carry):
        start, runs = carry
        cnt = cnt_ref[e]
        size = lax.shift_left(lax.shift_right_logical(cnt + (tile_rows - 1), shift), shift)
        offs_s[e] = start
        t0, t1 = tiles_of(e)
        fill(tile_expert_ref, t0, t1, e)
        fill(run_parity_ref, t0, t1, jnp.bitwise_and(runs, 1))
        fill(slot_ref, start + cnt, start + size, padding_row)
        return start + size, runs + (size > 0).astype(jnp.int32)

    end, _ = lax.fori_loop(0, n_experts, per_expert, (0, 0))
    n_used = lax.shift_right_logical(end, shift)
    n_used_ref[0] = n_used
    fill(tile_expert_ref, n_used, n_tiles, n_experts - 1)
    fill(run_parity_ref, n_used, n_tiles, 0)
    fill(next_expert_ref, n_used, n_tiles, -1)
    fill(slot_ref, end, slot_ref.shape[0], padding_row)

    def per_expert_reversed(k, following):
        e = n_experts - 1 - k
        t0, t1 = tiles_of(e)
        fill(next_expert_ref, t0, t1, following)
        return jnp.where(t1 > t0, e, following)

    lax.fori_loop(0, n_experts, per_expert_reversed, -1)

    def assign(token, carry):
        slot_ref[pos_ref[2 * token]] = token
        slot_ref[pos_ref[2 * token + 1]] = token
        return carry

    lax.fori_loop(0, pos_ref.shape[0] // 2, assign, 0, unroll=SUBLANES)


def _slot_map(pos, cnt, n_tiles, tile_rows):
    smem = pl.BlockSpec(memory_space=pltpu.SMEM)
    i32 = lambda *shape: jax.ShapeDtypeStruct(shape, jnp.int32)
    return pl.pallas_call(
        functools.partial(_slot_map_kernel, tile_rows),
        in_specs=[smem, smem],
        out_specs=[smem] * 5,
        out_shape=[i32(n_tiles * tile_rows), i32(n_tiles), i32(n_tiles), i32(n_tiles), i32(1)],
        scratch_shapes=[pltpu.SMEM((cnt.shape[0],), jnp.int32)],
        name="slot_map",
    )(pos, cnt)


def _expert_kernel(tile_expert_ref, next_expert_ref, run_parity_ref, n_used_ref, tokens_ref,
                   next_tokens_ref, hf_ref, wg_hbm, wu_hbm, wd_hbm, o_ref, xbuf, gsem, wg_f, wu_f,
                   wd_f, wsem, wg_s, wu_s, wd_s):
    i = pl.program_id(0)
    last = pl.num_programs(0) - 1
    n_used = n_used_ref[0]
    d, ff = wg_s.shape
    chunks = d // LANES
    pitch = _row_pitch(chunks)
    tm = o_ref.shape[0] // pitch

    def row_copy(ids_ref, slot, j):
        return pltpu.make_async_copy(_row_ref(hf_ref, ids_ref[0, 0, j], chunks),
                                     _row_ref(xbuf.at[slot], j, chunks), gsem.at[slot])

    def weight_copies(expert, slot):
        return [pltpu.make_async_copy(hbm.at[expert], stage.at[slot], wsem.at[slot])
                for hbm, stage in ((wg_hbm, wg_f), (wu_hbm, wu_f), (wd_hbm, wd_f))]

    @pl.when((i == 0) & (n_used > 0))
    def _():
        _start_row_gather(lambda j: _row_ref(hf_ref, tokens_ref[0, 0, j], chunks), chunks, xbuf.at[0],
                          gsem.at[0], tm)
        for cp in weight_copies(tile_expert_ref[0], run_parity_ref[0]):
            cp.start(priority=WEIGHT_DMA_PRIORITY)

    @pl.when(i < n_used)
    def _():
        @pl.when((i == 0) | (tile_expert_ref[i] != tile_expert_ref[jnp.maximum(i - 1, 0)]))
        def _():
            stage = run_parity_ref[i]
            for cp in weight_copies(tile_expert_ref[i], stage):
                cp.wait()

            @pl.when(next_expert_ref[i] >= 0)
            def _():
                for cp in weight_copies(next_expert_ref[i], 1 - stage):
                    cp.start(priority=WEIGHT_DMA_PRIORITY)

            wg_s[...] = wg_f[stage].astype(BF16)
            wu_s[...] = wu_f[stage].astype(BF16)
            wd_s[...] = wd_f[stage].astype(BF16)

        slot = i % 2
        _wait_row_gather(xbuf.at[slot], gsem.at[slot], tm, chunks)
        x = _load_gathered(xbuf.at[slot], tm, chunks).astype(BF16)

        down_split = min(EXPERT_DOWN_SPLIT, chunks)
        n_stages = 2 + down_split
        bounds = [tm * s // n_stages for s in range(n_stages + 1)]

        def start_next_rows(stage):
            for j in range(bounds[stage], bounds[stage + 1]):
                row_copy(next_tokens_ref, 1 - slot, j).start()

        halves = []
        for half in range(2):
            start_next_rows(half)
            cols = slice(half * ff // 2, (half + 1) * ff // 2)
            halves.append((_silu(_dot(x, wg_s[:, cols])) * _dot(x, wu_s[:, cols])).astype(BF16))
        a = jnp.concatenate(halves, axis=1)
        per_part = chunks // down_split
        for part in range(down_split):
            start_next_rows(2 + part)
            y = _dot(a, wd_s[:, part * per_part * LANES:(part + 1) * per_part * LANES])
            for c in range(per_part):
                o_ref[pl.ds(part * per_part + c, tm, stride=pitch), :] = y[:, c * LANES:(c + 1) * LANES]
        o_ref[pl.ds(chunks, tm, stride=pitch), :] = jnp.zeros((tm, LANES), o_ref.dtype)

        @pl.when(i == last)
        def _():
            _wait_row_gather(xbuf.at[1 - slot], gsem.at[1 - slot], tm, chunks)

    @pl.when(i >= n_used)
    def _():
        @pl.when((i == n_used) & (i > 0))
        def _():
            _wait_row_gather(xbuf.at[i % 2], gsem.at[i % 2], tm, chunks)

        o_ref[...] = jnp.zeros(o_ref.shape, o_ref.dtype)


def _experts(tile_meta, slot_token, hf, w_gate, w_up, w_down, n_tiles, tm):
    d, ff = w_gate.shape[-2:]
    chunks = d // LANES
    pitch = _row_pitch(chunks)
    slot_tiles = slot_token.reshape(n_tiles, 1, tm)
    any_space = pl.BlockSpec(memory_space=pl.ANY)
    ids_block = lambda ahead: pl.BlockSpec(
        (1, 1, tm), lambda i, *prefetch: (jnp.minimum(i + ahead, n_tiles - 1), 0, 0),
        memory_space=pltpu.SMEM)
    grid_spec = pltpu.PrefetchScalarGridSpec(
        num_scalar_prefetch=4,
        grid=(n_tiles,),
        in_specs=[ids_block(0), ids_block(1), any_space, any_space, any_space, any_space],
        out_specs=pl.BlockSpec((tm * pitch, LANES), lambda i, *prefetch: (i, 0)),
        scratch_shapes=[
            pltpu.VMEM((2, tm * pitch, LANES), hf.dtype),
            pltpu.SemaphoreType.DMA((2,)),
            pltpu.VMEM((2, d, ff), F32),
            pltpu.VMEM((2, d, ff), F32),
            pltpu.VMEM((2, ff, d), F32),
            pltpu.SemaphoreType.DMA((2,)),
            pltpu.VMEM((d, ff), BF16),
            pltpu.VMEM((d, ff), BF16),
            pltpu.VMEM((ff, d), BF16),
        ],
    )
    return pl.pallas_call(
        _expert_kernel,
        grid_spec=grid_spec,
        out_shape=jax.ShapeDtypeStruct((n_tiles * tm * pitch, LANES), hf.dtype),
        compiler_params=_params("arbitrary"),
        name="routed_experts",
    )(*tile_meta, slot_tiles, slot_tiles, hf, w_gate, w_up, w_down)


def _combine_kernel(pos_ref, ys_ref, x1_ref, route_ref, gf_ref, fw_ref, o_ref, ybuf, sem):
    i = pl.program_id(0)
    n_tiles = pl.num_programs(0)
    tm, d = x1_ref.shape
    chunks = d // LANES

    def start_tile(tile, slot):
        for choice in range(2):
            _start_row_gather(lambda j: _row_ref(ys_ref, pos_ref[(tile * tm + j) * 2 + choice], chunks),
                              chunks, ybuf.at[slot, choice], sem.at[slot, choice], tm)

    @pl.when(i == 0)
    def _():
        start_tile(0, 0)

    @pl.when(i + 1 < n_tiles)
    def _():
        start_tile(i + 1, (i + 1) % 2)

    slot = i % 2
    for choice in range(2):
        _wait_row_gather(ybuf.at[slot, choice], sem.at[slot, choice], tm, chunks)
    route = route_ref[...]
    moe = (route[:, R_W0:R_W0 + 1] * _load_gathered(ybuf.at[slot, 0], tm, chunks)
           + route[:, R_W1:R_W1 + 1] * _load_gathered(ybuf.at[slot, 1], tm, chunks))
    x2 = x1_ref[...] + gf_ref[0] * moe
    o_ref[...] = _rms(x2, fw_ref[...])


def _combine(pos, ys, x1, route, mod3, final_w, n, tm):
    m, d = x1.shape
    chunks = d // LANES
    grid_spec = pltpu.PrefetchScalarGridSpec(
        num_scalar_prefetch=1,
        grid=(m // tm,),
        in_specs=[
            pl.BlockSpec(memory_space=pl.ANY),
            pl.BlockSpec((tm, d), lambda i, pos_ref: (i, 0)),
            pl.BlockSpec((tm, LANES), lambda i, pos_ref: (i, 0)),
            pl.BlockSpec((1, 1, d), lambda i, pos_ref: (((i * tm) // n) * N_MOD + 5, 0, 0)),
            pl.BlockSpec((1, d), lambda i, pos_ref: (0, 0)),
        ],
        out_specs=pl.BlockSpec((tm, d), lambda i, pos_ref: (i, 0)),
        scratch_shapes=[pltpu.VMEM((2, 2, tm * _row_pitch(chunks), LANES), ys.dtype),
                        pltpu.SemaphoreType.DMA((2, 2))],
    )
    return pl.pallas_call(
        _combine_kernel,
        grid_spec=grid_spec,
        out_shape=jax.ShapeDtypeStruct((m, d), F32),
        compiler_params=_params("arbitrary"),
        name="combine_final_norm",
    )(pos, ys, x1, route, mod3, final_w.reshape(1, d))


def kernel(x, c, ctx, c_ctx, w_mod, b_mod, norm_mix_w, w_in, ret_decay_f, ret_decay_b, ret_gn_w, na_rpb, w_out, norm_ffn_w, w_router_group, b_router_group, w_router_expert, b_router_expert, w_gate, w_up, w_down, final_norm_w):
    assert w_mod.shape[0] == 1, "single trunk layer"
    batch, n, d = x.shape
    ctx_len = ctx.shape[1]
    n_groups = w_router_group.shape[-1]
    per_group = w_router_expert.shape[-1]
    n_experts = w_gate.shape[1]
    assert n_groups * per_group == n_experts and n_groups + n_experts <= LANES

    mod_rows = 8
    cc = jnp.zeros((mod_rows, d), F32).at[:batch].set(c).at[batch].set(c_ctx)
    mod = _modulation(cc, w_mod[0], b_mod[0])
    mod3 = mod.reshape(mod_rows * N_MOD, 1, d)

    w_in_b = w_in[0]
    tm = min(512, n)
    tm_in = min(1024, n)
    x2d = x.reshape(batch * n, d)
    proj = _in_projection(x2d, mod3, lambda i: (i * tm_in) // n, norm_mix_w[0], w_in_b, tm_in)
    cproj = _in_projection(ctx.reshape(batch * ctx_len, d), mod3, lambda i: batch, norm_mix_w[0],
                           w_in_b, batch * ctx_len, CTX_COL_BLOCKS)

    lg = jnp.stack([jax.nn.log_sigmoid(ret_decay_f[0].astype(F32)),
                    jax.nn.log_sigmoid(ret_decay_b[0].astype(F32))])
    ret = _retention(proj, cproj, lg, ret_gn_w[0], batch, n, ctx_len)
    na = _neighbourhood_attention(proj, cproj, na_rpb[0], batch, n, ctx_len)

    w_route = jnp.concatenate(
        [w_router_group[0], jnp.moveaxis(w_router_expert[0], 0, 1).reshape(d, n_experts)], axis=1)
    w_route = jnp.pad(w_route.astype(F32), ((0, 0), (0, LANES - n_groups - n_experts)))
    w_route_hi = w_route.astype(BF16)
    w_route = jnp.concatenate([w_route_hi, (w_route - w_route_hi.astype(F32)).astype(BF16)], axis=1)
    b_route = jnp.concatenate([b_router_group[0], b_router_expert[0].reshape(-1)])
    b_route = jnp.pad(b_route.astype(F32), (0, LANES - n_groups - n_experts)).reshape(1, LANES)
    x1, hf, route, counts = _out_projection(ret, na, w_out[0].astype(BF16), x2d, mod3, norm_ffn_w[0],
                                            w_route, b_route, n, n_groups, per_group, tm)

    te = EXPERT_TILE
    tokens = batch * n
    assert tokens < 1 << KEY_RANK_BITS and (2 * tokens) % te == 0
    n_tiles = (2 * tokens) // te + n_experts
    cnt = counts[0, n_groups:n_groups + n_experts].astype(jnp.int32)
    padded = (cnt + te - 1) // te * te
    starts = jnp.cumsum(padded) - padded
    keys = route[:, R_KEY0:R_KEY1 + 1].astype(jnp.int32).reshape(-1)
    expert = lax.shift_right_logical(keys, KEY_RANK_BITS)
    start_of = jnp.sum(jnp.where(expert[:, None] == jnp.arange(n_experts)[None, :], starts[None, :], 0), axis=1)
    pos = start_of + jnp.bitwise_and(keys, (1 << KEY_RANK_BITS) - 1)
    slot_token, *tile_meta = _slot_map(pos, cnt, n_tiles, te)

    ys = _experts(tile_meta, slot_token, hf, w_gate[0], w_up[0], w_down[0], n_tiles, te)
    out = _combine(pos, ys, x1, route, mod3, final_norm_w, n, min(512, n))
    return out.reshape(batch, n, d)
```

```python
import functools
import math

import jax
import jax.numpy as jnp
from jax import lax
from jax.experimental import pallas as pl
from jax.experimental.pallas import tpu as pltpu

F32 = jnp.float32
BF16 = jnp.bfloat16

GRID_W = 64
RET_HEADS = 4
RET_HEAD_DIM = 256
RET_WIDTH = RET_HEADS * RET_HEAD_DIM
NA_HEADS = 8
NA_HEAD_DIM = 128
NA_WIDTH = NA_HEADS * NA_HEAD_DIM
RET_CHUNK = 128
NA_ROWS = 8
NA_COLS = 16
ROPE_BASE = 10000.0
N_MOD = 6
NORM_EPS = 1e-6
ROPE_HALF = RET_HEAD_DIM // 2
RET_UNROLL = 8

INPROJ_TN = 1024
assert INPROJ_TN == RET_WIDTH == NA_WIDTH
CTX_COL_BLOCKS = (1, 2, 5, 6)
CTX_RET_K, CTX_RET_V, CTX_NA_K, CTX_NA_V = 0, 1, 2, 3

NA_QROWS = 16
NA_MASK = -1e30

LANES = 128
SUBLANES = 8
BF16_ROWS = 2 * SUBLANES
VMEM_LIMIT_BYTES = 56 * 1024 * 1024

R_KEY0, R_KEY1, R_W0, R_W1 = 0, 1, 2, 3
KEY_RANK_BITS = 16
KEY_RANK_SPAN = float(1 << KEY_RANK_BITS)

EXPERT_TILE = 256
WEIGHT_DMA_PRIORITY = 1
EXPERT_DOWN_SPLIT = 4


def _params(*sem):
    return pltpu.CompilerParams(dimension_semantics=sem, vmem_limit_bytes=VMEM_LIMIT_BYTES)


def _dot(a, b):
    return jnp.dot(a, b, preferred_element_type=F32)


def _dot_nt(a, b):
    return lax.dot_general(a, b, (((1,), (1,)), ((), ())), preferred_element_type=F32)


def _dot_tn(a, b):
    return lax.dot_general(a, b, (((0,), (0,)), ((), ())), preferred_element_type=F32)


def _rms(x, w):
    return x * lax.rsqrt(jnp.mean(x * x, axis=-1, keepdims=True) + NORM_EPS) * w


def _silu(x):
    return x * jax.nn.sigmoid(x)


def _row_pitch(chunks):
    return chunks + 1


def _store_rows(ref, val):
    n_rows, d = val.shape
    chunks = d // LANES
    pitch = _row_pitch(chunks)
    for c in range(chunks):
        ref[pl.ds(c, n_rows, stride=pitch), :] = val[:, c * LANES:(c + 1) * LANES]
    ref[pl.ds(chunks, n_rows, stride=pitch), :] = jnp.zeros((n_rows, LANES), ref.dtype)


def _row_ref(hbm, row, chunks):
    return hbm.at[pl.ds(row * _row_pitch(chunks), chunks), :]


def _start_row_gather(src_row, chunks, dst_buf, sem, n_rows):
    def body(j, carry):
        pltpu.make_async_copy(src_row(j), _row_ref(dst_buf, j, chunks), sem).start()
        return carry

    lax.fori_loop(0, n_rows, body, 0, unroll=SUBLANES)


def _wait_row_gather(dst_buf, sem, n_rows, chunks):
    view = dst_buf.at[pl.ds(0, n_rows * chunks), :]
    pltpu.make_async_copy(view, view, sem).wait()


def _load_gathered(buf, n_rows, chunks):
    return jnp.concatenate([buf[pl.ds(c, n_rows, stride=_row_pitch(chunks)), :] for c in range(chunks)],
                           axis=1)


def _mod_kernel(c_ref, w_ref, b_ref, o_ref):
    a = _silu(c_ref[...]).astype(BF16)
    o_ref[...] = _dot(a, w_ref[...].astype(BF16)) + b_ref[...]


def _modulation(cc, w_mod, b_mod):
    rows, d = cc.shape
    width = w_mod.shape[1]
    tn = next(t for t in (1024, 512, 256, LANES) if width % t == 0)
    return pl.pallas_call(
        _mod_kernel,
        grid=(width // tn,),
        in_specs=[
            pl.BlockSpec((rows, d), lambda j: (0, 0)),
            pl.BlockSpec((d, tn), lambda j: (0, j)),
            pl.BlockSpec((1, tn), lambda j: (0, j)),
        ],
        out_specs=pl.BlockSpec((rows, tn), lambda j: (0, j)),
        out_shape=jax.ShapeDtypeStruct((rows, width), F32),
        compiler_params=_params("arbitrary"),
        name="modulation",
    )(cc, w_mod, b_mod.reshape(1, width))


def _inproj_kernel(rows_per_step, x_ref, sh_ref, sc_ref, nw_ref, w_ref, o_ref, h_s):
    i = pl.program_id(0)
    j = pl.program_id(1)
    tm = o_ref.shape[0]

    def normalise(rows, slot):
        y = _rms(x_ref[rows, :], nw_ref[...])
        h_s[slot, rows, :] = (y * (1.0 + sc_ref[0]) + sh_ref[0]).astype(BF16)

    @pl.when((i == 0) & (j == 0))
    def _():
        normalise(pl.ds(0, tm), 0)

    o_ref[...] = _dot(h_s[i % 2], w_ref[...].astype(BF16)).astype(BF16)
    start = jnp.clip((j - 1) * rows_per_step, 0, tm - rows_per_step)
    normalise(pl.ds(pl.multiple_of(start, BF16_ROWS), rows_per_step), (i + 1) % 2)


def _in_projection(x2d, mod3, mod_row_of_tile, norm_w, w_in, tm, col_blocks=None):
    m, d = x2d.shape
    tn = INPROJ_TN
    if col_blocks is None:
        col_blocks = tuple(range(w_in.shape[1] // tn))
    n_blocks = len(col_blocks)
    n_tiles = m // tm
    assert n_blocks >= 2 and tm % BF16_ROWS == 0
    rows_per_step = -(-tm // ((n_blocks - 1) * BF16_ROWS)) * BF16_ROWS
    w_block = lambda j: sum(jnp.where(j == k, blk, 0) for k, blk in enumerate(col_blocks))
    src = lambda i, j: jnp.where((i == 0) & (j == 0), 0, jnp.minimum(i + 1, n_tiles - 1))
    return pl.pallas_call(
        functools.partial(_inproj_kernel, rows_per_step),
        grid=(n_tiles, n_blocks),
        in_specs=[
            pl.BlockSpec((tm, d), lambda i, j: (src(i, j), 0)),
            pl.BlockSpec((1, 1, d), lambda i, j: (mod_row_of_tile(src(i, j)) * N_MOD + 0, 0, 0)),
            pl.BlockSpec((1, 1, d), lambda i, j: (mod_row_of_tile(src(i, j)) * N_MOD + 1, 0, 0)),
            pl.BlockSpec((1, d), lambda i, j: (0, 0)),
            pl.BlockSpec((d, tn), lambda i, j: (0, w_block(j))),
        ],
        out_specs=pl.BlockSpec((tm, tn), lambda i, j: (i, j)),
        out_shape=jax.ShapeDtypeStruct((m, n_blocks * tn), BF16),
        scratch_shapes=[pltpu.VMEM((2, tm, d), BF16)],
        compiler_params=_params("arbitrary", "arbitrary"),
        name="in_projection",
    )(x2d, mod3, mod3, norm_w.reshape(1, d), w_in)


def _ret_kernel(lg_ref, q_ref, k_ref, v_ref, g_ref, ck_ref, cv_ref, cosr_ref, sinr_ref,
                cosc_ref, sinc_ref, gnw_ref, o_ref, qr_s, kr_s, sfh_s, sbh_s, sf_s, sb_s):
    head = pl.program_id(1)
    lgf = lg_ref[0, head]
    lgb = lg_ref[1, head]
    n = q_ref.shape[0]
    c = RET_CHUNK
    nc = n // c
    ctx_len = ck_ref.shape[0]
    k_scale = RET_HEAD_DIM ** -0.5

    posl = lax.broadcasted_iota(jnp.int32, (ctx_len, 1), 0).astype(F32)
    ck = ck_ref[...].astype(F32) * k_scale
    cv = cv_ref[...]
    sf_s[...] = _dot_tn((ck * jnp.exp(lgf * (ctx_len - 1.0 - posl))).astype(BF16), cv)
    sb_s[...] = _dot_tn((ck * jnp.exp(lgb * posl)).astype(BF16), cv)

    cosc = cosc_ref[...]
    sinc = sinc_ref[...]
    pos = lax.broadcasted_iota(jnp.int32, (c, 1), 0).astype(F32)
    qdec_f = jnp.exp(lgf * (pos + 1.0))
    kdec_f = jnp.exp(lgf * (c - 1.0 - pos))
    cdec_f = jnp.exp(lgf * c)
    qdec_b = jnp.exp(lgb * (c - pos))
    kdec_b = jnp.exp(lgb * pos)
    cdec_b = jnp.exp(lgb * c)

    def rope(x, cosr, sinr):
        xa = x[:, :ROPE_HALF]
        xb = x[:, ROPE_HALF:]
        ya = xa * cosr + pltpu.roll(xa, ROPE_HALF // 2, 1) * sinr
        yb = xb * cosc + pltpu.roll(xb, ROPE_HALF // 2, 1) * sinc
        return jnp.concatenate([ya, yb], axis=1)

    def fwd_chunk(ci, carry):
        r0 = pl.multiple_of(ci * c, c)
        cosr = cosr_ref[pl.ds(r0, c), :]
        sinr = sinr_ref[pl.ds(r0, c), :]
        qr_s[pl.ds(r0, c), :] = rope(q_ref[pl.ds(r0, c), :].astype(F32), cosr, sinr).astype(BF16)
        k = rope(k_ref[pl.ds(r0, c), :].astype(F32), cosr, sinr) * k_scale
        kr_s[pl.ds(r0, c), :] = k.astype(BF16)
        kv = _dot_tn((k * kdec_f).astype(BF16), v_ref[pl.ds(r0, c), :])
        state = sf_s[...]
        sfh_s[ci] = state.astype(BF16)
        sf_s[...] = state * cdec_f + kv
        return carry

    lax.fori_loop(0, nc, fwd_chunk, 0, unroll=RET_UNROLL)

    def bwd_chunk(i, carry):
        ci = nc - 1 - i
        r0 = pl.multiple_of(ci * c, c)
        k = kr_s[pl.ds(r0, c), :].astype(F32)
        kv = _dot_tn((k * kdec_b).astype(BF16), v_ref[pl.ds(r0, c), :])
        state = sb_s[...]
        sbh_s[ci] = state.astype(BF16)
        sb_s[...] = state * cdec_b + kv
        return carry

    lax.fori_loop(0, nc, bwd_chunk, 0, unroll=RET_UNROLL)

    diff = (lax.broadcasted_iota(jnp.int32, (c, c), 0)
            - lax.broadcasted_iota(jnp.int32, (c, c), 1)).astype(F32)
    intra = (jnp.where(diff >= 0, jnp.exp(lgf * jnp.maximum(diff, 0.0)), 0.0)
             + jnp.where(diff <= 0, jnp.exp(lgb * jnp.maximum(-diff, 0.0)), 0.0))
    gnw = gnw_ref[...]

    def out_chunk(ci, carry):
        r0 = pl.multiple_of(ci * c, c)
        qb = qr_s[pl.ds(r0, c), :]
        kb = kr_s[pl.ds(r0, c), :]
        q = qb.astype(F32)
        scores = _dot_nt(qb, kb) * intra
        o = (_dot(scores.astype(BF16), v_ref[pl.ds(r0, c), :])
             + _dot((q * qdec_f).astype(BF16), sfh_s[ci])
             + _dot((q * qdec_b).astype(BF16), sbh_s[ci]))
        mu = jnp.mean(o, axis=-1, keepdims=True)
        d = o - mu
        var = jnp.mean(d * d, axis=-1, keepdims=True)
        on = d * lax.rsqrt(var + NORM_EPS) * gnw
        gate = _silu(g_ref[pl.ds(r0, c), :].astype(F32))
        o_ref[pl.ds(r0, c), :] = (on * gate).astype(BF16)
        return carry

    lax.fori_loop(0, nc, out_chunk, 0, unroll=RET_UNROLL)


def _rope_tables(n):
    inv = ROPE_BASE ** (-jnp.arange(0, ROPE_HALF, 2, dtype=F32) / ROPE_HALF)

    def tables(pos):
        ang = pos.astype(F32)[:, None] * inv[None, :]
        cos = jnp.cos(ang)
        sin = jnp.sin(ang)
        return jnp.concatenate([cos, cos], axis=1), jnp.concatenate([-sin, sin], axis=1)

    t = jnp.arange(n)
    cosr, sinr = tables(t // GRID_W)
    cosc, sinc = tables(jnp.arange(RET_CHUNK) % GRID_W)
    return cosr, sinr, cosc, sinc


def _retention(proj, cproj, lg, gn_w, batch, n, ctx_len):
    hd = RET_HEAD_DIM
    cosr, sinr, cosc, sinc = _rope_tables(n)
    col = lambda which: (lambda b, h, lg_ref: (b, which * RET_HEADS + h))
    const = lambda b, h, lg_ref: (0, 0)
    grid_spec = pltpu.PrefetchScalarGridSpec(
        num_scalar_prefetch=1,
        grid=(batch, RET_HEADS),
        in_specs=[
            pl.BlockSpec((n, hd), col(0)),
            pl.BlockSpec((n, hd), col(1)),
            pl.BlockSpec((n, hd), col(2)),
            pl.BlockSpec((n, hd), col(3)),
            pl.BlockSpec((ctx_len, hd), lambda b, h, lg_ref: (b, CTX_RET_K * RET_HEADS + h)),
            pl.BlockSpec((ctx_len, hd), lambda b, h, lg_ref: (b, CTX_RET_V * RET_HEADS + h)),
            pl.BlockSpec((n, ROPE_HALF), const),
            pl.BlockSpec((n, ROPE_HALF), const),
            pl.BlockSpec((RET_CHUNK, ROPE_HALF), const),
            pl.BlockSpec((RET_CHUNK, ROPE_HALF), const),
            pl.BlockSpec((1, hd), lambda b, h, lg_ref: (0, h)),
        ],
        out_specs=pl.BlockSpec((n, hd), lambda b, h, lg_ref: (b, h)),
        scratch_shapes=[
            pltpu.VMEM((n, hd), BF16),
            pltpu.VMEM((n, hd), BF16),
            pltpu.VMEM((n // RET_CHUNK, hd, hd), BF16),
            pltpu.VMEM((n // RET_CHUNK, hd, hd), BF16),
            pltpu.VMEM((hd, hd), F32),
            pltpu.VMEM((hd, hd), F32),
        ],
    )
    return pl.pallas_call(
        _ret_kernel,
        grid_spec=grid_spec,
        out_shape=jax.ShapeDtypeStruct((batch * n, RET_WIDTH), BF16),
        compiler_params=_params("parallel", "arbitrary"),
        name="retention",
    )(lg, proj, proj, proj, proj, cproj, cproj, cosr, sinr, cosc, sinc, gn_w.reshape(1, RET_WIDTH))


def _na_build_bias(rpb_ref, head, pair_s):
    n_roff, n_coff = 2 * NA_ROWS - 1, 2 * NA_COLS - 1
    c = lax.broadcasted_iota(jnp.int32, (GRID_W, LANES), 0)
    lane = lax.broadcasted_iota(jnp.int32, (GRID_W, LANES), 1)
    kc = jnp.bitwise_and(lane, GRID_W - 1)
    cs = jnp.clip(c - NA_COLS // 2, 0, GRID_W - NA_COLS)
    d = jnp.where((kc >= cs) & (kc < cs + NA_COLS), kc - c + (NA_COLS - 1), -1)

    def table(i):
        acc = jnp.full((GRID_W, LANES), NA_MASK, F32)
        base = (head * n_roff + i) * n_coff
        for j in range(n_coff):
            acc = jnp.where(d == j, rpb_ref[base + j], acc)
        return acc

    prev = table(0)
    for i in range(1, n_roff):
        cur = table(i)
        pair_s[i - 1] = jnp.where(lane < GRID_W, prev, cur)
        prev = cur


def _na_kernel(rpb_ref, q_ref, k_ref, v_ref, kc_ref, vc_ref, o_ref, s_s, m_s, pair_s):
    @pl.when(pl.program_id(1) == 0)
    def _():
        _na_build_bias(rpb_ref, pl.program_id(0), pair_s)

    n = q_ref.shape[0]
    rows = n // GRID_W
    kb = NA_ROWS * GRID_W
    ctx_len = kc_ref.shape[0]
    scale = NA_HEAD_DIM ** -0.5
    kc = kc_ref[...]
    vc = jnp.concatenate([vc_ref[...], jnp.ones((ctx_len, NA_HEAD_DIM), BF16)], axis=1)
    ones_w = jnp.ones((kb, NA_HEAD_DIM), BF16)

    def window(r):
        rs = jnp.clip(r - NA_ROWS // 2, 0, rows - NA_ROWS)
        return rs, pl.multiple_of(r * GRID_W, GRID_W), pl.multiple_of(rs * GRID_W, GRID_W)

    def score_row(r):
        rs, q0, k0 = window(r)
        q = (q_ref[pl.ds(q0, GRID_W), :].astype(F32) * scale).astype(BF16)
        off = rs - r + NA_ROWS - 1
        bias = jnp.concatenate([pair_s[off + 2 * t] for t in range(NA_ROWS // 2)], axis=1)
        s_win = _dot_nt(q, k_ref[pl.ds(k0, kb), :]) + bias
        s_ctx = _dot_nt(q, kc)
        m = jnp.maximum(jnp.max(s_win, axis=-1, keepdims=True), jnp.max(s_ctx, axis=-1, keepdims=True))
        s_s[pl.ds(q0, GRID_W), :kb] = s_win
        s_s[pl.ds(q0, GRID_W), kb:] = s_ctx
        m_s[pl.ds(q0, GRID_W), :] = jnp.broadcast_to(m, (GRID_W, LANES))

    def value_row(r):
        _, q0, k0 = window(r)
        m = m_s[pl.ds(q0, GRID_W), :]
        p_win = jnp.exp(s_s[pl.ds(q0, GRID_W), :kb] - jnp.tile(m, (1, kb // LANES))).astype(BF16)
        p_ctx = jnp.exp(s_s[pl.ds(q0, GRID_W), kb:] - jnp.tile(m, (1, ctx_len // LANES))).astype(BF16)
        vw = jnp.concatenate([v_ref[pl.ds(k0, kb), :], ones_w], axis=1)
        o = _dot(p_win, vw) + _dot(p_ctx, vc)
        o_ref[pl.ds(q0, GRID_W), :] = (o[:, :NA_HEAD_DIM] / o[:, NA_HEAD_DIM:]).astype(BF16)

    qrows = math.gcd(rows, NA_QROWS)

    def sweep(row_fn):
        def group(g, carry):
            for u in range(qrows):
                row_fn(g * qrows + u)
            return carry
        lax.fori_loop(0, rows // qrows, group, 0)

    sweep(score_row)
    sweep(value_row)


def _neighbourhood_attention(proj, cproj, rpb, batch, n, ctx_len):
    hd = NA_HEAD_DIM
    rows = n // GRID_W
    assert rows >= NA_ROWS
    assert rpb.shape == (NA_HEADS, 2 * NA_ROWS - 1, 2 * NA_COLS - 1)
    base = 4 * RET_WIDTH // hd
    col = lambda which: (lambda h, b, rpb_ref: (b, base + which * NA_HEADS + h))
    grid_spec = pltpu.PrefetchScalarGridSpec(
        num_scalar_prefetch=1,
        grid=(NA_HEADS, batch),
        in_specs=[
            pl.BlockSpec((n, hd), col(0)),
            pl.BlockSpec((n, hd), col(1)),
            pl.BlockSpec((n, hd), col(2)),
            pl.BlockSpec((ctx_len, hd), lambda h, b, rpb_ref: (b, CTX_NA_K * NA_HEADS + h)),
            pl.BlockSpec((ctx_len, hd), lambda h, b, rpb_ref: (b, CTX_NA_V * NA_HEADS + h)),
        ],
        out_specs=pl.BlockSpec((n, hd), lambda h, b, rpb_ref: (b, h)),
        scratch_shapes=[
            pltpu.VMEM((n, NA_ROWS * GRID_W + ctx_len), F32),
            pltpu.VMEM((n, LANES), F32),
            pltpu.VMEM((2 * NA_ROWS - 2, GRID_W, LANES), F32),
        ],
    )
    return pl.pallas_call(
        _na_kernel,
        grid_spec=grid_spec,
        out_shape=jax.ShapeDtypeStruct((batch * n, NA_WIDTH), BF16),
        compiler_params=_params("parallel", "arbitrary"),
        name="neighbourhood_attention",
    )(rpb.astype(F32).reshape(-1), proj, proj, proj, cproj, cproj)


def _outproj_kernel(n_groups, per_group, ret_ref, na_ref, w1_ref, w2_ref, x_ref, ga_ref, shf_ref,
                    scf_ref, nw_ref, wr_ref, br_ref, x1_ref, hf_ref, route_ref, cnt_ref, carry_s,
                    earlier_s):
    tm = x_ref.shape[0]

    @pl.when(pl.program_id(0) == 0)
    def _():
        carry_s[...] = jnp.zeros_like(carry_s)
        earlier_s[...] = jnp.where(lax.broadcasted_iota(jnp.int32, (tm, tm), 0)
                                   > lax.broadcasted_iota(jnp.int32, (tm, tm), 1), 1.0, 0.0).astype(BF16)

    acc = _dot(ret_ref[...], w1_ref[...]) + _dot(na_ref[...], w2_ref[...])
    x1 = x_ref[...] + ga_ref[0] * acc
    x1_ref[...] = x1
    hf = _rms(x1, nw_ref[...]) * (1.0 + scf_ref[0]) + shf_ref[0]
    _store_rows(hf_ref, hf)

    hf_hi = hf.astype(BF16)
    hf_lo = (hf - hf_hi.astype(F32)).astype(BF16)
    p_hi = _dot(hf_hi, wr_ref[...])
    p_lo = _dot(hf_lo, wr_ref[...])
    logits = p_hi[:, :LANES] + p_hi[:, LANES:] + p_lo[:, :LANES] + br_ref[...]
    lane = lax.broadcasted_iota(jnp.int32, (tm, LANES), 1)
    neg = -jnp.inf

    def first_max(vals):
        top = jnp.max(vals, axis=-1, keepdims=True)
        idx = jnp.min(jnp.where(vals == top, lane, LANES), axis=-1, keepdims=True)
        return top, idx

    g_logits = jnp.where(lane < n_groups, logits, neg)
    g_top, g_sel = first_max(g_logits)
    g_w = 1.0 / jnp.sum(jnp.exp(g_logits - g_top), axis=-1, keepdims=True)
    lo = n_groups + g_sel * per_group
    e_logits = jnp.where((lane >= lo) & (lane < lo + per_group), logits, neg)
    v0, i0 = first_max(e_logits)
    v1, i1 = first_max(jnp.where(lane == i0, neg, e_logits))
    e1 = jnp.exp(v1 - v0)
    w0 = g_w / (1.0 + e1)
    w1 = g_w * e1 / (1.0 + e1)

    hit0 = lane == i0
    hit1 = lane == i1
    onehot = jnp.where(hit0 | hit1, 1.0, 0.0)
    before = _dot(earlier_s[...], onehot.astype(BF16)) + carry_s[...]
    rank0 = jnp.sum(jnp.where(hit0, before, 0.0), axis=-1, keepdims=True)
    rank1 = jnp.sum(jnp.where(hit1, before, 0.0), axis=-1, keepdims=True)
    carry_s[...] = carry_s[...] + jnp.sum(onehot, axis=0, keepdims=True)
    cnt_ref[...] = carry_s[...]

    key0 = (i0 - n_groups).astype(F32) * KEY_RANK_SPAN + rank0
    key1 = (i1 - n_groups).astype(F32) * KEY_RANK_SPAN + rank1
    fields = {R_KEY0: key0, R_KEY1: key1, R_W0: w0, R_W1: w1}
    route = jnp.zeros((tm, LANES), F32)
    for idx, val in fields.items():
        route = jnp.where(lane == idx, val, route)
    route_ref[...] = route


def _out_projection(ret, na, w_out, x2d, mod3, norm_w, w_route, b_route, n, n_groups, per_group, tm):
    m, d = x2d.shape
    pitch = _row_pitch(d // LANES)
    batch_of = lambda i: (i * tm) // n
    mod_spec = lambda which: pl.BlockSpec((1, 1, d), lambda i: (batch_of(i) * N_MOD + which, 0, 0))
    const2 = lambda i: (0, 0)
    return pl.pallas_call(
        functools.partial(_outproj_kernel, n_groups, per_group),
        grid=(m // tm,),
        in_specs=[
            pl.BlockSpec((tm, RET_WIDTH), lambda i: (i, 0)),
            pl.BlockSpec((tm, NA_WIDTH), lambda i: (i, 0)),
            pl.BlockSpec((RET_WIDTH, d), lambda i: (0, 0)),
            pl.BlockSpec((NA_WIDTH, d), lambda i: (RET_WIDTH // NA_WIDTH, 0)),
            pl.BlockSpec((tm, d), lambda i: (i, 0)),
            mod_spec(2), mod_spec(3), mod_spec(4),
            pl.BlockSpec((1, d), const2),
            pl.BlockSpec((d, 2 * LANES), const2),
            pl.BlockSpec((1, LANES), const2),
        ],
        out_specs=[
            pl.BlockSpec((tm, d), lambda i: (i, 0)),
            pl.BlockSpec((tm * pitch, LANES), lambda i: (i, 0)),
            pl.BlockSpec((tm, LANES), lambda i: (i, 0)),
            pl.BlockSpec((1, LANES), const2),
        ],
        out_shape=[
            jax.ShapeDtypeStruct((m, d), F32),
            jax.ShapeDtypeStruct((m * pitch, LANES), F32),
            jax.ShapeDtypeStruct((m, LANES), F32),
            jax.ShapeDtypeStruct((1, LANES), F32),
        ],
        scratch_shapes=[pltpu.VMEM((1, LANES), F32), pltpu.VMEM((tm, tm), BF16)],
        compiler_params=_params("arbitrary"),
        name="out_projection_router",
    )(ret, na, w_out, w_out, x2d, mod3, mod3, mod3, norm_w.reshape(1, d), w_route, b_route)


def _slot_map_kernel(tile_rows, pos_ref, cnt_ref, slot_ref, tile_expert_ref, next_expert_ref,
                     run_parity_ref, n_used_ref, offs_s):
    n_experts = cnt_ref.shape[0]
    n_tiles = tile_expert_ref.shape[0]
    shift = tile_rows.bit_length() - 1
    assert 1 << shift == tile_rows

    def fill(ref, lo, hi, val):
        value_at = val if callable(val) else (lambda s: val)
        groups = lax.div(hi - lo, SUBLANES)

        def group(g, carry):
            for k in range(SUBLANES):
                s = lo + g * SUBLANES + k
                ref[s] = value_at(s)
            return carry

        def single(s, carry):
            ref[s] = value_at(s)
            return carry

        lax.fori_loop(0, groups, group, 0)
        lax.fori_loop(lo + groups * SUBLANES, hi, single, 0)

    assert tile_rows <= pos_ref.shape[0] // 2
    padding_row = lambda s: jnp.bitwise_and(s, tile_rows - 1)

    def tiles_of(e):
        size = lax.shift_left(lax.shift_right_logical(cnt_ref[e] + (tile_rows - 1), shift), shift)
        return lax.shift_right_logical(offs_s[e], shift), lax.shift_right_logical(offs_s[e] + size, shift)

    def per_expert(e, carry):
        start, runs = carry
        cnt = cnt_ref[e]
        size = lax.shift_left(lax.shift_right_logical(cnt + (tile_rows - 1), shift), shift)
        offs_s[e] = start
        t0, t1 = tiles_of(e)
        fill(tile_expert_ref, t0, t1, e)
        fill(run_parity_ref, t0, t1, jnp.bitwise_and(runs, 1))
        fill(slot_ref, start + cnt, start + size, padding_row)
        return start + size, runs + (size > 0).astype(jnp.int32)

    end, _ = lax.fori_loop(0, n_experts, per_expert, (0, 0))
    n_used = lax.shift_right_logical(end, shift)
    n_used_ref[0] = n_used
    fill(tile_expert_ref, n_used, n_tiles, n_experts - 1)
    fill(run_parity_ref, n_used, n_tiles, 0)
    fill(next_expert_ref, n_used, n_tiles, -1)
    fill(slot_ref, end, slot_ref.shape[0], padding_row)

    def per_expert_reversed(k, following):
        e = n_experts - 1 - k
        t0, t1 = tiles_of(e)
        fill(next_expert_ref, t0, t1, following)
        return jnp.where(t1 > t0, e, following)

    lax.fori_loop(0, n_experts, per_expert_reversed, -1)

    def assign(token, carry):
        slot_ref[pos_ref[2 * token]] = token
        slot_ref[pos_ref[2 * token + 1]] = token
        return carry

    lax.fori_loop(0, pos_ref.shape[0] // 2, assign, 0, unroll=SUBLANES)


def _slot_map(pos, cnt, n_tiles, tile_rows):
    smem = pl.BlockSpec(memory_space=pltpu.SMEM)
    i32 = lambda *shape: jax.ShapeDtypeStruct(shape, jnp.int32)
    return pl.pallas_call(
        functools.partial(_slot_map_kernel, tile_rows),
        in_specs=[smem, smem],
        out_specs=[smem] * 5,
        out_shape=[i32(n_tiles * tile_rows), i32(n_tiles), i32(n_tiles), i32(n_tiles), i32(1)],
        scratch_shapes=[pltpu.SMEM((cnt.shape[0],), jnp.int32)],
        name="slot_map",
    )(pos, cnt)


def _expert_kernel(tile_expert_ref, next_expert_ref, run_parity_ref, n_used_ref, tokens_ref,
                   next_tokens_ref, hf_ref, wg_hbm, wu_hbm, wd_hbm, o_ref, xbuf, gsem, wg_f, wu_f,
                   wd_f, wsem, wg_s, wu_s, wd_s):
    i = pl.program_id(0)
    last = pl.num_programs(0) - 1
    n_used = n_used_ref[0]
    d, ff = wg_s.shape
    chunks = d // LANES
    pitch = _row_pitch(chunks)
    tm = o_ref.shape[0] // pitch

    def row_copy(ids_ref, slot, j):
        return pltpu.make_async_copy(_row_ref(hf_ref, ids_ref[0, 0, j], chunks),
                                     _row_ref(xbuf.at[slot], j, chunks), gsem.at[slot])

    def weight_copies(expert, slot):
        return [pltpu.make_async_copy(hbm.at[expert], stage.at[slot], wsem.at[slot])
                for hbm, stage in ((wg_hbm, wg_f), (wu_hbm, wu_f), (wd_hbm, wd_f))]

    @pl.when((i == 0) & (n_used > 0))
    def _():
        _start_row_gather(lambda j: _row_ref(hf_ref, tokens_ref[0, 0, j], chunks), chunks, xbuf.at[0],
                          gsem.at[0], tm)
        for cp in weight_copies(tile_expert_ref[0], run_parity_ref[0]):
            cp.start(priority=WEIGHT_DMA_PRIORITY)

    @pl.when(i < n_used)
    def _():
        @pl.when((i == 0) | (tile_expert_ref[i] != tile_expert_ref[jnp.maximum(i - 1, 0)]))
        def _():
            stage = run_parity_ref[i]
            for cp in weight_copies(tile_expert_ref[i], stage):
                cp.wait()

            @pl.when(next_expert_ref[i] >= 0)
            def _():
                for cp in weight_copies(next_expert_ref[i], 1 - stage):
                    cp.start(priority=WEIGHT_DMA_PRIORITY)

            wg_s[...] = wg_f[stage].astype(BF16)
            wu_s[...] = wu_f[stage].astype(BF16)
            wd_s[...] = wd_f[stage].astype(BF16)

        slot = i % 2
        _wait_row_gather(xbuf.at[slot], gsem.at[slot], tm, chunks)
        x = _load_gathered(xbuf.at[slot], tm, chunks).astype(BF16)

        down_split = min(EXPERT_DOWN_SPLIT, chunks)
        n_stages = 2 + down_split
        bounds = [tm * s // n_stages for s in range(n_stages + 1)]

        def start_next_rows(stage):
            for j in range(bounds[stage], bounds[stage + 1]):
                row_copy(next_tokens_ref, 1 - slot, j).start()

        halves = []
        for half in range(2):
            start_next_rows(half)
            cols = slice(half * ff // 2, (half + 1) * ff // 2)
            halves.append((_silu(_dot(x, wg_s[:, cols])) * _dot(x, wu_s[:, cols])).astype(BF16))
        a = jnp.concatenate(halves, axis=1)
        per_part = chunks // down_split
        for part in range(down_split):
            start_next_rows(2 + part)
            y = _dot(a, wd_s[:, part * per_part * LANES:(part + 1) * per_part * LANES])
            for c in range(per_part):
                o_ref[pl.ds(part * per_part + c, tm, stride=pitch), :] = y[:, c * LANES:(c + 1) * LANES]
        o_ref[pl.ds(chunks, tm, stride=pitch), :] = jnp.zeros((tm, LANES), o_ref.dtype)

        @pl.when(i == last)
        def _():
            _wait_row_gather(xbuf.at[1 - slot], gsem.at[1 - slot], tm, chunks)

    @pl.when(i >= n_used)
    def _():
        @pl.when((i == n_used) & (i > 0))
        def _():
            _wait_row_gather(xbuf.at[i % 2], gsem.at[i % 2], tm, chunks)

        o_ref[...] = jnp.zeros(o_ref.shape, o_ref.dtype)


def _experts(tile_meta, slot_token, hf, w_gate, w_up, w_down, n_tiles, tm):
    d, ff = w_gate.shape[-2:]
    chunks = d // LANES
    pitch = _row_pitch(chunks)
    slot_tiles = slot_token.reshape(n_tiles, 1, tm)
    any_space = pl.BlockSpec(memory_space=pl.ANY)
    ids_block = lambda ahead: pl.BlockSpec(
        (1, 1, tm), lambda i, *prefetch: (jnp.minimum(i + ahead, n_tiles - 1), 0, 0),
        memory_space=pltpu.SMEM)
    grid_spec = pltpu.PrefetchScalarGridSpec(
        num_scalar_prefetch=4,
        grid=(n_tiles,),
        in_specs=[ids_block(0), ids_block(1), any_space, any_space, any_space, any_space],
        out_specs=pl.BlockSpec((tm * pitch, LANES), lambda i, *prefetch: (i, 0)),
        scratch_shapes=[
            pltpu.VMEM((2, tm * pitch, LANES), hf.dtype),
            pltpu.SemaphoreType.DMA((2,)),
            pltpu.VMEM((2, d, ff), F32),
            pltpu.VMEM((2, d, ff), F32),
            pltpu.VMEM((2, ff, d), F32),
            pltpu.SemaphoreType.DMA((2,)),
            pltpu.VMEM((d, ff), BF16),
            pltpu.VMEM((d, ff), BF16),
            pltpu.VMEM((ff, d), BF16),
        ],
    )
    return pl.pallas_call(
        _expert_kernel,
        grid_spec=grid_spec,
        out_shape=jax.ShapeDtypeStruct((n_tiles * tm * pitch, LANES), hf.dtype),
        compiler_params=_params("arbitrary"),
        name="routed_experts",
    )(*tile_meta, slot_tiles, slot_tiles, hf, w_gate, w_up, w_down)


def _combine_kernel(pos_ref, ys_ref, x1_ref, route_ref, gf_ref, fw_ref, o_ref, ybuf, sem):
    i = pl.program_id(0)
    n_tiles = pl.num_programs(0)
    tm, d = x1_ref.shape
    chunks = d // LANES

    def start_tile(tile, slot):
        for choice in range(2):
            _start_row_gather(lambda j: _row_ref(ys_ref, pos_ref[(tile * tm + j) * 2 + choice], chunks),
                              chunks, ybuf.at[slot, choice], sem.at[slot, choice], tm)

    @pl.when(i == 0)
    def _():
        start_tile(0, 0)

    @pl.when(i + 1 < n_tiles)
    def _():
        start_tile(i + 1, (i + 1) % 2)

    slot = i % 2
    for choice in range(2):
        _wait_row_gather(ybuf.at[slot, choice], sem.at[slot, choice], tm, chunks)
    route = route_ref[...]
    moe = (route[:, R_W0:R_W0 + 1] * _load_gathered(ybuf.at[slot, 0], tm, chunks)
           + route[:, R_W1:R_W1 + 1] * _load_gathered(ybuf.at[slot, 1], tm, chunks))
    x2 = x1_ref[...] + gf_ref[0] * moe
    o_ref[...] = _rms(x2, fw_ref[...])


def _combine(pos, ys, x1, route, mod3, final_w, n, tm):
    m, d = x1.shape
    chunks = d // LANES
    grid_spec = pltpu.PrefetchScalarGridSpec(
        num_scalar_prefetch=1,
        grid=(m // tm,),
        in_specs=[
            pl.BlockSpec(memory_space=pl.ANY),
            pl.BlockSpec((tm, d), lambda i, pos_ref: (i, 0)),
            pl.BlockSpec((tm, LANES), lambda i, pos_ref: (i, 0)),
            pl.BlockSpec((1, 1, d), lambda i, pos_ref: (((i * tm) // n) * N_MOD + 5, 0, 0)),
            pl.BlockSpec((1, d), lambda i, pos_ref: (0, 0)),
        ],
        out_specs=pl.BlockSpec((tm, d), lambda i, pos_ref: (i, 0)),
        scratch_shapes=[pltpu.VMEM((2, 2, tm * _row_pitch(chunks), LANES), ys.dtype),
                        pltpu.SemaphoreType.DMA((2, 2))],
    )
    return pl.pallas_call(
        _combine_kernel,
        grid_spec=grid_spec,
        out_shape=jax.ShapeDtypeStruct((m, d), F32),
        compiler_params=_params("arbitrary"),
        name="combine_final_norm",
    )(pos, ys, x1, route, mod3, final_w.reshape(1, d))


def kernel(x, c, ctx, c_ctx, w_mod, b_mod, norm_mix_w, w_in, ret_decay_f, ret_decay_b, ret_gn_w, na_rpb, w_out, norm_ffn_w, w_router_group, b_router_group, w_router_expert, b_router_expert, w_gate, w_up, w_down, final_norm_w):
    assert w_mod.shape[0] == 1, "single trunk layer"
    batch, n, d = x.shape
    ctx_len = ctx.shape[1]
    n_groups = w_router_group.shape[-1]
    per_group = w_router_expert.shape[-1]
    n_experts = w_gate.shape[1]
    assert n_groups * per_group == n_experts and n_groups + n_experts <= LANES

    mod_rows = 8
    cc = jnp.zeros((mod_rows, d), F32).at[:batch].set(c).at[batch].set(c_ctx)
    mod = _modulation(cc, w_mod[0], b_mod[0])
    mod3 = mod.reshape(mod_rows * N_MOD, 1, d)

    w_in_b = w_in[0]
    tm = min(512, n)
    tm_in = min(1024, n)
    x2d = x.reshape(batch * n, d)
    proj = _in_projection(x2d, mod3, lambda i: (i * tm_in) // n, norm_mix_w[0], w_in_b, tm_in)
    cproj = _in_projection(ctx.reshape(batch * ctx_len, d), mod3, lambda i: batch, norm_mix_w[0],
                           w_in_b, batch * ctx_len, CTX_COL_BLOCKS)

    lg = jnp.stack([jax.nn.log_sigmoid(ret_decay_f[0].astype(F32)),
                    jax.nn.log_sigmoid(ret_decay_b[0].astype(F32))])
    ret = _retention(proj, cproj, lg, ret_gn_w[0], batch, n, ctx_len)
    na = _neighbourhood_attention(proj, cproj, na_rpb[0], batch, n, ctx_len)

    w_route = jnp.concatenate(
        [w_router_group[0], jnp.moveaxis(w_router_expert[0], 0, 1).reshape(d, n_experts)], axis=1)
    w_route = jnp.pad(w_route.astype(F32), ((0, 0), (0, LANES - n_groups - n_experts)))
    w_route_hi = w_route.astype(BF16)
    w_route = jnp.concatenate([w_route_hi, (w_route - w_route_hi.astype(F32)).astype(BF16)], axis=1)
    b_route = jnp.concatenate([b_router_group[0], b_router_expert[0].reshape(-1)])
    b_route = jnp.pad(b_route.astype(F32), (0, LANES - n_groups - n_experts)).reshape(1, LANES)
    x1, hf, route, counts = _out_projection(ret, na, w_out[0].astype(BF16), x2d, mod3, norm_ffn_w[0],
                                            w_route, b_route, n, n_groups, per_group, tm)

    te = EXPERT_TILE
    tokens = batch * n
    assert tokens < 1 << KEY_RANK_BITS and (2 * tokens) % te == 0
    n_tiles = (2 * tokens) // te + n_experts
    cnt = counts[0, n_groups:n_groups + n_experts].astype(jnp.int32)
    padded = (cnt + te - 1) // te * te
    starts = jnp.cumsum(padded) - padded
    keys = route[:, R_KEY0:R_KEY1 + 1].astype(jnp.int32).reshape(-1)
    expert = lax.shift_right_logical(keys, KEY_RANK_BITS)
    start_of = jnp.sum(jnp.where(expert[:, None] == jnp.arange(n_experts)[None, :], starts[None, :], 0), axis=1)
    pos = start_of + jnp.bitwise_and(keys, (1 << KEY_RANK_BITS) - 1)
    slot_token, *tile_meta = _slot_map(pos, cnt, n_tiles, te)

    ys = _experts(tile_meta, slot_token, hf, w_gate[0], w_up[0], w_down[0], n_tiles, te)
    out = _combine(pos, ys, x1, route, mod3, final_norm_w, n, min(128, n))
    return out.reshape(batch, n, d)
```

```python
import functools
import math

import jax
import jax.numpy as jnp
from jax import lax
from jax.experimental import pallas as pl
from jax.experimental.pallas import tpu as pltpu

F32 = jnp.float32
BF16 = jnp.bfloat16

GRID_W = 64
RET_HEADS = 4
RET_HEAD_DIM = 256
RET_WIDTH = RET_HEADS * RET_HEAD_DIM
NA_HEADS = 8
NA_HEAD_DIM = 128
NA_WIDTH = NA_HEADS * NA_HEAD_DIM
RET_CHUNK = 128
NA_ROWS = 8
NA_COLS = 16
ROPE_BASE = 10000.0
N_MOD = 6
NORM_EPS = 1e-6
ROPE_HALF = RET_HEAD_DIM // 2
RET_UNROLL = 8

INPROJ_TN = 1024
assert INPROJ_TN == RET_WIDTH == NA_WIDTH
CTX_COL_BLOCKS = (1, 2, 5, 6)
CTX_RET_K, CTX_RET_V, CTX_NA_K, CTX_NA_V = 0, 1, 2, 3

NA_QROWS = 16
NA_MASK = -1e30

LANES = 128
SUBLANES = 8
BF16_ROWS = 2 * SUBLANES
VMEM_LIMIT_BYTES = 56 * 1024 * 1024

R_KEY0, R_KEY1, R_W0, R_W1 = 0, 1, 2, 3
KEY_RANK_BITS = 16
KEY_RANK_SPAN = float(1 << KEY_RANK_BITS)

EXPERT_TILE = 256
EXPERT_DOWN_SPLIT = 4


def _params(*sem):
    return pltpu.CompilerParams(dimension_semantics=sem, vmem_limit_bytes=VMEM_LIMIT_BYTES)


def _dot(a, b):
    return jnp.dot(a, b, preferred_element_type=F32)


def _dot_nt(a, b):
    return lax.dot_general(a, b, (((1,), (1,)), ((), ())), preferred_element_type=F32)


def _dot_tn(a, b):
    return lax.dot_general(a, b, (((0,), (0,)), ((), ())), preferred_element_type=F32)


def _rms(x, w):
    return x * lax.rsqrt(jnp.mean(x * x, axis=-1, keepdims=True) + NORM_EPS) * w


def _silu(x):
    return x * jax.nn.sigmoid(x)


def _row_pitch(chunks):
    return chunks + 1


def _store_rows(ref, val):
    n_rows, d = val.shape
    chunks = d // LANES
    pitch = _row_pitch(chunks)
    for c in range(chunks):
        ref[pl.ds(c, n_rows, stride=pitch), :] = val[:, c * LANES:(c + 1) * LANES]
    ref[pl.ds(chunks, n_rows, stride=pitch), :] = jnp.zeros((n_rows, LANES), ref.dtype)


def _row_ref(hbm, row, chunks):
    return hbm.at[pl.ds(row * _row_pitch(chunks), chunks), :]


def _start_row_gather(src_row, chunks, dst_buf, sem, n_rows):
    def body(j, carry):
        pltpu.make_async_copy(src_row(j), _row_ref(dst_buf, j, chunks), sem).start()
        return carry

    lax.fori_loop(0, n_rows, body, 0, unroll=SUBLANES)


def _wait_row_gather(dst_buf, sem, n_rows, chunks):
    view = dst_buf.at[pl.ds(0, n_rows * chunks), :]
    pltpu.make_async_copy(view, view, sem).wait()


def _load_gathered(buf, n_rows, chunks):
    return jnp.concatenate([buf[pl.ds(c, n_rows, stride=_row_pitch(chunks)), :] for c in range(chunks)],
                           axis=1)


def _mod_kernel(c_ref, w_ref, b_ref, o_ref):
    a = _silu(c_ref[...]).astype(BF16)
    o_ref[...] = _dot(a, w_ref[...].astype(BF16)) + b_ref[...]


def _modulation(cc, w_mod, b_mod):
    rows, d = cc.shape
    width = w_mod.shape[1]
    tn = next(t for t in (1024, 512, 256, LANES) if width % t == 0)
    return pl.pallas_call(
        _mod_kernel,
        grid=(width // tn,),
        in_specs=[
            pl.BlockSpec((rows, d), lambda j: (0, 0)),
            pl.BlockSpec((d, tn), lambda j: (0, j)),
            pl.BlockSpec((1, tn), lambda j: (0, j)),
        ],
        out_specs=pl.BlockSpec((rows, tn), lambda j: (0, j)),
        out_shape=jax.ShapeDtypeStruct((rows, width), F32),
        compiler_params=_params("arbitrary"),
        name="modulation",
    )(cc, w_mod, b_mod.reshape(1, width))


def _inproj_kernel(rows_per_step, x_ref, sh_ref, sc_ref, nw_ref, w_ref, o_ref, h_s):
    i = pl.program_id(0)
    j = pl.program_id(1)
    tm = o_ref.shape[0]

    def normalise(rows, slot):
        y = _rms(x_ref[rows, :], nw_ref[...])
        h_s[slot, rows, :] = (y * (1.0 + sc_ref[0]) + sh_ref[0]).astype(BF16)

    @pl.when((i == 0) & (j == 0))
    def _():
        normalise(pl.ds(0, tm), 0)

    o_ref[...] = _dot(h_s[i % 2], w_ref[...].astype(BF16)).astype(BF16)
    start = jnp.clip((j - 1) * rows_per_step, 0, tm - rows_per_step)
    normalise(pl.ds(pl.multiple_of(start, BF16_ROWS), rows_per_step), (i + 1) % 2)


def _in_projection(x2d, mod3, mod_row_of_tile, norm_w, w_in, tm, col_blocks=None):
    m, d = x2d.shape
    tn = INPROJ_TN
    if col_blocks is None:
        col_blocks = tuple(range(w_in.shape[1] // tn))
    n_blocks = len(col_blocks)
    n_tiles = m // tm
    assert n_blocks >= 2 and tm % BF16_ROWS == 0
    rows_per_step = -(-tm // ((n_blocks - 1) * BF16_ROWS)) * BF16_ROWS
    w_block = lambda j: sum(jnp.where(j == k, blk, 0) for k, blk in enumerate(col_blocks))
    src = lambda i, j: jnp.where((i == 0) & (j == 0), 0, jnp.minimum(i + 1, n_tiles - 1))
    return pl.pallas_call(
        functools.partial(_inproj_kernel, rows_per_step),
        grid=(n_tiles, n_blocks),
        in_specs=[
            pl.BlockSpec((tm, d), lambda i, j: (src(i, j), 0)),
            pl.BlockSpec((1, 1, d), lambda i, j: (mod_row_of_tile(src(i, j)) * N_MOD + 0, 0, 0)),
            pl.BlockSpec((1, 1, d), lambda i, j: (mod_row_of_tile(src(i, j)) * N_MOD + 1, 0, 0)),
            pl.BlockSpec((1, d), lambda i, j: (0, 0)),
            pl.BlockSpec((d, tn), lambda i, j: (0, w_block(j))),
        ],
        out_specs=pl.BlockSpec((tm, tn), lambda i, j: (i, j)),
        out_shape=jax.ShapeDtypeStruct((m, n_blocks * tn), BF16),
        scratch_shapes=[pltpu.VMEM((2, tm, d), BF16)],
        compiler_params=_params("arbitrary", "arbitrary"),
        name="in_projection",
    )(x2d, mod3, mod3, norm_w.reshape(1, d), w_in)


def _ret_kernel(lg_ref, q_ref, k_ref, v_ref, g_ref, ck_ref, cv_ref, cosr_ref, sinr_ref,
                cosc_ref, sinc_ref, gnw_ref, o_ref, qr_s, kr_s, sfh_s, sbh_s, sf_s, sb_s):
    head = pl.program_id(1)
    lgf = lg_ref[0, head]
    lgb = lg_ref[1, head]
    n = q_ref.shape[0]
    c = RET_CHUNK
    nc = n // c
    ctx_len = ck_ref.shape[0]
    k_scale = RET_HEAD_DIM ** -0.5

    posl = lax.broadcasted_iota(jnp.int32, (ctx_len, 1), 0).astype(F32)
    ck = ck_ref[...].astype(F32) * k_scale
    cv = cv_ref[...]
    sf_s[...] = _dot_tn((ck * jnp.exp(lgf * (ctx_len - 1.0 - posl))).astype(BF16), cv)
    sb_s[...] = _dot_tn((ck * jnp.exp(lgb * posl)).astype(BF16), cv)

    cosc = cosc_ref[...]
    sinc = sinc_ref[...]
    pos = lax.broadcasted_iota(jnp.int32, (c, 1), 0).astype(F32)
    qdec_f = jnp.exp(lgf * (pos + 1.0))
    kdec_f = jnp.exp(lgf * (c - 1.0 - pos))
    cdec_f = jnp.exp(lgf * c)
    qdec_b = jnp.exp(lgb * (c - pos))
    kdec_b = jnp.exp(lgb * pos)
    cdec_b = jnp.exp(lgb * c)

    def rope(x, cosr, sinr):
        xa = x[:, :ROPE_HALF]
        xb = x[:, ROPE_HALF:]
        ya = xa * cosr + pltpu.roll(xa, ROPE_HALF // 2, 1) * sinr
        yb = xb * cosc + pltpu.roll(xb, ROPE_HALF // 2, 1) * sinc
        return jnp.concatenate([ya, yb], axis=1)

    def fwd_chunk(ci, carry):
        r0 = pl.multiple_of(ci * c, c)
        cosr = cosr_ref[pl.ds(r0, c), :]
        sinr = sinr_ref[pl.ds(r0, c), :]
        qr_s[pl.ds(r0, c), :] = rope(q_ref[pl.ds(r0, c), :].astype(F32), cosr, sinr).astype(BF16)
        k = rope(k_ref[pl.ds(r0, c), :].astype(F32), cosr, sinr) * k_scale
        kr_s[pl.ds(r0, c), :] = k.astype(BF16)
        kv = _dot_tn((k * kdec_f).astype(BF16), v_ref[pl.ds(r0, c), :])
        state = sf_s[...]
        sfh_s[ci] = state.astype(BF16)
        sf_s[...] = state * cdec_f + kv
        return carry

    lax.fori_loop(0, nc, fwd_chunk, 0, unroll=RET_UNROLL)

    def bwd_chunk(i, carry):
        ci = nc - 1 - i
        r0 = pl.multiple_of(ci * c, c)
        k = kr_s[pl.ds(r0, c), :].astype(F32)
        kv = _dot_tn((k * kdec_b).astype(BF16), v_ref[pl.ds(r0, c), :])
        state = sb_s[...]
        sbh_s[ci] = state.astype(BF16)
        sb_s[...] = state * cdec_b + kv
        return carry

    lax.fori_loop(0, nc, bwd_chunk, 0, unroll=RET_UNROLL)

    diff = (lax.broadcasted_iota(jnp.int32, (c, c), 0)
            - lax.broadcasted_iota(jnp.int32, (c, c), 1)).astype(F32)
    intra = (jnp.where(diff >= 0, jnp.exp(lgf * jnp.maximum(diff, 0.0)), 0.0)
             + jnp.where(diff <= 0, jnp.exp(lgb * jnp.maximum(-diff, 0.0)), 0.0))
    gnw = gnw_ref[...]

    def out_chunk(ci, carry):
        r0 = pl.multiple_of(ci * c, c)
        qb = qr_s[pl.ds(r0, c), :]
        kb = kr_s[pl.ds(r0, c), :]
        q = qb.astype(F32)
        scores = _dot_nt(qb, kb) * intra
        o = (_dot(scores.astype(BF16), v_ref[pl.ds(r0, c), :])
             + _dot((q * qdec_f).astype(BF16), sfh_s[ci])
             + _dot((q * qdec_b).astype(BF16), sbh_s[ci]))
        mu = jnp.mean(o, axis=-1, keepdims=True)
        d = o - mu
        var = jnp.mean(d * d, axis=-1, keepdims=True)
        on = d * lax.rsqrt(var + NORM_EPS) * gnw
        gate = _silu(g_ref[pl.ds(r0, c), :].astype(F32))
        o_ref[pl.ds(r0, c), :] = (on * gate).astype(BF16)
        return carry

    lax.fori_loop(0, nc, out_chunk, 0, unroll=RET_UNROLL)


def _rope_tables(n):
    inv = ROPE_BASE ** (-jnp.arange(0, ROPE_HALF, 2, dtype=F32) / ROPE_HALF)

    def tables(pos):
        ang = pos.astype(F32)[:, None] * inv[None, :]
        cos = jnp.cos(ang)
        sin = jnp.sin(ang)
        return jnp.concatenate([cos, cos], axis=1), jnp.concatenate([-sin, sin], axis=1)

    t = jnp.arange(n)
    cosr, sinr = tables(t // GRID_W)
    cosc, sinc = tables(jnp.arange(RET_CHUNK) % GRID_W)
    return cosr, sinr, cosc, sinc


def _retention(proj, cproj, lg, gn_w, batch, n, ctx_len):
    hd = RET_HEAD_DIM
    cosr, sinr, cosc, sinc = _rope_tables(n)
    col = lambda which: (lambda b, h, lg_ref: (b, which * RET_HEADS + h))
    const = lambda b, h, lg_ref: (0, 0)
    grid_spec = pltpu.PrefetchScalarGridSpec(
        num_scalar_prefetch=1,
        grid=(batch, RET_HEADS),
        in_specs=[
            pl.BlockSpec((n, hd), col(0)),
            pl.BlockSpec((n, hd), col(1)),
            pl.BlockSpec((n, hd), col(2)),
            pl.BlockSpec((n, hd), col(3)),
            pl.BlockSpec((ctx_len, hd), lambda b, h, lg_ref: (b, CTX_RET_K * RET_HEADS + h)),
            pl.BlockSpec((ctx_len, hd), lambda b, h, lg_ref: (b, CTX_RET_V * RET_HEADS + h)),
            pl.BlockSpec((n, ROPE_HALF), const),
            pl.BlockSpec((n, ROPE_HALF), const),
            pl.BlockSpec((RET_CHUNK, ROPE_HALF), const),
            pl.BlockSpec((RET_CHUNK, ROPE_HALF), const),
            pl.BlockSpec((1, hd), lambda b, h, lg_ref: (0, h)),
        ],
        out_specs=pl.BlockSpec((n, hd), lambda b, h, lg_ref: (b, h)),
        scratch_shapes=[
            pltpu.VMEM((n, hd), BF16),
            pltpu.VMEM((n, hd), BF16),
            pltpu.VMEM((n // RET_CHUNK, hd, hd), BF16),
            pltpu.VMEM((n // RET_CHUNK, hd, hd), BF16),
            pltpu.VMEM((hd, hd), F32),
            pltpu.VMEM((hd, hd), F32),
        ],
    )
    return pl.pallas_call(
        _ret_kernel,
        grid_spec=grid_spec,
        out_shape=jax.ShapeDtypeStruct((batch * n, RET_WIDTH), BF16),
        compiler_params=_params("parallel", "arbitrary"),
        name="retention",
    )(lg, proj, proj, proj, proj, cproj, cproj, cosr, sinr, cosc, sinc, gn_w.reshape(1, RET_WIDTH))


def _na_build_bias(rpb_ref, head, pair_s):
    n_roff, n_coff = 2 * NA_ROWS - 1, 2 * NA_COLS - 1
    c = lax.broadcasted_iota(jnp.int32, (GRID_W, LANES), 0)
    lane = lax.broadcasted_iota(jnp.int32, (GRID_W, LANES), 1)
    kc = jnp.bitwise_and(lane, GRID_W - 1)
    cs = jnp.clip(c - NA_COLS // 2, 0, GRID_W - NA_COLS)
    d = jnp.where((kc >= cs) & (kc < cs + NA_COLS), kc - c + (NA_COLS - 1), -1)

    def table(i):
        acc = jnp.full((GRID_W, LANES), NA_MASK, F32)
        base = (head * n_roff + i) * n_coff
        for j in range(n_coff):
            acc = jnp.where(d == j, rpb_ref[base + j], acc)
        return acc

    prev = table(0)
    for i in range(1, n_roff):
        cur = table(i)
        pair_s[i - 1] = jnp.where(lane < GRID_W, prev, cur)
        prev = cur


def _na_kernel(rpb_ref, q_ref, k_ref, v_ref, kc_ref, vc_ref, o_ref, s_s, m_s, pair_s):
    @pl.when(pl.program_id(1) == 0)
    def _():
        _na_build_bias(rpb_ref, pl.program_id(0), pair_s)

    n = q_ref.shape[0]
    rows = n // GRID_W
    kb = NA_ROWS * GRID_W
    ctx_len = kc_ref.shape[0]
    scale = NA_HEAD_DIM ** -0.5
    kc = kc_ref[...]
    vc = jnp.concatenate([vc_ref[...], jnp.ones((ctx_len, NA_HEAD_DIM), BF16)], axis=1)
    ones_w = jnp.ones((kb, NA_HEAD_DIM), BF16)

    def window(r):
        rs = jnp.clip(r - NA_ROWS // 2, 0, rows - NA_ROWS)
        return rs, pl.multiple_of(r * GRID_W, GRID_W), pl.multiple_of(rs * GRID_W, GRID_W)

    def score_row(r):
        rs, q0, k0 = window(r)
        q = (q_ref[pl.ds(q0, GRID_W), :].astype(F32) * scale).astype(BF16)
        off = rs - r + NA_ROWS - 1
        bias = jnp.concatenate([pair_s[off + 2 * t] for t in range(NA_ROWS // 2)], axis=1)
        s_win = _dot_nt(q, k_ref[pl.ds(k0, kb), :]) + bias
        s_ctx = _dot_nt(q, kc)
        m = jnp.maximum(jnp.max(s_win, axis=-1, keepdims=True), jnp.max(s_ctx, axis=-1, keepdims=True))
        s_s[pl.ds(q0, GRID_W), :kb] = s_win
        s_s[pl.ds(q0, GRID_W), kb:] = s_ctx
        m_s[pl.ds(q0, GRID_W), :] = jnp.broadcast_to(m, (GRID_W, LANES))

    def value_row(r):
        _, q0, k0 = window(r)
        m = m_s[pl.ds(q0, GRID_W), :]
        p_win = jnp.exp(s_s[pl.ds(q0, GRID_W), :kb] - jnp.tile(m, (1, kb // LANES))).astype(BF16)
        p_ctx = jnp.exp(s_s[pl.ds(q0, GRID_W), kb:] - jnp.tile(m, (1, ctx_len // LANES))).astype(BF16)
        vw = jnp.concatenate([v_ref[pl.ds(k0, kb), :], ones_w], axis=1)
        o = _dot(p_win, vw) + _dot(p_ctx, vc)
        o_ref[pl.ds(q0, GRID_W), :] = (o[:, :NA_HEAD_DIM] / o[:, NA_HEAD_DIM:]).astype(BF16)

    qrows = math.gcd(rows, NA_QROWS)

    def sweep(row_fn):
        def group(g, carry):
            for u in range(qrows):
                row_fn(g * qrows + u)
            return carry
        lax.fori_loop(0, rows // qrows, group, 0)

    sweep(score_row)
    sweep(value_row)


def _neighbourhood_attention(proj, cproj, rpb, batch, n, ctx_len):
    hd = NA_HEAD_DIM
    rows = n // GRID_W
    assert rows >= NA_ROWS
    assert rpb.shape == (NA_HEADS, 2 * NA_ROWS - 1, 2 * NA_COLS - 1)
    base = 4 * RET_WIDTH // hd
    col = lambda which: (lambda h, b, rpb_ref: (b, base + which * NA_HEADS + h))
    grid_spec = pltpu.PrefetchScalarGridSpec(
        num_scalar_prefetch=1,
        grid=(NA_HEADS, batch),
        in_specs=[
            pl.BlockSpec((n, hd), col(0)),
            pl.BlockSpec((n, hd), col(1)),
            pl.BlockSpec((n, hd), col(2)),
            pl.BlockSpec((ctx_len, hd), lambda h, b, rpb_ref: (b, CTX_NA_K * NA_HEADS + h)),
            pl.BlockSpec((ctx_len, hd), lambda h, b, rpb_ref: (b, CTX_NA_V * NA_HEADS + h)),
        ],
        out_specs=pl.BlockSpec((n, hd), lambda h, b, rpb_ref: (b, h)),
        scratch_shapes=[
            pltpu.VMEM((n, NA_ROWS * GRID_W + ctx_len), F32),
            pltpu.VMEM((n, LANES), F32),
            pltpu.VMEM((2 * NA_ROWS - 2, GRID_W, LANES), F32),
        ],
    )
    return pl.pallas_call(
        _na_kernel,
        grid_spec=grid_spec,
        out_shape=jax.ShapeDtypeStruct((batch * n, NA_WIDTH), BF16),
        compiler_params=_params("parallel", "arbitrary"),
        name="neighbourhood_attention",
    )(rpb.astype(F32).reshape(-1), proj, proj, proj, cproj, cproj)


def _outproj_kernel(n_groups, per_group, ret_ref, na_ref, w1_ref, w2_ref, x_ref, ga_ref, shf_ref,
                    scf_ref, nw_ref, wr_ref, br_ref, x1_ref, hf_ref, route_ref, cnt_ref, carry_s,
                    earlier_s):
    tm = x_ref.shape[0]

    @pl.when(pl.program_id(0) == 0)
    def _():
        carry_s[...] = jnp.zeros_like(carry_s)
        earlier_s[...] = jnp.where(lax.broadcasted_iota(jnp.int32, (tm, tm), 0)
                                   > lax.broadcasted_iota(jnp.int32, (tm, tm), 1), 1.0, 0.0).astype(BF16)

    acc = _dot(ret_ref[...], w1_ref[...]) + _dot(na_ref[...], w2_ref[...])
    x1 = x_ref[...] + ga_ref[0] * acc
    x1_ref[...] = x1
    hf = _rms(x1, nw_ref[...]) * (1.0 + scf_ref[0]) + shf_ref[0]
    _store_rows(hf_ref, hf)

    hf_hi = hf.astype(BF16)
    hf_lo = (hf - hf_hi.astype(F32)).astype(BF16)
    p_hi = _dot(hf_hi, wr_ref[...])
    p_lo = _dot(hf_lo, wr_ref[...])
    logits = p_hi[:, :LANES] + p_hi[:, LANES:] + p_lo[:, :LANES] + br_ref[...]
    lane = lax.broadcasted_iota(jnp.int32, (tm, LANES), 1)
    neg = -jnp.inf

    def first_max(vals):
        top = jnp.max(vals, axis=-1, keepdims=True)
        idx = jnp.min(jnp.where(vals == top, lane, LANES), axis=-1, keepdims=True)
        return top, idx

    g_logits = jnp.where(lane < n_groups, logits, neg)
    g_top, g_sel = first_max(g_logits)
    g_w = 1.0 / jnp.sum(jnp.exp(g_logits - g_top), axis=-1, keepdims=True)
    lo = n_groups + g_sel * per_group
    e_logits = jnp.where((lane >= lo) & (lane < lo + per_group), logits, neg)
    v0, i0 = first_max(e_logits)
    v1, i1 = first_max(jnp.where(lane == i0, neg, e_logits))
    e1 = jnp.exp(v1 - v0)
    w0 = g_w / (1.0 + e1)
    w1 = g_w * e1 / (1.0 + e1)

    hit0 = lane == i0
    hit1 = lane == i1
    onehot = jnp.where(hit0 | hit1, 1.0, 0.0)
    before = _dot(earlier_s[...], onehot.astype(BF16)) + carry_s[...]
    rank0 = jnp.sum(jnp.where(hit0, before, 0.0), axis=-1, keepdims=True)
    rank1 = jnp.sum(jnp.where(hit1, before, 0.0), axis=-1, keepdims=True)
    carry_s[...] = carry_s[...] + jnp.sum(onehot, axis=0, keepdims=True)
    cnt_ref[...] = carry_s[...]

    key0 = (i0 - n_groups).astype(F32) * KEY_RANK_SPAN + rank0
    key1 = (i1 - n_groups).astype(F32) * KEY_RANK_SPAN + rank1
    fields = {R_KEY0: key0, R_KEY1: key1, R_W0: w0, R_W1: w1}
    route = jnp.zeros((tm, LANES), F32)
    for idx, val in fields.items():
        route = jnp.where(lane == idx, val, route)
    route_ref[...] = route


def _out_projection(ret, na, w_out, x2d, mod3, norm_w, w_route, b_route, n, n_groups, per_group, tm):
    m, d = x2d.shape
    pitch = _row_pitch(d // LANES)
    batch_of = lambda i: (i * tm) // n
    mod_spec = lambda which: pl.BlockSpec((1, 1, d), lambda i: (batch_of(i) * N_MOD + which, 0, 0))
    const2 = lambda i: (0, 0)
    return pl.pallas_call(
        functools.partial(_outproj_kernel, n_groups, per_group),
        grid=(m // tm,),
        in_specs=[
            pl.BlockSpec((tm, RET_WIDTH), lambda i: (i, 0)),
            pl.BlockSpec((tm, NA_WIDTH), lambda i: (i, 0)),
            pl.BlockSpec((RET_WIDTH, d), lambda i: (0, 0)),
            pl.BlockSpec((NA_WIDTH, d), lambda i: (RET_WIDTH // NA_WIDTH, 0)),
            pl.BlockSpec((tm, d), lambda i: (i, 0)),
            mod_spec(2), mod_spec(3), mod_spec(4),
            pl.BlockSpec((1, d), const2),
            pl.BlockSpec((d, 2 * LANES), const2),
            pl.BlockSpec((1, LANES), const2),
        ],
        out_specs=[
            pl.BlockSpec((tm, d), lambda i: (i, 0)),
            pl.BlockSpec((tm * pitch, LANES), lambda i: (i, 0)),
            pl.BlockSpec((tm, LANES), lambda i: (i, 0)),
            pl.BlockSpec((1, LANES), const2),
        ],
        out_shape=[
            jax.ShapeDtypeStruct((m, d), F32),
            jax.ShapeDtypeStruct((m * pitch, LANES), F32),
            jax.ShapeDtypeStruct((m, LANES), F32),
            jax.ShapeDtypeStruct((1, LANES), F32),
        ],
        scratch_shapes=[pltpu.VMEM((1, LANES), F32), pltpu.VMEM((tm, tm), BF16)],
        compiler_params=_params("arbitrary"),
        name="out_projection_router",
    )(ret, na, w_out, w_out, x2d, mod3, mod3, mod3, norm_w.reshape(1, d), w_route, b_route)


def _slot_map_kernel(tile_rows, pos_ref, cnt_ref, slot_ref, tile_expert_ref, next_expert_ref,
                     run_parity_ref, n_used_ref, offs_s):
    n_experts = cnt_ref.shape[0]
    n_tiles = tile_expert_ref.shape[0]
    shift = tile_rows.bit_length() - 1
    assert 1 << shift == tile_rows

    def fill(ref, lo, hi, val):
        value_at = val if callable(val) else (lambda s: val)
        groups = lax.div(hi - lo, SUBLANES)

        def group(g, carry):
            for k in range(SUBLANES):
                s = lo + g * SUBLANES + k
                ref[s] = value_at(s)
            return carry

        def single(s, carry):
            ref[s] = value_at(s)
            return carry

        lax.fori_loop(0, groups, group, 0)
        lax.fori_loop(lo + groups * SUBLANES, hi, single, 0)

    assert tile_rows <= pos_ref.shape[0] // 2
    padding_row = lambda s: jnp.bitwise_and(s, tile_rows - 1)

    def tiles_of(e):
        size = lax.shift_left(lax.shift_right_logical(cnt_ref[e] + (tile_rows - 1), shift), shift)
        return lax.shift_right_logical(offs_s[e], shift), lax.shift_right_logical(offs_s[e] + size, shift)

    def per_expert(e, carry):
        start, runs = carry
        cnt = cnt_ref[e]
        size = lax.shift_left(lax.shift_right_logical(cnt + (tile_rows - 1), shift), shift)
        offs_s[e] = start
        t0, t1 = tiles_of(e)
        fill(tile_expert_ref, t0, t1, e)
        fill(run_parity_ref, t0, t1, jnp.bitwise_and(runs, 1))
        fill(slot_ref, start + cnt, start + size, padding_row)
        return start + size, runs + (size > 0).astype(jnp.int32)

    end, _ = lax.fori_loop(0, n_experts, per_expert, (0, 0))
    n_used = lax.shift_right_logical(end, shift)
    n_used_ref[0] = n_used
    fill(tile_expert_ref, n_used, n_tiles, n_experts - 1)
    fill(run_parity_ref, n_used, n_tiles, 0)
    fill(next_expert_ref, n_used, n_tiles, -1)
    fill(slot_ref, end, slot_ref.shape[0], padding_row)

    def per_expert_reversed(k, following):
        e = n_experts - 1 - k
        t0, t1 = tiles_of(e)
        fill(next_expert_ref, t0, t1, following)
        return jnp.where(t1 > t0, e, following)

    lax.fori_loop(0, n_experts, per_expert_reversed, -1)

    def assign(token, carry):
        slot_ref[pos_ref[2 * token]] = token
        slot_ref[pos_ref[2 * token + 1]] = token
        return carry

    lax.fori_loop(0, pos_ref.shape[0] // 2, assign, 0, unroll=SUBLANES)


def _slot_map(pos, cnt, n_tiles, tile_rows):
    smem = pl.BlockSpec(memory_space=pltpu.SMEM)
    i32 = lambda *shape: jax.ShapeDtypeStruct(shape, jnp.int32)
    return pl.pallas_call(
        functools.partial(_slot_map_kernel, tile_rows),
        in_specs=[smem, smem],
        out_specs=[smem] * 5,
        out_shape=[i32(n_tiles * tile_rows), i32(n_tiles), i32(n_tiles), i32(n_tiles), i32(1)],
        scratch_shapes=[pltpu.SMEM((cnt.shape[0],), jnp.int32)],
        name="slot_map",
    )(pos, cnt)


def _expert_kernel(tile_expert_ref, next_expert_ref, run_parity_ref, n_used_ref, tokens_ref,
                   next_tokens_ref, hf_ref, wg_hbm, wu_hbm, wd_hbm, o_ref, xbuf, gsem, wg_f, wu_f,
                   wd_f, wsem, wg_s, wu_s, wd_s):
    i = pl.program_id(0)
    last = pl.num_programs(0) - 1
    n_used = n_used_ref[0]
    d, ff = wg_s.shape
    chunks = d // LANES
    pitch = _row_pitch(chunks)
    tm = o_ref.shape[0] // pitch

    def row_copy(ids_ref, slot, j):
        return pltpu.make_async_copy(_row_ref(hf_ref, ids_ref[0, 0, j], chunks),
                                     _row_ref(xbuf.at[slot], j, chunks), gsem.at[slot])

    def weight_copies(expert, slot):
        return [pltpu.make_async_copy(hbm.at[expert], stage.at[slot], wsem.at[slot])
                for hbm, stage in ((wg_hbm, wg_f), (wu_hbm, wu_f), (wd_hbm, wd_f))]

    @pl.when((i == 0) & (n_used > 0))
    def _():
        _start_row_gather(lambda j: _row_ref(hf_ref, tokens_ref[0, 0, j], chunks), chunks, xbuf.at[0],
                          gsem.at[0], tm)
        for cp in weight_copies(tile_expert_ref[0], run_parity_ref[0]):
            cp.start()

    @pl.when(i < n_used)
    def _():
        @pl.when((i == 0) | (tile_expert_ref[i] != tile_expert_ref[jnp.maximum(i - 1, 0)]))
        def _():
            stage = run_parity_ref[i]
            for cp in weight_copies(tile_expert_ref[i], stage):
                cp.wait()

            @pl.when(next_expert_ref[i] >= 0)
            def _():
                for cp in weight_copies(next_expert_ref[i], 1 - stage):
                    cp.start()

            wg_s[...] = wg_f[stage].astype(BF16)
            wu_s[...] = wu_f[stage].astype(BF16)
            wd_s[...] = wd_f[stage].astype(BF16)

        slot = i % 2
        _wait_row_gather(xbuf.at[slot], gsem.at[slot], tm, chunks)
        x = _load_gathered(xbuf.at[slot], tm, chunks).astype(BF16)

        down_split = min(EXPERT_DOWN_SPLIT, chunks)
        n_stages = 2 + down_split
        bounds = [tm * s // n_stages for s in range(n_stages + 1)]

        def start_next_rows(stage):
            for j in range(bounds[stage], bounds[stage + 1]):
                row_copy(next_tokens_ref, 1 - slot, j).start()

        halves = []
        for half in range(2):
            start_next_rows(half)
            cols = slice(half * ff // 2, (half + 1) * ff // 2)
            halves.append((_silu(_dot(x, wg_s[:, cols])) * _dot(x, wu_s[:, cols])).astype(BF16))
        a = jnp.concatenate(halves, axis=1)
        per_part = chunks // down_split
        for part in range(down_split):
            start_next_rows(2 + part)
            y = _dot(a, wd_s[:, part * per_part * LANES:(part + 1) * per_part * LANES])
            for c in range(per_part):
                o_ref[pl.ds(part * per_part + c, tm, stride=pitch), :] = y[:, c * LANES:(c + 1) * LANES]
        o_ref[pl.ds(chunks, tm, stride=pitch), :] = jnp.zeros((tm, LANES), o_ref.dtype)

        @pl.when(i == last)
        def _():
            _wait_row_gather(xbuf.at[1 - slot], gsem.at[1 - slot], tm, chunks)

    @pl.when(i >= n_used)
    def _():
        @pl.when((i == n_used) & (i > 0))
        def _():
            _wait_row_gather(xbuf.at[i % 2], gsem.at[i % 2], tm, chunks)

        o_ref[...] = jnp.zeros(o_ref.shape, o_ref.dtype)


def _experts(tile_meta, slot_token, hf, w_gate, w_up, w_down, n_tiles, tm):
    d, ff = w_gate.shape[-2:]
    chunks = d // LANES
    pitch = _row_pitch(chunks)
    slot_tiles = slot_token.reshape(n_tiles, 1, tm)
    any_space = pl.BlockSpec(memory_space=pl.ANY)
    ids_block = lambda ahead: pl.BlockSpec(
        (1, 1, tm), lambda i, *prefetch: (jnp.minimum(i + ahead, n_tiles - 1), 0, 0),
        memory_space=pltpu.SMEM)
    grid_spec = pltpu.PrefetchScalarGridSpec(
        num_scalar_prefetch=4,
        grid=(n_tiles,),
        in_specs=[ids_block(0), ids_block(1), any_space, any_space, any_space, any_space],
        out_specs=pl.BlockSpec((tm * pitch, LANES), lambda i, *prefetch: (i, 0)),
        scratch_shapes=[
            pltpu.VMEM((2, tm * pitch, LANES), hf.dtype),
            pltpu.SemaphoreType.DMA((2,)),
            pltpu.VMEM((2, d, ff), F32),
            pltpu.VMEM((2, d, ff), F32),
            pltpu.VMEM((2, ff, d), F32),
            pltpu.SemaphoreType.DMA((2,)),
            pltpu.VMEM((d, ff), BF16),
            pltpu.VMEM((d, ff), BF16),
            pltpu.VMEM((ff, d), BF16),
        ],
    )
    return pl.pallas_call(
        _expert_kernel,
        grid_spec=grid_spec,
        out_shape=jax.ShapeDtypeStruct((n_tiles * tm * pitch, LANES), hf.dtype),
        compiler_params=_params("arbitrary"),
        name="routed_experts",
    )(*tile_meta, slot_tiles, slot_tiles, hf, w_gate, w_up, w_down)


def _combine_kernel(pos_ref, ys_ref, x1_ref, route_ref, gf_ref, fw_ref, o_ref, ybuf, sem):
    i = pl.program_id(0)
    n_tiles = pl.num_programs(0)
    tm, d = x1_ref.shape
    chunks = d // LANES

    def start_tile(tile, slot):
        for choice in range(2):
            _start_row_gather(lambda j: _row_ref(ys_ref, pos_ref[(tile * tm + j) * 2 + choice], chunks),
                              chunks, ybuf.at[slot, choice], sem.at[slot, choice], tm)

    @pl.when(i == 0)
    def _():
        start_tile(0, 0)

    @pl.when(i + 1 < n_tiles)
    def _():
        start_tile(i + 1, (i + 1) % 2)

    slot = i % 2
    for choice in range(2):
        _wait_row_gather(ybuf.at[slot, choice], sem.at[slot, choice], tm, chunks)
    route = route_ref[...]
    moe = (route[:, R_W0:R_W0 + 1] * _load_gathered(ybuf.at[slot, 0], tm, chunks)
           + route[:, R_W1:R_W1 + 1] * _load_gathered(ybuf.at[slot, 1], tm, chunks))
    x2 = x1_ref[...] + gf_ref[0] * moe
    o_ref[...] = _rms(x2, fw_ref[...])


def _combine(pos, ys, x1, route, mod3, final_w, n, tm):
    m, d = x1.shape
    chunks = d // LANES
    grid_spec = pltpu.PrefetchScalarGridSpec(
        num_scalar_prefetch=1,
        grid=(m // tm,),
        in_specs=[
            pl.BlockSpec(memory_space=pl.ANY),
            pl.BlockSpec((tm, d), lambda i, pos_ref: (i, 0)),
            pl.BlockSpec((tm, LANES), lambda i, pos_ref: (i, 0)),
            pl.BlockSpec((1, 1, d), lambda i, pos_ref: (((i * tm) // n) * N_MOD + 5, 0, 0)),
            pl.BlockSpec((1, d), lambda i, pos_ref: (0, 0)),
        ],
        out_specs=pl.BlockSpec((tm, d), lambda i, pos_ref: (i, 0)),
        scratch_shapes=[pltpu.VMEM((2, 2, tm * _row_pitch(chunks), LANES), ys.dtype),
                        pltpu.SemaphoreType.DMA((2, 2))],
    )
    return pl.pallas_call(
        _combine_kernel,
        grid_spec=grid_spec,
        out_shape=jax.ShapeDtypeStruct((m, d), F32),
        compiler_params=_params("arbitrary"),
        name="combine_final_norm",
    )(pos, ys, x1, route, mod3, final_w.reshape(1, d))


def kernel(x, c, ctx, c_ctx, w_mod, b_mod, norm_mix_w, w_in, ret_decay_f, ret_decay_b, ret_gn_w, na_rpb, w_out, norm_ffn_w, w_router_group, b_router_group, w_router_expert, b_router_expert, w_gate, w_up, w_down, final_norm_w):
    assert w_mod.shape[0] == 1, "single trunk layer"
    batch, n, d = x.shape
    ctx_len = ctx.shape[1]
    n_groups = w_router_group.shape[-1]
    per_group = w_router_expert.shape[-1]
    n_experts = w_gate.shape[1]
    assert n_groups * per_group == n_experts and n_groups + n_experts <= LANES

    mod_rows = 8
    cc = jnp.zeros((mod_rows, d), F32).at[:batch].set(c).at[batch].set(c_ctx)
    mod = _modulation(cc, w_mod[0], b_mod[0])
    mod3 = mod.reshape(mod_rows * N_MOD, 1, d)

    w_in_b = w_in[0]
    tm = min(512, n)
    tm_in = min(1024, n)
    x2d = x.reshape(batch * n, d)
    proj = _in_projection(x2d, mod3, lambda i: (i * tm_in) // n, norm_mix_w[0], w_in_b, tm_in)
    cproj = _in_projection(ctx.reshape(batch * ctx_len, d), mod3, lambda i: batch, norm_mix_w[0],
                           w_in_b, batch * ctx_len, CTX_COL_BLOCKS)

    lg = jnp.stack([jax.nn.log_sigmoid(ret_decay_f[0].astype(F32)),
                    jax.nn.log_sigmoid(ret_decay_b[0].astype(F32))])
    ret = _retention(proj, cproj, lg, ret_gn_w[0], batch, n, ctx_len)
    na = _neighbourhood_attention(proj, cproj, na_rpb[0], batch, n, ctx_len)

    w_route = jnp.concatenate(
        [w_router_group[0], jnp.moveaxis(w_router_expert[0], 0, 1).reshape(d, n_experts)], axis=1)
    w_route = jnp.pad(w_route.astype(F32), ((0, 0), (0, LANES - n_groups - n_experts)))
    w_route_hi = w_route.astype(BF16)
    w_route = jnp.concatenate([w_route_hi, (w_route - w_route_hi.astype(F32)).astype(BF16)], axis=1)
    b_route = jnp.concatenate([b_router_group[0], b_router_expert[0].reshape(-1)])
    b_route = jnp.pad(b_route.astype(F32), (0, LANES - n_groups - n_experts)).reshape(1, LANES)
    x1, hf, route, counts = _out_projection(ret, na, w_out[0].astype(BF16), x2d, mod3, norm_ffn_w[0],
                                            w_route, b_route, n, n_groups, per_group, tm)

    te = EXPERT_TILE
    tokens = batch * n
    assert tokens < 1 << KEY_RANK_BITS and (2 * tokens) % te == 0
    n_tiles = (2 * tokens) // te + n_experts
    cnt = counts[0, n_groups:n_groups + n_experts].astype(jnp.int32)
    padded = (cnt + te - 1) // te * te
    starts = jnp.cumsum(padded) - padded
    keys = route[:, R_KEY0:R_KEY1 + 1].astype(jnp.int32).reshape(-1)
    expert = lax.shift_right_logical(keys, KEY_RANK_BITS)
    start_of = jnp.sum(jnp.where(expert[:, None] == jnp.arange(n_experts)[None, :], starts[None, :], 0), axis=1)
    pos = start_of + jnp.bitwise_and(keys, (1 << KEY_RANK_BITS) - 1)
    slot_token, *tile_meta = _slot_map(pos, cnt, n_tiles, te)

    ys = _experts(tile_meta, slot_token, hf, w_gate[0], w_up[0], w_down[0], n_tiles, te)
    out = _combine(pos, ys, x1, route, mod3, final_norm_w, n, min(256, n))
    return out.reshape(batch, n, d)
```

```python
import functools
import math

import jax
import jax.numpy as jnp
from jax import lax
from jax.experimental import pallas as pl
from jax.experimental.pallas import tpu as pltpu

F32 = jnp.float32
BF16 = jnp.bfloat16

GRID_W = 64
RET_HEADS = 4
RET_HEAD_DIM = 256
RET_WIDTH = RET_HEADS * RET_HEAD_DIM
NA_HEADS = 8
NA_HEAD_DIM = 128
NA_WIDTH = NA_HEADS * NA_HEAD_DIM
RET_CHUNK = 128
NA_ROWS = 8
NA_COLS = 16
ROPE_BASE = 10000.0
N_MOD = 6
NORM_EPS = 1e-6
ROPE_HALF = RET_HEAD_DIM // 2
RET_UNROLL = 8

INPROJ_TN = 1024
assert INPROJ_TN == RET_WIDTH == NA_WIDTH
CTX_COL_BLOCKS = (1, 2, 5, 6)
CTX_RET_K, CTX_RET_V, CTX_NA_K, CTX_NA_V = 0, 1, 2, 3

NA_QROWS = 16
NA_MASK = -1e30

LANES = 128
SUBLANES = 8
BF16_ROWS = 2 * SUBLANES
VMEM_LIMIT_BYTES = 56 * 1024 * 1024

R_KEY0, R_KEY1, R_W0, R_W1 = 0, 1, 2, 3
KEY_RANK_BITS = 16
KEY_RANK_SPAN = float(1 << KEY_RANK_BITS)

EXPERT_TILE = 256
WEIGHT_DMA_PRIORITY = 1
GATHER_DMA_PRIORITY = 1
EXPERT_DOWN_SPLIT = 4


def _params(*sem):
    return pltpu.CompilerParams(dimension_semantics=sem, vmem_limit_bytes=VMEM_LIMIT_BYTES)


def _dot(a, b):
    return jnp.dot(a, b, preferred_element_type=F32)


def _dot_nt(a, b):
    return lax.dot_general(a, b, (((1,), (1,)), ((), ())), preferred_element_type=F32)


def _dot_tn(a, b):
    return lax.dot_general(a, b, (((0,), (0,)), ((), ())), preferred_element_type=F32)


def _rms(x, w):
    return x * lax.rsqrt(jnp.mean(x * x, axis=-1, keepdims=True) + NORM_EPS) * w


def _silu(x):
    return x * jax.nn.sigmoid(x)


def _row_pitch(chunks):
    return chunks + 1


def _store_rows(ref, val):
    n_rows, d = val.shape
    chunks = d // LANES
    pitch = _row_pitch(chunks)
    for c in range(chunks):
        ref[pl.ds(c, n_rows, stride=pitch), :] = val[:, c * LANES:(c + 1) * LANES]
    ref[pl.ds(chunks, n_rows, stride=pitch), :] = jnp.zeros((n_rows, LANES), ref.dtype)


def _row_ref(hbm, row, chunks):
    return hbm.at[pl.ds(row * _row_pitch(chunks), chunks), :]


def _start_row_gather(src_row, chunks, dst_buf, sem, n_rows, priority=0):
    def body(j, carry):
        pltpu.make_async_copy(src_row(j), _row_ref(dst_buf, j, chunks), sem).start(priority=priority)
        return carry

    lax.fori_loop(0, n_rows, body, 0, unroll=SUBLANES)


def _wait_row_gather(dst_buf, sem, n_rows, chunks):
    view = dst_buf.at[pl.ds(0, n_rows * chunks), :]
    pltpu.make_async_copy(view, view, sem).wait()


def _load_gathered(buf, n_rows, chunks):
    return jnp.concatenate([buf[pl.ds(c, n_rows, stride=_row_pitch(chunks)), :] for c in range(chunks)],
                           axis=1)


def _mod_kernel(c_ref, w_ref, b_ref, o_ref):
    a = _silu(c_ref[...]).astype(BF16)
    o_ref[...] = _dot(a, w_ref[...].astype(BF16)) + b_ref[...]


def _modulation(cc, w_mod, b_mod):
    rows, d = cc.shape
    width = w_mod.shape[1]
    tn = next(t for t in (1024, 512, 256, LANES) if width % t == 0)
    return pl.pallas_call(
        _mod_kernel,
        grid=(width // tn,),
        in_specs=[
            pl.BlockSpec((rows, d), lambda j: (0, 0)),
            pl.BlockSpec((d, tn), lambda j: (0, j)),
            pl.BlockSpec((1, tn), lambda j: (0, j)),
        ],
        out_specs=pl.BlockSpec((rows, tn), lambda j: (0, j)),
        out_shape=jax.ShapeDtypeStruct((rows, width), F32),
        compiler_params=_params("arbitrary"),
        name="modulation",
    )(cc, w_mod, b_mod.reshape(1, width))


def _inproj_kernel(rows_per_step, x_ref, sh_ref, sc_ref, nw_ref, w_ref, o_ref, h_s):
    i = pl.program_id(0)
    j = pl.program_id(1)
    tm = o_ref.shape[0]

    def normalise(rows, slot):
        y = _rms(x_ref[rows, :], nw_ref[...])
        h_s[slot, rows, :] = (y * (1.0 + sc_ref[0]) + sh_ref[0]).astype(BF16)

    @pl.when((i == 0) & (j == 0))
    def _():
        normalise(pl.ds(0, tm), 0)

    o_ref[...] = _dot(h_s[i % 2], w_ref[...].astype(BF16)).astype(BF16)
    start = jnp.clip((j - 1) * rows_per_step, 0, tm - rows_per_step)
    normalise(pl.ds(pl.multiple_of(start, BF16_ROWS), rows_per_step), (i + 1) % 2)


def _in_projection(x2d, mod3, mod_row_of_tile, norm_w, w_in, tm, col_blocks=None):
    m, d = x2d.shape
    tn = INPROJ_TN
    if col_blocks is None:
        col_blocks = tuple(range(w_in.shape[1] // tn))
    n_blocks = len(col_blocks)
    n_tiles = m // tm
    assert n_blocks >= 2 and tm % BF16_ROWS == 0
    rows_per_step = -(-tm // ((n_blocks - 1) * BF16_ROWS)) * BF16_ROWS
    w_block = lambda j: sum(jnp.where(j == k, blk, 0) for k, blk in enumerate(col_blocks))
    src = lambda i, j: jnp.where((i == 0) & (j == 0), 0, jnp.minimum(i + 1, n_tiles - 1))
    return pl.pallas_call(
        functools.partial(_inproj_kernel, rows_per_step),
        grid=(n_tiles, n_blocks),
        in_specs=[
            pl.BlockSpec((tm, d), lambda i, j: (src(i, j), 0)),
            pl.BlockSpec((1, 1, d), lambda i, j: (mod_row_of_tile(src(i, j)) * N_MOD + 0, 0, 0)),
            pl.BlockSpec((1, 1, d), lambda i, j: (mod_row_of_tile(src(i, j)) * N_MOD + 1, 0, 0)),
            pl.BlockSpec((1, d), lambda i, j: (0, 0)),
            pl.BlockSpec((d, tn), lambda i, j: (0, w_block(j))),
        ],
        out_specs=pl.BlockSpec((tm, tn), lambda i, j: (i, j)),
        out_shape=jax.ShapeDtypeStruct((m, n_blocks * tn), BF16),
        scratch_shapes=[pltpu.VMEM((2, tm, d), BF16)],
        compiler_params=_params("arbitrary", "arbitrary"),
        name="in_projection",
    )(x2d, mod3, mod3, norm_w.reshape(1, d), w_in)


def _ret_kernel(lg_ref, q_ref, k_ref, v_ref, g_ref, ck_ref, cv_ref, cosr_ref, sinr_ref,
                cosc_ref, sinc_ref, gnw_ref, o_ref, qr_s, kr_s, sfh_s, sbh_s, sf_s, sb_s):
    head = pl.program_id(1)
    lgf = lg_ref[0, head]
    lgb = lg_ref[1, head]
    n = q_ref.shape[0]
    c = RET_CHUNK
    nc = n // c
    ctx_len = ck_ref.shape[0]
    k_scale = RET_HEAD_DIM ** -0.5

    posl = lax.broadcasted_iota(jnp.int32, (ctx_len, 1), 0).astype(F32)
    ck = ck_ref[...].astype(F32) * k_scale
    cv = cv_ref[...]
    sf_s[...] = _dot_tn((ck * jnp.exp(lgf * (ctx_len - 1.0 - posl))).astype(BF16), cv)
    sb_s[...] = _dot_tn((ck * jnp.exp(lgb * posl)).astype(BF16), cv)

    cosc = cosc_ref[...]
    sinc = sinc_ref[...]
    pos = lax.broadcasted_iota(jnp.int32, (c, 1), 0).astype(F32)
    qdec_f = jnp.exp(lgf * (pos + 1.0))
    kdec_f = jnp.exp(lgf * (c - 1.0 - pos))
    cdec_f = jnp.exp(lgf * c)
    qdec_b = jnp.exp(lgb * (c - pos))
    kdec_b = jnp.exp(lgb * pos)
    cdec_b = jnp.exp(lgb * c)

    def rope(x, cosr, sinr):
        xa = x[:, :ROPE_HALF]
        xb = x[:, ROPE_HALF:]
        ya = xa * cosr + pltpu.roll(xa, ROPE_HALF // 2, 1) * sinr
        yb = xb * cosc + pltpu.roll(xb, ROPE_HALF // 2, 1) * sinc
        return jnp.concatenate([ya, yb], axis=1)

    def fwd_chunk(ci, carry):
        r0 = pl.multiple_of(ci * c, c)
        cosr = cosr_ref[pl.ds(r0, c), :]
        sinr = sinr_ref[pl.ds(r0, c), :]
        qr_s[pl.ds(r0, c), :] = rope(q_ref[pl.ds(r0, c), :].astype(F32), cosr, sinr).astype(BF16)
        k = rope(k_ref[pl.ds(r0, c), :].astype(F32), cosr, sinr) * k_scale
        kr_s[pl.ds(r0, c), :] = k.astype(BF16)
        kv = _dot_tn((k * kdec_f).astype(BF16), v_ref[pl.ds(r0, c), :])
        state = sf_s[...]
        sfh_s[ci] = state.astype(BF16)
        sf_s[...] = state * cdec_f + kv
        return carry

    lax.fori_loop(0, nc, fwd_chunk, 0, unroll=RET_UNROLL)

    def bwd_chunk(i, carry):
        ci = nc - 1 - i
        r0 = pl.multiple_of(ci * c, c)
        k = kr_s[pl.ds(r0, c), :].astype(F32)
        kv = _dot_tn((k * kdec_b).astype(BF16), v_ref[pl.ds(r0, c), :])
        state = sb_s[...]
        sbh_s[ci] = state.astype(BF16)
        sb_s[...] = state * cdec_b + kv
        return carry

    lax.fori_loop(0, nc, bwd_chunk, 0, unroll=RET_UNROLL)

    diff = (lax.broadcasted_iota(jnp.int32, (c, c), 0)
            - lax.broadcasted_iota(jnp.int32, (c, c), 1)).astype(F32)
    intra = (jnp.where(diff >= 0, jnp.exp(lgf * jnp.maximum(diff, 0.0)), 0.0)
             + jnp.where(diff <= 0, jnp.exp(lgb * jnp.maximum(-diff, 0.0)), 0.0))
    gnw = gnw_ref[...]

    def out_chunk(ci, carry):
        r0 = pl.multiple_of(ci * c, c)
        qb = qr_s[pl.ds(r0, c), :]
        kb = kr_s[pl.ds(r0, c), :]
        q = qb.astype(F32)
        scores = _dot_nt(qb, kb) * intra
        o = (_dot(scores.astype(BF16), v_ref[pl.ds(r0, c), :])
             + _dot((q * qdec_f).astype(BF16), sfh_s[ci])
             + _dot((q * qdec_b).astype(BF16), sbh_s[ci]))
        mu = jnp.mean(o, axis=-1, keepdims=True)
        d = o - mu
        var = jnp.mean(d * d, axis=-1, keepdims=True)
        on = d * lax.rsqrt(var + NORM_EPS) * gnw
        gate = _silu(g_ref[pl.ds(r0, c), :].astype(F32))
        o_ref[pl.ds(r0, c), :] = (on * gate).astype(BF16)
        return carry

    lax.fori_loop(0, nc, out_chunk, 0, unroll=RET_UNROLL)


def _rope_tables(n):
    inv = ROPE_BASE ** (-jnp.arange(0, ROPE_HALF, 2, dtype=F32) / ROPE_HALF)

    def tables(pos):
        ang = pos.astype(F32)[:, None] * inv[None, :]
        cos = jnp.cos(ang)
        sin = jnp.sin(ang)
        return jnp.concatenate([cos, cos], axis=1), jnp.concatenate([-sin, sin], axis=1)

    t = jnp.arange(n)
    cosr, sinr = tables(t // GRID_W)
    cosc, sinc = tables(jnp.arange(RET_CHUNK) % GRID_W)
    return cosr, sinr, cosc, sinc


def _retention(proj, cproj, lg, gn_w, batch, n, ctx_len):
    hd = RET_HEAD_DIM
    cosr, sinr, cosc, sinc = _rope_tables(n)
    col = lambda which: (lambda b, h, lg_ref: (b, which * RET_HEADS + h))
    const = lambda b, h, lg_ref: (0, 0)
    grid_spec = pltpu.PrefetchScalarGridSpec(
        num_scalar_prefetch=1,
        grid=(batch, RET_HEADS),
        in_specs=[
            pl.BlockSpec((n, hd), col(0)),
            pl.BlockSpec((n, hd), col(1)),
            pl.BlockSpec((n, hd), col(2)),
            pl.BlockSpec((n, hd), col(3)),
            pl.BlockSpec((ctx_len, hd), lambda b, h, lg_ref: (b, CTX_RET_K * RET_HEADS + h)),
            pl.BlockSpec((ctx_len, hd), lambda b, h, lg_ref: (b, CTX_RET_V * RET_HEADS + h)),
            pl.BlockSpec((n, ROPE_HALF), const),
            pl.BlockSpec((n, ROPE_HALF), const),
            pl.BlockSpec((RET_CHUNK, ROPE_HALF), const),
            pl.BlockSpec((RET_CHUNK, ROPE_HALF), const),
            pl.BlockSpec((1, hd), lambda b, h, lg_ref: (0, h)),
        ],
        out_specs=pl.BlockSpec((n, hd), lambda b, h, lg_ref: (b, h)),
        scratch_shapes=[
            pltpu.VMEM((n, hd), BF16),
            pltpu.VMEM((n, hd), BF16),
            pltpu.VMEM((n // RET_CHUNK, hd, hd), BF16),
            pltpu.VMEM((n // RET_CHUNK, hd, hd), BF16),
            pltpu.VMEM((hd, hd), F32),
            pltpu.VMEM((hd, hd), F32),
        ],
    )
    return pl.pallas_call(
        _ret_kernel,
        grid_spec=grid_spec,
        out_shape=jax.ShapeDtypeStruct((batch * n, RET_WIDTH), BF16),
        compiler_params=_params("parallel", "arbitrary"),
        name="retention",
    )(lg, proj, proj, proj, proj, cproj, cproj, cosr, sinr, cosc, sinc, gn_w.reshape(1, RET_WIDTH))


def _na_build_bias(rpb_ref, head, pair_s):
    n_roff, n_coff = 2 * NA_ROWS - 1, 2 * NA_COLS - 1
    c = lax.broadcasted_iota(jnp.int32, (GRID_W, LANES), 0)
    lane = lax.broadcasted_iota(jnp.int32, (GRID_W, LANES), 1)
    kc = jnp.bitwise_and(lane, GRID_W - 1)
    cs = jnp.clip(c - NA_COLS // 2, 0, GRID_W - NA_COLS)
    d = jnp.where((kc >= cs) & (kc < cs + NA_COLS), kc - c + (NA_COLS - 1), -1)

    def table(i):
        acc = jnp.full((GRID_W, LANES), NA_MASK, F32)
        base = (head * n_roff + i) * n_coff
        for j in range(n_coff):
            acc = jnp.where(d == j, rpb_ref[base + j], acc)
        return acc

    prev = table(0)
    for i in range(1, n_roff):
        cur = table(i)
        pair_s[i - 1] = jnp.where(lane < GRID_W, prev, cur)
        prev = cur


def _na_kernel(rpb_ref, q_ref, k_ref, v_ref, kc_ref, vc_ref, o_ref, s_s, m_s, pair_s):
    @pl.when(pl.program_id(1) == 0)
    def _():
        _na_build_bias(rpb_ref, pl.program_id(0), pair_s)

    n = q_ref.shape[0]
    rows = n // GRID_W
    kb = NA_ROWS * GRID_W
    ctx_len = kc_ref.shape[0]
    scale = NA_HEAD_DIM ** -0.5
    kc = kc_ref[...]
    vc = jnp.concatenate([vc_ref[...], jnp.ones((ctx_len, NA_HEAD_DIM), BF16)], axis=1)
    ones_w = jnp.ones((kb, NA_HEAD_DIM), BF16)

    def window(r):
        rs = jnp.clip(r - NA_ROWS // 2, 0, rows - NA_ROWS)
        return rs, pl.multiple_of(r * GRID_W, GRID_W), pl.multiple_of(rs * GRID_W, GRID_W)

    def score_row(r):
        rs, q0, k0 = window(r)
        q = (q_ref[pl.ds(q0, GRID_W), :].astype(F32) * scale).astype(BF16)
        off = rs - r + NA_ROWS - 1
        bias = jnp.concatenate([pair_s[off + 2 * t] for t in range(NA_ROWS // 2)], axis=1)
        s_win = _dot_nt(q, k_ref[pl.ds(k0, kb), :]) + bias
        s_ctx = _dot_nt(q, kc)
        m = jnp.maximum(jnp.max(s_win, axis=-1, keepdims=True), jnp.max(s_ctx, axis=-1, keepdims=True))
        s_s[pl.ds(q0, GRID_W), :kb] = s_win
        s_s[pl.ds(q0, GRID_W), kb:] = s_ctx
        m_s[pl.ds(q0, GRID_W), :] = jnp.broadcast_to(m, (GRID_W, LANES))

    def value_row(r):
        _, q0, k0 = window(r)
        m = m_s[pl.ds(q0, GRID_W), :]
        p_win = jnp.exp(s_s[pl.ds(q0, GRID_W), :kb] - jnp.tile(m, (1, kb // LANES))).astype(BF16)
        p_ctx = jnp.exp(s_s[pl.ds(q0, GRID_W), kb:] - jnp.tile(m, (1, ctx_len // LANES))).astype(BF16)
        vw = jnp.concatenate([v_ref[pl.ds(k0, kb), :], ones_w], axis=1)
        o = _dot(p_win, vw) + _dot(p_ctx, vc)
        o_ref[pl.ds(q0, GRID_W), :] = (o[:, :NA_HEAD_DIM] / o[:, NA_HEAD_DIM:]).astype(BF16)

    qrows = math.gcd(rows, NA_QROWS)

    def sweep(row_fn):
        def group(g, carry):
            for u in range(qrows):
                row_fn(g * qrows + u)
            return carry
        lax.fori_loop(0, rows // qrows, group, 0)

    sweep(score_row)
    sweep(value_row)


def _neighbourhood_attention(proj, cproj, rpb, batch, n, ctx_len):
    hd = NA_HEAD_DIM
    rows = n // GRID_W
    assert rows >= NA_ROWS
    assert rpb.shape == (NA_HEADS, 2 * NA_ROWS - 1, 2 * NA_COLS - 1)
    base = 4 * RET_WIDTH // hd
    col = lambda which: (lambda h, b, rpb_ref: (b, base + which * NA_HEADS + h))
    grid_spec = pltpu.PrefetchScalarGridSpec(
        num_scalar_prefetch=1,
        grid=(NA_HEADS, batch),
        in_specs=[
            pl.BlockSpec((n, hd), col(0)),
            pl.BlockSpec((n, hd), col(1)),
            pl.BlockSpec((n, hd), col(2)),
            pl.BlockSpec((ctx_len, hd), lambda h, b, rpb_ref: (b, CTX_NA_K * NA_HEADS + h)),
            pl.BlockSpec((ctx_len, hd), lambda h, b, rpb_ref: (b, CTX_NA_V * NA_HEADS + h)),
        ],
        out_specs=pl.BlockSpec((n, hd), lambda h, b, rpb_ref: (b, h)),
        scratch_shapes=[
            pltpu.VMEM((n, NA_ROWS * GRID_W + ctx_len), F32),
            pltpu.VMEM((n, LANES), F32),
            pltpu.VMEM((2 * NA_ROWS - 2, GRID_W, LANES), F32),
        ],
    )
    return pl.pallas_call(
        _na_kernel,
        grid_spec=grid_spec,
        out_shape=jax.ShapeDtypeStruct((batch * n, NA_WIDTH), BF16),
        compiler_params=_params("parallel", "arbitrary"),
        name="neighbourhood_attention",
    )(rpb.astype(F32).reshape(-1), proj, proj, proj, cproj, cproj)


def _outproj_kernel(n_groups, per_group, ret_ref, na_ref, w1_ref, w2_ref, x_ref, ga_ref, shf_ref,
                    scf_ref, nw_ref, wr_ref, br_ref, x1_ref, hf_ref, route_ref, cnt_ref, carry_s,
                    earlier_s):
    tm = x_ref.shape[0]

    @pl.when(pl.program_id(0) == 0)
    def _():
        carry_s[...] = jnp.zeros_like(carry_s)
        earlier_s[...] = jnp.where(lax.broadcasted_iota(jnp.int32, (tm, tm), 0)
                                   > lax.broadcasted_iota(jnp.int32, (tm, tm), 1), 1.0, 0.0).astype(BF16)

    acc = _dot(ret_ref[...], w1_ref[...]) + _dot(na_ref[...], w2_ref[...])
    x1 = x_ref[...] + ga_ref[0] * acc
    x1_ref[...] = x1
    hf = _rms(x1, nw_ref[...]) * (1.0 + scf_ref[0]) + shf_ref[0]
    _store_rows(hf_ref, hf)

    hf_hi = hf.astype(BF16)
    hf_lo = (hf - hf_hi.astype(F32)).astype(BF16)
    p_hi = _dot(hf_hi, wr_ref[...])
    p_lo = _dot(hf_lo, wr_ref[...])
    logits = p_hi[:, :LANES] + p_hi[:, LANES:] + p_lo[:, :LANES] + br_ref[...]
    lane = lax.broadcasted_iota(jnp.int32, (tm, LANES), 1)
    neg = -jnp.inf

    def first_max(vals):
        top = jnp.max(vals, axis=-1, keepdims=True)
        idx = jnp.min(jnp.where(vals == top, lane, LANES), axis=-1, keepdims=True)
        return top, idx

    g_logits = jnp.where(lane < n_groups, logits, neg)
    g_top, g_sel = first_max(g_logits)
    g_w = 1.0 / jnp.sum(jnp.exp(g_logits - g_top), axis=-1, keepdims=True)
    lo = n_groups + g_sel * per_group
    e_logits = jnp.where((lane >= lo) & (lane < lo + per_group), logits, neg)
    v0, i0 = first_max(e_logits)
    v1, i1 = first_max(jnp.where(lane == i0, neg, e_logits))
    e1 = jnp.exp(v1 - v0)
    w0 = g_w / (1.0 + e1)
    w1 = g_w * e1 / (1.0 + e1)

    hit0 = lane == i0
    hit1 = lane == i1
    onehot = jnp.where(hit0 | hit1, 1.0, 0.0)
    before = _dot(earlier_s[...], onehot.astype(BF16)) + carry_s[...]
    rank0 = jnp.sum(jnp.where(hit0, before, 0.0), axis=-1, keepdims=True)
    rank1 = jnp.sum(jnp.where(hit1, before, 0.0), axis=-1, keepdims=True)
    carry_s[...] = carry_s[...] + jnp.sum(onehot, axis=0, keepdims=True)
    cnt_ref[...] = carry_s[...]

    key0 = (i0 - n_groups).astype(F32) * KEY_RANK_SPAN + rank0
    key1 = (i1 - n_groups).astype(F32) * KEY_RANK_SPAN + rank1
    fields = {R_KEY0: key0, R_KEY1: key1, R_W0: w0, R_W1: w1}
    route = jnp.zeros((tm, LANES), F32)
    for idx, val in fields.items():
        route = jnp.where(lane == idx, val, route)
    route_ref[...] = route


def _out_projection(ret, na, w_out, x2d, mod3, norm_w, w_route, b_route, n, n_groups, per_group, tm):
    m, d = x2d.shape
    pitch = _row_pitch(d // LANES)
    batch_of = lambda i: (i * tm) // n
    mod_spec = lambda which: pl.BlockSpec((1, 1, d), lambda i: (batch_of(i) * N_MOD + which, 0, 0))
    const2 = lambda i: (0, 0)
    return pl.pallas_call(
        functools.partial(_outproj_kernel, n_groups, per_group),
        grid=(m // tm,),
        in_specs=[
            pl.BlockSpec((tm, RET_WIDTH), lambda i: (i, 0)),
            pl.BlockSpec((tm, NA_WIDTH), lambda i: (i, 0)),
            pl.BlockSpec((RET_WIDTH, d), lambda i: (0, 0)),
            pl.BlockSpec((NA_WIDTH, d), lambda i: (RET_WIDTH // NA_WIDTH, 0)),
            pl.BlockSpec((tm, d), lambda i: (i, 0)),
            mod_spec(2), mod_spec(3), mod_spec(4),
            pl.BlockSpec((1, d), const2),
            pl.BlockSpec((d, 2 * LANES), const2),
            pl.BlockSpec((1, LANES), const2),
        ],
        out_specs=[
            pl.BlockSpec((tm, d), lambda i: (i, 0)),
            pl.BlockSpec((tm * pitch, LANES), lambda i: (i, 0)),
            pl.BlockSpec((tm, LANES), lambda i: (i, 0)),
            pl.BlockSpec((1, LANES), const2),
        ],
        out_shape=[
            jax.ShapeDtypeStruct((m, d), F32),
            jax.ShapeDtypeStruct((m * pitch, LANES), F32),
            jax.ShapeDtypeStruct((m, LANES), F32),
            jax.ShapeDtypeStruct((1, LANES), F32),
        ],
        scratch_shapes=[pltpu.VMEM((1, LANES), F32), pltpu.VMEM((tm, tm), BF16)],
        compiler_params=_params("arbitrary"),
        name="out_projection_router",
    )(ret, na, w_out, w_out, x2d, mod3, mod3, mod3, norm_w.reshape(1, d), w_route, b_route)


def _slot_map_kernel(tile_rows, pos_ref, cnt_ref, slot_ref, tile_expert_ref, next_expert_ref,
                     run_parity_ref, n_used_ref, offs_s):
    n_experts = cnt_ref.shape[0]
    n_tiles = tile_expert_ref.shape[0]
    shift = tile_rows.bit_length() - 1
    assert 1 << shift == tile_rows

    def fill(ref, lo, hi, val):
        value_at = val if callable(val) else (lambda s: val)
        groups = lax.div(hi - lo, SUBLANES)

        def group(g, carry):
            for k in range(SUBLANES):
                s = lo + g * SUBLANES + k
                ref[s] = value_at(s)
            return carry

        def single(s, carry):
            ref[s] = value_at(s)
            return carry

        lax.fori_loop(0, groups, group, 0)
        lax.fori_loop(lo + groups * SUBLANES, hi, single, 0)

    assert tile_rows <= pos_ref.shape[0] // 2
    padding_row = lambda s: jnp.bitwise_and(s, tile_rows - 1)

    def tiles_of(e):
        size = lax.shift_left(lax.shift_right_logical(cnt_ref[e] + (tile_rows - 1), shift), shift)
        return lax.shift_right_logical(offs_s[e], shift), lax.shift_right_logical(offs_s[e] + size, shift)

    def per_expert(e, carry):
        start, runs = carry
        cnt = cnt_ref[e]
        size = lax.shift_left(lax.shift_right_logical(cnt + (tile_rows - 1), shift), shift)
        offs_s[e] = start
        t0, t1 = tiles_of(e)
        fill(tile_expert_ref, t0, t1, e)
        fill(run_parity_ref, t0, t1, jnp.bitwise_and(runs, 1))
        fill(slot_ref, start + cnt, start + size, padding_row)
        return start + size, runs + (size > 0).astype(jnp.int32)

    end, _ = lax.fori_loop(0, n_experts, per_expert, (0, 0))
    n_used = lax.shift_right_logical(end, shift)
    n_used_ref[0] = n_used
    fill(tile_expert_ref, n_used, n_tiles, n_experts - 1)
    fill(run_parity_ref, n_used, n_tiles, 0)
    fill(next_expert_ref, n_used, n_tiles, -1)
    fill(slot_ref, end, slot_ref.shape[0], padding_row)

    def per_expert_reversed(k, following):
        e = n_experts - 1 - k
        t0, t1 = tiles_of(e)
        fill(next_expert_ref, t0, t1, following)
        return jnp.where(t1 > t0, e, following)

    lax.fori_loop(0, n_experts, per_expert_reversed, -1)

    def assign(token, carry):
        slot_ref[pos_ref[2 * token]] = token
        slot_ref[pos_ref[2 * token + 1]] = token
        return carry

    lax.fori_loop(0, pos_ref.shape[0] // 2, assign, 0, unroll=SUBLANES)


def _slot_map(pos, cnt, n_tiles, tile_rows):
    smem = pl.BlockSpec(memory_space=pltpu.SMEM)
    i32 = lambda *shape: jax.ShapeDtypeStruct(shape, jnp.int32)
    return pl.pallas_call(
        functools.partial(_slot_map_kernel, tile_rows),
        in_specs=[smem, smem],
        out_specs=[smem] * 5,
        out_shape=[i32(n_tiles * tile_rows), i32(n_tiles), i32(n_tiles), i32(n_tiles), i32(1)],
        scratch_shapes=[pltpu.SMEM((cnt.shape[0],), jnp.int32)],
        name="slot_map",
    )(pos, cnt)


def _expert_kernel(tile_expert_ref, next_expert_ref, run_parity_ref, n_used_ref, tokens_ref,
                   next_tokens_ref, hf_ref, wg_hbm, wu_hbm, wd_hbm, o_ref, xbuf, gsem, wg_f, wu_f,
                   wd_f, wsem, wg_s, wu_s, wd_s):
    i = pl.program_id(0)
    last = pl.num_programs(0) - 1
    n_used = n_used_ref[0]
    d, ff = wg_s.shape
    chunks = d // LANES
    pitch = _row_pitch(chunks)
    tm = o_ref.shape[0] // pitch

    def row_copy(ids_ref, slot, j):
        return pltpu.make_async_copy(_row_ref(hf_ref, ids_ref[0, 0, j], chunks),
                                     _row_ref(xbuf.at[slot], j, chunks), gsem.at[slot])

    def weight_copies(expert, slot):
        return [pltpu.make_async_copy(hbm.at[expert], stage.at[slot], wsem.at[slot])
                for hbm, stage in ((wg_hbm, wg_f), (wu_hbm, wu_f), (wd_hbm, wd_f))]

    @pl.when((i == 0) & (n_used > 0))
    def _():
        _start_row_gather(lambda j: _row_ref(hf_ref, tokens_ref[0, 0, j], chunks), chunks, xbuf.at[0],
                          gsem.at[0], tm)
        for cp in weight_copies(tile_expert_ref[0], run_parity_ref[0]):
            cp.start(priority=WEIGHT_DMA_PRIORITY)

    @pl.when(i < n_used)
    def _():
        @pl.when((i == 0) | (tile_expert_ref[i] != tile_expert_ref[jnp.maximum(i - 1, 0)]))
        def _():
            stage = run_parity_ref[i]
            for cp in weight_copies(tile_expert_ref[i], stage):
                cp.wait()

            @pl.when(next_expert_ref[i] >= 0)
            def _():
                for cp in weight_copies(next_expert_ref[i], 1 - stage):
                    cp.start(priority=WEIGHT_DMA_PRIORITY)

            wg_s[...] = wg_f[stage].astype(BF16)
            wu_s[...] = wu_f[stage].astype(BF16)
            wd_s[...] = wd_f[stage].astype(BF16)

        slot = i % 2
        _wait_row_gather(xbuf.at[slot], gsem.at[slot], tm, chunks)
        x = _load_gathered(xbuf.at[slot], tm, chunks).astype(BF16)

        down_split = min(EXPERT_DOWN_SPLIT, chunks)
        n_stages = 2 + down_split
        bounds = [tm * s // n_stages for s in range(n_stages + 1)]

        def start_next_rows(stage):
            for j in range(bounds[stage], bounds[stage + 1]):
                row_copy(next_tokens_ref, 1 - slot, j).start()

        halves = []
        for half in range(2):
            start_next_rows(half)
            cols = slice(half * ff // 2, (half + 1) * ff // 2)
            halves.append((_silu(_dot(x, wg_s[:, cols])) * _dot(x, wu_s[:, cols])).astype(BF16))
        a = jnp.concatenate(halves, axis=1)
        per_part = chunks // down_split
        for part in range(down_split):
            start_next_rows(2 + part)
            y = _dot(a, wd_s[:, part * per_part * LANES:(part + 1) * per_part * LANES])
            for c in range(per_part):
                o_ref[pl.ds(part * per_part + c, tm, stride=pitch), :] = y[:, c * LANES:(c + 1) * LANES]
        o_ref[pl.ds(chunks, tm, stride=pitch), :] = jnp.zeros((tm, LANES), o_ref.dtype)

        @pl.when(i == last)
        def _():
            _wait_row_gather(xbuf.at[1 - slot], gsem.at[1 - slot], tm, chunks)

    @pl.when(i >= n_used)
    def _():
        @pl.when((i == n_used) & (i > 0))
        def _():
            _wait_row_gather(xbuf.at[i % 2], gsem.at[i % 2], tm, chunks)

        o_ref[...] = jnp.zeros(o_ref.shape, o_ref.dtype)


def _experts(tile_meta, slot_token, hf, w_gate, w_up, w_down, n_tiles, tm):
    d, ff = w_gate.shape[-2:]
    chunks = d // LANES
    pitch = _row_pitch(chunks)
    slot_tiles = slot_token.reshape(n_tiles, 1, tm)
    any_space = pl.BlockSpec(memory_space=pl.ANY)
    ids_block = lambda ahead: pl.BlockSpec(
        (1, 1, tm), lambda i, *prefetch: (jnp.minimum(i + ahead, n_tiles - 1), 0, 0),
        memory_space=pltpu.SMEM)
    grid_spec = pltpu.PrefetchScalarGridSpec(
        num_scalar_prefetch=4,
        grid=(n_tiles,),
        in_specs=[ids_block(0), ids_block(1), any_space, any_space, any_space, any_space],
        out_specs=pl.BlockSpec((tm * pitch, LANES), lambda i, *prefetch: (i, 0)),
        scratch_shapes=[
            pltpu.VMEM((2, tm * pitch, LANES), hf.dtype),
            pltpu.SemaphoreType.DMA((2,)),
            pltpu.VMEM((2, d, ff), F32),
            pltpu.VMEM((2, d, ff), F32),
            pltpu.VMEM((2, ff, d), F32),
            pltpu.SemaphoreType.DMA((2,)),
            pltpu.VMEM((d, ff), BF16),
            pltpu.VMEM((d, ff), BF16),
            pltpu.VMEM((ff, d), BF16),
        ],
    )
    return pl.pallas_call(
        _expert_kernel,
        grid_spec=grid_spec,
        out_shape=jax.ShapeDtypeStruct((n_tiles * tm * pitch, LANES), hf.dtype),
        compiler_params=_params("arbitrary"),
        name="routed_experts",
    )(*tile_meta, slot_tiles, slot_tiles, hf, w_gate, w_up, w_down)


def _combine_kernel(pos_ref, ys_ref, x1_ref, route_ref, gf_ref, fw_ref, o_ref, ybuf, sem):
    i = pl.program_id(0)
    n_tiles = pl.num_programs(0)
    tm, d = x1_ref.shape
    chunks = d // LANES

    def start_tile(tile, slot):
        for choice in range(2):
            _start_row_gather(lambda j: _row_ref(ys_ref, pos_ref[(tile * tm + j) * 2 + choice], chunks),
                              chunks, ybuf.at[slot, choice], sem.at[slot, choice], tm,
                              priority=GATHER_DMA_PRIORITY)

    @pl.when(i == 0)
    def _():
        start_tile(0, 0)

    @pl.when(i + 1 < n_tiles)
    def _():
        start_tile(i + 1, (i + 1) % 2)

    slot = i % 2
    for choice in range(2):
        _wait_row_gather(ybuf.at[slot, choice], sem.at[slot, choice], tm, chunks)
    route = route_ref[...]
    moe = (route[:, R_W0:R_W0 + 1] * _load_gathered(ybuf.at[slot, 0], tm, chunks)
           + route[:, R_W1:R_W1 + 1] * _load_gathered(ybuf.at[slot, 1], tm, chunks))
    x2 = x1_ref[...] + gf_ref[0] * moe
    o_ref[...] = _rms(x2, fw_ref[...])


def _combine(pos, ys, x1, route, mod3, final_w, n, tm):
    m, d = x1.shape
    chunks = d // LANES
    grid_spec = pltpu.PrefetchScalarGridSpec(
        num_scalar_prefetch=1,
        grid=(m // tm,),
        in_specs=[
            pl.BlockSpec(memory_space=pl.ANY),
            pl.BlockSpec((tm, d), lambda i, pos_ref: (i, 0)),
            pl.BlockSpec((tm, LANES), lambda i, pos_ref: (i, 0)),
            pl.BlockSpec((1, 1, d), lambda i, pos_ref: (((i * tm) // n) * N_MOD + 5, 0, 0)),
            pl.BlockSpec((1, d), lambda i, pos_ref: (0, 0)),
        ],
        out_specs=pl.BlockSpec((tm, d), lambda i, pos_ref: (i, 0)),
        scratch_shapes=[pltpu.VMEM((2, 2, tm * _row_pitch(chunks), LANES), ys.dtype),
                        pltpu.SemaphoreType.DMA((2, 2))],
    )
    return pl.pallas_call(
        _combine_kernel,
        grid_spec=grid_spec,
        out_shape=jax.ShapeDtypeStruct((m, d), F32),
        compiler_params=_params("arbitrary"),
        name="combine_final_norm",
    )(pos, ys, x1, route, mod3, final_w.reshape(1, d))


def kernel(x, c, ctx, c_ctx, w_mod, b_mod, norm_mix_w, w_in, ret_decay_f, ret_decay_b, ret_gn_w, na_rpb, w_out, norm_ffn_w, w_router_group, b_router_group, w_router_expert, b_router_expert, w_gate, w_up, w_down, final_norm_w):
    assert w_mod.shape[0] == 1, "single trunk layer"
    batch, n, d = x.shape
    ctx_len = ctx.shape[1]
    n_groups = w_router_group.shape[-1]
    per_group = w_router_expert.shape[-1]
    n_experts = w_gate.shape[1]
    assert n_groups * per_group == n_experts and n_groups + n_experts <= LANES

    mod_rows = 8
    cc = jnp.zeros((mod_rows, d), F32).at[:batch].set(c).at[batch].set(c_ctx)
    mod = _modulation(cc, w_mod[0], b_mod[0])
    mod3 = mod.reshape(mod_rows * N_MOD, 1, d)

    w_in_b = w_in[0]
    tm = min(512, n)
    tm_in = min(1024, n)
    x2d = x.reshape(batch * n, d)
    proj = _in_projection(x2d, mod3, lambda i: (i * tm_in) // n, norm_mix_w[0], w_in_b, tm_in)
    cproj = _in_projection(ctx.reshape(batch * ctx_len, d), mod3, lambda i: batch, norm_mix_w[0],
                           w_in_b, batch * ctx_len, CTX_COL_BLOCKS)

    lg = jnp.stack([jax.nn.log_sigmoid(ret_decay_f[0].astype(F32)),
                    jax.nn.log_sigmoid(ret_decay_b[0].astype(F32))])
    ret = _retention(proj, cproj, lg, ret_gn_w[0], batch, n, ctx_len)
    na = _neighbourhood_attention(proj, cproj, na_rpb[0], batch, n, ctx_len)

    w_route = jnp.concatenate(
        [w_router_group[0], jnp.moveaxis(w_router_expert[0], 0, 1).reshape(d, n_experts)], axis=1)
    w_route = jnp.pad(w_route.astype(F32), ((0, 0), (0, LANES - n_groups - n_experts)))
    w_route_hi = w_route.astype(BF16)
    w_route = jnp.concatenate([w_route_hi, (w_route - w_route_hi.astype(F32)).astype(BF16)], axis=1)
    b_route = jnp.concatenate([b_router_group[0], b_router_expert[0].reshape(-1)])
    b_route = jnp.pad(b_route.astype(F32), (0, LANES - n_groups - n_experts)).reshape(1, LANES)
    x1, hf, route, counts = _out_projection(ret, na, w_out[0].astype(BF16), x2d, mod3, norm_ffn_w[0],
                                            w_route, b_route, n, n_groups, per_group, tm)

    te = EXPERT_TILE
    tokens = batch * n
    assert tokens < 1 << KEY_RANK_BITS and (2 * tokens) % te == 0
    n_tiles = (2 * tokens) // te + n_experts
    cnt = counts[0, n_groups:n_groups + n_experts].astype(jnp.int32)
    padded = (cnt + te - 1) // te * te
    starts = jnp.cumsum(padded) - padded
    keys = route[:, R_KEY0:R_KEY1 + 1].astype(jnp.int32).reshape(-1)
    expert = lax.shift_right_logical(keys, KEY_RANK_BITS)
    start_of = jnp.sum(jnp.where(expert[:, None] == jnp.arange(n_experts)[None, :], starts[None, :], 0), axis=1)
    pos = start_of + jnp.bitwise_and(keys, (1 << KEY_RANK_BITS) - 1)
    slot_token, *tile_meta = _slot_map(pos, cnt, n_tiles, te)

    ys = _experts(tile_meta, slot_token, hf, w_gate[0], w_up[0], w_down[0], n_tiles, te)
    out = _combine(pos, ys, x1, route, mod3, final_norm_w, n, min(256, n))
    return out.reshape(batch, n, d)
```

```python
import functools
import math

import jax
import jax.numpy as jnp
from jax import lax
from jax.experimental import pallas as pl
from jax.experimental.pallas import tpu as pltpu

F32 = jnp.float32
BF16 = jnp.bfloat16

GRID_W = 64
RET_HEADS = 4
RET_HEAD_DIM = 256
RET_WIDTH = RET_HEADS * RET_HEAD_DIM
NA_HEADS = 8
NA_HEAD_DIM = 128
NA_WIDTH = NA_HEADS * NA_HEAD_DIM
RET_CHUNK = 128
NA_ROWS = 8
NA_COLS = 16
ROPE_BASE = 10000.0
N_MOD = 6
NORM_EPS = 1e-6
ROPE_HALF = RET_HEAD_DIM // 2
RET_UNROLL = 8

INPROJ_TN = 1024
assert INPROJ_TN == RET_WIDTH == NA_WIDTH
CTX_COL_BLOCKS = (1, 2, 5, 6)
CTX_RET_K, CTX_RET_V, CTX_NA_K, CTX_NA_V = 0, 1, 2, 3

NA_QROWS = 16
NA_MASK = -1e30

LANES = 128
SUBLANES = 8
BF16_ROWS = 2 * SUBLANES
VMEM_LIMIT_BYTES = 56 * 1024 * 1024

R_KEY0, R_KEY1, R_W0, R_W1 = 0, 1, 2, 3
KEY_RANK_BITS = 16
KEY_RANK_SPAN = float(1 << KEY_RANK_BITS)

EXPERT_TILE = 256
WEIGHT_DMA_PRIORITY = 1
EXPERT_DOWN_SPLIT = 4


def _params(*sem):
    return pltpu.CompilerParams(dimension_semantics=sem, vmem_limit_bytes=VMEM_LIMIT_BYTES)


def _dot(a, b):
    return jnp.dot(a, b, preferred_element_type=F32)


def _dot_nt(a, b):
    return lax.dot_general(a, b, (((1,), (1,)), ((), ())), preferred_element_type=F32)


def _dot_tn(a, b):
    return lax.dot_general(a, b, (((0,), (0,)), ((), ())), preferred_element_type=F32)


def _rms(x, w):
    return x * lax.rsqrt(jnp.mean(x * x, axis=-1, keepdims=True) + NORM_EPS) * w


def _silu(x):
    return x * jax.nn.sigmoid(x)


def _row_pitch(chunks):
    return chunks + 1


def _store_rows(ref, val):
    n_rows, d = val.shape
    chunks = d // LANES
    pitch = _row_pitch(chunks)
    for c in range(chunks):
        ref[pl.ds(c, n_rows, stride=pitch), :] = val[:, c * LANES:(c + 1) * LANES]
    ref[pl.ds(chunks, n_rows, stride=pitch), :] = jnp.zeros((n_rows, LANES), ref.dtype)


def _row_ref(hbm, row, chunks):
    return hbm.at[pl.ds(row * _row_pitch(chunks), chunks), :]


def _start_row_gather(src_row, chunks, dst_buf, sem, n_rows):
    def body(j, carry):
        pltpu.make_async_copy(src_row(j), _row_ref(dst_buf, j, chunks), sem).start()
        return carry

    lax.fori_loop(0, n_rows, body, 0, unroll=SUBLANES)


def _wait_row_gather(dst_buf, sem, n_rows, chunks):
    view = dst_buf.at[pl.ds(0, n_rows * chunks), :]
    pltpu.make_async_copy(view, view, sem).wait()


def _load_gathered(buf, n_rows, chunks):
    return jnp.concatenate([buf[pl.ds(c, n_rows, stride=_row_pitch(chunks)), :] for c in range(chunks)],
                           axis=1)


def _mod_kernel(c_ref, w_ref, b_ref, o_ref):
    a = _silu(c_ref[...]).astype(BF16)
    o_ref[...] = _dot(a, w_ref[...].astype(BF16)) + b_ref[...]


def _modulation(cc, w_mod, b_mod):
    rows, d = cc.shape
    width = w_mod.shape[1]
    tn = next(t for t in (1024, 512, 256, LANES) if width % t == 0)
    return pl.pallas_call(
        _mod_kernel,
        grid=(width // tn,),
        in_specs=[
            pl.BlockSpec((rows, d), lambda j: (0, 0)),
            pl.BlockSpec((d, tn), lambda j: (0, j)),
            pl.BlockSpec((1, tn), lambda j: (0, j)),
        ],
        out_specs=pl.BlockSpec((rows, tn), lambda j: (0, j)),
        out_shape=jax.ShapeDtypeStruct((rows, width), F32),
        compiler_params=_params("arbitrary"),
        name="modulation",
    )(cc, w_mod, b_mod.reshape(1, width))


def _inproj_kernel(rows_per_step, x_ref, sh_ref, sc_ref, nw_ref, w_ref, o_ref, h_s):
    i = pl.program_id(0)
    j = pl.program_id(1)
    tm = o_ref.shape[0]

    def normalise(rows, slot):
        y = _rms(x_ref[rows, :], nw_ref[...])
        h_s[slot, rows, :] = (y * (1.0 + sc_ref[0]) + sh_ref[0]).astype(BF16)

    @pl.when((i == 0) & (j == 0))
    def _():
        normalise(pl.ds(0, tm), 0)

    o_ref[...] = _dot(h_s[i % 2], w_ref[...].astype(BF16)).astype(BF16)
    start = jnp.clip((j - 1) * rows_per_step, 0, tm - rows_per_step)
    normalise(pl.ds(pl.multiple_of(start, BF16_ROWS), rows_per_step), (i + 1) % 2)


def _in_projection(x2d, mod3, mod_row_of_tile, norm_w, w_in, tm, col_blocks=None):
    m, d = x2d.shape
    tn = INPROJ_TN
    if col_blocks is None:
        col_blocks = tuple(range(w_in.shape[1] // tn))
    n_blocks = len(col_blocks)
    n_tiles = m // tm
    assert n_blocks >= 2 and tm % BF16_ROWS == 0
    rows_per_step = -(-tm // ((n_blocks - 1) * BF16_ROWS)) * BF16_ROWS
    w_block = lambda j: sum(jnp.where(j == k, blk, 0) for k, blk in enumerate(col_blocks))
    src = lambda i, j: jnp.where((i == 0) & (j == 0), 0, jnp.minimum(i + 1, n_tiles - 1))
    return pl.pallas_call(
        functools.partial(_inproj_kernel, rows_per_step),
        grid=(n_tiles, n_blocks),
        in_specs=[
            pl.BlockSpec((tm, d), lambda i, j: (src(i, j), 0)),
            pl.BlockSpec((1, 1, d), lambda i, j: (mod_row_of_tile(src(i, j)) * N_MOD + 0, 0, 0)),
            pl.BlockSpec((1, 1, d), lambda i, j: (mod_row_of_tile(src(i, j)) * N_MOD + 1, 0, 0)),
            pl.BlockSpec((1, d), lambda i, j: (0, 0)),
            pl.BlockSpec((d, tn), lambda i, j: (0, w_block(j))),
        ],
        out_specs=pl.BlockSpec((tm, tn), lambda i, j: (i, j)),
        out_shape=jax.ShapeDtypeStruct((m, n_blocks * tn), BF16),
        scratch_shapes=[pltpu.VMEM((2, tm, d), BF16)],
        compiler_params=_params("arbitrary", "arbitrary"),
        name="in_projection",
    )(x2d, mod3, mod3, norm_w.reshape(1, d), w_in)


def _ret_kernel(lg_ref, q_ref, k_ref, v_ref, g_ref, ck_ref, cv_ref, cosr_ref, sinr_ref,
                cosc_ref, sinc_ref, gnw_ref, o_ref, qr_s, kr_s, sfh_s, sbh_s, sf_s, sb_s):
    head = pl.program_id(1)
    lgf = lg_ref[0, head]
    lgb = lg_ref[1, head]
    n = q_ref.shape[0]
    c = RET_CHUNK
    nc = n // c
    ctx_len = ck_ref.shape[0]
    k_scale = RET_HEAD_DIM ** -0.5

    posl = lax.broadcasted_iota(jnp.int32, (ctx_len, 1), 0).astype(F32)
    ck = ck_ref[...].astype(F32) * k_scale
    cv = cv_ref[...]
    sf_s[...] = _dot_tn((ck * jnp.exp(lgf * (ctx_len - 1.0 - posl))).astype(BF16), cv)
    sb_s[...] = _dot_tn((ck * jnp.exp(lgb * posl)).astype(BF16), cv)

    cosc = cosc_ref[...]
    sinc = sinc_ref[...]
    pos = lax.broadcasted_iota(jnp.int32, (c, 1), 0).astype(F32)
    qdec_f = jnp.exp(lgf * (pos + 1.0))
    kdec_f = jnp.exp(lgf * (c - 1.0 - pos))
    cdec_f = jnp.exp(lgf * c)
    qdec_b = jnp.exp(lgb * (c - pos))
    kdec_b = jnp.exp(lgb * pos)
    cdec_b = jnp.exp(lgb * c)

    def rope(x, cosr, sinr):
        xa = x[:, :ROPE_HALF]
        xb = x[:, ROPE_HALF:]
        ya = xa * cosr + pltpu.roll(xa, ROPE_HALF // 2, 1) * sinr
        yb = xb * cosc + pltpu.roll(xb, ROPE_HALF // 2, 1) * sinc
        return jnp.concatenate([ya, yb], axis=1)

    def fwd_chunk(ci, carry):
        r0 = pl.multiple_of(ci * c, c)
        cosr = cosr_ref[pl.ds(r0, c), :]
        sinr = sinr_ref[pl.ds(r0, c), :]
        qr_s[pl.ds(r0, c), :] = rope(q_ref[pl.ds(r0, c), :].astype(F32), cosr, sinr).astype(BF16)
        k = rope(k_ref[pl.ds(r0, c), :].astype(F32), cosr, sinr) * k_scale
        kr_s[pl.ds(r0, c), :] = k.astype(BF16)
        kv = _dot_tn((k * kdec_f).astype(BF16), v_ref[pl.ds(r0, c), :])
        state = sf_s[...]
        sfh_s[ci] = state.astype(BF16)
        sf_s[...] = state * cdec_f + kv
        return carry

    lax.fori_loop(0, nc, fwd_chunk, 0, unroll=RET_UNROLL)

    def bwd_chunk(i, carry):
        ci = nc - 1 - i
        r0 = pl.multiple_of(ci * c, c)
        k = kr_s[pl.ds(r0, c), :].astype(F32)
        kv = _dot_tn((k * kdec_b).astype(BF16), v_ref[pl.ds(r0, c), :])
        state = sb_s[...]
        sbh_s[ci] = state.astype(BF16)
        sb_s[...] = state * cdec_b + kv
        return carry

    lax.fori_loop(0, nc, bwd_chunk, 0, unroll=RET_UNROLL)

    diff = (lax.broadcasted_iota(jnp.int32, (c, c), 0)
            - lax.broadcasted_iota(jnp.int32, (c, c), 1)).astype(F32)
    intra = (jnp.where(diff >= 0, jnp.exp(lgf * jnp.maximum(diff, 0.0)), 0.0)
             + jnp.where(diff <= 0, jnp.exp(lgb * jnp.maximum(-diff, 0.0)), 0.0))
    gnw = gnw_ref[...]

    def out_chunk(ci, carry):
        r0 = pl.multiple_of(ci * c, c)
        qb = qr_s[pl.ds(r0, c), :]
        kb = kr_s[pl.ds(r0, c), :]
        q = qb.astype(F32)
        scores = _dot_nt(qb, kb) * intra
        o = (_dot(scores.astype(BF16), v_ref[pl.ds(r0, c), :])
             + _dot((q * qdec_f).astype(BF16), sfh_s[ci])
             + _dot((q * qdec_b).astype(BF16), sbh_s[ci]))
        mu = jnp.mean(o, axis=-1, keepdims=True)
        d = o - mu
        var = jnp.mean(d * d, axis=-1, keepdims=True)
        on = d * lax.rsqrt(var + NORM_EPS) * gnw
        gate = _silu(g_ref[pl.ds(r0, c), :].astype(F32))
        o_ref[pl.ds(r0, c), :] = (on * gate).astype(BF16)
        return carry

    lax.fori_loop(0, nc, out_chunk, 0, unroll=RET_UNROLL)


def _rope_tables(n):
    inv = ROPE_BASE ** (-jnp.arange(0, ROPE_HALF, 2, dtype=F32) / ROPE_HALF)

    def tables(pos):
        ang = pos.astype(F32)[:, None] * inv[None, :]
        cos = jnp.cos(ang)
        sin = jnp.sin(ang)
        return jnp.concatenate([cos, cos], axis=1), jnp.concatenate([-sin, sin], axis=1)

    t = jnp.arange(n)
    cosr, sinr = tables(t // GRID_W)
    cosc, sinc = tables(jnp.arange(RET_CHUNK) % GRID_W)
    return cosr, sinr, cosc, sinc


def _retention(proj, cproj, lg, gn_w, batch, n, ctx_len):
    hd = RET_HEAD_DIM
    cosr, sinr, cosc, sinc = _rope_tables(n)
    col = lambda which: (lambda b, h, lg_ref: (b, which * RET_HEADS + h))
    const = lambda b, h, lg_ref: (0, 0)
    grid_spec = pltpu.PrefetchScalarGridSpec(
        num_scalar_prefetch=1,
        grid=(batch, RET_HEADS),
        in_specs=[
            pl.BlockSpec((n, hd), col(0)),
            pl.BlockSpec((n, hd), col(1)),
            pl.BlockSpec((n, hd), col(2)),
            pl.BlockSpec((n, hd), col(3)),
            pl.BlockSpec((ctx_len, hd), lambda b, h, lg_ref: (b, CTX_RET_K * RET_HEADS + h)),
            pl.BlockSpec((ctx_len, hd), lambda b, h, lg_ref: (b, CTX_RET_V * RET_HEADS + h)),
            pl.BlockSpec((n, ROPE_HALF), const),
            pl.BlockSpec((n, ROPE_HALF), const),
            pl.BlockSpec((RET_CHUNK, ROPE_HALF), const),
            pl.BlockSpec((RET_CHUNK, ROPE_HALF), const),
            pl.BlockSpec((1, hd), lambda b, h, lg_ref: (0, h)),
        ],
        out_specs=pl.BlockSpec((n, hd), lambda b, h, lg_ref: (b, h)),
        scratch_shapes=[
            pltpu.VMEM((n, hd), BF16),
            pltpu.VMEM((n, hd), BF16),
            pltpu.VMEM((n // RET_CHUNK, hd, hd), BF16),
            pltpu.VMEM((n // RET_CHUNK, hd, hd), BF16),
            pltpu.VMEM((hd, hd), F32),
            pltpu.VMEM((hd, hd), F32),
        ],
    )
    return pl.pallas_call(
        _ret_kernel,
        grid_spec=grid_spec,
        out_shape=jax.ShapeDtypeStruct((batch * n, RET_WIDTH), BF16),
        compiler_params=_params("parallel", "arbitrary"),
        name="retention",
    )(lg, proj, proj, proj, proj, cproj, cproj, cosr, sinr, cosc, sinc, gn_w.reshape(1, RET_WIDTH))


def _na_build_bias(rpb_ref, head, pair_s):
    n_roff, n_coff = 2 * NA_ROWS - 1, 2 * NA_COLS - 1
    c = lax.broadcasted_iota(jnp.int32, (GRID_W, LANES), 0)
    lane = lax.broadcasted_iota(jnp.int32, (GRID_W, LANES), 1)
    kc = jnp.bitwise_and(lane, GRID_W - 1)
    cs = jnp.clip(c - NA_COLS // 2, 0, GRID_W - NA_COLS)
    d = jnp.where((kc >= cs) & (kc < cs + NA_COLS), kc - c + (NA_COLS - 1), -1)

    def table(i):
        acc = jnp.full((GRID_W, LANES), NA_MASK, F32)
        base = (head * n_roff + i) * n_coff
        for j in range(n_coff):
            acc = jnp.where(d == j, rpb_ref[base + j], acc)
        return acc

    prev = table(0)
    for i in range(1, n_roff):
        cur = table(i)
        pair_s[i - 1] = jnp.where(lane < GRID_W, prev, cur)
        prev = cur


def _na_kernel(rpb_ref, q_ref, k_ref, v_ref, kc_ref, vc_ref, o_ref, s_s, m_s, pair_s):
    @pl.when(pl.program_id(1) == 0)
    def _():
        _na_build_bias(rpb_ref, pl.program_id(0), pair_s)

    n = q_ref.shape[0]
    rows = n // GRID_W
    kb = NA_ROWS * GRID_W
    ctx_len = kc_ref.shape[0]
    scale = NA_HEAD_DIM ** -0.5
    kc = kc_ref[...]
    vc = jnp.concatenate([vc_ref[...], jnp.ones((ctx_len, NA_HEAD_DIM), BF16)], axis=1)
    ones_w = jnp.ones((kb, NA_HEAD_DIM), BF16)

    def window(r):
        rs = jnp.clip(r - NA_ROWS // 2, 0, rows - NA_ROWS)
        return rs, pl.multiple_of(r * GRID_W, GRID_W), pl.multiple_of(rs * GRID_W, GRID_W)

    def score_row(r):
        rs, q0, k0 = window(r)
        q = (q_ref[pl.ds(q0, GRID_W), :].astype(F32) * scale).astype(BF16)
        off = rs - r + NA_ROWS - 1
        bias = jnp.concatenate([pair_s[off + 2 * t] for t in range(NA_ROWS // 2)], axis=1)
        s_win = _dot_nt(q, k_ref[pl.ds(k0, kb), :]) + bias
        s_ctx = _dot_nt(q, kc)
        m = jnp.maximum(jnp.max(s_win, axis=-1, keepdims=True), jnp.max(s_ctx, axis=-1, keepdims=True))
        s_s[pl.ds(q0, GRID_W), :kb] = s_win
        s_s[pl.ds(q0, GRID_W), kb:] = s_ctx
        m_s[pl.ds(q0, GRID_W), :] = jnp.broadcast_to(m, (GRID_W, LANES))

    def value_row(r):
        _, q0, k0 = window(r)
        m = m_s[pl.ds(q0, GRID_W), :]
        p_win = jnp.exp(s_s[pl.ds(q0, GRID_W), :kb] - jnp.tile(m, (1, kb // LANES))).astype(BF16)
        p_ctx = jnp.exp(s_s[pl.ds(q0, GRID_W), kb:] - jnp.tile(m, (1, ctx_len // LANES))).astype(BF16)
        vw = jnp.concatenate([v_ref[pl.ds(k0, kb), :], ones_w], axis=1)
        o = _dot(p_win, vw) + _dot(p_ctx, vc)
        o_ref[pl.ds(q0, GRID_W), :] = (o[:, :NA_HEAD_DIM] / o[:, NA_HEAD_DIM:]).astype(BF16)

    qrows = math.gcd(rows, NA_QROWS)

    def sweep(row_fn):
        def group(g, carry):
            for u in range(qrows):
                row_fn(g * qrows + u)
            return carry
        lax.fori_loop(0, rows // qrows, group, 0)

    sweep(score_row)
    sweep(value_row)


def _neighbourhood_attention(proj, cproj, rpb, batch, n, ctx_len):
    hd = NA_HEAD_DIM
    rows = n // GRID_W
    assert rows >= NA_ROWS
    assert rpb.shape == (NA_HEADS, 2 * NA_ROWS - 1, 2 * NA_COLS - 1)
    base = 4 * RET_WIDTH // hd
    col = lambda which: (lambda h, b, rpb_ref: (b, base + which * NA_HEADS + h))
    grid_spec = pltpu.PrefetchScalarGridSpec(
        num_scalar_prefetch=1,
        grid=(NA_HEADS, batch),
        in_specs=[
            pl.BlockSpec((n, hd), col(0)),
            pl.BlockSpec((n, hd), col(1)),
            pl.BlockSpec((n, hd), col(2)),
            pl.BlockSpec((ctx_len, hd), lambda h, b, rpb_ref: (b, CTX_NA_K * NA_HEADS + h)),
            pl.BlockSpec((ctx_len, hd), lambda h, b, rpb_ref: (b, CTX_NA_V * NA_HEADS + h)),
        ],
        out_specs=pl.BlockSpec((n, hd), lambda h, b, rpb_ref: (b, h)),
        scratch_shapes=[
            pltpu.VMEM((n, NA_ROWS * GRID_W + ctx_len), F32),
            pltpu.VMEM((n, LANES), F32),
            pltpu.VMEM((2 * NA_ROWS - 2, GRID_W, LANES), F32),
        ],
    )
    return pl.pallas_call(
        _na_kernel,
        grid_spec=grid_spec,
        out_shape=jax.ShapeDtypeStruct((batch * n, NA_WIDTH), BF16),
        compiler_params=_params("parallel", "arbitrary"),
        name="neighbourhood_attention",
    )(rpb.astype(F32).reshape(-1), proj, proj, proj, cproj, cproj)


def _outproj_kernel(n_groups, per_group, ret_ref, na_ref, w1_ref, w2_ref, x_ref, ga_ref, shf_ref,
                    scf_ref, nw_ref, wr_ref, br_ref, x1_ref, hf_ref, route_ref, cnt_ref, carry_s,
                    earlier_s):
    tm = x_ref.shape[0]

    @pl.when(pl.program_id(0) == 0)
    def _():
        carry_s[...] = jnp.zeros_like(carry_s)
        earlier_s[...] = jnp.where(lax.broadcasted_iota(jnp.int32, (tm, tm), 0)
                                   > lax.broadcasted_iota(jnp.int32, (tm, tm), 1), 1.0, 0.0).astype(BF16)

    acc = _dot(ret_ref[...], w1_ref[...]) + _dot(na_ref[...], w2_ref[...])
    x1 = x_ref[...] + ga_ref[0] * acc
    x1_ref[...] = x1
    hf = _rms(x1, nw_ref[...]) * (1.0 + scf_ref[0]) + shf_ref[0]
    _store_rows(hf_ref, hf)

    hf_hi = hf.astype(BF16)
    hf_lo = (hf - hf_hi.astype(F32)).astype(BF16)
    p_hi = _dot(hf_hi, wr_ref[...])
    p_lo = _dot(hf_lo, wr_ref[...])
    logits = p_hi[:, :LANES] + p_hi[:, LANES:] + p_lo[:, :LANES] + br_ref[...]
    lane = lax.broadcasted_iota(jnp.int32, (tm, LANES), 1)
    neg = -jnp.inf

    def first_max(vals):
        top = jnp.max(vals, axis=-1, keepdims=True)
        idx = jnp.min(jnp.where(vals == top, lane, LANES), axis=-1, keepdims=True)
        return top, idx

    g_logits = jnp.where(lane < n_groups, logits, neg)
    g_top, g_sel = first_max(g_logits)
    g_w = 1.0 / jnp.sum(jnp.exp(g_logits - g_top), axis=-1, keepdims=True)
    lo = n_groups + g_sel * per_group
    e_logits = jnp.where((lane >= lo) & (lane < lo + per_group), logits, neg)
    v0, i0 = first_max(e_logits)
    v1, i1 = first_max(jnp.where(lane == i0, neg, e_logits))
    e1 = jnp.exp(v1 - v0)
    w0 = g_w / (1.0 + e1)
    w1 = g_w * e1 / (1.0 + e1)

    hit0 = lane == i0
    hit1 = lane == i1
    onehot = jnp.where(hit0 | hit1, 1.0, 0.0)
    before = _dot(earlier_s[...], onehot.astype(BF16)) + carry_s[...]
    rank0 = jnp.sum(jnp.where(hit0, before, 0.0), axis=-1, keepdims=True)
    rank1 = jnp.sum(jnp.where(hit1, before, 0.0), axis=-1, keepdims=True)
    carry_s[...] = carry_s[...] + jnp.sum(onehot, axis=0, keepdims=True)
    cnt_ref[...] = carry_s[...]

    key0 = (i0 - n_groups).astype(F32) * KEY_RANK_SPAN + rank0
    key1 = (i1 - n_groups).astype(F32) * KEY_RANK_SPAN + rank1
    fields = {R_KEY0: key0, R_KEY1: key1, R_W0: w0, R_W1: w1}
    route = jnp.zeros((tm, LANES), F32)
    for idx, val in fields.items():
        route = jnp.where(lane == idx, val, route)
    route_ref[...] = route


def _out_projection(ret, na, w_out, x2d, mod3, norm_w, w_route, b_route, n, n_groups, per_group, tm):
    m, d = x2d.shape
    pitch = _row_pitch(d // LANES)
    batch_of = lambda i: (i * tm) // n
    mod_spec = lambda which: pl.BlockSpec((1, 1, d), lambda i: (batch_of(i) * N_MOD + which, 0, 0))
    const2 = lambda i: (0, 0)
    return pl.pallas_call(
        functools.partial(_outproj_kernel, n_groups, per_group),
        grid=(m // tm,),
        in_specs=[
            pl.BlockSpec((tm, RET_WIDTH), lambda i: (i, 0)),
            pl.BlockSpec((tm, NA_WIDTH), lambda i: (i, 0)),
            pl.BlockSpec((RET_WIDTH, d), lambda i: (0, 0)),
            pl.BlockSpec((NA_WIDTH, d), lambda i: (RET_WIDTH // NA_WIDTH, 0)),
            pl.BlockSpec((tm, d), lambda i: (i, 0)),
            mod_spec(2), mod_spec(3), mod_spec(4),
            pl.BlockSpec((1, d), const2),
            pl.BlockSpec((d, 2 * LANES), const2),
            pl.BlockSpec((1, LANES), const2),
        ],
        out_specs=[
            pl.BlockSpec((tm, d), lambda i: (i, 0)),
            pl.BlockSpec((tm * pitch, LANES), lambda i: (i, 0)),
            pl.BlockSpec((tm, LANES), lambda i: (i, 0)),
            pl.BlockSpec((1, LANES), const2),
        ],
        out_shape=[
            jax.ShapeDtypeStruct((m, d), F32),
            jax.ShapeDtypeStruct((m * pitch, LANES), F32),
            jax.ShapeDtypeStruct((m, LANES), F32),
            jax.ShapeDtypeStruct((1, LANES), F32),
        ],
        scratch_shapes=[pltpu.VMEM((1, LANES), F32), pltpu.VMEM((tm, tm), BF16)],
        compiler_params=_params("arbitrary"),
        name="out_projection_router",
    )(ret, na, w_out, w_out, x2d, mod3, mod3, mod3, norm_w.reshape(1, d), w_route, b_route)


def _slot_map_kernel(tile_rows, pos_ref, cnt_ref, slot_ref, tile_expert_ref, next_expert_ref,
                     run_parity_ref, n_used_ref, offs_s):
    n_experts = cnt_ref.shape[0]
    n_tiles = tile_expert_ref.shape[0]
    shift = tile_rows.bit_length() - 1
    assert 1 << shift == tile_rows

    def fill(ref, lo, hi, val):
        value_at = val if callable(val) else (lambda s: val)
        groups = lax.div(hi - lo, SUBLANES)

        def group(g, carry):
            for k in range(SUBLANES):
                s = lo + g * SUBLANES + k
                ref[s] = value_at(s)
            return carry

        def single(s, carry):
            ref[s] = value_at(s)
            return carry

        lax.fori_loop(0, groups, group, 0)
        lax.fori_loop(lo + groups * SUBLANES, hi, single, 0)

    assert tile_rows <= pos_ref.shape[0] // 2
    padding_row = lambda s: jnp.bitwise_and(s, tile_rows - 1)

    def tiles_of(e):
        size = lax.shift_left(lax.shift_right_logical(cnt_ref[e] + (tile_rows - 1), shift), shift)
        return lax.shift_right_logical(offs_s[e], shift), lax.shift_right_logical(offs_s[e] + size, shift)

    def per_expert(e, carry):
        start, runs = carry
        cnt = cnt_ref[e]
        size = lax.shift_left(lax.shift_right_logical(cnt + (tile_rows - 1), shift), shift)
        offs_s[e] = start
        t0, t1 = tiles_of(e)
        fill(tile_expert_ref, t0, t1, e)
        fill(run_parity_ref, t0, t1, jnp.bitwise_and(runs, 1))
        fill(slot_ref, start + cnt, start + size, padding_row)
        return start + size, runs + (size > 0).astype(jnp.int32)

    end, _ = lax.fori_loop(0, n_experts, per_expert, (0, 0))
    n_used = lax.shift_right_logical(end, shift)
    n_used_ref[0] = n_used
    fill(tile_expert_ref, n_used, n_tiles, n_experts - 1)
    fill(run_parity_ref, n_used, n_tiles, 0)
    fill(next_expert_ref, n_used, n_tiles, -1)
    fill(slot_ref, end, slot_ref.shape[0], padding_row)

    def per_expert_reversed(k, following):
        e = n_experts - 1 - k
        t0, t1 = tiles_of(e)
        fill(next_expert_ref, t0, t1, following)
        return jnp.where(t1 > t0, e, following)

    lax.fori_loop(0, n_experts, per_expert_reversed, -1)

    def assign(token, carry):
        slot_ref[pos_ref[2 * token]] = token
        slot_ref[pos_ref[2 * token + 1]] = token
        return carry

    lax.fori_loop(0, pos_ref.shape[0] // 2, assign, 0, unroll=SUBLANES)


def _slot_map(pos, cnt, n_tiles, tile_rows):
    smem = pl.BlockSpec(memory_space=pltpu.SMEM)
    i32 = lambda *shape: jax.ShapeDtypeStruct(shape, jnp.int32)
    return pl.pallas_call(
        functools.partial(_slot_map_kernel, tile_rows),
        in_specs=[smem, smem],
        out_specs=[smem] * 5,
        out_shape=[i32(n_tiles * tile_rows), i32(n_tiles), i32(n_tiles), i32(n_tiles), i32(1)],
        scratch_shapes=[pltpu.SMEM((cnt.shape[0],), jnp.int32)],
        name="slot_map",
    )(pos, cnt)


def _expert_kernel(tile_expert_ref, next_expert_ref, run_parity_ref, n_used_ref, tokens_ref,
                   next_tokens_ref, hf_ref, wg_hbm, wu_hbm, wd_hbm, o_ref, xbuf, gsem, wg_f, wu_f,
                   wd_f, wsem, wg_s, wu_s, wd_s):
    i = pl.program_id(0)
    last = pl.num_programs(0) - 1
    n_used = n_used_ref[0]
    d, ff = wg_s.shape
    chunks = d // LANES
    pitch = _row_pitch(chunks)
    tm = o_ref.shape[0] // pitch

    def row_copy(ids_ref, slot, j):
        return pltpu.make_async_copy(_row_ref(hf_ref, ids_ref[0, 0, j], chunks),
                                     _row_ref(xbuf.at[slot], j, chunks), gsem.at[slot])

    def weight_copies(expert, slot):
        return [pltpu.make_async_copy(hbm.at[expert], stage.at[slot], wsem.at[slot])
                for hbm, stage in ((wg_hbm, wg_f), (wu_hbm, wu_f), (wd_hbm, wd_f))]

    @pl.when((i == 0) & (n_used > 0))
    def _():
        _start_row_gather(lambda j: _row_ref(hf_ref, tokens_ref[0, 0, j], chunks), chunks, xbuf.at[0],
                          gsem.at[0], tm)
        for cp in weight_copies(tile_expert_ref[0], run_parity_ref[0]):
            cp.start(priority=WEIGHT_DMA_PRIORITY)

    @pl.when(i < n_used)
    def _():
        @pl.when((i == 0) | (tile_expert_ref[i] != tile_expert_ref[jnp.maximum(i - 1, 0)]))
        def _():
            stage = run_parity_ref[i]
            for cp in weight_copies(tile_expert_ref[i], stage):
                cp.wait()

            @pl.when(next_expert_ref[i] >= 0)
            def _():
                for cp in weight_copies(next_expert_ref[i], 1 - stage):
                    cp.start(priority=WEIGHT_DMA_PRIORITY)

            wg_s[...] = wg_f[stage].astype(BF16)
            wu_s[...] = wu_f[stage].astype(BF16)
            wd_s[...] = wd_f[stage].astype(BF16)

        slot = i % 2
        _wait_row_gather(xbuf.at[slot], gsem.at[slot], tm, chunks)
        x = _load_gathered(xbuf.at[slot], tm, chunks).astype(BF16)

        down_split = min(EXPERT_DOWN_SPLIT, chunks)
        n_stages = 2 + down_split
        bounds = [tm * s // n_stages for s in range(n_stages + 1)]

        def start_next_rows(stage):
            for j in range(bounds[stage], bounds[stage + 1]):
                row_copy(next_tokens_ref, 1 - slot, j).start()

        halves = []
        for half in range(2):
            start_next_rows(half)
            cols = slice(half * ff // 2, (half + 1) * ff // 2)
            halves.append((_silu(_dot(x, wg_s[:, cols])) * _dot(x, wu_s[:, cols])).astype(BF16))
        a = jnp.concatenate(halves, axis=1)
        per_part = chunks // down_split
        for part in range(down_split):
            start_next_rows(2 + part)
            y = _dot(a, wd_s[:, part * per_part * LANES:(part + 1) * per_part * LANES])
            for c in range(per_part):
                o_ref[pl.ds(part * per_part + c, tm, stride=pitch), :] = y[:, c * LANES:(c + 1) * LANES]
        o_ref[pl.ds(chunks, tm, stride=pitch), :] = jnp.zeros((tm, LANES), o_ref.dtype)

        @pl.when(i == last)
        def _():
            _wait_row_gather(xbuf.at[1 - slot], gsem.at[1 - slot], tm, chunks)

    @pl.when(i >= n_used)
    def _():
        @pl.when((i == n_used) & (i > 0))
        def _():
            _wait_row_gather(xbuf.at[i % 2], gsem.at[i % 2], tm, chunks)

        o_ref[...] = jnp.zeros(o_ref.shape, o_ref.dtype)


def _experts(tile_meta, slot_token, hf, w_gate, w_up, w_down, n_tiles, tm):
    d, ff = w_gate.shape[-2:]
    chunks = d // LANES
    pitch = _row_pitch(chunks)
    slot_tiles = slot_token.reshape(n_tiles, 1, tm)
    any_space = pl.BlockSpec(memory_space=pl.ANY)
    ids_block = lambda ahead: pl.BlockSpec(
        (1, 1, tm), lambda i, *prefetch: (jnp.minimum(i + ahead, n_tiles - 1), 0, 0),
        memory_space=pltpu.SMEM)
    grid_spec = pltpu.PrefetchScalarGridSpec(
        num_scalar_prefetch=4,
        grid=(n_tiles,),
        in_specs=[ids_block(0), ids_block(1), any_space, any_space, any_space, any_space],
        out_specs=pl.BlockSpec((tm * pitch, LANES), lambda i, *prefetch: (i, 0)),
        scratch_shapes=[
            pltpu.VMEM((2, tm * pitch, LANES), hf.dtype),
            pltpu.SemaphoreType.DMA((2,)),
            pltpu.VMEM((2, d, ff), F32),
            pltpu.VMEM((2, d, ff), F32),
            pltpu.VMEM((2, ff, d), F32),
            pltpu.SemaphoreType.DMA((2,)),
            pltpu.VMEM((d, ff), BF16),
            pltpu.VMEM((d, ff), BF16),
            pltpu.VMEM((ff, d), BF16),
        ],
    )
    return pl.pallas_call(
        _expert_kernel,
        grid_spec=grid_spec,
        out_shape=jax.ShapeDtypeStruct((n_tiles * tm * pitch, LANES), hf.dtype),
        compiler_params=_params("arbitrary"),
        name="routed_experts",
    )(*tile_meta, slot_tiles, slot_tiles, hf, w_gate, w_up, w_down)


def _combine_kernel(pos_ref, ys_ref, x1_ref, route_ref, gf_ref, fw_ref, o_ref, ybuf, sem):
    i = pl.program_id(0)
    n_tiles = pl.num_programs(0)
    tm, d = x1_ref.shape
    chunks = d // LANES

    def start_tile(tile, slot):
        for choice in range(2):
            _start_row_gather(lambda j: _row_ref(ys_ref, pos_ref[(tile * tm + j) * 2 + choice], chunks),
                              chunks, ybuf.at[slot, choice], sem.at[slot, choice], tm)

    @pl.when(i == 0)
    def _():
        start_tile(0, 0)

    @pl.when(i + 1 < n_tiles)
    def _():
        start_tile(i + 1, (i + 1) % 2)

    slot = i % 2
    for choice in range(2):
        _wait_row_gather(ybuf.at[slot, choice], sem.at[slot, choice], tm, chunks)
    route = route_ref[...]
    moe = (route[:, R_W0:R_W0 + 1] * _load_gathered(ybuf.at[slot, 0], tm, chunks)
           + route[:, R_W1:R_W1 + 1] * _load_gathered(ybuf.at[slot, 1], tm, chunks))
    x2 = x1_ref[...] + gf_ref[0] * moe
    o_ref[...] = _rms(x2, fw_ref[...])


def _combine(pos, ys, x1, route, mod3, final_w, n, tm):
    m, d = x1.shape
    chunks = d // LANES
    grid_spec = pltpu.PrefetchScalarGridSpec(
        num_scalar_prefetch=1,
        grid=(m // tm,),
        in_specs=[
            pl.BlockSpec(memory_space=pl.ANY),
            pl.BlockSpec((tm, d), lambda i, pos_ref: (i, 0)),
            pl.BlockSpec((tm, LANES), lambda i, pos_ref: (i, 0)),
            pl.BlockSpec((1, 1, d), lambda i, pos_ref: (((i * tm) // n) * N_MOD + 5, 0, 0)),
            pl.BlockSpec((1, d), lambda i, pos_ref: (0, 0)),
        ],
        out_specs=pl.BlockSpec((tm, d), lambda i, pos_ref: (i, 0)),
        scratch_shapes=[pltpu.VMEM((2, 2, tm * _row_pitch(chunks), LANES), ys.dtype),
                        pltpu.SemaphoreType.DMA((2, 2))],
    )
    return pl.pallas_call(
        _combine_kernel,
        grid_spec=grid_spec,
        out_shape=jax.ShapeDtypeStruct((m, d), F32),
        compiler_params=_params("arbitrary"),
        name="combine_final_norm",
    )(pos, ys, x1, route, mod3, final_w.reshape(1, d))


def kernel(x, c, ctx, c_ctx, w_mod, b_mod, norm_mix_w, w_in, ret_decay_f, ret_decay_b, ret_gn_w, na_rpb, w_out, norm_ffn_w, w_router_group, b_router_group, w_router_expert, b_router_expert, w_gate, w_up, w_down, final_norm_w):
    assert w_mod.shape[0] == 1, "single trunk layer"
    batch, n, d = x.shape
    ctx_len = ctx.shape[1]
    n_groups = w_router_group.shape[-1]
    per_group = w_router_expert.shape[-1]
    n_experts = w_gate.shape[1]
    assert n_groups * per_group == n_experts and n_groups + n_experts <= LANES

    mod_rows = 8
    cc = jnp.zeros((mod_rows, d), F32).at[:batch].set(c).at[batch].set(c_ctx)
    mod = _modulation(cc, w_mod[0], b_mod[0])
    mod3 = mod.reshape(mod_rows * N_MOD, 1, d)

    w_in_b = w_in[0]
    tm = min(512, n)
    tm_in = min(1024, n)
    x2d = x.reshape(batch * n, d)
    proj = _in_projection(x2d, mod3, lambda i: (i * tm_in) // n, norm_mix_w[0], w_in_b, tm_in)
    cproj = _in_projection(ctx.reshape(batch * ctx_len, d), mod3, lambda i: batch, norm_mix_w[0],
                           w_in_b, batch * ctx_len, CTX_COL_BLOCKS)

    lg = jnp.stack([jax.nn.log_sigmoid(ret_decay_f[0].astype(F32)),
                    jax.nn.log_sigmoid(ret_decay_b[0].astype(F32))])
    ret = _retention(proj, cproj, lg, ret_gn_w[0], batch, n, ctx_len)
    na = _neighbourhood_attention(proj, cproj, na_rpb[0], batch, n, ctx_len)

    w_route = jnp.concatenate(
        [w_router_group[0], jnp.moveaxis(w_router_expert[0], 0, 1).reshape(d, n_experts)], axis=1)
    w_route = jnp.pad(w_route.astype(F32), ((0, 0), (0, LANES - n_groups - n_experts)))
    w_route_hi = w_route.astype(BF16)
    w_route = jnp.concatenate([w_route_hi, (w_route - w_route_hi.astype(F32)).astype(BF16)], axis=1)
    b_route = jnp.concatenate([b_router_group[0], b_router_expert[0].reshape(-1)])
    b_route = jnp.pad(b_route.astype(F32), (0, LANES - n_groups - n_experts)).reshape(1, LANES)
    x1, hf, route, counts = _out_projection(ret, na, w_out[0].astype(BF16), x2d, mod3, norm_ffn_w[0],
                                            w_route, b_route, n, n_groups, per_group, tm)

    te = EXPERT_TILE
    tokens = batch * n
    assert tokens < 1 << KEY_RANK_BITS and (2 * tokens) % te == 0
    n_tiles = (2 * tokens) // te + n_experts
    cnt = counts[0, n_groups:n_groups + n_experts].astype(jnp.int32)
    padded = (cnt + te - 1) // te * te
    starts = jnp.cumsum(padded) - padded
    keys = route[:, R_KEY0:R_KEY1 + 1].astype(jnp.int32).reshape(-1)
    expert = lax.shift_right_logical(keys, KEY_RANK_BITS)
    start_of = jnp.sum(jnp.where(expert[:, None] == jnp.arange(n_experts)[None, :], starts[None, :], 0), axis=1)
    pos = start_of + jnp.bitwise_and(keys, (1 << KEY_RANK_BITS) - 1)
    slot_token, *tile_meta = _slot_map(pos, cnt, n_tiles, te)

    ys = _experts(tile_meta, slot_token, hf, w_gate[0], w_up[0], w_down[0], n_tiles, te)
    out = _combine(pos, ys, x1, route, mod3, final_norm_w, n, min(256, n))
    return out.reshape(batch, n, d)
```

```python
import functools
import math

import jax
import jax.numpy as jnp
from jax import lax
from jax.experimental import pallas as pl
from jax.experimental.pallas import tpu as pltpu

F32 = jnp.float32
BF16 = jnp.bfloat16

GRID_W = 64
RET_HEADS = 4
RET_HEAD_DIM = 256
RET_WIDTH = RET_HEADS * RET_HEAD_DIM
NA_HEADS = 8
NA_HEAD_DIM = 128
NA_WIDTH = NA_HEADS * NA_HEAD_DIM
RET_CHUNK = 128
NA_ROWS = 8
NA_COLS = 16
ROPE_BASE = 10000.0
N_MOD = 6
NORM_EPS = 1e-6
ROPE_HALF = RET_HEAD_DIM // 2
RET_UNROLL = 8

INPROJ_TN = 1024
assert INPROJ_TN == RET_WIDTH == NA_WIDTH
CTX_COL_BLOCKS = (1, 2, 5, 6)
CTX_RET_K, CTX_RET_V, CTX_NA_K, CTX_NA_V = 0, 1, 2, 3

NA_QROWS = 16
NA_MASK = -1e30

LANES = 128
SUBLANES = 8
BF16_ROWS = 2 * SUBLANES
VMEM_LIMIT_BYTES = 56 * 1024 * 1024

R_KEY0, R_KEY1, R_W0, R_W1 = 0, 1, 2, 3
KEY_RANK_BITS = 16
KEY_RANK_SPAN = float(1 << KEY_RANK_BITS)

EXPERT_TILE = 256
WEIGHT_DMA_PRIORITY = 0
EXPERT_DOWN_SPLIT = 4


def _params(*sem):
    return pltpu.CompilerParams(dimension_semantics=sem, vmem_limit_bytes=VMEM_LIMIT_BYTES)


def _dot(a, b):
    return jnp.dot(a, b, preferred_element_type=F32)


def _dot_nt(a, b):
    return lax.dot_general(a, b, (((1,), (1,)), ((), ())), preferred_element_type=F32)


def _dot_tn(a, b):
    return lax.dot_general(a, b, (((0,), (0,)), ((), ())), preferred_element_type=F32)


def _rms(x, w):
    return x * lax.rsqrt(jnp.mean(x * x, axis=-1, keepdims=True) + NORM_EPS) * w


def _silu(x):
    return x * jax.nn.sigmoid(x)


def _row_pitch(chunks):
    return chunks + 1


def _store_rows(ref, val):
    n_rows, d = val.shape
    chunks = d // LANES
    pitch = _row_pitch(chunks)
    for c in range(chunks):
        ref[pl.ds(c, n_rows, stride=pitch), :] = val[:, c * LANES:(c + 1) * LANES]
    ref[pl.ds(chunks, n_rows, stride=pitch), :] = jnp.zeros((n_rows, LANES), ref.dtype)


def _row_ref(hbm, row, chunks):
    return hbm.at[pl.ds(row * _row_pitch(chunks), chunks), :]


def _start_row_gather(src_row, chunks, dst_buf, sem, n_rows):
    def body(j, carry):
        pltpu.make_async_copy(src_row(j), _row_ref(dst_buf, j, chunks), sem).start()
        return carry

    lax.fori_loop(0, n_rows, body, 0, unroll=SUBLANES)


def _wait_row_gather(dst_buf, sem, n_rows, chunks):
    view = dst_buf.at[pl.ds(0, n_rows * chunks), :]
    pltpu.make_async_copy(view, view, sem).wait()


def _load_gathered(buf, n_rows, chunks):
    return jnp.concatenate([buf[pl.ds(c, n_rows, stride=_row_pitch(chunks)), :] for c in range(chunks)],
                           axis=1)


def _mod_kernel(c_ref, w_ref, b_ref, o_ref):
    a = _silu(c_ref[...]).astype(BF16)
    o_ref[...] = _dot(a, w_ref[...].astype(BF16)) + b_ref[...]


def _modulation(cc, w_mod, b_mod):
    rows, d = cc.shape
    width = w_mod.shape[1]
    tn = next(t for t in (1024, 512, 256, LANES) if width % t == 0)
    return pl.pallas_call(
        _mod_kernel,
        grid=(width // tn,),
        in_specs=[
            pl.BlockSpec((rows, d), lambda j: (0, 0)),
            pl.BlockSpec((d, tn), lambda j: (0, j)),
            pl.BlockSpec((1, tn), lambda j: (0, j)),
        ],
        out_specs=pl.BlockSpec((rows, tn), lambda j: (0, j)),
        out_shape=jax.ShapeDtypeStruct((rows, width), F32),
        compiler_params=_params("arbitrary"),
        name="modulation",
    )(cc, w_mod, b_mod.reshape(1, width))


def _inproj_kernel(rows_per_step, x_ref, sh_ref, sc_ref, nw_ref, w_ref, o_ref, h_s):
    i = pl.program_id(0)
    j = pl.program_id(1)
    tm = o_ref.shape[0]

    def normalise(rows, slot):
        y = _rms(x_ref[rows, :], nw_ref[...])
        h_s[slot, rows, :] = (y * (1.0 + sc_ref[0]) + sh_ref[0]).astype(BF16)

    @pl.when((i == 0) & (j == 0))
    def _():
        normalise(pl.ds(0, tm), 0)

    o_ref[...] = _dot(h_s[i % 2], w_ref[...].astype(BF16)).astype(BF16)
    start = jnp.clip((j - 1) * rows_per_step, 0, tm - rows_per_step)
    normalise(pl.ds(pl.multiple_of(start, BF16_ROWS), rows_per_step), (i + 1) % 2)


def _in_projection(x2d, mod3, mod_row_of_tile, norm_w, w_in, tm, col_blocks=None):
    m, d = x2d.shape
    tn = INPROJ_TN
    if col_blocks is None:
        col_blocks = tuple(range(w_in.shape[1] // tn))
    n_blocks = len(col_blocks)
    n_tiles = m // tm
    assert n_blocks >= 2 and tm % BF16_ROWS == 0
    rows_per_step = -(-tm // ((n_blocks - 1) * BF16_ROWS)) * BF16_ROWS
    w_block = lambda j: sum(jnp.where(j == k, blk, 0) for k, blk in enumerate(col_blocks))
    src = lambda i, j: jnp.where((i == 0) & (j == 0), 0, jnp.minimum(i + 1, n_tiles - 1))
    return pl.pallas_call(
        functools.partial(_inproj_kernel, rows_per_step),
        grid=(n_tiles, n_blocks),
        in_specs=[
            pl.BlockSpec((tm, d), lambda i, j: (src(i, j), 0)),
            pl.BlockSpec((1, 1, d), lambda i, j: (mod_row_of_tile(src(i, j)) * N_MOD + 0, 0, 0)),
            pl.BlockSpec((1, 1, d), lambda i, j: (mod_row_of_tile(src(i, j)) * N_MOD + 1, 0, 0)),
            pl.BlockSpec((1, d), lambda i, j: (0, 0)),
            pl.BlockSpec((d, tn), lambda i, j: (0, w_block(j))),
        ],
        out_specs=pl.BlockSpec((tm, tn), lambda i, j: (i, j)),
        out_shape=jax.ShapeDtypeStruct((m, n_blocks * tn), BF16),
        scratch_shapes=[pltpu.VMEM((2, tm, d), BF16)],
        compiler_params=_params("arbitrary", "arbitrary"),
        name="in_projection",
    )(x2d, mod3, mod3, norm_w.reshape(1, d), w_in)


def _ret_kernel(lg_ref, q_ref, k_ref, v_ref, g_ref, ck_ref, cv_ref, cosr_ref, sinr_ref,
                cosc_ref, sinc_ref, gnw_ref, o_ref, qr_s, kr_s, sfh_s, sbh_s, sf_s, sb_s):
    head = pl.program_id(1)
    lgf = lg_ref[0, head]
    lgb = lg_ref[1, head]
    n = q_ref.shape[0]
    c = RET_CHUNK
    nc = n // c
    ctx_len = ck_ref.shape[0]
    k_scale = RET_HEAD_DIM ** -0.5

    posl = lax.broadcasted_iota(jnp.int32, (ctx_len, 1), 0).astype(F32)
    ck = ck_ref[...].astype(F32) * k_scale
    cv = cv_ref[...]
    sf_s[...] = _dot_tn((ck * jnp.exp(lgf * (ctx_len - 1.0 - posl))).astype(BF16), cv)
    sb_s[...] = _dot_tn((ck * jnp.exp(lgb * posl)).astype(BF16), cv)

    cosc = cosc_ref[...]
    sinc = sinc_ref[...]
    pos = lax.broadcasted_iota(jnp.int32, (c, 1), 0).astype(F32)
    qdec_f = jnp.exp(lgf * (pos + 1.0))
    kdec_f = jnp.exp(lgf * (c - 1.0 - pos))
    cdec_f = jnp.exp(lgf * c)
    qdec_b = jnp.exp(lgb * (c - pos))
    kdec_b = jnp.exp(lgb * pos)
    cdec_b = jnp.exp(lgb * c)

    def rope(x, cosr, sinr):
        xa = x[:, :ROPE_HALF]
        xb = x[:, ROPE_HALF:]
        ya = xa * cosr + pltpu.roll(xa, ROPE_HALF // 2, 1) * sinr
        yb = xb * cosc + pltpu.roll(xb, ROPE_HALF // 2, 1) * sinc
        return jnp.concatenate([ya, yb], axis=1)

    def fwd_chunk(ci, carry):
        r0 = pl.multiple_of(ci * c, c)
        cosr = cosr_ref[pl.ds(r0, c), :]
        sinr = sinr_ref[pl.ds(r0, c), :]
        qr_s[pl.ds(r0, c), :] = rope(q_ref[pl.ds(r0, c), :].astype(F32), cosr, sinr).astype(BF16)
        k = rope(k_ref[pl.ds(r0, c), :].astype(F32), cosr, sinr) * k_scale
        kr_s[pl.ds(r0, c), :] = k.astype(BF16)
        kv = _dot_tn((k * kdec_f).astype(BF16), v_ref[pl.ds(r0, c), :])
        state = sf_s[...]
        sfh_s[ci] = state.astype(BF16)
        sf_s[...] = state * cdec_f + kv
        return carry

    lax.fori_loop(0, nc, fwd_chunk, 0, unroll=RET_UNROLL)

    def bwd_chunk(i, carry):
        ci = nc - 1 - i
        r0 = pl.multiple_of(ci * c, c)
        k = kr_s[pl.ds(r0, c), :].astype(F32)
        kv = _dot_tn((k * kdec_b).astype(BF16), v_ref[pl.ds(r0, c), :])
        state = sb_s[...]
        sbh_s[ci] = state.astype(BF16)
        sb_s[...] = state * cdec_b + kv
        return carry

    lax.fori_loop(0, nc, bwd_chunk, 0, unroll=RET_UNROLL)

    diff = (lax.broadcasted_iota(jnp.int32, (c, c), 0)
            - lax.broadcasted_iota(jnp.int32, (c, c), 1)).astype(F32)
    intra = (jnp.where(diff >= 0, jnp.exp(lgf * jnp.maximum(diff, 0.0)), 0.0)
             + jnp.where(diff <= 0, jnp.exp(lgb * jnp.maximum(-diff, 0.0)), 0.0))
    gnw = gnw_ref[...]

    def out_chunk(ci, carry):
        r0 = pl.multiple_of(ci * c, c)
        qb = qr_s[pl.ds(r0, c), :]
        kb = kr_s[pl.ds(r0, c), :]
        q = qb.astype(F32)
        scores = _dot_nt(qb, kb) * intra
        o = (_dot(scores.astype(BF16), v_ref[pl.ds(r0, c), :])
             + _dot((q * qdec_f).astype(BF16), sfh_s[ci])
             + _dot((q * qdec_b).astype(BF16), sbh_s[ci]))
        mu = jnp.mean(o, axis=-1, keepdims=True)
        d = o - mu
        var = jnp.mean(d * d, axis=-1, keepdims=True)
        on = d * lax.rsqrt(var + NORM_EPS) * gnw
        gate = _silu(g_ref[pl.ds(r0, c), :].astype(F32))
        o_ref[pl.ds(r0, c), :] = (on * gate).astype(BF16)
        return carry

    lax.fori_loop(0, nc, out_chunk, 0, unroll=RET_UNROLL)


def _rope_tables(n):
    inv = ROPE_BASE ** (-jnp.arange(0, ROPE_HALF, 2, dtype=F32) / ROPE_HALF)

    def tables(pos):
        ang = pos.astype(F32)[:, None] * inv[None, :]
        cos = jnp.cos(ang)
        sin = jnp.sin(ang)
        return jnp.concatenate([cos, cos], axis=1), jnp.concatenate([-sin, sin], axis=1)

    t = jnp.arange(n)
    cosr, sinr = tables(t // GRID_W)
    cosc, sinc = tables(jnp.arange(RET_CHUNK) % GRID_W)
    return cosr, sinr, cosc, sinc


def _retention(proj, cproj, lg, gn_w, batch, n, ctx_len):
    hd = RET_HEAD_DIM
    cosr, sinr, cosc, sinc = _rope_tables(n)
    col = lambda which: (lambda b, h, lg_ref: (b, which * RET_HEADS + h))
    const = lambda b, h, lg_ref: (0, 0)
    grid_spec = pltpu.PrefetchScalarGridSpec(
        num_scalar_prefetch=1,
        grid=(batch, RET_HEADS),
        in_specs=[
            pl.BlockSpec((n, hd), col(0)),
            pl.BlockSpec((n, hd), col(1)),
            pl.BlockSpec((n, hd), col(2)),
            pl.BlockSpec((n, hd), col(3)),
            pl.BlockSpec((ctx_len, hd), lambda b, h, lg_ref: (b, CTX_RET_K * RET_HEADS + h)),
            pl.BlockSpec((ctx_len, hd), lambda b, h, lg_ref: (b, CTX_RET_V * RET_HEADS + h)),
            pl.BlockSpec((n, ROPE_HALF), const),
            pl.BlockSpec((n, ROPE_HALF), const),
            pl.BlockSpec((RET_CHUNK, ROPE_HALF), const),
            pl.BlockSpec((RET_CHUNK, ROPE_HALF), const),
            pl.BlockSpec((1, hd), lambda b, h, lg_ref: (0, h)),
        ],
        out_specs=pl.BlockSpec((n, hd), lambda b, h, lg_ref: (b, h)),
        scratch_shapes=[
            pltpu.VMEM((n, hd), BF16),
            pltpu.VMEM((n, hd), BF16),
            pltpu.VMEM((n // RET_CHUNK, hd, hd), BF16),
            pltpu.VMEM((n // RET_CHUNK, hd, hd), BF16),
            pltpu.VMEM((hd, hd), F32),
            pltpu.VMEM((hd, hd), F32),
        ],
    )
    return pl.pallas_call(
        _ret_kernel,
        grid_spec=grid_spec,
        out_shape=jax.ShapeDtypeStruct((batch * n, RET_WIDTH), BF16),
        compiler_params=_params("parallel", "arbitrary"),
        name="retention",
    )(lg, proj, proj, proj, proj, cproj, cproj, cosr, sinr, cosc, sinc, gn_w.reshape(1, RET_WIDTH))


def _na_build_bias(rpb_ref, head, pair_s):
    n_roff, n_coff = 2 * NA_ROWS - 1, 2 * NA_COLS - 1
    c = lax.broadcasted_iota(jnp.int32, (GRID_W, LANES), 0)
    lane = lax.broadcasted_iota(jnp.int32, (GRID_W, LANES), 1)
    kc = jnp.bitwise_and(lane, GRID_W - 1)
    cs = jnp.clip(c - NA_COLS // 2, 0, GRID_W - NA_COLS)
    d = jnp.where((kc >= cs) & (kc < cs + NA_COLS), kc - c + (NA_COLS - 1), -1)

    def table(i):
        acc = jnp.full((GRID_W, LANES), NA_MASK, F32)
        base = (head * n_roff + i) * n_coff
        for j in range(n_coff):
            acc = jnp.where(d == j, rpb_ref[base + j], acc)
        return acc

    prev = table(0)
    for i in range(1, n_roff):
        cur = table(i)
        pair_s[i - 1] = jnp.where(lane < GRID_W, prev, cur)
        prev = cur


def _na_kernel(rpb_ref, q_ref, k_ref, v_ref, kc_ref, vc_ref, o_ref, s_s, m_s, pair_s):
    @pl.when(pl.program_id(1) == 0)
    def _():
        _na_build_bias(rpb_ref, pl.program_id(0), pair_s)

    n = q_ref.shape[0]
    rows = n // GRID_W
    kb = NA_ROWS * GRID_W
    ctx_len = kc_ref.shape[0]
    scale = NA_HEAD_DIM ** -0.5
    kc = kc_ref[...]
    vc = jnp.concatenate([vc_ref[...], jnp.ones((ctx_len, NA_HEAD_DIM), BF16)], axis=1)
    ones_w = jnp.ones((kb, NA_HEAD_DIM), BF16)

    def window(r):
        rs = jnp.clip(r - NA_ROWS // 2, 0, rows - NA_ROWS)
        return rs, pl.multiple_of(r * GRID_W, GRID_W), pl.multiple_of(rs * GRID_W, GRID_W)

    def score_row(r):
        rs, q0, k0 = window(r)
        q = (q_ref[pl.ds(q0, GRID_W), :].astype(F32) * scale).astype(BF16)
        off = rs - r + NA_ROWS - 1
        bias = jnp.concatenate([pair_s[off + 2 * t] for t in range(NA_ROWS // 2)], axis=1)
        s_win = _dot_nt(q, k_ref[pl.ds(k0, kb), :]) + bias
        s_ctx = _dot_nt(q, kc)
        m = jnp.maximum(jnp.max(s_win, axis=-1, keepdims=True), jnp.max(s_ctx, axis=-1, keepdims=True))
        s_s[pl.ds(q0, GRID_W), :kb] = s_win
        s_s[pl.ds(q0, GRID_W), kb:] = s_ctx
        m_s[pl.ds(q0, GRID_W), :] = jnp.broadcast_to(m, (GRID_W, LANES))

    def value_row(r):
        _, q0, k0 = window(r)
        m = m_s[pl.ds(q0, GRID_W), :]
        p_win = jnp.exp(s_s[pl.ds(q0, GRID_W), :kb] - jnp.tile(m, (1, kb // LANES))).astype(BF16)
        p_ctx = jnp.exp(s_s[pl.ds(q0, GRID_W), kb:] - jnp.tile(m, (1, ctx_len // LANES))).astype(BF16)
        vw = jnp.concatenate([v_ref[pl.ds(k0, kb), :], ones_w], axis=1)
        o = _dot(p_win, vw) + _dot(p_ctx, vc)
        o_ref[pl.ds(q0, GRID_W), :] = (o[:, :NA_HEAD_DIM] / o[:, NA_HEAD_DIM:]).astype(BF16)

    qrows = math.gcd(rows, NA_QROWS)

    def sweep(row_fn):
        def group(g, carry):
            for u in range(qrows):
                row_fn(g * qrows + u)
            return carry
        lax.fori_loop(0, rows // qrows, group, 0)

    sweep(score_row)
    sweep(value_row)


def _neighbourhood_attention(proj, cproj, rpb, batch, n, ctx_len):
    hd = NA_HEAD_DIM
    rows = n // GRID_W
    assert rows >= NA_ROWS
    assert rpb.shape == (NA_HEADS, 2 * NA_ROWS - 1, 2 * NA_COLS - 1)
    base = 4 * RET_WIDTH // hd
    col = lambda which: (lambda h, b, rpb_ref: (b, base + which * NA_HEADS + h))
    grid_spec = pltpu.PrefetchScalarGridSpec(
        num_scalar_prefetch=1,
        grid=(NA_HEADS, batch),
        in_specs=[
            pl.BlockSpec((n, hd), col(0)),
            pl.BlockSpec((n, hd), col(1)),
            pl.BlockSpec((n, hd), col(2)),
            pl.BlockSpec((ctx_len, hd), lambda h, b, rpb_ref: (b, CTX_NA_K * NA_HEADS + h)),
            pl.BlockSpec((ctx_len, hd), lambda h, b, rpb_ref: (b, CTX_NA_V * NA_HEADS + h)),
        ],
        out_specs=pl.BlockSpec((n, hd), lambda h, b, rpb_ref: (b, h)),
        scratch_shapes=[
            pltpu.VMEM((n, NA_ROWS * GRID_W + ctx_len), F32),
            pltpu.VMEM((n, LANES), F32),
            pltpu.VMEM((2 * NA_ROWS - 2, GRID_W, LANES), F32),
        ],
    )
    return pl.pallas_call(
        _na_kernel,
        grid_spec=grid_spec,
        out_shape=jax.ShapeDtypeStruct((batch * n, NA_WIDTH), BF16),
        compiler_params=_params("parallel", "arbitrary"),
        name="neighbourhood_attention",
    )(rpb.astype(F32).reshape(-1), proj, proj, proj, cproj, cproj)


def _outproj_kernel(n_groups, per_group, ret_ref, na_ref, w1_ref, w2_ref, x_ref, ga_ref, shf_ref,
                    scf_ref, nw_ref, wr_ref, br_ref, x1_ref, hf_ref, route_ref, cnt_ref, carry_s,
                    earlier_s):
    tm = x_ref.shape[0]

    @pl.when(pl.program_id(0) == 0)
    def _():
        carry_s[...] = jnp.zeros_like(carry_s)
        earlier_s[...] = jnp.where(lax.broadcasted_iota(jnp.int32, (tm, tm), 0)
                                   > lax.broadcasted_iota(jnp.int32, (tm, tm), 1), 1.0, 0.0).astype(BF16)

    acc = _dot(ret_ref[...], w1_ref[...]) + _dot(na_ref[...], w2_ref[...])
    x1 = x_ref[...] + ga_ref[0] * acc
    x1_ref[...] = x1
    hf = _rms(x1, nw_ref[...]) * (1.0 + scf_ref[0]) + shf_ref[0]
    _store_rows(hf_ref, hf)

    hf_hi = hf.astype(BF16)
    hf_lo = (hf - hf_hi.astype(F32)).astype(BF16)
    p_hi = _dot(hf_hi, wr_ref[...])
    p_lo = _dot(hf_lo, wr_ref[...])
    logits = p_hi[:, :LANES] + p_hi[:, LANES:] + p_lo[:, :LANES] + br_ref[...]
    lane = lax.broadcasted_iota(jnp.int32, (tm, LANES), 1)
    neg = -jnp.inf

    def first_max(vals):
        top = jnp.max(vals, axis=-1, keepdims=True)
        idx = jnp.min(jnp.where(vals == top, lane, LANES), axis=-1, keepdims=True)
        return top, idx

    g_logits = jnp.where(lane < n_groups, logits, neg)
    g_top, g_sel = first_max(g_logits)
    g_w = 1.0 / jnp.sum(jnp.exp(g_logits - g_top), axis=-1, keepdims=True)
    lo = n_groups + g_sel * per_group
    e_logits = jnp.where((lane >= lo) & (lane < lo + per_group), logits, neg)
    v0, i0 = first_max(e_logits)
    v1, i1 = first_max(jnp.where(lane == i0, neg, e_logits))
    e1 = jnp.exp(v1 - v0)
    w0 = g_w / (1.0 + e1)
    w1 = g_w * e1 / (1.0 + e1)

    hit0 = lane == i0
    hit1 = lane == i1
    onehot = jnp.where(hit0 | hit1, 1.0, 0.0)
    before = _dot(earlier_s[...], onehot.astype(BF16)) + carry_s[...]
    rank0 = jnp.sum(jnp.where(hit0, before, 0.0), axis=-1, keepdims=True)
    rank1 = jnp.sum(jnp.where(hit1, before, 0.0), axis=-1, keepdims=True)
    carry_s[...] = carry_s[...] + jnp.sum(onehot, axis=0, keepdims=True)
    cnt_ref[...] = carry_s[...]

    key0 = (i0 - n_groups).astype(F32) * KEY_RANK_SPAN + rank0
    key1 = (i1 - n_groups).astype(F32) * KEY_RANK_SPAN + rank1
    fields = {R_KEY0: key0, R_KEY1: key1, R_W0: w0, R_W1: w1}
    route = jnp.zeros((tm, LANES), F32)
    for idx, val in fields.items():
        route = jnp.where(lane == idx, val, route)
    route_ref[...] = route


def _out_projection(ret, na, w_out, x2d, mod3, norm_w, w_route, b_route, n, n_groups, per_group, tm):
    m, d = x2d.shape
    pitch = _row_pitch(d // LANES)
    batch_of = lambda i: (i * tm) // n
    mod_spec = lambda which: pl.BlockSpec((1, 1, d), lambda i: (batch_of(i) * N_MOD + which, 0, 0))
    const2 = lambda i: (0, 0)
    return pl.pallas_call(
        functools.partial(_outproj_kernel, n_groups, per_group),
        grid=(m // tm,),
        in_specs=[
            pl.BlockSpec((tm, RET_WIDTH), lambda i: (i, 0)),
            pl.BlockSpec((tm, NA_WIDTH), lambda i: (i, 0)),
            pl.BlockSpec((RET_WIDTH, d), lambda i: (0, 0)),
            pl.BlockSpec((NA_WIDTH, d), lambda i: (RET_WIDTH // NA_WIDTH, 0)),
            pl.BlockSpec((tm, d), lambda i: (i, 0)),
            mod_spec(2), mod_spec(3), mod_spec(4),
            pl.BlockSpec((1, d), const2),
            pl.BlockSpec((d, 2 * LANES), const2),
            pl.BlockSpec((1, LANES), const2),
        ],
        out_specs=[
            pl.BlockSpec((tm, d), lambda i: (i, 0)),
            pl.BlockSpec((tm * pitch, LANES), lambda i: (i, 0)),
            pl.BlockSpec((tm, LANES), lambda i: (i, 0)),
            pl.BlockSpec((1, LANES), const2),
        ],
        out_shape=[
            jax.ShapeDtypeStruct((m, d), F32),
            jax.ShapeDtypeStruct((m * pitch, LANES), F32),
            jax.ShapeDtypeStruct((m, LANES), F32),
            jax.ShapeDtypeStruct((1, LANES), F32),
        ],
        scratch_shapes=[pltpu.VMEM((1, LANES), F32), pltpu.VMEM((tm, tm), BF16)],
        compiler_params=_params("arbitrary"),
        name="out_projection_router",
    )(ret, na, w_out, w_out, x2d, mod3, mod3, mod3, norm_w.reshape(1, d), w_route, b_route)


def _slot_map_kernel(tile_rows, pos_ref, cnt_ref, slot_ref, tile_expert_ref, next_expert_ref,
                     run_parity_ref, n_used_ref, offs_s):
    n_experts = cnt_ref.shape[0]
    n_tiles = tile_expert_ref.shape[0]
    shift = tile_rows.bit_length() - 1
    assert 1 << shift == tile_rows

    def fill(ref, lo, hi, val):
        value_at = val if callable(val) else (lambda s: val)
        groups = lax.div(hi - lo, SUBLANES)

        def group(g, carry):
            for k in range(SUBLANES):
                s = lo + g * SUBLANES + k
                ref[s] = value_at(s)
            return carry

        def single(s, carry):
            ref[s] = value_at(s)
            return carry

        lax.fori_loop(0, groups, group, 0)
        lax.fori_loop(lo + groups * SUBLANES, hi, single, 0)

    assert tile_rows <= pos_ref.shape[0] // 2
    padding_row = lambda s: jnp.bitwise_and(s, tile_rows - 1)

    def tiles_of(e):
        size = lax.shift_left(lax.shift_right_logical(cnt_ref[e] + (tile_rows - 1), shift), shift)
        return lax.shift_right_logical(offs_s[e], shift), lax.shift_right_logical(offs_s[e] + size, shift)

    def per_expert(e, carry):
        start, runs = carry
        cnt = cnt_ref[e]
        size = lax.shift_left(lax.shift_right_logical(cnt + (tile_rows - 1), shift), shift)
        offs_s[e] = start
        t0, t1 = tiles_of(e)
        fill(tile_expert_ref, t0, t1, e)
        fill(run_parity_ref, t0, t1, jnp.bitwise_and(runs, 1))
        fill(slot_ref, start + cnt, start + size, padding_row)
        return start + size, runs + (size > 0).astype(jnp.int32)

    end, _ = lax.fori_loop(0, n_experts, per_expert, (0, 0))
    n_used = lax.shift_right_logical(end, shift)
    n_used_ref[0] = n_used
    fill(tile_expert_ref, n_used, n_tiles, n_experts - 1)
    fill(run_parity_ref, n_used, n_tiles, 0)
    fill(next_expert_ref, n_used, n_tiles, -1)
    fill(slot_ref, end, slot_ref.shape[0], padding_row)

    def per_expert_reversed(k, following):
        e = n_experts - 1 - k
        t0, t1 = tiles_of(e)
        fill(next_expert_ref, t0, t1, following)
        return jnp.where(t1 > t0, e, following)

    lax.fori_loop(0, n_experts, per_expert_reversed, -1)

    def assign(token, carry):
        slot_ref[pos_ref[2 * token]] = token
        slot_ref[pos_ref[2 * token + 1]] = token
        return carry

    lax.fori_loop(0, pos_ref.shape[0] // 2, assign, 0, unroll=SUBLANES)


def _slot_map(pos, cnt, n_tiles, tile_rows):
    smem = pl.BlockSpec(memory_space=pltpu.SMEM)
    i32 = lambda *shape: jax.ShapeDtypeStruct(shape, jnp.int32)
    return pl.pallas_call(
        functools.partial(_slot_map_kernel, tile_rows),
        in_specs=[smem, smem],
        out_specs=[smem] * 5,
        out_shape=[i32(n_tiles * tile_rows), i32(n_tiles), i32(n_tiles), i32(n_tiles), i32(1)],
        scratch_shapes=[pltpu.SMEM((cnt.shape[0],), jnp.int32)],
        name="slot_map",
    )(pos, cnt)


def _expert_kernel(tile_expert_ref, next_expert_ref, run_parity_ref, n_used_ref, tokens_ref,
                   next_tokens_ref, hf_ref, wg_hbm, wu_hbm, wd_hbm, o_ref, xbuf, gsem, wg_f, wu_f,
                   wd_f, wsem, wg_s, wu_s, wd_s):
    i = pl.program_id(0)
    last = pl.num_programs(0) - 1
    n_used = n_used_ref[0]
    d, ff = wg_s.shape
    chunks = d // LANES
    pitch = _row_pitch(chunks)
    tm = o_ref.shape[0] // pitch

    def row_copy(ids_ref, slot, j):
        return pltpu.make_async_copy(_row_ref(hf_ref, ids_ref[0, 0, j], chunks),
                                     _row_ref(xbuf.at[slot], j, chunks), gsem.at[slot])

    def weight_copies(expert, slot):
        return [pltpu.make_async_copy(hbm.at[expert], stage.at[slot], wsem.at[slot])
                for hbm, stage in ((wg_hbm, wg_f), (wu_hbm, wu_f), (wd_hbm, wd_f))]

    @pl.when((i == 0) & (n_used > 0))
    def _():
        _start_row_gather(lambda j: _row_ref(hf_ref, tokens_ref[0, 0, j], chunks), chunks, xbuf.at[0],
                          gsem.at[0], tm)
        for cp in weight_copies(tile_expert_ref[0], run_parity_ref[0]):
            cp.start(priority=WEIGHT_DMA_PRIORITY)

    @pl.when(i < n_used)
    def _():
        @pl.when((i == 0) | (tile_expert_ref[i] != tile_expert_ref[jnp.maximum(i - 1, 0)]))
        def _():
            stage = run_parity_ref[i]
            for cp in weight_copies(tile_expert_ref[i], stage):
                cp.wait()

            @pl.when(next_expert_ref[i] >= 0)
            def _():
                for cp in weight_copies(next_expert_ref[i], 1 - stage):
                    cp.start(priority=WEIGHT_DMA_PRIORITY)

            wg_s[...] = wg_f[stage].astype(BF16)
            wu_s[...] = wu_f[stage].astype(BF16)
            wd_s[...] = wd_f[stage].astype(BF16)

        slot = i % 2
        _wait_row_gather(xbuf.at[slot], gsem.at[slot], tm, chunks)
        x = _load_gathered(xbuf.at[slot], tm, chunks).astype(BF16)

        down_split = min(EXPERT_DOWN_SPLIT, chunks)
        n_stages = 2 + down_split
        bounds = [tm * s // n_stages for s in range(n_stages + 1)]

        def start_next_rows(stage):
            for j in range(bounds[stage], bounds[stage + 1]):
                row_copy(next_tokens_ref, 1 - slot, j).start(priority=1 - WEIGHT_DMA_PRIORITY)

        halves = []
        for half in range(2):
            start_next_rows(half)
            cols = slice(half * ff // 2, (half + 1) * ff // 2)
            halves.append((_silu(_dot(x, wg_s[:, cols])) * _dot(x, wu_s[:, cols])).astype(BF16))
        a = jnp.concatenate(halves, axis=1)
        per_part = chunks // down_split
        for part in range(down_split):
            start_next_rows(2 + part)
            y = _dot(a, wd_s[:, part * per_part * LANES:(part + 1) * per_part * LANES])
            for c in range(per_part):
                o_ref[pl.ds(part * per_part + c, tm, stride=pitch), :] = y[:, c * LANES:(c + 1) * LANES]
        o_ref[pl.ds(chunks, tm, stride=pitch), :] = jnp.zeros((tm, LANES), o_ref.dtype)

        @pl.when(i == last)
        def _():
            _wait_row_gather(xbuf.at[1 - slot], gsem.at[1 - slot], tm, chunks)

    @pl.when(i >= n_used)
    def _():
        @pl.when((i == n_used) & (i > 0))
        def _():
            _wait_row_gather(xbuf.at[i % 2], gsem.at[i % 2], tm, chunks)

        o_ref[...] = jnp.zeros(o_ref.shape, o_ref.dtype)


def _experts(tile_meta, slot_token, hf, w_gate, w_up, w_down, n_tiles, tm):
    d, ff = w_gate.shape[-2:]
    chunks = d // LANES
    pitch = _row_pitch(chunks)
    slot_tiles = slot_token.reshape(n_tiles, 1, tm)
    any_space = pl.BlockSpec(memory_space=pl.ANY)
    ids_block = lambda ahead: pl.BlockSpec(
        (1, 1, tm), lambda i, *prefetch: (jnp.minimum(i + ahead, n_tiles - 1), 0, 0),
        memory_space=pltpu.SMEM)
    grid_spec = pltpu.PrefetchScalarGridSpec(
        num_scalar_prefetch=4,
        grid=(n_tiles,),
        in_specs=[ids_block(0), ids_block(1), any_space, any_space, any_space, any_space],
        out_specs=pl.BlockSpec((tm * pitch, LANES), lambda i, *prefetch: (i, 0)),
        scratch_shapes=[
            pltpu.VMEM((2, tm * pitch, LANES), hf.dtype),
            pltpu.SemaphoreType.DMA((2,)),
            pltpu.VMEM((2, d, ff), F32),
            pltpu.VMEM((2, d, ff), F32),
            pltpu.VMEM((2, ff, d), F32),
            pltpu.SemaphoreType.DMA((2,)),
            pltpu.VMEM((d, ff), BF16),
            pltpu.VMEM((d, ff), BF16),
            pltpu.VMEM((ff, d), BF16),
        ],
    )
    return pl.pallas_call(
        _expert_kernel,
        grid_spec=grid_spec,
        out_shape=jax.ShapeDtypeStruct((n_tiles * tm * pitch, LANES), hf.dtype),
        compiler_params=_params("arbitrary"),
        name="routed_experts",
    )(*tile_meta, slot_tiles, slot_tiles, hf, w_gate, w_up, w_down)


def _combine_kernel(pos_ref, ys_ref, x1_ref, route_ref, gf_ref, fw_ref, o_ref, ybuf, sem):
    i = pl.program_id(0)
    n_tiles = pl.num_programs(0)
    tm, d = x1_ref.shape
    chunks = d // LANES

    def start_tile(tile, slot):
        for choice in range(2):
            _start_row_gather(lambda j: _row_ref(ys_ref, pos_ref[(tile * tm + j) * 2 + choice], chunks),
                              chunks, ybuf.at[slot, choice], sem.at[slot, choice], tm)

    @pl.when(i == 0)
    def _():
        start_tile(0, 0)

    @pl.when(i + 1 < n_tiles)
    def _():
        start_tile(i + 1, (i + 1) % 2)

    slot = i % 2
    for choice in range(2):
        _wait_row_gather(ybuf.at[slot, choice], sem.at[slot, choice], tm, chunks)
    route = route_ref[...]
    moe = (route[:, R_W0:R_W0 + 1] * _load_gathered(ybuf.at[slot, 0], tm, chunks)
           + route[:, R_W1:R_W1 + 1] * _load_gathered(ybuf.at[slot, 1], tm, chunks))
    x2 = x1_ref[...] + gf_ref[0] * moe
    o_ref[...] = _rms(x2, fw_ref[...])


def _combine(pos, ys, x1, route, mod3, final_w, n, tm):
    m, d = x1.shape
    chunks = d // LANES
    grid_spec = pltpu.PrefetchScalarGridSpec(
        num_scalar_prefetch=1,
        grid=(m // tm,),
        in_specs=[
            pl.BlockSpec(memory_space=pl.ANY),
            pl.BlockSpec((tm, d), lambda i, pos_ref: (i, 0)),
            pl.BlockSpec((tm, LANES), lambda i, pos_ref: (i, 0)),
            pl.BlockSpec((1, 1, d), lambda i, pos_ref: (((i * tm) // n) * N_MOD + 5, 0, 0)),
            pl.BlockSpec((1, d), lambda i, pos_ref: (0, 0)),
        ],
        out_specs=pl.BlockSpec((tm, d), lambda i, pos_ref: (i, 0)),
        scratch_shapes=[pltpu.VMEM((2, 2, tm * _row_pitch(chunks), LANES), ys.dtype),
                        pltpu.SemaphoreType.DMA((2, 2))],
    )
    return pl.pallas_call(
        _combine_kernel,
        grid_spec=grid_spec,
        out_shape=jax.ShapeDtypeStruct((m, d), F32),
        compiler_params=_params("arbitrary"),
        name="combine_final_norm",
    )(pos, ys, x1, route, mod3, final_w.reshape(1, d))


def kernel(x, c, ctx, c_ctx, w_mod, b_mod, norm_mix_w, w_in, ret_decay_f, ret_decay_b, ret_gn_w, na_rpb, w_out, norm_ffn_w, w_router_group, b_router_group, w_router_expert, b_router_expert, w_gate, w_up, w_down, final_norm_w):
    assert w_mod.shape[0] == 1, "single trunk layer"
    batch, n, d = x.shape
    ctx_len = ctx.shape[1]
    n_groups = w_router_group.shape[-1]
    per_group = w_router_expert.shape[-1]
    n_experts = w_gate.shape[1]
    assert n_groups * per_group == n_experts and n_groups + n_experts <= LANES

    mod_rows = 8
    cc = jnp.zeros((mod_rows, d), F32).at[:batch].set(c).at[batch].set(c_ctx)
    mod = _modulation(cc, w_mod[0], b_mod[0])
    mod3 = mod.reshape(mod_rows * N_MOD, 1, d)

    w_in_b = w_in[0]
    tm = min(512, n)
    tm_in = min(1024, n)
    x2d = x.reshape(batch * n, d)
    proj = _in_projection(x2d, mod3, lambda i: (i * tm_in) // n, norm_mix_w[0], w_in_b, tm_in)
    cproj = _in_projection(ctx.reshape(batch * ctx_len, d), mod3, lambda i: batch, norm_mix_w[0],
                           w_in_b, batch * ctx_len, CTX_COL_BLOCKS)

    lg = jnp.stack([jax.nn.log_sigmoid(ret_decay_f[0].astype(F32)),
                    jax.nn.log_sigmoid(ret_decay_b[0].astype(F32))])
    ret = _retention(proj, cproj, lg, ret_gn_w[0], batch, n, ctx_len)
    na = _neighbourhood_attention(proj, cproj, na_rpb[0], batch, n, ctx_len)

    w_route = jnp.concatenate(
        [w_router_group[0], jnp.moveaxis(w_router_expert[0], 0, 1).reshape(d, n_experts)], axis=1)
    w_route = jnp.pad(w_route.astype(F32), ((0, 0), (0, LANES - n_groups - n_experts)))
    w_route_hi = w_route.astype(BF16)
    w_route = jnp.concatenate([w_route_hi, (w_route - w_route_hi.astype(F32)).astype(BF16)], axis=1)
    b_route = jnp.concatenate([b_router_group[0], b_router_expert[0].reshape(-1)])
    b_route = jnp.pad(b_route.astype(F32), (0, LANES - n_groups - n_experts)).reshape(1, LANES)
    x1, hf, route, counts = _out_projection(ret, na, w_out[0].astype(BF16), x2d, mod3, norm_ffn_w[0],
                                            w_route, b_route, n, n_groups, per_group, tm)

    te = EXPERT_TILE
    tokens = batch * n
    assert tokens < 1 << KEY_RANK_BITS and (2 * tokens) % te == 0
    n_tiles = (2 * tokens) // te + n_experts
    cnt = counts[0, n_groups:n_groups + n_experts].astype(jnp.int32)
    padded = (cnt + te - 1) // te * te
    starts = jnp.cumsum(padded) - padded
    keys = route[:, R_KEY0:R_KEY1 + 1].astype(jnp.int32).reshape(-1)
    expert = lax.shift_right_logical(keys, KEY_RANK_BITS)
    start_of = jnp.sum(jnp.where(expert[:, None] == jnp.arange(n_experts)[None, :], starts[None, :], 0), axis=1)
    pos = start_of + jnp.bitwise_and(keys, (1 << KEY_RANK_BITS) - 1)
    slot_token, *tile_meta = _slot_map(pos, cnt, n_tiles, te)

    ys = _experts(tile_meta, slot_token, hf, w_gate[0], w_up[0], w_down[0], n_tiles, te)
    out = _combine(pos, ys, x1, route, mod3, final_norm_w, n, min(256, n))
    return out.reshape(batch, n, d)
```
